```python
import math
import jax, jax.numpy as jnp
from jax import lax
import numpy as np

D_MODEL = 2048
BATCH = 4
SEQ = 2048
DEPTH = 1

D_MIX = D_MODEL
D_ATTN = D_MIX // 2
N_HEADS = 8
HEAD_DIM = D_ATTN // N_HEADS
D_SSM = D_MIX - D_ATTN
SSM_GROUP = 16
N_SSM_GROUPS = D_SSM // SSM_GROUP
SSM_STATE = 64
DT_MIN = 1e-3
DT_MAX = 1e-1
Q_BLOCK = 128
N_EXPERTS = 64
TOP_K = 6
D_EXPERT = 512
D_SHARED = 512
ROUTED_SCALE = 2.5
MOE_BLOCK = 128
D_PLE = 256
NORM_EPS = 1e-6

kernel_name = 'hybrid_sba_s5_moe_block'


def rmsnorm(x, g):
    xf = x.astype(jnp.float32)
    y = xf * lax.rsqrt(jnp.mean(xf * xf, axis=-1, keepdims=True) + NORM_EPS)
    return (y * g.astype(jnp.float32)).astype(x.dtype)


def stick_breaking_attention(q, k, v):
    q = q.astype(jnp.float32)
    k = k.astype(jnp.float32)
    v = v.astype(jnp.float32)
    seq = q.shape[2]
    scale = 1.0 / math.sqrt(q.shape[-1])
    outs = []
    for blk in range(seq // Q_BLOCK):
        q0 = blk * Q_BLOCK
        q1 = q0 + Q_BLOCK
        qb = q[:, :, q0:q1]
        kb = k[:, :, :q1]
        vb = v[:, :, :q1]
        z = jnp.einsum('bhqd,bhkd->bhqk', qb, kb) * scale
        t_pos = jnp.arange(q0, q1)[:, None]
        s_pos = jnp.arange(q1)[None, :]
        causal = s_pos < t_pos
        sp = jnp.where(causal, jax.nn.softplus(z), 0.0)
        after = lax.cumsum(sp, axis=3, reverse=True) - sp
        w = jnp.where(causal, jnp.exp(jax.nn.log_sigmoid(z) - after), 0.0)
        outs.append(jnp.einsum('bhqk,bhkd->bhqd', w, vb))
    return jnp.concatenate(outs, axis=2)


def s5_groups(u, lam_re, lam_im, log_dt, b_re, b_im, c_re, c_im, d_skip):
    u = u.astype(jnp.float32)
    lam_re = lam_re.astype(jnp.float32)
    lam_im = lam_im.astype(jnp.float32)
    dt = jnp.exp(log_dt.astype(jnp.float32))[:, None]
    mag = jnp.exp(lam_re * dt)
    ang = lam_im * dt
    abar_re = mag * jnp.cos(ang)
    abar_im = mag * jnp.sin(ang)
    nr = abar_re - 1.0
    ni = abar_im
    den = lam_re * lam_re + lam_im * lam_im
    coef_re = ((nr * lam_re + ni * lam_im) / den)[..., None]
    coef_im = ((ni * lam_re - nr * lam_im) / den)[..., None]
    b_re = b_re.astype(jnp.float32)
    b_im = b_im.astype(jnp.float32)
    bbar_re = coef_re * b_re - coef_im * b_im
    bbar_im = coef_re * b_im + coef_im * b_re
    bu_re = jnp.einsum('bsgh,gph->bsgp', u, bbar_re)
    bu_im = jnp.einsum('bsgh,gph->bsgp', u, bbar_im)
    seq = u.shape[1]
    a_re = jnp.broadcast_to(abar_re[None, None], (1, seq) + abar_re.shape)
    a_im = jnp.broadcast_to(abar_im[None, None], (1, seq) + abar_im.shape)

    def combine(left, right):
        a1r, a1i, b1r, b1i = left
        a2r, a2i, b2r, b2i = right
        return (a2r * a1r - a2i * a1i,
                a2r * a1i + a2i * a1r,
                a2r * b1r - a2i * b1i + b2r,
                a2r * b1i + a2i * b1r + b2i)

    _, _, x_re, x_im = lax.associative_scan(combine, (a_re, a_im, bu_re, bu_im), axis=1)
    y = (jnp.einsum('bsgp,ghp->bsgh', x_re, c_re.astype(jnp.float32))
         - jnp.einsum('bsgp,ghp->bsgh', x_im, c_im.astype(jnp.float32))
         + d_skip.astype(jnp.float32) * u)
    return y


def swiglu(x, wg, wu, wd):
    return (jax.nn.silu(x @ wg) * (x @ wu)) @ wd


def routed_moe(xf, w_router, router_bias, wg, wu, wd):
    T, D = xf.shape
    scores = jax.nn.sigmoid((xf @ w_router).astype(jnp.float32))
    _, idx = lax.top_k(scores + router_bias.astype(jnp.float32), TOP_K)
    gate = jnp.take_along_axis(scores, idx, axis=1)
    gate = gate / jnp.sum(gate, axis=1, keepdims=True) * ROUTED_SCALE
    n_assign = T * TOP_K
    flat_e = idx.reshape(-1)
    flat_tok = (jnp.arange(n_assign, dtype=jnp.int32) // TOP_K).astype(jnp.int32)
    flat_w = gate.reshape(-1)
    order = jnp.argsort(flat_e)
    se = flat_e[order]
    stok = flat_tok[order]
    sw = flat_w[order]
    counts = jnp.bincount(flat_e, length=N_EXPERTS).astype(jnp.int32)
    start = jnp.cumsum(counts) - counts
    pcounts = (counts + MOE_BLOCK - 1) // MOE_BLOCK * MOE_BLOCK
    pend = jnp.cumsum(pcounts)
    pstart = pend - pcounts
    dest = pstart[se] + jnp.arange(n_assign, dtype=jnp.int32) - start[se]
    n_pad = n_assign + N_EXPERTS * MOE_BLOCK
    n_blocks = n_pad // MOE_BLOCK
    buf_tok = jnp.full((n_pad,), T, dtype=jnp.int32).at[dest].set(stok)
    buf_w = jnp.zeros((n_pad,), jnp.float32).at[dest].set(sw)
    blk_start = jnp.arange(n_blocks, dtype=jnp.int32) * MOE_BLOCK
    blk_e = jnp.minimum(jnp.searchsorted(pend, blk_start, side='right'), N_EXPERTS - 1)
    x_pad = jnp.concatenate([xf, jnp.zeros((1, D), xf.dtype)], axis=0)

    def body(acc, blk):
        tok, w, e = blk
        xb = x_pad[tok]
        hb = jax.nn.silu(xb @ wg[e]) * (xb @ wu[e])
        yb = (hb @ wd[e]) * w[:, None].astype(xb.dtype)
        return acc.at[tok].add(yb.astype(acc.dtype)), None

    acc, _ = lax.scan(body, jnp.zeros((T + 1, D), xf.dtype),
                      (buf_tok.reshape(n_blocks, MOE_BLOCK), buf_w.reshape(n_blocks, MOE_BLOCK), blk_e))
    return acc[:T]


def setup_inputs(seed: int = 0) -> dict:
    key = jax.random.key(seed)
    ks = jax.random.split(key, 32)
    L, D, G, H, P = DEPTH, D_MODEL, N_SSM_GROUPS, SSM_GROUP, SSM_STATE
    nrm = lambda k, shape, s: jax.random.normal(k, shape, jnp.float32) * s
    gain = lambda k, shape: 1.0 + 0.02 * jax.random.normal(k, shape, jnp.float32)
    n_idx = jnp.arange(P, dtype=jnp.float32)
    return {
        'x': nrm(ks[0], (BATCH, SEQ, D), 1.0),
        'p': nrm(ks[1], (DEPTH, BATCH, SEQ, D_PLE), 1.0),
        'norm_mix': gain(ks[2], (L, D)),
        'w_in': nrm(ks[3], (L, D, 3 * D_ATTN + D_SSM), D ** -0.5),
        'q_norm': gain(ks[4], (L, HEAD_DIM)),
        'k_norm': gain(ks[5], (L, HEAD_DIM)),
        'ssm_lam_re': -0.5 + 0.01 * jax.random.normal(ks[6], (L, G, P), jnp.float32),
        'ssm_lam_im': math.pi * n_idx[None, None, :] + 0.01 * jax.random.normal(ks[7], (L, G, P), jnp.float32),
        'ssm_log_dt': jax.random.uniform(ks[8], (L, G), jnp.float32, math.log(DT_MIN), math.log(DT_MAX)),
        'ssm_b_re': nrm(ks[9], (L, G, P, H), (2.0 * H) ** -0.5),
        'ssm_b_im': nrm(ks[10], (L, G, P, H), (2.0 * H) ** -0.5),
        'ssm_c_re': nrm(ks[11], (L, G, H, P), (2.0 * P) ** -0.5),
        'ssm_c_im': nrm(ks[12], (L, G, H, P), (2.0 * P) ** -0.5),
        'ssm_d': nrm(ks[13], (L, G, H), 0.5),
        'w_glu': nrm(ks[14], (L, D_SSM, D_SSM), D_SSM ** -0.5),
        'attn_out_norm': gain(ks[15], (L, D_ATTN)),
        'ssm_out_norm': gain(ks[16], (L, D_SSM)),
        'w_out': nrm(ks[17], (L, D_MIX, D), D_MIX ** -0.5),
        'norm_ffn': gain(ks[18], (L, D)),
        'w_router': nrm(ks[19], (L, D, N_EXPERTS), D ** -0.5),
        'router_bias': nrm(ks[20], (L, N_EXPERTS), 0.01),
        'w_exp_gate': nrm(ks[21], (L, N_EXPERTS, D, D_EXPERT), D ** -0.5),
        'w_exp_up': nrm(ks[22], (L, N_EXPERTS, D, D_EXPERT), D ** -0.5),
        'w_exp_down': nrm(ks[23], (L, N_EXPERTS, D_EXPERT, D), D_EXPERT ** -0.5),
        'w_sh_gate': nrm(ks[24], (L, D, D_SHARED), D ** -0.5),
        'w_sh_up': nrm(ks[25], (L, D, D_SHARED), D ** -0.5),
        'w_sh_down': nrm(ks[26], (L, D_SHARED, D), D_SHARED ** -0.5),
        'norm_ple': gain(ks[27], (L, D)),
        'w_ple_gate': nrm(ks[28], (L, D, D), D ** -0.5),
        'w_ple_proj': nrm(ks[29], (L, D_PLE, D), D_PLE ** -0.5),
    }


def reference(x, p, norm_mix, w_in, q_norm, k_norm, ssm_lam_re, ssm_lam_im, ssm_log_dt,
              ssm_b_re, ssm_b_im, ssm_c_re, ssm_c_im, ssm_d, w_glu, attn_out_norm, ssm_out_norm,
              w_out, norm_ffn, w_router, router_bias, w_exp_gate, w_exp_up, w_exp_down,
              w_sh_gate, w_sh_up, w_sh_down, norm_ple, w_ple_gate, w_ple_proj):
    bsz, seq, _ = x.shape
    h = x
    for i in range(DEPTH):
        a = rmsnorm(h, norm_mix[i])
        z = a @ w_in[i]
        q, k, v, u = jnp.split(z, [D_ATTN, 2 * D_ATTN, 3 * D_ATTN], axis=-1)
        q = rmsnorm(q.reshape(bsz, seq, N_HEADS, HEAD_DIM), q_norm[i])
        k = rmsnorm(k.reshape(bsz, seq, N_HEADS, HEAD_DIM), k_norm[i])
        v = v.reshape(bsz, seq, N_HEADS, HEAD_DIM)
        o_attn = stick_breaking_attention(q.transpose(0, 2, 1, 3), k.transpose(0, 2, 1, 3),
                                          v.transpose(0, 2, 1, 3))
        o_attn = o_attn.transpose(0, 2, 1, 3).reshape(bsz, seq, D_ATTN).astype(x.dtype)
        y = s5_groups(u.reshape(bsz, seq, N_SSM_GROUPS, SSM_GROUP), ssm_lam_re[i], ssm_lam_im[i],
                      ssm_log_dt[i], ssm_b_re[i], ssm_b_im[i], ssm_c_re[i], ssm_c_im[i], ssm_d[i])
        y = jax.nn.gelu(y.reshape(bsz, seq, D_SSM)).astype(x.dtype)
        y = y * jax.nn.sigmoid(y @ w_glu[i])
        mix = jnp.concatenate([rmsnorm(o_attn, attn_out_norm[i]), rmsnorm(y, ssm_out_norm[i])], axis=-1)
        h = h + mix @ w_out[i]
        m = rmsnorm(h, norm_ffn[i]).reshape(bsz * seq, D_MODEL)
        ffn = (routed_moe(m, w_router[i], router_bias[i], w_exp_gate[i], w_exp_up[i], w_exp_down[i])
               + swiglu(m, w_sh_gate[i], w_sh_up[i], w_sh_down[i]))
        h = h + ffn.reshape(bsz, seq, D_MODEL)
        gate = jax.nn.sigmoid(rmsnorm(h, norm_ple[i]) @ w_ple_gate[i])
        h = h + gate * (p[i] @ w_ple_proj[i])
    return h
```

```python
import functools
import math

import jax
import jax.numpy as jnp
from jax import lax
from jax.experimental import pallas as pl
from jax.experimental.pallas import tpu as pltpu

NORM_EPS = 1e-6
TOP_K = 6
ROUTED_SCALE = 2.5
IDX_COLS = 8
VMEM_LIMIT = 56 * 1024 * 1024

F32 = jnp.float32
BF16 = jnp.bfloat16


def _rms(x, g):
    return x * lax.rsqrt(jnp.mean(x * x, axis=-1, keepdims=True) + NORM_EPS) * g


def _sigmoid(x):
    return 1.0 / (1.0 + jnp.exp(-x))


def _params(sem):
    return pltpu.CompilerParams(dimension_semantics=sem, vmem_limit_bytes=VMEM_LIMIT)


def _inproj_kernel(x_ref, g_ref, w_ref, hg_ref, o_ref, xn_ref, *, n_norm_tiles, head_dim):
    j = pl.program_id(1)

    @pl.when(j == 0)
    def _():
        xn_ref[...] = _rms(x_ref[...], g_ref[...]).astype(BF16)

    acc = jnp.dot(xn_ref[...], w_ref[...], preferred_element_type=F32)
    tn = acc.shape[1]

    @pl.when(j < n_norm_tiles)
    def _():
        hg = hg_ref[...]
        for h in range(tn // head_dim):
            sl = slice(h * head_dim, (h + 1) * head_dim)
            o_ref[:, sl] = _rms(acc[:, sl], hg[:, sl]).astype(o_ref.dtype)

    @pl.when(j >= n_norm_tiles)
    def _():
        o_ref[...] = acc.astype(o_ref.dtype)


def _inproj(x2, g, w, hg, *, n_norm_tiles, head_dim, tm, tn):
    t, d = x2.shape
    n = w.shape[1]
    return pl.pallas_call(
        functools.partial(_inproj_kernel, n_norm_tiles=n_norm_tiles, head_dim=head_dim),
        grid=(t // tm, n // tn),
        in_specs=[
            pl.BlockSpec((tm, d), lambda i, j: (i, 0)),
            pl.BlockSpec((1, d), lambda i, j: (0, 0)),
            pl.BlockSpec((d, tn), lambda i, j: (0, j)),
            pl.BlockSpec((1, tn), lambda i, j: (0, j)),
        ],
        out_specs=pl.BlockSpec((tm, tn), lambda i, j: (i, j)),
        out_shape=jax.ShapeDtypeStruct((t, n), BF16),
        scratch_shapes=[pltpu.VMEM((tm, d), BF16)],
        compiler_params=_params(("parallel", "arbitrary")),
        name="inproj",
    )(x2, g, w, hg)


def _attn_kernel(q_ref, k_ref, v_ref, o_ref, *, blk):
    i = pl.program_id(2)
    q = q_ref[...]
    row = lax.broadcasted_iota(jnp.int32, (blk, blk), 0)
    col = lax.broadcasted_iota(jnp.int32, (blk, blk), 1)
    causal = col < row
    later = jnp.where(row > col, 1.0, 0.0).astype(BF16)

    def step(kb, carry, diag):
        acc, run = carry
        start = pl.multiple_of(kb * blk, blk)
        k = k_ref[pl.ds(start, blk), :]
        v = v_ref[pl.ds(start, blk), :]
        z = lax.dot_general(q, k, (((1,), (1,)), ((), ())), preferred_element_type=F32)
        sp = jnp.maximum(z, 0.0) + jnp.log(1.0 + jnp.exp(-jnp.abs(z)))
        if diag:
            sp = jnp.where(causal, sp, 0.0)
        hi = sp.astype(BF16)
        lo = (sp - hi.astype(F32)).astype(BF16)
        inner = (jnp.dot(hi, later, preferred_element_type=F32)
                 + jnp.dot(lo, later, preferred_element_type=F32))
        w = jnp.exp(z - sp - (inner + run))
        if diag:
            w = jnp.where(causal, w, 0.0)
        acc = acc + jnp.dot(w.astype(BF16), v, preferred_element_type=F32)
        run = run + jnp.sum(sp, axis=-1, keepdims=True)
        return acc, run

    init = (jnp.zeros(o_ref.shape, F32), jnp.zeros((blk, 1), F32))
    carry = step(i, init, True)
    acc, _ = lax.fori_loop(0, i, lambda n, c: step(i - 1 - n, c, False), carry)
    o_ref[...] = acc.astype(o_ref.dtype)


def _attention(z, *, batch, seq, n_heads, head_dim, blk):
    nq = seq // blk
    return pl.pallas_call(
        functools.partial(_attn_kernel, blk=blk),
        grid=(batch, n_heads, nq),
        in_specs=[
            pl.BlockSpec((blk, head_dim), lambda b, h, i: (b * nq + i, h)),
            pl.BlockSpec((seq, head_dim), lambda b, h, i: (b, n_heads + h)),
            pl.BlockSpec((seq, head_dim), lambda b, h, i: (b, 2 * n_heads + h)),
        ],
        out_specs=pl.BlockSpec((blk, head_dim), lambda b, h, i: (b * nq + i, h)),
        out_shape=jax.ShapeDtypeStruct((batch * seq, n_heads * head_dim), BF16),
        compiler_params=_params(("parallel", "parallel", "arbitrary")),
        name="sb_attention",
    )(z, z, z)


def _s5_prepare(lam_re, lam_im, log_dt, b_re, b_im, c_re, c_im, d_skip, chunk, n_chunks):
    hp = lax.Precision.HIGHEST
    g, p = lam_re.shape
    h = b_re.shape[-1]
    dt = jnp.exp(log_dt)[:, None]
    ks = jnp.arange(chunk + 1, dtype=F32)[None, :, None]
    mag = jnp.exp(lam_re[:, None, :] * dt[:, None, :] * ks)
    ang = lam_im[:, None, :] * dt[:, None, :] * ks
    ak_re = mag * jnp.cos(ang)
    ak_im = mag * jnp.sin(ang)
    nr = ak_re[:, 1] - 1.0
    ni = ak_im[:, 1]
    den = lam_re * lam_re + lam_im * lam_im
    coef_re = ((nr * lam_re + ni * lam_im) / den)[..., None]
    coef_im = ((ni * lam_re - nr * lam_im) / den)[..., None]
    bbar_re = coef_re * b_re - coef_im * b_im
    bbar_im = coef_re * b_im + coef_im * b_re
    ca_re = c_re[:, None] * ak_re[:, :, None, :] - c_im[:, None] * ak_im[:, :, None, :]
    ca_im = c_re[:, None] * ak_im[:, :, None, :] + c_im[:, None] * ak_re[:, :, None, :]
    kmat = (jnp.einsum("gkop,gph->gkoh", ca_re[:, :chunk], bbar_re, precision=hp)
            - jnp.einsum("gkop,gph->gkoh", ca_im[:, :chunk], bbar_im, precision=hp))
    jj = jnp.arange(chunk)[:, None]
    ii = jnp.arange(chunk)[None, :]
    lag = ii - jj
    blocks = kmat[:, jnp.clip(lag, 0, chunk - 1)]
    blocks = jnp.where((lag >= 0)[None, :, :, None, None], blocks, 0.0)
    eye_l = jnp.eye(chunk, dtype=F32)
    eye_h = jnp.eye(h, dtype=F32)
    blocks = blocks + (eye_l[None, :, :, None, None] * eye_h[None, None, None]
                       * d_skip[:, None, None, None, :])
    m_intra = blocks.transpose(0, 1, 4, 2, 3).reshape(g, chunk * h, chunk * h)
    rk_re = ak_re[:, chunk - 1::-1][:, :chunk]
    rk_im = ak_im[:, chunk - 1::-1][:, :chunk]
    in_re = rk_re[:, :, None, :] * bbar_re.transpose(0, 2, 1)[:, None] - rk_im[:, :, None, :] * bbar_im.transpose(0, 2, 1)[:, None]
    in_im = rk_re[:, :, None, :] * bbar_im.transpose(0, 2, 1)[:, None] + rk_im[:, :, None, :] * bbar_re.transpose(0, 2, 1)[:, None]
    m_in = jnp.concatenate([in_re, in_im], axis=-1).reshape(g, chunk * h, 2 * p)
    out_re = ca_re[:, 1:].transpose(0, 3, 1, 2).reshape(g, p, chunk * h)
    out_im = -ca_im[:, 1:].transpose(0, 3, 1, 2).reshape(g, p, chunk * h)
    m_out = jnp.concatenate([out_re, out_im], axis=1)
    steps = max(1, int(math.ceil(math.log2(max(n_chunks, 2)))))
    cr, ci = ak_re[:, chunk], ak_im[:, chunk]
    sc_a, sc_b = [], []
    for _ in range(steps):
        sc_a.append(jnp.concatenate([cr, cr], axis=-1))
        sc_b.append(jnp.concatenate([-ci, ci], axis=-1))
        cr, ci = cr * cr - ci * ci, 2.0 * cr * ci
    return (m_intra.astype(BF16), m_in.astype(BF16), m_out.astype(BF16),
            jnp.stack(sc_a, axis=1), jnp.stack(sc_b, axis=1))


def _s5_kernel(u_ref, mi_ref, min_ref, mout_ref, sa_ref, sb_ref, y_ref, *, n_chunks, steps, state):
    u = u_ref[...]
    y = jnp.dot(u, mi_ref[...], preferred_element_type=F32)
    x = jnp.dot(u, min_ref[...], preferred_element_type=F32)
    rows = x.shape[0]
    c = lax.rem(lax.broadcasted_iota(jnp.int32, x.shape, 0), n_chunks)
    sa = sa_ref[...]
    sb = sb_ref[...]
    for k in range(steps):
        sh = 1 << k
        xs = jnp.where(c >= sh, pltpu.roll(x, sh, axis=0), 0.0)
        x = x + xs * sa[k:k + 1, :] + pltpu.roll(xs, state, axis=1) * sb[k:k + 1, :]
    s_in = jnp.where(c >= 1, pltpu.roll(x, 1, axis=0), 0.0)
    del rows
    y_ref[...] = y + jnp.dot(s_in.astype(BF16), mout_ref[...], preferred_element_type=F32)


def _s5(u_g, m_intra, m_in, m_out, sc_a, sc_b, *, n_chunks):
    g, rows, lh = u_g.shape
    p2 = m_in.shape[-1]
    steps = sc_a.shape[1]
    return pl.pallas_call(
        functools.partial(_s5_kernel, n_chunks=n_chunks, steps=steps, state=p2 // 2),
        grid=(g,),
        in_specs=[
            pl.BlockSpec((None, rows, lh), lambda i: (i, 0, 0)),
            pl.BlockSpec((None, lh, lh), lambda i: (i, 0, 0)),
            pl.BlockSpec((None, lh, p2), lambda i: (i, 0, 0)),
            pl.BlockSpec((None, p2, lh), lambda i: (i, 0, 0)),
            pl.BlockSpec((None, steps, p2), lambda i: (i, 0, 0)),
            pl.BlockSpec((None, steps, p2), lambda i: (i, 0, 0)),
        ],
        out_specs=pl.BlockSpec((None, rows, lh), lambda i: (i, 0, 0)),
        out_shape=jax.ShapeDtypeStruct((g, rows, lh), F32),
        compiler_params=_params(("parallel",)),
        name="s5_chunked",
    )(u_g, m_intra, m_in, m_out, sc_a, sc_b)


def _postmix_kernel(oa_ref, ys_ref, x_ref, wglu_ref, woa_ref, wob_ref, ga_ref, gs_ref, gf_ref,
                    wr_ref, rb_ref, h1_ref, idx_ref, gate_ref, loc_ref, cnt_ref, run_ref, *, n_experts):
    step = pl.program_id(0)

    @pl.when(step == 0)
    def _():
        run_ref[...] = jnp.zeros(run_ref.shape, F32)

    y = ys_ref[...]
    y = 0.5 * y * (1.0 + jnp.tanh(math.sqrt(2.0 / math.pi) * (y + 0.044715 * (y * y * y))))
    y = y * _sigmoid(jnp.dot(y.astype(BF16), wglu_ref[...], preferred_element_type=F32))
    na = _rms(oa_ref[...].astype(F32), ga_ref[...]).astype(BF16)
    ns = _rms(y, gs_ref[...]).astype(BF16)
    h1 = (x_ref[...] + jnp.dot(na, woa_ref[...], preferred_element_type=F32)
          + jnp.dot(ns, wob_ref[...], preferred_element_type=F32))
    h1_ref[...] = h1

    m = _rms(h1, gf_ref[...])
    logits = jnp.dot(m, wr_ref[...], preferred_element_type=F32, precision=lax.Precision.HIGHEST)
    scores = _sigmoid(logits)
    vals = scores + rb_ref[...]
    tm = vals.shape[0]
    lane = lax.broadcasted_iota(jnp.int32, vals.shape, 1).astype(F32)
    picks, firsts, raws = [], [], []
    sel = jnp.zeros(vals.shape, F32)
    for _ in range(TOP_K):
        mx = jnp.max(vals, axis=1, keepdims=True)
        first = jnp.min(jnp.where(vals == mx, lane, float(n_experts)), axis=1, keepdims=True)
        pick = lane == first
        picks.append(pick)
        firsts.append(first)
        raws.append(jnp.sum(jnp.where(pick, scores, 0.0), axis=1, keepdims=True))
        vals = jnp.where(pick, -jnp.inf, vals)
        sel = jnp.where(pick, 1.0, sel)
    denom = raws[0]
    for r in raws[1:]:
        denom = denom + r

    r_i = lax.broadcasted_iota(jnp.int32, (tm, tm), 0)
    c_i = lax.broadcasted_iota(jnp.int32, (tm, tm), 1)
    before = jnp.where(c_i < r_i, 1.0, 0.0).astype(BF16)
    rank = jnp.dot(before, sel.astype(BF16), preferred_element_type=F32) + run_ref[...]
    run_ref[...] = run_ref[...] + jnp.sum(sel, axis=0, keepdims=True)
    cnt_ref[...] = run_ref[...]

    lane8 = lax.broadcasted_iota(jnp.int32, (tm, IDX_COLS), 1)
    idx8 = jnp.zeros((tm, IDX_COLS), jnp.int32)
    loc8 = jnp.zeros((tm, IDX_COLS), jnp.int32)
    gate8 = jnp.zeros((tm, IDX_COLS), F32)
    for k in range(TOP_K):
        loc = jnp.sum(jnp.where(picks[k], rank, 0.0), axis=1, keepdims=True).astype(jnp.int32)
        idx8 = jnp.where(lane8 == k, firsts[k].astype(jnp.int32), idx8)
        loc8 = jnp.where(lane8 == k, loc, loc8)
        gate8 = jnp.where(lane8 == k, raws[k] / denom * ROUTED_SCALE, gate8)
    idx_ref[...] = idx8
    loc_ref[...] = loc8
    gate_ref[...] = gate8


def _postmix(oa, ys, x2, wglu, woa, wob, ga, gs, gf, wr, rb, *, tm):
    t, d = x2.shape
    da = oa.shape[1]
    dsm = ys.shape[1]
    e = wr.shape[1]
    row = lambda i: (i, 0)
    fix = lambda i: (0, 0)
    return pl.pallas_call(
        functools.partial(_postmix_kernel, n_experts=e),
        grid=(t // tm,),
        in_specs=[
            pl.BlockSpec((tm, da), row), pl.BlockSpec((tm, dsm), row), pl.BlockSpec((tm, d), row),
            pl.BlockSpec((dsm, dsm), fix), pl.BlockSpec((da, d), fix), pl.BlockSpec((dsm, d), fix),
            pl.BlockSpec((1, da), fix), pl.BlockSpec((1, dsm), fix), pl.BlockSpec((1, d), fix),
            pl.BlockSpec((d, e), fix), pl.BlockSpec((1, e), fix),
        ],
        out_specs=[
            pl.BlockSpec((tm, d), row), pl.BlockSpec((tm, IDX_COLS), row), pl.BlockSpec((tm, IDX_COLS), row),
            pl.BlockSpec((tm, IDX_COLS), row), pl.BlockSpec((1, e), fix),
        ],
        out_shape=[
            jax.ShapeDtypeStruct((t, d), F32), jax.ShapeDtypeStruct((t, IDX_COLS), jnp.int32),
            jax.ShapeDtypeStruct((t, IDX_COLS), F32), jax.ShapeDtypeStruct((t, IDX_COLS), jnp.int32),
            jax.ShapeDtypeStruct((1, e), F32),
        ],
        scratch_shapes=[pltpu.VMEM((1, e), F32)],
        compiler_params=_params(("arbitrary",)),
        name="postmix_router",
    )(oa, ys, x2, wglu, woa, wob, ga, gs, gf, wr, rb)


def _row_copy(src, s, dst, d, sem):
    return pltpu.make_async_copy(src.at[pl.ds(s, 1)], dst.at[pl.ds(d, 1)], sem)


def _dispatch_kernel(dest_ref, h1_ref, gf_ref, wg_ref, wu_ref, wd_ref, xs_in_ref, h1s_ref, xs_ref,
                     m_ref, sem):
    del xs_in_ref
    tm = h1_ref.shape[0]
    h1 = h1_ref[...]
    m = _rms(h1, gf_ref[...])
    m_ref[...] = m

    def issue(t, carry):
        for k in range(TOP_K):
            _row_copy(m_ref, t, xs_ref, dest_ref[t * IDX_COLS + k], sem).start()
        return carry

    lax.fori_loop(0, tm, issue, 0)
    mb = m.astype(BF16)
    a = jnp.dot(mb, wg_ref[...], preferred_element_type=F32)
    b = jnp.dot(mb, wu_ref[...], preferred_element_type=F32)
    hid = (a * _sigmoid(a) * b).astype(BF16)
    h1s_ref[...] = h1 + jnp.dot(hid, wd_ref[...], preferred_element_type=F32)

    def drain(t, carry):
        for _ in range(TOP_K):
            _row_copy(m_ref, 0, xs_ref, 0, sem).wait()
        return carry

    lax.fori_loop(0, tm, drain, 0)


def _dispatch(dest_flat, h1, gf, wg, wu, wd, xs0, *, tm):
    t, d = h1.shape
    f = wg.shape[1]
    row = lambda i: (i, 0)
    fix = lambda i: (0, 0)
    return pl.pallas_call(
        _dispatch_kernel,
        grid=(t // tm,),
        in_specs=[
            pl.BlockSpec((tm * IDX_COLS,), lambda i: (i,), memory_space=pltpu.SMEM),
            pl.BlockSpec((tm, d), row), pl.BlockSpec((1, d), fix),
            pl.BlockSpec((d, f), fix), pl.BlockSpec((d, f), fix), pl.BlockSpec((f, d), fix),
            pl.BlockSpec(memory_space=pl.ANY),
        ],
        out_specs=[pl.BlockSpec((tm, d), row), pl.BlockSpec(memory_space=pl.ANY)],
        out_shape=[jax.ShapeDtypeStruct((t, d), F32), jax.ShapeDtypeStruct(xs0.shape, xs0.dtype)],
        scratch_shapes=[pltpu.VMEM((tm, d), F32), pltpu.SemaphoreType.DMA(())],
        input_output_aliases={6: 1},
        compiler_params=_params(("arbitrary",)),
        name="dispatch_shared",
    )(dest_flat, h1, gf, wg, wu, wd, xs0)


def _experts_kernel(be_ref, nu_ref, xs_ref, wg_ref, wu_ref, wd_ref, ys_ref, wgb_ref, wub_ref, wdb_ref):
    i = pl.program_id(0)
    e = be_ref[i]
    prev = be_ref[jnp.maximum(i - 1, 0)]

    @pl.when((i == 0) | (e != prev))
    def _():
        wgb_ref[...] = wg_ref[...].astype(BF16)
        wub_ref[...] = wu_ref[...].astype(BF16)
        wdb_ref[...] = wd_ref[...].astype(BF16)

    @pl.when(i < nu_ref[0])
    def _():
        xb = xs_ref[...].astype(BF16)
        a = jnp.dot(xb, wgb_ref[...], preferred_element_type=F32)
        b = jnp.dot(xb, wub_ref[...], preferred_element_type=F32)
        hid = (a * _sigmoid(a) * b).astype(BF16)
        ys_ref[...] = jnp.dot(hid, wdb_ref[...], preferred_element_type=F32)

    @pl.when(i >= nu_ref[0])
    def _():
        ys_ref[...] = jnp.zeros(ys_ref.shape, F32)


def _experts(blk_e, n_used, xs, wg, wu, wd, *, tm):
    n_pad, d = xs.shape
    e, _, f = wg.shape
    del e
    rows = lambda i, be, nu: (jnp.minimum(i, nu[0] - 1), 0)
    orow = lambda i, be, nu: (i, 0)
    wsel = lambda i, be, nu: (be[i], 0, 0)
    return pl.pallas_call(
        _experts_kernel,
        grid_spec=pltpu.PrefetchScalarGridSpec(
            num_scalar_prefetch=2,
            grid=(n_pad // tm,),
            in_specs=[
                pl.BlockSpec((tm, d), rows),
                pl.BlockSpec((None, d, f), wsel), pl.BlockSpec((None, d, f), wsel),
                pl.BlockSpec((None, f, d), wsel),
            ],
            out_specs=pl.BlockSpec((tm, d), orow),
            scratch_shapes=[pltpu.VMEM((d, f), BF16), pltpu.VMEM((d, f), BF16), pltpu.VMEM((f, d), BF16)],
        ),
        out_shape=jax.ShapeDtypeStruct((n_pad, d), F32),
        compiler_params=_params(("arbitrary",)),
        name="routed_experts",
    )(blk_e, n_used, xs, wg, wu, wd)


def _final_kernel(dest_ref, h_ref, gate_ref, p_ref, gp_ref, wgate_ref, wproj_ref, ys_ref, o_ref, buf_ref, sem):
    tm = h_ref.shape[0]

    def issue(t, carry):
        for k in range(TOP_K):
            _row_copy(ys_ref, dest_ref[t * IDX_COLS + k], buf_ref.at[k], t, sem).start()
        return carry

    lax.fori_loop(0, tm, issue, 0)
    pp = jnp.dot(p_ref[...].astype(BF16), wproj_ref[...], preferred_element_type=F32)

    def drain(t, carry):
        for k in range(TOP_K):
            _row_copy(ys_ref, 0, buf_ref.at[k], 0, sem).wait()
        return carry

    lax.fori_loop(0, tm, drain, 0)
    h2 = h_ref[...]
    gate = gate_ref[...]
    for k in range(TOP_K):
        h2 = h2 + gate[:, k:k + 1] * buf_ref[k]
    n = _rms(h2, gp_ref[...]).astype(BF16)
    o_ref[...] = h2 + _sigmoid(jnp.dot(n, wgate_ref[...], preferred_element_type=F32)) * pp


def _final(dest_flat, h1s, gate, p2, gp, wgate, wproj, ys, *, tm):
    t, d = h1s.shape
    dp = p2.shape[1]
    row = lambda i: (i, 0)
    fix = lambda i: (0, 0)
    return pl.pallas_call(
        _final_kernel,
        grid=(t // tm,),
        in_specs=[
            pl.BlockSpec((tm * IDX_COLS,), lambda i: (i,), memory_space=pltpu.SMEM),
            pl.BlockSpec((tm, d), row), pl.BlockSpec((tm, IDX_COLS), row), pl.BlockSpec((tm, dp), row),
            pl.BlockSpec((1, d), fix), pl.BlockSpec((d, d), fix), pl.BlockSpec((dp, d), fix),
            pl.BlockSpec(memory_space=pl.ANY),
        ],
        out_specs=pl.BlockSpec((tm, d), row),
        out_shape=jax.ShapeDtypeStruct((t, d), F32),
        scratch_shapes=[pltpu.VMEM((TOP_K, tm, d), F32), pltpu.SemaphoreType.DMA(())],
        compiler_params=_params(("arbitrary",)),
        name="combine_ple",
    )(dest_flat, h1s, gate, p2, gp, wgate, wproj, ys)


def _largest_tile(n, cap):
    t = min(n, cap)
    while n % t:
        t //= 2
    return t


def _layer(h, p_l, norm_mix, w_in, q_norm, k_norm, lam_re, lam_im, log_dt, b_re, b_im, c_re, c_im, d_skip,
           w_glu, attn_out_norm, ssm_out_norm, w_out, norm_ffn, w_router, router_bias, w_exp_gate,
           w_exp_up, w_exp_down, w_sh_gate, w_sh_up, w_sh_down, norm_ple, w_ple_gate, w_ple_proj):
    bsz, seq, d = h.shape
    t = bsz * seq
    head_dim = q_norm.shape[-1]
    d_attn = attn_out_norm.shape[-1]
    n_heads = d_attn // head_dim
    d_ssm = ssm_out_norm.shape[-1]
    n_groups, n_state = lam_re.shape
    grp = b_re.shape[-1]
    n_experts = w_router.shape[-1]
    x2 = h.reshape(t, d)

    tn = d_attn if d_attn % 128 == 0 else head_dim
    scale = 1.0 / math.sqrt(head_dim)
    hg = jnp.concatenate([jnp.tile(q_norm * scale, n_heads), jnp.tile(k_norm, n_heads),
                          jnp.ones((w_in.shape[1] - 2 * d_attn,), F32)])[None, :]
    z = _inproj(x2, norm_mix[None, :], w_in.astype(BF16), hg, n_norm_tiles=2 * d_attn // tn,
                head_dim=head_dim, tm=_largest_tile(t, 1024), tn=tn)

    o_attn = _attention(z, batch=bsz, seq=seq, n_heads=n_heads, head_dim=head_dim, blk=_largest_tile(seq, 256))

    chunk = _largest_tile(seq, 32)
    n_chunks = seq // chunk
    mats = _s5_prepare(lam_re, lam_im, log_dt, b_re, b_im, c_re, c_im, d_skip, chunk, n_chunks)
    u = z[:, 3 * d_attn:].reshape(bsz, n_chunks, chunk, n_groups, grp)
    u_g = u.transpose(3, 0, 1, 2, 4).reshape(n_groups, bsz * n_chunks, chunk * grp)
    y_g = _s5(u_g, *mats, n_chunks=n_chunks)
    y_ssm = (y_g.reshape(n_groups, bsz, n_chunks, chunk, grp).transpose(1, 2, 3, 0, 4).reshape(t, d_ssm))

    tm = _largest_tile(t, 256)
    w_out_b = w_out.astype(BF16)
    h1, idx, gate, loc, counts = _postmix(
        o_attn, y_ssm, x2, w_glu.astype(BF16), w_out_b[:d_attn], w_out_b[d_attn:],
        attn_out_norm[None, :], ssm_out_norm[None, :], norm_ffn[None, :], w_router, router_bias[None, :], tm=tm)

    rb = _largest_tile(t, 256)
    n_pad = t * TOP_K + n_experts * rb
    n_blocks = n_pad // rb
    cnt = counts[0].astype(jnp.int32)
    pcnt = (cnt + rb - 1) // rb * rb
    pend = jnp.cumsum(pcnt)
    pstart = pend - pcnt
    dest = (pstart[idx] + loc).reshape(-1)
    n_used = (pend[-1] // rb).astype(jnp.int32)
    blk_start = jnp.arange(n_blocks, dtype=jnp.int32) * rb
    blk_e = jnp.minimum(jnp.searchsorted(pend, blk_start, side="right"), n_experts - 1).astype(jnp.int32)
    blk_e = jnp.where(jnp.arange(n_blocks) < n_used, blk_e, blk_e[n_used - 1])

    xs0 = jnp.zeros((n_pad, d), F32)
    h1s, xs = _dispatch(dest, h1, norm_ffn[None, :], w_sh_gate.astype(BF16), w_sh_up.astype(BF16),
                        w_sh_down.astype(BF16), xs0, tm=tm)
    ys = _experts(blk_e, n_used[None], xs, w_exp_gate, w_exp_up, w_exp_down, tm=rb)
    out = _final(dest, h1s, gate, p_l.reshape(t, -1), norm_ple[None, :], w_ple_gate.astype(BF16),
                 w_ple_proj.astype(BF16), ys, tm=tm)
    return out.reshape(bsz, seq, d)


def kernel(x, p, norm_mix, w_in, q_norm, k_norm, ssm_lam_re, ssm_lam_im, ssm_log_dt, ssm_b_re, ssm_b_im,
           ssm_c_re, ssm_c_im, ssm_d, w_glu, attn_out_norm, ssm_out_norm, w_out, norm_ffn, w_router,
           router_bias, w_exp_gate, w_exp_up, w_exp_down, w_sh_gate, w_sh_up, w_sh_down, norm_ple,
           w_ple_gate, w_ple_proj):
    h = x
    for i in range(p.shape[0]):
        h = _layer(h, p[i], norm_mix[i], w_in[i], q_norm[i], k_norm[i], ssm_lam_re[i], ssm_lam_im[i],
                   ssm_log_dt[i], ssm_b_re[i], ssm_b_im[i], ssm_c_re[i], ssm_c_im[i], ssm_d[i], w_glu[i],
                   attn_out_norm[i], ssm_out_norm[i], w_out[i], norm_ffn[i], w_router[i], router_bias[i],
                   w_exp_gate[i], w_exp_up[i], w_exp_down[i], w_sh_gate[i], w_sh_up[i], w_sh_down[i],
                   norm_ple[i], w_ple_gate[i], w_ple_proj[i])
    return h
```

```python
import functools
import math

import jax
import jax.numpy as jnp
from jax import lax
from jax.experimental import pallas as pl
from jax.experimental.pallas import tpu as pltpu

NORM_EPS = 1e-6
TOP_K = 6
ROUTED_SCALE = 2.5
IDX_ROWS = 8
LANES = 128
S5_CHUNK = 8
VMEM_LIMIT = 56 * 1024 * 1024

F32 = jnp.float32
BF16 = jnp.bfloat16
HIGH_HALF = -65536


def _rms(x, g):
    return x * lax.rsqrt(jnp.mean(x * x, axis=-1, keepdims=True) + NORM_EPS) * g


def _sigmoid(x):
    return 1.0 / (1.0 + jnp.exp(-x))


def _params(sem):
    return pltpu.CompilerParams(dimension_semantics=sem, vmem_limit_bytes=VMEM_LIMIT)


def _pack_halves(x):
    n = x.shape[1] // 2
    lo = lax.bitcast_convert_type(x[:, :n].astype(BF16).astype(F32), jnp.int32)
    hi = lax.bitcast_convert_type(x[:, n:].astype(BF16).astype(F32), jnp.int32)
    return (hi & HIGH_HALF) | lax.shift_right_logical(lo, 16)


def _unpack_halves(p):
    lo = lax.bitcast_convert_type(lax.shift_left(p, 16), F32)
    hi = lax.bitcast_convert_type(p & HIGH_HALF, F32)
    return lo, hi


def _inproj_kernel(x_ref, g_ref, w_ref, hg_ref, o_ref, u_ref, xn_ref, *, n_norm_tiles, n_qkv_tiles, head_dim):
    j = pl.program_id(1)

    @pl.when(j == 0)
    def _():
        xn_ref[...] = _rms(x_ref[...], g_ref[...]).astype(BF16)

    acc = jnp.dot(xn_ref[...], w_ref[...], preferred_element_type=F32)
    tn = acc.shape[1]

    @pl.when(j < n_norm_tiles)
    def _():
        hg = hg_ref[...]
        for h in range(tn // head_dim):
            sl = slice(h * head_dim, (h + 1) * head_dim)
            o_ref[:, sl] = _rms(acc[:, sl], hg[:, sl]).astype(o_ref.dtype)

    @pl.when((j >= n_norm_tiles) & (j < n_qkv_tiles))
    def _():
        o_ref[...] = acc.astype(o_ref.dtype)

    @pl.when(j >= n_qkv_tiles)
    def _():
        u_ref[...] = acc


def _inproj(x2, g, w, hg, *, d_qkv, n_norm_tiles, head_dim, tm, tn):
    t, d = x2.shape
    n = w.shape[1]
    nq = d_qkv // tn
    return pl.pallas_call(
        functools.partial(_inproj_kernel, n_norm_tiles=n_norm_tiles, n_qkv_tiles=nq, head_dim=head_dim),
        grid=(t // tm, n // tn),
        in_specs=[
            pl.BlockSpec((tm, d), lambda i, j: (i, 0)),
            pl.BlockSpec((1, d), lambda i, j: (0, 0)),
            pl.BlockSpec((d, tn), lambda i, j: (0, j)),
            pl.BlockSpec((1, tn), lambda i, j: (0, j)),
        ],
        out_specs=[
            pl.BlockSpec((tm, tn), lambda i, j: (i, jnp.minimum(j, nq - 1))),
            pl.BlockSpec((tm, tn), lambda i, j: (i, jnp.maximum(j - nq, 0))),
        ],
        out_shape=[jax.ShapeDtypeStruct((t, d_qkv), BF16), jax.ShapeDtypeStruct((t, n - d_qkv), F32)],
        scratch_shapes=[pltpu.VMEM((tm, d), BF16)],
        compiler_params=_params(("parallel", "arbitrary")),
        name="inproj",
    )(x2, g, w, hg)


def _attn_kernel(q_ref, k_ref, v_ref, o_ref, *, blk):
    i = pl.program_id(2)
    q = q_ref[...]
    row = lax.broadcasted_iota(jnp.int32, (blk, blk), 0)
    col = lax.broadcasted_iota(jnp.int32, (blk, blk), 1)
    causal = col < row
    later = jnp.where(row > col, 1.0, 0.0).astype(BF16)

    def step(kb, carry, diag):
        acc, run = carry
        start = pl.multiple_of(kb * blk, blk)
        k = k_ref[pl.ds(start, blk), :]
        v = v_ref[pl.ds(start, blk), :]
        z = lax.dot_general(q, k, (((1,), (1,)), ((), ())), preferred_element_type=F32)
        sp = jnp.maximum(z, 0.0) + jnp.log(1.0 + jnp.exp(-jnp.abs(z)))
        if diag:
            sp = jnp.where(causal, sp, 0.0)
        hi = sp.astype(BF16)
        lo = (sp - hi.astype(F32)).astype(BF16)
        inner = (jnp.dot(hi, later, preferred_element_type=F32)
                 + jnp.dot(lo, later, preferred_element_type=F32))
        w = jnp.exp(z - sp - (inner + run))
        if diag:
            w = jnp.where(causal, w, 0.0)
        acc = acc + jnp.dot(w.astype(BF16), v, preferred_element_type=F32)
        run = run + jnp.sum(sp, axis=-1, keepdims=True)
        return acc, run

    init = (jnp.zeros(o_ref.shape, F32), jnp.zeros((blk, 1), F32))
    carry = step(i, init, True)
    acc, _ = lax.fori_loop(0, i, lambda n, c: step(i - 1 - n, c, False), carry)
    o_ref[...] = acc.astype(o_ref.dtype)


def _attention(z, *, batch, seq, n_heads, head_dim, blk):
    nq = seq // blk
    return pl.pallas_call(
        functools.partial(_attn_kernel, blk=blk),
        grid=(batch, n_heads, nq),
        in_specs=[
            pl.BlockSpec((blk, head_dim), lambda b, h, i: (b * nq + i, h)),
            pl.BlockSpec((seq, head_dim), lambda b, h, i: (b, n_heads + h)),
            pl.BlockSpec((seq, head_dim), lambda b, h, i: (b, 2 * n_heads + h)),
        ],
        out_specs=pl.BlockSpec((blk, head_dim), lambda b, h, i: (b * nq + i, h)),
        out_shape=jax.ShapeDtypeStruct((batch * seq, n_heads * head_dim), BF16),
        compiler_params=_params(("parallel", "parallel", "arbitrary")),
        name="sb_attention",
    )(z, z, z)


def _s5_prepare(lam_re, lam_im, log_dt, b_re, b_im, c_re, c_im, d_skip, n_chunks):
    hp = lax.Precision.HIGHEST
    chunk = S5_CHUNK
    g, p = lam_re.shape
    h = b_re.shape[-1]
    gg = LANES // h
    o = g // gg
    dt = jnp.exp(log_dt)[:, None]
    ks = jnp.arange(chunk + 1, dtype=F32)[None, :, None]
    mag = jnp.exp(lam_re[:, None, :] * dt[:, None, :] * ks)
    ang = lam_im[:, None, :] * dt[:, None, :] * ks
    ak_re = mag * jnp.cos(ang)
    ak_im = mag * jnp.sin(ang)
    nr = ak_re[:, 1] - 1.0
    ni = ak_im[:, 1]
    den = lam_re * lam_re + lam_im * lam_im
    coef_re = ((nr * lam_re + ni * lam_im) / den)[..., None]
    coef_im = ((ni * lam_re - nr * lam_im) / den)[..., None]
    bbar_re = coef_re * b_re - coef_im * b_im
    bbar_im = coef_re * b_im + coef_im * b_re
    ca_re = c_re[:, None] * ak_re[:, :, None, :] - c_im[:, None] * ak_im[:, :, None, :]
    ca_im = c_re[:, None] * ak_im[:, :, None, :] + c_im[:, None] * ak_re[:, :, None, :]
    eye_g = jnp.eye(gg, dtype=F32)

    cam_re = ca_re[:, :chunk].transpose(0, 3, 1, 2).reshape(g, p, chunk * h)
    cam_im = ca_im[:, :chunk].transpose(0, 3, 1, 2).reshape(g, p, chunk * h)
    kt = (jnp.einsum("gph,gpm->ghm", bbar_re, cam_re, precision=hp)
          - jnp.einsum("gph,gpm->ghm", bbar_im, cam_im, precision=hp))
    kt = kt.reshape(o, gg, h, chunk, h)
    bd = kt[:, :, :, :, None, :] * eye_g[None, :, None, None, :, None]
    bd = bd.transpose(0, 3, 1, 2, 4, 5).reshape(o, chunk, LANES, LANES)
    dvec = d_skip.reshape(o, LANES)
    lags = [bd[:, k] for k in range(chunk)]
    lags[0] = lags[0] + dvec[:, :, None] * jnp.eye(LANES, dtype=F32)[None]
    zero_blk = jnp.zeros_like(lags[0])
    m_intra = jnp.concatenate(
        [jnp.concatenate([lags[j - i] if j >= i else zero_blk for j in range(chunk)], axis=2)
         for i in range(chunk)], axis=1)

    def expand_in(ak_part_a, ak_part_b, b_a, b_b, sign):
        rk_a = jnp.stack([ak_part_a[:, chunk - 1 - i] for i in range(chunk)], axis=1)
        rk_b = jnp.stack([ak_part_b[:, chunk - 1 - i] for i in range(chunk)], axis=1)
        val = (rk_a[:, :, None, :] * b_a.transpose(0, 2, 1)[:, None]
               + sign * rk_b[:, :, None, :] * b_b.transpose(0, 2, 1)[:, None])
        val = val.reshape(o, gg, chunk, h, p).transpose(0, 2, 1, 3, 4)
        val = val[:, :, :, :, None, :] * eye_g[None, None, :, None, :, None]
        return val.reshape(o, chunk * LANES, gg * p)

    m_in = jnp.concatenate([expand_in(ak_re, ak_im, bbar_re, bbar_im, -1.0),
                            expand_in(ak_re, ak_im, bbar_im, bbar_re, 1.0)], axis=-1)

    def expand_out(ca):
        val = ca[:, 1:].reshape(o, gg, chunk, h, p).transpose(0, 4, 2, 1, 3)
        val = val[:, None] * eye_g[None, :, None, None, :, None]
        return val.reshape(o, gg * p, chunk * LANES)

    m_out = jnp.concatenate([expand_out(ca_re), -expand_out(ca_im)], axis=1)

    steps = max(1, int(math.ceil(math.log2(max(n_chunks, 2)))))
    cr, ci = ak_re[:, chunk].reshape(o, gg * p), ak_im[:, chunk].reshape(o, gg * p)
    sc_a, sc_b = [], []
    for _ in range(steps):
        sc_a.append(jnp.concatenate([cr, cr], axis=-1))
        sc_b.append(jnp.concatenate([-ci, ci], axis=-1))
        cr, ci = cr * cr - ci * ci, 2.0 * cr * ci
    return (m_intra.astype(BF16), m_in.astype(BF16), m_out.astype(BF16),
            jnp.stack(sc_a, axis=1), jnp.stack(sc_b, axis=1))


def _s5_kernel(u_ref, mi_ref, min_ref, mout_ref, sa_ref, sb_ref, y_ref, *, n_chunks, steps):
    chunk = S5_CHUNK
    n = u_ref.shape[0] // chunk
    u = jnp.concatenate([u_ref[pl.ds(i, n, stride=chunk), :] for i in range(chunk)], axis=1).astype(BF16)
    y = jnp.dot(u, mi_ref[...], preferred_element_type=F32)
    x = jnp.dot(u, min_ref[...], preferred_element_type=F32)
    half = x.shape[1] // 2
    c = lax.rem(lax.broadcasted_iota(jnp.int32, x.shape, 0), n_chunks)
    sa = sa_ref[...]
    sb = sb_ref[...]
    for k in range(steps):
        sh = 1 << k
        xs = jnp.where(c >= sh, pltpu.roll(x, sh, axis=0), 0.0)
        x = x + xs * sa[k:k + 1, :] + pltpu.roll(xs, half, axis=1) * sb[k:k + 1, :]
    s_in = jnp.where(c >= 1, pltpu.roll(x, 1, axis=0), 0.0)
    y = y + jnp.dot(s_in.astype(BF16), mout_ref[...], preferred_element_type=F32)
    for i in range(chunk):
        y_ref[pl.ds(i, n, stride=chunk), :] = y[:, i * LANES:(i + 1) * LANES]


def _s5(u, m_intra, m_in, m_out, sc_a, sc_b, *, n_chunks):
    t, d_ssm = u.shape
    o, lh, st = m_in.shape
    steps = sc_a.shape[1]
    mat = lambda i: (i, 0, 0)
    return pl.pallas_call(
        functools.partial(_s5_kernel, n_chunks=n_chunks, steps=steps),
        grid=(o,),
        in_specs=[
            pl.BlockSpec((t, LANES), lambda i: (0, i)),
            pl.BlockSpec((None, lh, lh), mat), pl.BlockSpec((None, lh, st), mat), pl.BlockSpec((None, st, lh), mat),
            pl.BlockSpec((None, steps, st), mat), pl.BlockSpec((None, steps, st), mat),
        ],
        out_specs=pl.BlockSpec((t, LANES), lambda i: (0, i)),
        out_shape=jax.ShapeDtypeStruct((t, d_ssm), F32),
        compiler_params=_params(("parallel",)),
        name="s5_chunked",
    )(u, m_intra, m_in, m_out, sc_a, sc_b)


def _postmix_kernel(oa_ref, ys_ref, x_ref, wglu_ref, woa_ref, wob_ref, ga_ref, gs_ref, gf_ref,
                    wr_ref, rb_ref, h1_ref, idx_ref, gate_ref, loc_ref, cnt_ref, run_ref, *, n_experts):
    step = pl.program_id(0)

    @pl.when(step == 0)
    def _():
        run_ref[...] = jnp.zeros(run_ref.shape, F32)

    y = ys_ref[...]
    y = 0.5 * y * (1.0 + jnp.tanh(math.sqrt(2.0 / math.pi) * (y + 0.044715 * (y * y * y))))
    y = y * _sigmoid(jnp.dot(y.astype(BF16), wglu_ref[...], preferred_element_type=F32))
    na = _rms(oa_ref[...].astype(F32), ga_ref[...]).astype(BF16)
    ns = _rms(y, gs_ref[...]).astype(BF16)
    h1 = (x_ref[...] + jnp.dot(na, woa_ref[...], preferred_element_type=F32)
          + jnp.dot(ns, wob_ref[...], preferred_element_type=F32))
    h1_ref[...] = h1

    m = _rms(h1, gf_ref[...])
    logits = jnp.dot(m, wr_ref[...], preferred_element_type=F32, precision=lax.Precision.HIGHEST)
    scores = _sigmoid(logits)
    vals = scores + rb_ref[...]
    tm = vals.shape[0]
    lane = lax.broadcasted_iota(jnp.int32, vals.shape, 1).astype(F32)
    picks, firsts, raws = [], [], []
    sel = jnp.zeros(vals.shape, F32)
    for _ in range(TOP_K):
        mx = jnp.max(vals, axis=1, keepdims=True)
        first = jnp.min(jnp.where(vals == mx, lane, float(n_experts)), axis=1, keepdims=True)
        pick = lane == first
        picks.append(pick)
        firsts.append(first)
        raws.append(jnp.sum(jnp.where(pick, scores, 0.0), axis=1, keepdims=True))
        vals = jnp.where(pick, -jnp.inf, vals)
        sel = jnp.where(pick, 1.0, sel)
    denom = raws[0]
    for r in raws[1:]:
        denom = denom + r

    r_i = lax.broadcasted_iota(jnp.int32, (tm, tm), 0)
    c_i = lax.broadcasted_iota(jnp.int32, (tm, tm), 1)
    before = jnp.where(c_i < r_i, 1.0, 0.0).astype(BF16)
    rank = jnp.dot(before, sel.astype(BF16), preferred_element_type=F32) + run_ref[...]
    run_ref[...] = run_ref[...] + jnp.sum(sel, axis=0, keepdims=True)
    cnt_ref[...] = run_ref[...]

    slot = lax.broadcasted_iota(jnp.int32, (tm, LANES), 1)
    idx_w = jnp.zeros((tm, LANES), F32)
    loc_w = jnp.zeros((tm, LANES), F32)
    gate_w = jnp.zeros((tm, LANES), F32)
    for k in range(TOP_K):
        loc = jnp.sum(jnp.where(picks[k], rank, 0.0), axis=1, keepdims=True)
        idx_w = jnp.where(slot == k, firsts[k], idx_w)
        loc_w = jnp.where(slot == k, loc, loc_w)
        gate_w = jnp.where(slot == k, raws[k] / denom * ROUTED_SCALE, gate_w)
    idx_ref[...] = idx_w.T[:IDX_ROWS].astype(jnp.int32)
    loc_ref[...] = loc_w.T[:IDX_ROWS].astype(jnp.int32)
    gate_ref[...] = gate_w[:, :IDX_ROWS]


def _postmix(oa, ys, x2, wglu, woa, wob, ga, gs, gf, wr, rb, *, tm):
    t, d = x2.shape
    da = oa.shape[1]
    dsm = ys.shape[1]
    e = wr.shape[1]
    row = lambda i: (i, 0)
    col = lambda i: (0, i)
    fix = lambda i: (0, 0)
    return pl.pallas_call(
        functools.partial(_postmix_kernel, n_experts=e),
        grid=(t // tm,),
        in_specs=[
            pl.BlockSpec((tm, da), row), pl.BlockSpec((tm, dsm), row), pl.BlockSpec((tm, d), row),
            pl.BlockSpec((dsm, dsm), fix), pl.BlockSpec((da, d), fix), pl.BlockSpec((dsm, d), fix),
            pl.BlockSpec((1, da), fix), pl.BlockSpec((1, dsm), fix), pl.BlockSpec((1, d), fix),
            pl.BlockSpec((d, e), fix), pl.BlockSpec((1, e), fix),
        ],
        out_specs=[
            pl.BlockSpec((tm, d), row), pl.BlockSpec((IDX_ROWS, tm), col), pl.BlockSpec((tm, IDX_ROWS), row),
            pl.BlockSpec((IDX_ROWS, tm), col), pl.BlockSpec((1, e), fix),
        ],
        out_shape=[
            jax.ShapeDtypeStruct((t, d), F32), jax.ShapeDtypeStruct((IDX_ROWS, t), jnp.int32),
            jax.ShapeDtypeStruct((t, IDX_ROWS), F32), jax.ShapeDtypeStruct((IDX_ROWS, t), jnp.int32),
            jax.ShapeDtypeStruct((1, e), F32),
        ],
        scratch_shapes=[pltpu.VMEM((1, e), F32)],
        compiler_params=_params(("arbitrary",)),
        name="postmix_router",
    )(oa, ys, x2, wglu, woa, wob, ga, gs, gf, wr, rb)


def _row_copy(src, s, dst, d, sem):
    return pltpu.make_async_copy(src.at[pl.ds(s, 1)], dst.at[pl.ds(d, 1)], sem)


def _dispatch_kernel(dest_ref, zflag_ref, h1_ref, gf_ref, wg_ref, wu_ref, wd_ref, h1s_ref, xs_ref,
                     m_ref, zero_ref, sem, zsem):
    tm = h1_ref.shape[0]
    rb = zero_ref.shape[0]
    n_blocks = zflag_ref.shape[0]

    @pl.when(pl.program_id(0) == 0)
    def _():
        zero_ref[...] = jnp.zeros(zero_ref.shape, zero_ref.dtype)

        def zcopy(b):
            return pltpu.make_async_copy(zero_ref, xs_ref.at[pl.ds(pl.multiple_of(b * rb, rb), rb)], zsem)

        def zstart(b, carry):
            @pl.when(zflag_ref[b] != 0)
            def _():
                zcopy(b).start()
            return carry

        def zwait(b, carry):
            @pl.when(zflag_ref[b] != 0)
            def _():
                zcopy(b).wait()
            return carry

        lax.fori_loop(0, n_blocks, zstart, 0)
        lax.fori_loop(0, n_blocks, zwait, 0)

    h1 = h1_ref[...]
    m = _rms(h1, gf_ref[...])
    m_ref[...] = _pack_halves(m)

    def issue(t, carry):
        for k in range(TOP_K):
            _row_copy(m_ref, t, xs_ref, dest_ref[k, t], sem).start()
        return carry

    lax.fori_loop(0, tm, issue, 0)
    mb = m.astype(BF16)
    a = jnp.dot(mb, wg_ref[...], preferred_element_type=F32)
    b = jnp.dot(mb, wu_ref[...], preferred_element_type=F32)
    hid = (a * _sigmoid(a) * b).astype(BF16)
    h1s_ref[...] = h1 + jnp.dot(hid, wd_ref[...], preferred_element_type=F32)
    for _ in range(TOP_K):
        pltpu.make_async_copy(m_ref, xs_ref.at[pl.ds(0, tm)], sem).wait()


def _dispatch(dest, zflag, h1, gf, wg, wu, wd, *, tm, rb, n_pad):
    t, d = h1.shape
    f = wg.shape[1]
    row = lambda i: (i, 0)
    fix = lambda i: (0, 0)
    return pl.pallas_call(
        _dispatch_kernel,
        grid=(t // tm,),
        in_specs=[
            pl.BlockSpec((IDX_ROWS, tm), lambda i: (0, i), memory_space=pltpu.SMEM),
            pl.BlockSpec(memory_space=pltpu.SMEM),
            pl.BlockSpec((tm, d), row), pl.BlockSpec((1, d), fix),
            pl.BlockSpec((d, f), fix), pl.BlockSpec((d, f), fix), pl.BlockSpec((f, d), fix),
        ],
        out_specs=[pl.BlockSpec((tm, d), row), pl.BlockSpec(memory_space=pl.ANY)],
        out_shape=[jax.ShapeDtypeStruct((t, d), F32), jax.ShapeDtypeStruct((n_pad, d // 2), jnp.int32)],
        scratch_shapes=[pltpu.VMEM((tm, d // 2), jnp.int32), pltpu.VMEM((rb, d // 2), jnp.int32),
                        pltpu.SemaphoreType.DMA(()), pltpu.SemaphoreType.DMA(())],
        compiler_params=_params(("arbitrary",)),
        name="dispatch_shared",
    )(dest, zflag, h1, gf, wg, wu, wd)


def _experts_kernel(be_ref, nu_ref, xs_ref, wg_ref, wu_ref, wd_ref, ys_ref, wgb_ref, wub_ref, wdb_ref):
    i = pl.program_id(0)
    e = be_ref[i]
    prev = be_ref[jnp.maximum(i - 1, 0)]

    @pl.when((i == 0) | (e != prev))
    def _():
        wgb_ref[...] = wg_ref[...].astype(BF16)
        wub_ref[...] = wu_ref[...].astype(BF16)
        wdb_ref[...] = wd_ref[...].astype(BF16)

    @pl.when(i < nu_ref[0])
    def _():
        lo, hi = _unpack_halves(xs_ref[...])
        lo = lo.astype(BF16)
        hi = hi.astype(BF16)
        n = lo.shape[1]
        a = (jnp.dot(lo, wgb_ref[:n, :], preferred_element_type=F32)
             + jnp.dot(hi, wgb_ref[n:, :], preferred_element_type=F32))
        b = (jnp.dot(lo, wub_ref[:n, :], preferred_element_type=F32)
             + jnp.dot(hi, wub_ref[n:, :], preferred_element_type=F32))
        hid = (a * _sigmoid(a) * b).astype(BF16)
        ys_ref[...] = _pack_halves(jnp.dot(hid, wdb_ref[...], preferred_element_type=F32))

    @pl.when(i >= nu_ref[0])
    def _():
        ys_ref[...] = jnp.zeros(ys_ref.shape, ys_ref.dtype)


def _experts(blk_e, n_used, xs, wg, wu, wd, *, tm):
    n_pad, dh = xs.shape
    _, d, f = wg.shape
    rows = lambda i, be, nu: (jnp.minimum(i, nu[0] - 1), 0)
    orow = lambda i, be, nu: (i, 0)
    wsel = lambda i, be, nu: (be[i], 0, 0)
    return pl.pallas_call(
        _experts_kernel,
        grid_spec=pltpu.PrefetchScalarGridSpec(
            num_scalar_prefetch=2,
            grid=(n_pad // tm,),
            in_specs=[
                pl.BlockSpec((tm, dh), rows),
                pl.BlockSpec((None, d, f), wsel), pl.BlockSpec((None, d, f), wsel),
                pl.BlockSpec((None, f, d), wsel),
            ],
            out_specs=pl.BlockSpec((tm, dh), orow),
            scratch_shapes=[pltpu.VMEM((d, f), BF16), pltpu.VMEM((d, f), BF16), pltpu.VMEM((f, d), BF16)],
        ),
        out_shape=jax.ShapeDtypeStruct((n_pad, dh), jnp.int32),
        compiler_params=_params(("arbitrary",)),
        name="routed_experts",
    )(blk_e, n_used, xs, wg, wu, wd)


def _final_kernel(dest_ref, h_ref, gate_ref, p_ref, gp_ref, wgate_ref, wproj_ref, ys_ref, o_ref, buf_ref, sem):
    tm = h_ref.shape[0]

    def issue(t, carry):
        for k in range(TOP_K):
            _row_copy(ys_ref, dest_ref[k, t], buf_ref.at[k], t, sem).start()
        return carry

    lax.fori_loop(0, tm, issue, 0)
    pp = jnp.dot(p_ref[...].astype(BF16), wproj_ref[...], preferred_element_type=F32)
    for k in range(TOP_K):
        pltpu.make_async_copy(ys_ref.at[pl.ds(0, tm)], buf_ref.at[k], sem).wait()
    gate = gate_ref[...]
    h = h_ref[...]
    n = h.shape[1] // 2
    acc_lo = h[:, :n]
    acc_hi = h[:, n:]
    for k in range(TOP_K):
        lo, hi = _unpack_halves(buf_ref[k])
        acc_lo = acc_lo + gate[:, k:k + 1] * lo
        acc_hi = acc_hi + gate[:, k:k + 1] * hi
    h2 = jnp.concatenate([acc_lo, acc_hi], axis=1)
    nrm = _rms(h2, gp_ref[...]).astype(BF16)
    o_ref[...] = h2 + _sigmoid(jnp.dot(nrm, wgate_ref[...], preferred_element_type=F32)) * pp


def _final(dest, h1s, gate, p2, gp, wgate, wproj, ys, *, tm):
    t, d = h1s.shape
    dp = p2.shape[1]
    row = lambda i: (i, 0)
    fix = lambda i: (0, 0)
    return pl.pallas_call(
        _final_kernel,
        grid=(t // tm,),
        in_specs=[
            pl.BlockSpec((IDX_ROWS, tm), lambda i: (0, i), memory_space=pltpu.SMEM),
            pl.BlockSpec((tm, d), row), pl.BlockSpec((tm, IDX_ROWS), row), pl.BlockSpec((tm, dp), row),
            pl.BlockSpec((1, d), fix), pl.BlockSpec((d, d), fix), pl.BlockSpec((dp, d), fix),
            pl.BlockSpec(memory_space=pl.ANY),
        ],
        out_specs=pl.BlockSpec((tm, d), row),
        out_shape=jax.ShapeDtypeStruct((t, d), F32),
        scratch_shapes=[pltpu.VMEM((TOP_K, tm, d // 2), jnp.int32), pltpu.SemaphoreType.DMA(())],
        compiler_params=_params(("arbitrary",)),
        name="combine_ple",
    )(dest, h1s, gate, p2, gp, wgate, wproj, ys)


def _largest_tile(n, cap):
    t = min(n, cap)
    while n % t:
        t //= 2
    return t


def _layer(h, p_l, norm_mix, w_in, q_norm, k_norm, lam_re, lam_im, log_dt, b_re, b_im, c_re, c_im, d_skip,
           w_glu, attn_out_norm, ssm_out_norm, w_out, norm_ffn, w_router, router_bias, w_exp_gate,
           w_exp_up, w_exp_down, w_sh_gate, w_sh_up, w_sh_down, norm_ple, w_ple_gate, w_ple_proj):
    bsz, seq, d = h.shape
    t = bsz * seq
    head_dim = q_norm.shape[-1]
    d_attn = attn_out_norm.shape[-1]
    n_heads = d_attn // head_dim
    n_experts = w_router.shape[-1]
    x2 = h.reshape(t, d)

    tn = d_attn if d_attn % 128 == 0 else head_dim
    scale = 1.0 / math.sqrt(head_dim)
    hg = jnp.concatenate([jnp.tile(q_norm * scale, n_heads), jnp.tile(k_norm, n_heads),
                          jnp.ones((w_in.shape[1] - 2 * d_attn,), F32)])[None, :]
    z, u = _inproj(x2, norm_mix[None, :], w_in.astype(BF16), hg, d_qkv=3 * d_attn,
                   n_norm_tiles=2 * d_attn // tn, head_dim=head_dim, tm=_largest_tile(t, 1024), tn=tn)

    o_attn = _attention(z, batch=bsz, seq=seq, n_heads=n_heads, head_dim=head_dim, blk=_largest_tile(seq, 256))

    n_chunks = seq // S5_CHUNK
    mats = _s5_prepare(lam_re, lam_im, log_dt, b_re, b_im, c_re, c_im, d_skip, n_chunks)
    y_ssm = _s5(u, *mats, n_chunks=n_chunks)

    tm = _largest_tile(t, 256)
    w_out_b = w_out.astype(BF16)
    h1, idx, gate, loc, counts = _postmix(
        o_attn, y_ssm, x2, w_glu.astype(BF16), w_out_b[:d_attn], w_out_b[d_attn:],
        attn_out_norm[None, :], ssm_out_norm[None, :], norm_ffn[None, :], w_router, router_bias[None, :], tm=tm)

    rb = _largest_tile(t, 256)
    n_pad = t * TOP_K + n_experts * rb
    n_blocks = n_pad // rb
    cnt = counts[0].astype(jnp.int32)
    pcnt = (cnt + rb - 1) // rb * rb
    pend = jnp.cumsum(pcnt)
    pstart = pend - pcnt
    dest = pstart[idx] + loc
    n_used = (pend[-1] // rb).astype(jnp.int32)
    blk_ids = jnp.arange(n_blocks, dtype=jnp.int32)
    blk_e = jnp.minimum(jnp.searchsorted(pend, blk_ids * rb, side="right"), n_experts - 1).astype(jnp.int32)
    blk_e = jnp.where(blk_ids < n_used, blk_e, blk_e[n_used - 1])
    last_blk = jnp.where(pcnt > 0, pend // rb - 1, n_blocks)
    zflag = ((blk_ids >= n_used) | jnp.any(blk_ids[:, None] == last_blk[None, :], axis=1)).astype(jnp.int32)

    h1s, xs = _dispatch(dest, zflag, h1, norm_ffn[None, :], w_sh_gate.astype(BF16), w_sh_up.astype(BF16),
                        w_sh_down.astype(BF16), tm=tm, rb=rb, n_pad=n_pad)
    ys = _experts(blk_e, n_used[None], xs, w_exp_gate, w_exp_up, w_exp_down, tm=rb)
    out = _final(dest, h1s, gate, p_l.reshape(t, -1), norm_ple[None, :], w_ple_gate.astype(BF16),
                 w_ple_proj.astype(BF16), ys, tm=tm)
    return out.reshape(bsz, seq, d)


def kernel(x, p, norm_mix, w_in, q_norm, k_norm, ssm_lam_re, ssm_lam_im, ssm_log_dt, ssm_b_re, ssm_b_im,
           ssm_c_re, ssm_c_im, ssm_d, w_glu, attn_out_norm, ssm_out_norm, w_out, norm_ffn, w_router,
           router_bias, w_exp_gate, w_exp_up, w_exp_down, w_sh_gate, w_sh_up, w_sh_down, norm_ple,
           w_ple_gate, w_ple_proj):
    h = x
    for i in range(p.shape[0]):
        h = _layer(h, p[i], norm_mix[i], w_in[i], q_norm[i], k_norm[i], ssm_lam_re[i], ssm_lam_im[i],
                   ssm_log_dt[i], ssm_b_re[i], ssm_b_im[i], ssm_c_re[i], ssm_c_im[i], ssm_d[i], w_glu[i],
                   attn_out_norm[i], ssm_out_norm[i], w_out[i], norm_ffn[i], w_router[i], router_bias[i],
                   w_exp_gate[i], w_exp_up[i], w_exp_down[i], w_sh_gate[i], w_sh_up[i], w_sh_down[i],
                   norm_ple[i], w_ple_gate[i], w_ple_proj[i])
    return h
```

```python
import functools
import math

import jax
import jax.numpy as jnp
from jax import lax
from jax.experimental import pallas as pl
from jax.experimental.pallas import tpu as pltpu

NORM_EPS = 1e-6
TOP_K = 6
ROUTED_SCALE = 2.5
IDX_ROWS = 8
LANES = 128
S5_CHUNK = 8
VMEM_LIMIT = 56 * 1024 * 1024

F32 = jnp.float32
BF16 = jnp.bfloat16
HIGH_HALF = -65536


def _rms(x, g):
    return x * lax.rsqrt(jnp.mean(x * x, axis=-1, keepdims=True) + NORM_EPS) * g


def _sigmoid(x):
    return 1.0 / (1.0 + jnp.exp(-x))


def _params(sem):
    return pltpu.CompilerParams(dimension_semantics=sem, vmem_limit_bytes=VMEM_LIMIT)


def _pack_halves(x):
    n = x.shape[1] // 2
    lo = lax.bitcast_convert_type(x[:, :n].astype(BF16).astype(F32), jnp.int32)
    hi = lax.bitcast_convert_type(x[:, n:].astype(BF16).astype(F32), jnp.int32)
    return (hi & HIGH_HALF) | lax.shift_right_logical(lo, 16)


def _unpack_halves(p):
    lo = lax.bitcast_convert_type(lax.shift_left(p, 16), F32)
    hi = lax.bitcast_convert_type(p & HIGH_HALF, F32)
    return lo, hi


def _inproj_kernel(x_ref, g_ref, w_ref, hg_ref, o_ref, u_ref, xn_ref, *, n_norm_tiles, n_qkv_tiles, head_dim):
    j = pl.program_id(1)

    @pl.when(j == 0)
    def _():
        xn_ref[...] = _rms(x_ref[...], g_ref[...]).astype(BF16)

    acc = jnp.dot(xn_ref[...], w_ref[...], preferred_element_type=F32)
    tn = acc.shape[1]

    @pl.when(j < n_norm_tiles)
    def _():
        hg = hg_ref[...]
        for h in range(tn // head_dim):
            sl = slice(h * head_dim, (h + 1) * head_dim)
            o_ref[:, sl] = _rms(acc[:, sl], hg[:, sl]).astype(o_ref.dtype)

    @pl.when((j >= n_norm_tiles) & (j < n_qkv_tiles))
    def _():
        o_ref[...] = acc.astype(o_ref.dtype)

    @pl.when(j >= n_qkv_tiles)
    def _():
        u_ref[...] = acc


def _inproj(x2, g, w, hg, *, d_qkv, n_norm_tiles, head_dim, tm, tn):
    t, d = x2.shape
    n = w.shape[1]
    nq = d_qkv // tn
    return pl.pallas_call(
        functools.partial(_inproj_kernel, n_norm_tiles=n_norm_tiles, n_qkv_tiles=nq, head_dim=head_dim),
        grid=(t // tm, n // tn),
        in_specs=[
            pl.BlockSpec((tm, d), lambda i, j: (i, 0)),
            pl.BlockSpec((1, d), lambda i, j: (0, 0)),
            pl.BlockSpec((d, tn), lambda i, j: (0, j)),
            pl.BlockSpec((1, tn), lambda i, j: (0, j)),
        ],
        out_specs=[
            pl.BlockSpec((tm, tn), lambda i, j: (i, jnp.minimum(j, nq - 1))),
            pl.BlockSpec((tm, tn), lambda i, j: (i, jnp.maximum(j - nq, 0))),
        ],
        out_shape=[jax.ShapeDtypeStruct((t, d_qkv), BF16), jax.ShapeDtypeStruct((t, n - d_qkv), F32)],
        scratch_shapes=[pltpu.VMEM((tm, d), BF16)],
        compiler_params=_params(("parallel", "arbitrary")),
        name="inproj",
    )(x2, g, w, hg)


def _attn_kernel(q_ref, k_ref, v_ref, o_ref, *, blk):
    i = pl.program_id(2)
    q = q_ref[...]
    row = lax.broadcasted_iota(jnp.int32, (blk, blk), 0)
    col = lax.broadcasted_iota(jnp.int32, (blk, blk), 1)
    causal = col < row
    later = jnp.where(row > col, 1.0, 0.0).astype(BF16)

    def step(kb, carry, diag):
        acc, run = carry
        start = pl.multiple_of(kb * blk, blk)
        k = k_ref[pl.ds(start, blk), :]
        v = v_ref[pl.ds(start, blk), :]
        z = lax.dot_general(q, k, (((1,), (1,)), ((), ())), preferred_element_type=F32)
        sp = jnp.maximum(z, 0.0) + jnp.log(1.0 + jnp.exp(-jnp.abs(z)))
        if diag:
            sp = jnp.where(causal, sp, 0.0)
        hi = sp.astype(BF16)
        lo = (sp - hi.astype(F32)).astype(BF16)
        inner = (jnp.dot(hi, later, preferred_element_type=F32)
                 + jnp.dot(lo, later, preferred_element_type=F32))
        w = jnp.exp(z - sp - (inner + run))
        if diag:
            w = jnp.where(causal, w, 0.0)
        acc = acc + jnp.dot(w.astype(BF16), v, preferred_element_type=F32)
        run = run + jnp.sum(sp, axis=-1, keepdims=True)
        return acc, run

    init = (jnp.zeros(o_ref.shape, F32), jnp.zeros((blk, 1), F32))
    carry = step(i, init, True)
    acc, _ = lax.fori_loop(0, i, lambda n, c: step(i - 1 - n, c, False), carry)
    o_ref[...] = acc.astype(o_ref.dtype)


def _attention(z, *, batch, seq, n_heads, head_dim, blk):
    nq = seq // blk
    return pl.pallas_call(
        functools.partial(_attn_kernel, blk=blk),
        grid=(batch, n_heads, nq),
        in_specs=[
            pl.BlockSpec((blk, head_dim), lambda b, h, i: (b * nq + i, h)),
            pl.BlockSpec((seq, head_dim), lambda b, h, i: (b, n_heads + h)),
            pl.BlockSpec((seq, head_dim), lambda b, h, i: (b, 2 * n_heads + h)),
        ],
        out_specs=pl.BlockSpec((blk, head_dim), lambda b, h, i: (b * nq + i, h)),
        out_shape=jax.ShapeDtypeStruct((batch * seq, n_heads * head_dim), BF16),
        compiler_params=_params(("parallel", "parallel", "arbitrary")),
        name="sb_attention",
    )(z, z, z)


def _s5_prepare(lam_re, lam_im, log_dt, b_re, b_im, c_re, c_im, d_skip, n_chunks):
    hp = lax.Precision.HIGHEST
    chunk = S5_CHUNK
    g, p = lam_re.shape
    h = b_re.shape[-1]
    gg = LANES // h
    o = g // gg
    dt = jnp.exp(log_dt)[:, None]
    ks = jnp.arange(chunk + 1, dtype=F32)[None, :, None]
    mag = jnp.exp(lam_re[:, None, :] * dt[:, None, :] * ks)
    ang = lam_im[:, None, :] * dt[:, None, :] * ks
    ak_re = mag * jnp.cos(ang)
    ak_im = mag * jnp.sin(ang)
    nr = ak_re[:, 1] - 1.0
    ni = ak_im[:, 1]
    den = lam_re * lam_re + lam_im * lam_im
    coef_re = ((nr * lam_re + ni * lam_im) / den)[..., None]
    coef_im = ((ni * lam_re - nr * lam_im) / den)[..., None]
    bbar_re = coef_re * b_re - coef_im * b_im
    bbar_im = coef_re * b_im + coef_im * b_re
    ca_re = c_re[:, None] * ak_re[:, :, None, :] - c_im[:, None] * ak_im[:, :, None, :]
    ca_im = c_re[:, None] * ak_im[:, :, None, :] + c_im[:, None] * ak_re[:, :, None, :]
    eye_g = jnp.eye(gg, dtype=F32)

    cam_re = ca_re[:, :chunk].transpose(0, 3, 1, 2).reshape(g, p, chunk * h)
    cam_im = ca_im[:, :chunk].transpose(0, 3, 1, 2).reshape(g, p, chunk * h)
    kt = (jnp.einsum("gph,gpm->ghm", bbar_re, cam_re, precision=hp)
          - jnp.einsum("gph,gpm->ghm", bbar_im, cam_im, precision=hp))
    kt = kt.reshape(o, gg, h, chunk, h)
    bd = kt[:, :, :, :, None, :] * eye_g[None, :, None, None, :, None]
    bd = bd.transpose(0, 3, 1, 2, 4, 5).reshape(o, chunk, LANES, LANES)
    dvec = d_skip.reshape(o, LANES)
    lags = [bd[:, k] for k in range(chunk)]
    lags[0] = lags[0] + dvec[:, :, None] * jnp.eye(LANES, dtype=F32)[None]
    zero_blk = jnp.zeros_like(lags[0])
    m_intra = jnp.concatenate(
        [jnp.concatenate([lags[j - i] if j >= i else zero_blk for j in range(chunk)], axis=2)
         for i in range(chunk)], axis=1)

    def expand_in(ak_part_a, ak_part_b, b_a, b_b, sign):
        rk_a = jnp.stack([ak_part_a[:, chunk - 1 - i] for i in range(chunk)], axis=1)
        rk_b = jnp.stack([ak_part_b[:, chunk - 1 - i] for i in range(chunk)], axis=1)
        val = (rk_a[:, :, None, :] * b_a.transpose(0, 2, 1)[:, None]
               + sign * rk_b[:, :, None, :] * b_b.transpose(0, 2, 1)[:, None])
        val = val.reshape(o, gg, chunk, h, p).transpose(0, 2, 1, 3, 4)
        val = val[:, :, :, :, None, :] * eye_g[None, None, :, None, :, None]
        return val.reshape(o, chunk * LANES, gg * p)

    m_in = jnp.concatenate([expand_in(ak_re, ak_im, bbar_re, bbar_im, -1.0),
                            expand_in(ak_re, ak_im, bbar_im, bbar_re, 1.0)], axis=-1)

    def expand_out(ca):
        val = ca[:, 1:].reshape(o, gg, chunk, h, p).transpose(0, 4, 2, 1, 3)
        val = val[:, None] * eye_g[None, :, None, None, :, None]
        return val.reshape(o, gg * p, chunk * LANES)

    m_out = jnp.concatenate([expand_out(ca_re), -expand_out(ca_im)], axis=1)

    steps = max(1, int(math.ceil(math.log2(max(n_chunks, 2)))))
    cr, ci = ak_re[:, chunk].reshape(o, gg * p), ak_im[:, chunk].reshape(o, gg * p)
    sc_a, sc_b = [], []
    for _ in range(steps):
        sc_a.append(jnp.concatenate([cr, cr], axis=-1))
        sc_b.append(jnp.concatenate([-ci, ci], axis=-1))
        cr, ci = cr * cr - ci * ci, 2.0 * cr * ci
    return (m_intra.astype(BF16), m_in.astype(BF16), m_out.astype(BF16),
            jnp.stack(sc_a, axis=1), jnp.stack(sc_b, axis=1))


def _s5_kernel(u_ref, mi_ref, min_ref, mout_ref, sa_ref, sb_ref, y_ref, *, n_chunks, steps):
    chunk = S5_CHUNK
    n = u_ref.shape[0] // chunk
    u = jnp.concatenate([u_ref[pl.ds(i, n, stride=chunk), :] for i in range(chunk)], axis=1).astype(BF16)
    y = jnp.dot(u, mi_ref[...], preferred_element_type=F32)
    x = jnp.dot(u, min_ref[...], preferred_element_type=F32)
    half = x.shape[1] // 2
    c = lax.rem(lax.broadcasted_iota(jnp.int32, x.shape, 0), n_chunks)
    sa = sa_ref[...]
    sb = sb_ref[...]
    for k in range(steps):
        sh = 1 << k
        xs = jnp.where(c >= sh, pltpu.roll(x, sh, axis=0), 0.0)
        x = x + xs * sa[k:k + 1, :] + pltpu.roll(xs, half, axis=1) * sb[k:k + 1, :]
    s_in = jnp.where(c >= 1, pltpu.roll(x, 1, axis=0), 0.0)
    y = y + jnp.dot(s_in.astype(BF16), mout_ref[...], preferred_element_type=F32)
    for i in range(chunk):
        y_ref[pl.ds(i, n, stride=chunk), :] = y[:, i * LANES:(i + 1) * LANES]


def _s5(u, m_intra, m_in, m_out, sc_a, sc_b, *, n_chunks):
    t, d_ssm = u.shape
    o, lh, st = m_in.shape
    steps = sc_a.shape[1]
    mat = lambda i: (i, 0, 0)
    return pl.pallas_call(
        functools.partial(_s5_kernel, n_chunks=n_chunks, steps=steps),
        grid=(o,),
        in_specs=[
            pl.BlockSpec((t, LANES), lambda i: (0, i)),
            pl.BlockSpec((None, lh, lh), mat), pl.BlockSpec((None, lh, st), mat), pl.BlockSpec((None, st, lh), mat),
            pl.BlockSpec((None, steps, st), mat), pl.BlockSpec((None, steps, st), mat),
        ],
        out_specs=pl.BlockSpec((t, LANES), lambda i: (0, i)),
        out_shape=jax.ShapeDtypeStruct((t, d_ssm), F32),
        compiler_params=_params(("parallel",)),
        name="s5_chunked",
    )(u, m_intra, m_in, m_out, sc_a, sc_b)


def _postmix_kernel(oa_ref, ys_ref, x_ref, wglu_ref, woa_ref, wob_ref, ga_ref, gs_ref, gf_ref,
                    wr_ref, rb_ref, h1_ref, idx_ref, gate_ref, loc_ref, cnt_ref, run_ref, *, n_experts):
    step = pl.program_id(0)

    @pl.when(step == 0)
    def _():
        run_ref[...] = jnp.zeros(run_ref.shape, F32)

    y = ys_ref[...]
    y = 0.5 * y * (1.0 + jnp.tanh(math.sqrt(2.0 / math.pi) * (y + 0.044715 * (y * y * y))))
    y = y * _sigmoid(jnp.dot(y.astype(BF16), wglu_ref[...], preferred_element_type=F32))
    na = _rms(oa_ref[...].astype(F32), ga_ref[...]).astype(BF16)
    ns = _rms(y, gs_ref[...]).astype(BF16)
    h1 = (x_ref[...] + jnp.dot(na, woa_ref[...], preferred_element_type=F32)
          + jnp.dot(ns, wob_ref[...], preferred_element_type=F32))
    h1_ref[...] = h1

    m = _rms(h1, gf_ref[...])
    logits = jnp.dot(m, wr_ref[...], preferred_element_type=F32, precision=lax.Precision.HIGHEST)
    scores = _sigmoid(logits)
    vals = scores + rb_ref[...]
    tm = vals.shape[0]
    lane = lax.broadcasted_iota(jnp.int32, vals.shape, 1).astype(F32)
    picks, firsts, raws = [], [], []
    sel = jnp.zeros(vals.shape, F32)
    for _ in range(TOP_K):
        mx = jnp.max(vals, axis=1, keepdims=True)
        first = jnp.min(jnp.where(vals == mx, lane, float(n_experts)), axis=1, keepdims=True)
        pick = lane == first
        picks.append(pick)
        firsts.append(first)
        raws.append(jnp.sum(jnp.where(pick, scores, 0.0), axis=1, keepdims=True))
        vals = jnp.where(pick, -jnp.inf, vals)
        sel = jnp.where(pick, 1.0, sel)
    denom = raws[0]
    for r in raws[1:]:
        denom = denom + r

    r_i = lax.broadcasted_iota(jnp.int32, (tm, tm), 0)
    c_i = lax.broadcasted_iota(jnp.int32, (tm, tm), 1)
    before = jnp.where(c_i < r_i, 1.0, 0.0).astype(BF16)
    rank = jnp.dot(before, sel.astype(BF16), preferred_element_type=F32) + run_ref[...]
    run_ref[...] = run_ref[...] + jnp.sum(sel, axis=0, keepdims=True)
    cnt_ref[...] = run_ref[...]

    slot = lax.broadcasted_iota(jnp.int32, (tm, LANES), 1)
    idx_w = jnp.zeros((tm, LANES), F32)
    loc_w = jnp.zeros((tm, LANES), F32)
    gate_w = jnp.zeros((tm, LANES), F32)
    for k in range(TOP_K):
        loc = jnp.sum(jnp.where(picks[k], rank, 0.0), axis=1, keepdims=True)
        idx_w = jnp.where(slot == k, firsts[k], idx_w)
        loc_w = jnp.where(slot == k, loc, loc_w)
        gate_w = jnp.where(slot == k, raws[k] / denom * ROUTED_SCALE, gate_w)
    idx_ref[...] = idx_w.T[:IDX_ROWS].astype(jnp.int32)
    loc_ref[...] = loc_w.T[:IDX_ROWS].astype(jnp.int32)
    gate_ref[...] = gate_w[:, :IDX_ROWS]


def _postmix(oa, ys, x2, wglu, woa, wob, ga, gs, gf, wr, rb, *, tm):
    t, d = x2.shape
    da = oa.shape[1]
    dsm = ys.shape[1]
    e = wr.shape[1]
    row = lambda i: (i, 0)
    col = lambda i: (0, i)
    fix = lambda i: (0, 0)
    return pl.pallas_call(
        functools.partial(_postmix_kernel, n_experts=e),
        grid=(t // tm,),
        in_specs=[
            pl.BlockSpec((tm, da), row), pl.BlockSpec((tm, dsm), row), pl.BlockSpec((tm, d), row),
            pl.BlockSpec((dsm, dsm), fix), pl.BlockSpec((da, d), fix), pl.BlockSpec((dsm, d), fix),
            pl.BlockSpec((1, da), fix), pl.BlockSpec((1, dsm), fix), pl.BlockSpec((1, d), fix),
            pl.BlockSpec((d, e), fix), pl.BlockSpec((1, e), fix),
        ],
        out_specs=[
            pl.BlockSpec((tm, d), row), pl.BlockSpec((IDX_ROWS, tm), col), pl.BlockSpec((tm, IDX_ROWS), row),
            pl.BlockSpec((IDX_ROWS, tm), col), pl.BlockSpec((1, e), fix),
        ],
        out_shape=[
            jax.ShapeDtypeStruct((t, d), F32), jax.ShapeDtypeStruct((IDX_ROWS, t), jnp.int32),
            jax.ShapeDtypeStruct((t, IDX_ROWS), F32), jax.ShapeDtypeStruct((IDX_ROWS, t), jnp.int32),
            jax.ShapeDtypeStruct((1, e), F32),
        ],
        scratch_shapes=[pltpu.VMEM((1, e), F32)],
        compiler_params=_params(("arbitrary",)),
        name="postmix_router",
    )(oa, ys, x2, wglu, woa, wob, ga, gs, gf, wr, rb)


def _rows_to_tiles(ref, packed, sub):
    rows = packed.shape[0]
    for s in range(sub):
        ref[pl.ds(s, rows, stride=sub), :] = packed[:, s * LANES:(s + 1) * LANES]


def _tiles_to_rows(ref, rows, sub):
    return jnp.concatenate([ref[pl.ds(s, rows, stride=sub), :] for s in range(sub)], axis=1)


def _tile_rows(ref, r, sub):
    return ref.at[pl.ds(pl.multiple_of(r * sub, sub), sub)]


def _dispatch_kernel(dest_ref, zflag_ref, h1_ref, gf_ref, wg_ref, wu_ref, wd_ref, h1s_ref, xs_ref,
                     m_ref, zero_ref, sem, zsem, *, sub):
    tm = h1_ref.shape[0]
    zrows = zero_ref.shape[0]
    n_blocks = zflag_ref.shape[0]

    @pl.when(pl.program_id(0) == 0)
    def _():
        zero_ref[...] = jnp.zeros(zero_ref.shape, zero_ref.dtype)

        def zcopy(b):
            return pltpu.make_async_copy(zero_ref, xs_ref.at[pl.ds(pl.multiple_of(b * zrows, zrows), zrows)], zsem)

        def zstart(b, carry):
            @pl.when(zflag_ref[b] != 0)
            def _():
                zcopy(b).start()
            return carry

        def zwait(b, carry):
            @pl.when(zflag_ref[b] != 0)
            def _():
                zcopy(b).wait()
            return carry

        lax.fori_loop(0, n_blocks, zstart, 0)
        lax.fori_loop(0, n_blocks, zwait, 0)

    h1 = h1_ref[...]
    m = _rms(h1, gf_ref[...])
    _rows_to_tiles(m_ref, _pack_halves(m), sub)

    def issue(t, carry):
        src = _tile_rows(m_ref, t, sub)
        for k in range(TOP_K):
            pltpu.make_async_copy(src, _tile_rows(xs_ref, dest_ref[k * tm + t], sub), sem).start()
        return carry

    lax.fori_loop(0, tm, issue, 0)
    mb = m.astype(BF16)
    a = jnp.dot(mb, wg_ref[...], preferred_element_type=F32)
    b = jnp.dot(mb, wu_ref[...], preferred_element_type=F32)
    hid = (a * _sigmoid(a) * b).astype(BF16)
    h1s_ref[...] = h1 + jnp.dot(hid, wd_ref[...], preferred_element_type=F32)
    for _ in range(TOP_K):
        pltpu.make_async_copy(m_ref, xs_ref.at[pl.ds(0, tm * sub)], sem).wait()


def _dispatch(dest, zflag, h1, gf, wg, wu, wd, *, tm, rb, n_pad):
    t, d = h1.shape
    f = wg.shape[1]
    sub = d // 2 // LANES
    row = lambda i: (i, 0)
    fix = lambda i: (0, 0)
    return pl.pallas_call(
        functools.partial(_dispatch_kernel, sub=sub),
        grid=(t // tm,),
        in_specs=[
            pl.BlockSpec((IDX_ROWS * tm,), lambda i: (i,), memory_space=pltpu.SMEM),
            pl.BlockSpec(memory_space=pltpu.SMEM),
            pl.BlockSpec((tm, d), row), pl.BlockSpec((1, d), fix),
            pl.BlockSpec((d, f), fix), pl.BlockSpec((d, f), fix), pl.BlockSpec((f, d), fix),
        ],
        out_specs=[pl.BlockSpec((tm, d), row), pl.BlockSpec(memory_space=pl.ANY)],
        out_shape=[jax.ShapeDtypeStruct((t, d), F32), jax.ShapeDtypeStruct((n_pad * sub, LANES), jnp.int32)],
        scratch_shapes=[pltpu.VMEM((tm * sub, LANES), jnp.int32), pltpu.VMEM((rb * sub, LANES), jnp.int32),
                        pltpu.SemaphoreType.DMA(()), pltpu.SemaphoreType.DMA(())],
        compiler_params=_params(("arbitrary",)),
        name="dispatch_shared",
    )(dest, zflag, h1, gf, wg, wu, wd)


def _experts_kernel(be_ref, first_ref, nxt_ref, nu_ref, xs_ref, wg_hbm, wu_hbm, wd_hbm, ys_ref,
                    wgf_ref, wuf_ref, wdf_ref, wgb_ref, wub_ref, wdb_ref, sem, *, sub):
    i = pl.program_id(0)
    tm = xs_ref.shape[0] // sub

    def weight_copies(e):
        return (pltpu.make_async_copy(wg_hbm.at[e], wgf_ref, sem.at[0]),
                pltpu.make_async_copy(wu_hbm.at[e], wuf_ref, sem.at[1]),
                pltpu.make_async_copy(wd_hbm.at[e], wdf_ref, sem.at[2]))

    @pl.when(i == 0)
    def _():
        for c in weight_copies(be_ref[0]):
            c.start()

    @pl.when((i < nu_ref[0]) & (first_ref[i] != 0))
    def _():
        for c in weight_copies(be_ref[i]):
            c.wait()
        wgb_ref[...] = wgf_ref[...].astype(BF16)
        wub_ref[...] = wuf_ref[...].astype(BF16)
        wdb_ref[...] = wdf_ref[...].astype(BF16)

        @pl.when(nxt_ref[i] >= 0)
        def _():
            for c in weight_copies(nxt_ref[i]):
                c.start()

    @pl.when(i < nu_ref[0])
    def _():
        lo, hi = _unpack_halves(_tiles_to_rows(xs_ref, tm, sub))
        lo = lo.astype(BF16)
        hi = hi.astype(BF16)
        n = lo.shape[1]
        a = (jnp.dot(lo, wgb_ref[:n, :], preferred_element_type=F32)
             + jnp.dot(hi, wgb_ref[n:, :], preferred_element_type=F32))
        b = (jnp.dot(lo, wub_ref[:n, :], preferred_element_type=F32)
             + jnp.dot(hi, wub_ref[n:, :], preferred_element_type=F32))
        hid = (a * _sigmoid(a) * b).astype(BF16)
        _rows_to_tiles(ys_ref, _pack_halves(jnp.dot(hid, wdb_ref[...], preferred_element_type=F32)), sub)

    @pl.when(i >= nu_ref[0])
    def _():
        ys_ref[...] = jnp.zeros(ys_ref.shape, ys_ref.dtype)


def _experts(blk_e, first, nxt, n_used, xs, wg, wu, wd, *, tm):
    rows_sub, _ = xs.shape
    _, d, f = wg.shape
    sub = d // 2 // LANES
    n_blocks = rows_sub // (tm * sub)
    rows = lambda i, be, fi, nx, nu: (jnp.maximum(jnp.minimum(i, nu[0] - 1), 0), 0)
    orow = lambda i, be, fi, nx, nu: (i, 0)
    return pl.pallas_call(
        functools.partial(_experts_kernel, sub=sub),
        grid_spec=pltpu.PrefetchScalarGridSpec(
            num_scalar_prefetch=4,
            grid=(n_blocks,),
            in_specs=[
                pl.BlockSpec((tm * sub, LANES), rows),
                pl.BlockSpec(memory_space=pl.ANY), pl.BlockSpec(memory_space=pl.ANY),
                pl.BlockSpec(memory_space=pl.ANY),
            ],
            out_specs=pl.BlockSpec((tm * sub, LANES), orow),
            scratch_shapes=[pltpu.VMEM((d, f), F32), pltpu.VMEM((d, f), F32), pltpu.VMEM((f, d), F32),
                            pltpu.VMEM((d, f), BF16), pltpu.VMEM((d, f), BF16), pltpu.VMEM((f, d), BF16),
                            pltpu.SemaphoreType.DMA((3,))],
        ),
        out_shape=jax.ShapeDtypeStruct(xs.shape, jnp.int32),
        compiler_params=_params(("arbitrary",)),
        name="routed_experts",
    )(blk_e, first, nxt, n_used, xs, wg, wu, wd)


def _final_kernel(dest_ref, h_ref, gate_ref, p_ref, gp_ref, wgate_ref, wproj_ref, ys_ref, o_ref, buf_ref, sem,
                  *, sub):
    tm = h_ref.shape[0]

    def issue(t, carry):
        for k in range(TOP_K):
            pltpu.make_async_copy(_tile_rows(ys_ref, dest_ref[k * tm + t], sub),
                                  _tile_rows(buf_ref.at[k], t, sub), sem).start()
        return carry

    lax.fori_loop(0, tm, issue, 0)
    pp = jnp.dot(p_ref[...].astype(BF16), wproj_ref[...], preferred_element_type=F32)
    for k in range(TOP_K):
        pltpu.make_async_copy(ys_ref.at[pl.ds(0, tm * sub)], buf_ref.at[k], sem).wait()
    gate = gate_ref[...]
    h = h_ref[...]
    n = h.shape[1] // 2
    acc_lo = h[:, :n]
    acc_hi = h[:, n:]
    for k in range(TOP_K):
        lo, hi = _unpack_halves(_tiles_to_rows(buf_ref.at[k], tm, sub))
        acc_lo = acc_lo + gate[:, k:k + 1] * lo
        acc_hi = acc_hi + gate[:, k:k + 1] * hi
    h2 = jnp.concatenate([acc_lo, acc_hi], axis=1)
    nrm = _rms(h2, gp_ref[...]).astype(BF16)
    o_ref[...] = h2 + _sigmoid(jnp.dot(nrm, wgate_ref[...], preferred_element_type=F32)) * pp


def _final(dest, h1s, gate, p2, gp, wgate, wproj, ys, *, tm):
    t, d = h1s.shape
    dp = p2.shape[1]
    sub = d // 2 // LANES
    row = lambda i: (i, 0)
    fix = lambda i: (0, 0)
    return pl.pallas_call(
        functools.partial(_final_kernel, sub=sub),
        grid=(t // tm,),
        in_specs=[
            pl.BlockSpec((IDX_ROWS * tm,), lambda i: (i,), memory_space=pltpu.SMEM),
            pl.BlockSpec((tm, d), row), pl.BlockSpec((tm, IDX_ROWS), row), pl.BlockSpec((tm, dp), row),
            pl.BlockSpec((1, d), fix), pl.BlockSpec((d, d), fix), pl.BlockSpec((dp, d), fix),
            pl.BlockSpec(memory_space=pl.ANY),
        ],
        out_specs=pl.BlockSpec((tm, d), row),
        out_shape=jax.ShapeDtypeStruct((t, d), F32),
        scratch_shapes=[pltpu.VMEM((TOP_K, tm * sub, LANES), jnp.int32), pltpu.SemaphoreType.DMA(())],
        compiler_params=_params(("arbitrary",)),
        name="combine_ple",
    )(dest, h1s, gate, p2, gp, wgate, wproj, ys)


def _largest_tile(n, cap):
    t = min(n, cap)
    while n % t:
        t //= 2
    return t


def _layer(h, p_l, norm_mix, w_in, q_norm, k_norm, lam_re, lam_im, log_dt, b_re, b_im, c_re, c_im, d_skip,
           w_glu, attn_out_norm, ssm_out_norm, w_out, norm_ffn, w_router, router_bias, w_exp_gate,
           w_exp_up, w_exp_down, w_sh_gate, w_sh_up, w_sh_down, norm_ple, w_ple_gate, w_ple_proj):
    bsz, seq, d = h.shape
    t = bsz * seq
    head_dim = q_norm.shape[-1]
    d_attn = attn_out_norm.shape[-1]
    n_heads = d_attn // head_dim
    n_experts = w_router.shape[-1]
    x2 = h.reshape(t, d)

    tn = d_attn if d_attn % 128 == 0 else head_dim
    scale = 1.0 / math.sqrt(head_dim)
    hg = jnp.concatenate([jnp.tile(q_norm * scale, n_heads), jnp.tile(k_norm, n_heads),
                          jnp.ones((w_in.shape[1] - 2 * d_attn,), F32)])[None, :]
    z, u = _inproj(x2, norm_mix[None, :], w_in.astype(BF16), hg, d_qkv=3 * d_attn,
                   n_norm_tiles=2 * d_attn // tn, head_dim=head_dim, tm=_largest_tile(t, 1024), tn=tn)

    o_attn = _attention(z, batch=bsz, seq=seq, n_heads=n_heads, head_dim=head_dim, blk=_largest_tile(seq, 256))

    n_chunks = seq // S5_CHUNK
    mats = _s5_prepare(lam_re, lam_im, log_dt, b_re, b_im, c_re, c_im, d_skip, n_chunks)
    y_ssm = _s5(u, *mats, n_chunks=n_chunks)

    tm = _largest_tile(t, 256)
    w_out_b = w_out.astype(BF16)
    h1, idx, gate, loc, counts = _postmix(
        o_attn, y_ssm, x2, w_glu.astype(BF16), w_out_b[:d_attn], w_out_b[d_attn:],
        attn_out_norm[None, :], ssm_out_norm[None, :], norm_ffn[None, :], w_router, router_bias[None, :], tm=tm)

    rb = _largest_tile(t, 256)
    n_pad = t * TOP_K + n_experts * rb
    n_blocks = n_pad // rb
    cnt = counts[0].astype(jnp.int32)
    pcnt = (cnt + rb - 1) // rb * rb
    pend = jnp.cumsum(pcnt)
    pstart = pend - pcnt
    onehot = idx[:, :, None] == jnp.arange(n_experts, dtype=jnp.int32)[None, None, :]
    dest = jnp.sum(jnp.where(onehot, pstart[None, None, :], 0), axis=-1) + loc
    dest = dest.reshape(IDX_ROWS, t // tm, tm).transpose(1, 0, 2).reshape(-1)
    n_used = (pend[-1] // rb).astype(jnp.int32)
    blk_ids = jnp.arange(n_blocks, dtype=jnp.int32)
    starts = (blk_ids * rb)[:, None]
    owner = (starts >= pstart[None, :]) & (starts < pend[None, :])
    blk_e = jnp.sum(jnp.where(owner, jnp.arange(n_experts, dtype=jnp.int32)[None, :], 0), axis=1)
    is_last = jnp.any(owner & (starts + rb == pend[None, :]), axis=1)
    is_first = jnp.any(owner & (starts == pstart[None, :]), axis=1).astype(jnp.int32)
    zflag = ((blk_ids >= n_used) | is_last).astype(jnp.int32)
    first_pos = jnp.where(is_first > 0, blk_ids, n_blocks)
    nxt_pos = jnp.concatenate([lax.cummin(first_pos, axis=0, reverse=True)[1:],
                               jnp.full((1,), n_blocks, jnp.int32)])
    nxt_hit = nxt_pos[:, None] == blk_ids[None, :]
    nxt = jnp.where(nxt_pos < n_blocks, jnp.sum(jnp.where(nxt_hit, blk_e[None, :], 0), axis=1), -1)
    last_e = jnp.sum(jnp.where(blk_ids == n_used - 1, blk_e, 0))
    blk_e = jnp.where(blk_ids < n_used, blk_e, last_e).astype(jnp.int32)

    h1s, xs = _dispatch(dest, zflag, h1, norm_ffn[None, :], w_sh_gate.astype(BF16), w_sh_up.astype(BF16),
                        w_sh_down.astype(BF16), tm=tm, rb=rb, n_pad=n_pad)
    ys = _experts(blk_e, is_first, nxt.astype(jnp.int32), n_used[None], xs, w_exp_gate, w_exp_up, w_exp_down,
                  tm=rb)
    out = _final(dest, h1s, gate, p_l.reshape(t, -1), norm_ple[None, :], w_ple_gate.astype(BF16),
                 w_ple_proj.astype(BF16), ys, tm=tm)
    return out.reshape(bsz, seq, d)


def kernel(x, p, norm_mix, w_in, q_norm, k_norm, ssm_lam_re, ssm_lam_im, ssm_log_dt, ssm_b_re, ssm_b_im,
           ssm_c_re, ssm_c_im, ssm_d, w_glu, attn_out_norm, ssm_out_norm, w_out, norm_ffn, w_router,
           router_bias, w_exp_gate, w_exp_up, w_exp_down, w_sh_gate, w_sh_up, w_sh_down, norm_ple,
           w_ple_gate, w_ple_proj):
    h = x
    for i in range(p.shape[0]):
        h = _layer(h, p[i], norm_mix[i], w_in[i], q_norm[i], k_norm[i], ssm_lam_re[i], ssm_lam_im[i],
                   ssm_log_dt[i], ssm_b_re[i], ssm_b_im[i], ssm_c_re[i], ssm_c_im[i], ssm_d[i], w_glu[i],
                   attn_out_norm[i], ssm_out_norm[i], w_out[i], norm_ffn[i], w_router[i], router_bias[i],
                   w_exp_gate[i], w_exp_up[i], w_exp_down[i], w_sh_gate[i], w_sh_up[i], w_sh_down[i],
                   norm_ple[i], w_ple_gate[i], w_ple_proj[i])
    return h
```

```python
import functools
import math

import jax
import jax.numpy as jnp
from jax import lax
from jax.experimental import pallas as pl
from jax.experimental.pallas import tpu as pltpu

NORM_EPS = 1e-6
TOP_K = 6
ROUTED_SCALE = 2.5
IDX_ROWS = 8
LANES = 128
S5_CHUNK = 8
VMEM_LIMIT = 56 * 1024 * 1024

F32 = jnp.float32
BF16 = jnp.bfloat16
HIGH_HALF = -65536


def _rms(x, g):
    return x * lax.rsqrt(jnp.mean(x * x, axis=-1, keepdims=True) + NORM_EPS) * g


def _sigmoid(x):
    return 1.0 / (1.0 + jnp.exp(-x))


def _params(sem):
    return pltpu.CompilerParams(dimension_semantics=sem, vmem_limit_bytes=VMEM_LIMIT)


def _pack_halves(x):
    n = x.shape[1] // 2
    lo = lax.bitcast_convert_type(x[:, :n].astype(BF16).astype(F32), jnp.int32)
    hi = lax.bitcast_convert_type(x[:, n:].astype(BF16).astype(F32), jnp.int32)
    return (hi & HIGH_HALF) | lax.shift_right_logical(lo, 16)


def _unpack_halves(p):
    lo = lax.bitcast_convert_type(lax.shift_left(p, 16), F32)
    hi = lax.bitcast_convert_type(p & HIGH_HALF, F32)
    return lo, hi


def _inproj_kernel(x_ref, g_ref, w_ref, hg_ref, o_ref, u_ref, xn_ref, *, n_norm_tiles, n_qkv_tiles, head_dim):
    j = pl.program_id(1)

    @pl.when(j == 0)
    def _():
        xn_ref[...] = _rms(x_ref[...], g_ref[...]).astype(BF16)

    acc = jnp.dot(xn_ref[...], w_ref[...], preferred_element_type=F32)
    tn = acc.shape[1]

    @pl.when(j < n_norm_tiles)
    def _():
        hg = hg_ref[...]
        for h in range(tn // head_dim):
            sl = slice(h * head_dim, (h + 1) * head_dim)
            o_ref[:, sl] = _rms(acc[:, sl], hg[:, sl]).astype(o_ref.dtype)

    @pl.when((j >= n_norm_tiles) & (j < n_qkv_tiles))
    def _():
        o_ref[...] = acc.astype(o_ref.dtype)

    @pl.when(j >= n_qkv_tiles)
    def _():
        u_ref[...] = acc


def _inproj(x2, g, w, hg, *, d_qkv, n_norm_tiles, head_dim, tm, tn):
    t, d = x2.shape
    n = w.shape[1]
    nq = d_qkv // tn
    return pl.pallas_call(
        functools.partial(_inproj_kernel, n_norm_tiles=n_norm_tiles, n_qkv_tiles=nq, head_dim=head_dim),
        grid=(t // tm, n // tn),
        in_specs=[
            pl.BlockSpec((tm, d), lambda i, j: (i, 0)),
            pl.BlockSpec((1, d), lambda i, j: (0, 0)),
            pl.BlockSpec((d, tn), lambda i, j: (0, j)),
            pl.BlockSpec((1, tn), lambda i, j: (0, j)),
        ],
        out_specs=[
            pl.BlockSpec((tm, tn), lambda i, j: (i, jnp.minimum(j, nq - 1))),
            pl.BlockSpec((tm, tn), lambda i, j: (i, jnp.maximum(j - nq, 0))),
        ],
        out_shape=[jax.ShapeDtypeStruct((t, d_qkv), BF16), jax.ShapeDtypeStruct((t, n - d_qkv), F32)],
        scratch_shapes=[pltpu.VMEM((tm, d), BF16)],
        compiler_params=_params(("parallel", "arbitrary")),
        name="inproj",
    )(x2, g, w, hg)


SKIP_AFTER = 110.0


def _attn_kernel(q_ref, k_ref, v_ref, o_ref, *, blk, hd, heads):
    i = pl.program_id(2)
    row = lax.broadcasted_iota(jnp.int32, (blk, blk), 0)
    col = lax.broadcasted_iota(jnp.int32, (blk, blk), 1)
    causal = col < row
    later = jnp.where(row > col, 1.0, 0.0).astype(BF16)
    lanes = [slice(h * hd, (h + 1) * hd) for h in range(heads)]
    qs = [q_ref[:, sl] for sl in lanes]

    def step(kb, accs, runs, diag):
        start = pl.multiple_of(kb * blk, blk)
        new_accs, new_runs = [], []
        for h, sl in enumerate(lanes):
            k = k_ref[pl.ds(start, blk), sl]
            v = v_ref[pl.ds(start, blk), sl]
            z = lax.dot_general(qs[h], k, (((1,), (1,)), ((), ())), preferred_element_type=F32)
            sp = jnp.maximum(z, 0.0) + jnp.log(1.0 + jnp.exp(-jnp.abs(z)))
            if diag:
                sp = jnp.where(causal, sp, 0.0)
            hi = sp.astype(BF16)
            lo = (sp - hi.astype(F32)).astype(BF16)
            inner = (jnp.dot(hi, later, preferred_element_type=F32)
                     + jnp.dot(lo, later, preferred_element_type=F32))
            w = jnp.exp(z - sp - (inner + runs[h]))
            if diag:
                w = jnp.where(causal, w, 0.0)
            new_accs.append(accs[h] + jnp.dot(w.astype(BF16), v, preferred_element_type=F32))
            new_runs.append(runs[h] + jnp.sum(sp, axis=-1, keepdims=True))
        return tuple(new_accs), tuple(new_runs)

    def keep_going(runs):
        low = runs[0]
        for r in runs[1:]:
            low = jnp.minimum(low, r)
        return (jnp.min(low) < SKIP_AFTER).astype(jnp.int32)

    accs = tuple(jnp.zeros((blk, hd), F32) for _ in lanes)
    runs = tuple(jnp.zeros((blk, 1), F32) for _ in lanes)
    accs, runs = step(i, accs, runs, True)

    def cond(c):
        n, go, _, _ = c
        return (n < i) & (go > 0)

    def body(c):
        n, _, accs, runs = c
        accs, runs = step(i - 1 - n, accs, runs, False)
        return n + 1, keep_going(runs), accs, runs

    _, _, accs, _ = lax.while_loop(cond, body, (jnp.int32(0), keep_going(runs), accs, runs))
    for h, sl in enumerate(lanes):
        o_ref[:, sl] = accs[h].astype(o_ref.dtype)


def _attention(z, *, batch, seq, n_heads, head_dim, blk):
    nq = seq // blk
    heads = 2 if n_heads % 2 == 0 else 1
    hw = heads * head_dim
    ng = n_heads // heads
    return pl.pallas_call(
        functools.partial(_attn_kernel, blk=blk, hd=head_dim, heads=heads),
        grid=(batch, ng, nq),
        in_specs=[
            pl.BlockSpec((blk, hw), lambda b, h, i: (b * nq + i, h)),
            pl.BlockSpec((seq, hw), lambda b, h, i: (b, ng + h)),
            pl.BlockSpec((seq, hw), lambda b, h, i: (b, 2 * ng + h)),
        ],
        out_specs=pl.BlockSpec((blk, hw), lambda b, h, i: (b * nq + i, h)),
        out_shape=jax.ShapeDtypeStruct((batch * seq, n_heads * head_dim), BF16),
        compiler_params=_params(("parallel", "parallel", "arbitrary")),
        name="sb_attention",
    )(z, z, z)


def _s5_prepare(lam_re, lam_im, log_dt, b_re, b_im, c_re, c_im, d_skip, n_chunks):
    hp = lax.Precision.HIGHEST
    chunk = S5_CHUNK
    g, p = lam_re.shape
    h = b_re.shape[-1]
    gg = LANES // h
    o = g // gg
    dt = jnp.exp(log_dt)[:, None]
    ks = jnp.arange(chunk + 1, dtype=F32)[None, :, None]
    mag = jnp.exp(lam_re[:, None, :] * dt[:, None, :] * ks)
    ang = lam_im[:, None, :] * dt[:, None, :] * ks
    ak_re = mag * jnp.cos(ang)
    ak_im = mag * jnp.sin(ang)
    nr = ak_re[:, 1] - 1.0
    ni = ak_im[:, 1]
    den = lam_re * lam_re + lam_im * lam_im
    coef_re = ((nr * lam_re + ni * lam_im) / den)[..., None]
    coef_im = ((ni * lam_re - nr * lam_im) / den)[..., None]
    bbar_re = coef_re * b_re - coef_im * b_im
    bbar_im = coef_re * b_im + coef_im * b_re
    ca_re = c_re[:, None] * ak_re[:, :, None, :] - c_im[:, None] * ak_im[:, :, None, :]
    ca_im = c_re[:, None] * ak_im[:, :, None, :] + c_im[:, None] * ak_re[:, :, None, :]
    eye_g = jnp.eye(gg, dtype=F32)

    cam_re = ca_re[:, :chunk].transpose(0, 3, 1, 2).reshape(g, p, chunk * h)
    cam_im = ca_im[:, :chunk].transpose(0, 3, 1, 2).reshape(g, p, chunk * h)
    kt = (jnp.einsum("gph,gpm->ghm", bbar_re, cam_re, precision=hp)
          - jnp.einsum("gph,gpm->ghm", bbar_im, cam_im, precision=hp))
    kt = kt.reshape(o, gg, h, chunk, h)
    bd = kt[:, :, :, :, None, :] * eye_g[None, :, None, None, :, None]
    bd = bd.transpose(0, 3, 1, 2, 4, 5).reshape(o, chunk, LANES, LANES)
    dvec = d_skip.reshape(o, LANES)
    lags = [bd[:, k] for k in range(chunk)]
    lags[0] = lags[0] + dvec[:, :, None] * jnp.eye(LANES, dtype=F32)[None]
    zero_blk = jnp.zeros_like(lags[0])
    m_intra = jnp.concatenate(
        [jnp.concatenate([lags[j - i] if j >= i else zero_blk for j in range(chunk)], axis=2)
         for i in range(chunk)], axis=1)

    def expand_in(ak_part_a, ak_part_b, b_a, b_b, sign):
        rk_a = jnp.stack([ak_part_a[:, chunk - 1 - i] for i in range(chunk)], axis=1)
        rk_b = jnp.stack([ak_part_b[:, chunk - 1 - i] for i in range(chunk)], axis=1)
        val = (rk_a[:, :, None, :] * b_a.transpose(0, 2, 1)[:, None]
               + sign * rk_b[:, :, None, :] * b_b.transpose(0, 2, 1)[:, None])
        val = val.reshape(o, gg, chunk, h, p).transpose(0, 2, 1, 3, 4)
        val = val[:, :, :, :, None, :] * eye_g[None, None, :, None, :, None]
        return val.reshape(o, chunk * LANES, gg * p)

    m_in = jnp.concatenate([expand_in(ak_re, ak_im, bbar_re, bbar_im, -1.0),
                            expand_in(ak_re, ak_im, bbar_im, bbar_re, 1.0)], axis=-1)

    def expand_out(ca):
        val = ca[:, 1:].reshape(o, gg, chunk, h, p).transpose(0, 4, 2, 1, 3)
        val = val[:, None] * eye_g[None, :, None, None, :, None]
        return val.reshape(o, gg * p, chunk * LANES)

    m_out = jnp.concatenate([expand_out(ca_re), -expand_out(ca_im)], axis=1)

    steps = max(1, int(math.ceil(math.log2(max(n_chunks, 2)))))
    cr, ci = ak_re[:, chunk].reshape(o, gg * p), ak_im[:, chunk].reshape(o, gg * p)
    sc_a, sc_b = [], []
    for _ in range(steps):
        sc_a.append(jnp.concatenate([cr, cr], axis=-1))
        sc_b.append(jnp.concatenate([-ci, ci], axis=-1))
        cr, ci = cr * cr - ci * ci, 2.0 * cr * ci
    return (m_intra.astype(BF16), m_in.astype(BF16), m_out.astype(BF16),
            jnp.stack(sc_a, axis=1), jnp.stack(sc_b, axis=1))


def _s5_kernel(u_ref, mi_ref, min_ref, mout_ref, sa_ref, sb_ref, y_ref, *, n_chunks, steps):
    chunk = S5_CHUNK
    n = u_ref.shape[0] // chunk
    u = jnp.concatenate([u_ref[pl.ds(i, n, stride=chunk), :] for i in range(chunk)], axis=1).astype(BF16)
    y = jnp.dot(u, mi_ref[...], preferred_element_type=F32)
    x = jnp.dot(u, min_ref[...], preferred_element_type=F32)
    half = x.shape[1] // 2
    c = lax.rem(lax.broadcasted_iota(jnp.int32, x.shape, 0), n_chunks)
    sa = sa_ref[...]
    sb = sb_ref[...]
    for k in range(steps):
        sh = 1 << k
        xs = jnp.where(c >= sh, pltpu.roll(x, sh, axis=0), 0.0)
        x = x + xs * sa[k:k + 1, :] + pltpu.roll(xs, half, axis=1) * sb[k:k + 1, :]
    s_in = jnp.where(c >= 1, pltpu.roll(x, 1, axis=0), 0.0)
    y = y + jnp.dot(s_in.astype(BF16), mout_ref[...], preferred_element_type=F32)
    for i in range(chunk):
        y_ref[pl.ds(i, n, stride=chunk), :] = y[:, i * LANES:(i + 1) * LANES]


def _s5(u, m_intra, m_in, m_out, sc_a, sc_b, *, n_chunks):
    t, d_ssm = u.shape
    o, lh, st = m_in.shape
    steps = sc_a.shape[1]
    mat = lambda i: (i, 0, 0)
    return pl.pallas_call(
        functools.partial(_s5_kernel, n_chunks=n_chunks, steps=steps),
        grid=(o,),
        in_specs=[
            pl.BlockSpec((t, LANES), lambda i: (0, i)),
            pl.BlockSpec((None, lh, lh), mat), pl.BlockSpec((None, lh, st), mat), pl.BlockSpec((None, st, lh), mat),
            pl.BlockSpec((None, steps, st), mat), pl.BlockSpec((None, steps, st), mat),
        ],
        out_specs=pl.BlockSpec((t, LANES), lambda i: (0, i)),
        out_shape=jax.ShapeDtypeStruct((t, d_ssm), F32),
        compiler_params=_params(("parallel",)),
        name="s5_chunked",
    )(u, m_intra, m_in, m_out, sc_a, sc_b)


def _postmix_kernel(oa_ref, ys_ref, x_ref, wglu_ref, woa_ref, wob_ref, ga_ref, gs_ref, gf_ref,
                    wr_ref, rb_ref, h1_ref, idx_ref, gate_ref, loc_ref, cnt_ref, run_ref, *, n_experts):
    step = pl.program_id(0)

    @pl.when(step == 0)
    def _():
        run_ref[...] = jnp.zeros(run_ref.shape, F32)

    y = ys_ref[...]
    y = 0.5 * y * (1.0 + jnp.tanh(math.sqrt(2.0 / math.pi) * (y + 0.044715 * (y * y * y))))
    y = y * _sigmoid(jnp.dot(y.astype(BF16), wglu_ref[...], preferred_element_type=F32))
    na = _rms(oa_ref[...].astype(F32), ga_ref[...]).astype(BF16)
    ns = _rms(y, gs_ref[...]).astype(BF16)
    h1 = (x_ref[...] + jnp.dot(na, woa_ref[...], preferred_element_type=F32)
          + jnp.dot(ns, wob_ref[...], preferred_element_type=F32))
    h1_ref[...] = h1

    m = _rms(h1, gf_ref[...])
    logits = jnp.dot(m, wr_ref[...], preferred_element_type=F32, precision=lax.Precision.HIGHEST)
    scores = _sigmoid(logits)
    vals = scores + rb_ref[...]
    tm = vals.shape[0]
    lane = lax.broadcasted_iota(jnp.int32, vals.shape, 1).astype(F32)
    picks, firsts, raws = [], [], []
    sel = jnp.zeros(vals.shape, F32)
    for _ in range(TOP_K):
        mx = jnp.max(vals, axis=1, keepdims=True)
        first = jnp.min(jnp.where(vals == mx, lane, float(n_experts)), axis=1, keepdims=True)
        pick = lane == first
        picks.append(pick)
        firsts.append(first)
        raws.append(jnp.sum(jnp.where(pick, scores, 0.0), axis=1, keepdims=True))
        vals = jnp.where(pick, -jnp.inf, vals)
        sel = jnp.where(pick, 1.0, sel)
    denom = raws[0]
    for r in raws[1:]:
        denom = denom + r

    r_i = lax.broadcasted_iota(jnp.int32, (tm, tm), 0)
    c_i = lax.broadcasted_iota(jnp.int32, (tm, tm), 1)
    before = jnp.where(c_i < r_i, 1.0, 0.0).astype(BF16)
    rank = jnp.dot(before, sel.astype(BF16), preferred_element_type=F32) + run_ref[...]
    run_ref[...] = run_ref[...] + jnp.sum(sel, axis=0, keepdims=True)
    cnt_ref[...] = run_ref[...]

    slot = lax.broadcasted_iota(jnp.int32, (tm, LANES), 1)
    idx_w = jnp.zeros((tm, LANES), F32)
    loc_w = jnp.zeros((tm, LANES), F32)
    gate_w = jnp.zeros((tm, LANES), F32)
    for k in range(TOP_K):
        loc = jnp.sum(jnp.where(picks[k], rank, 0.0), axis=1, keepdims=True)
        idx_w = jnp.where(slot == k, firsts[k], idx_w)
        loc_w = jnp.where(slot == k, loc, loc_w)
        gate_w = jnp.where(slot == k, raws[k] / denom * ROUTED_SCALE, gate_w)
    idx_ref[...] = idx_w.T[:IDX_ROWS].astype(jnp.int32)
    loc_ref[...] = loc_w.T[:IDX_ROWS].astype(jnp.int32)
    gate_ref[...] = gate_w[:, :IDX_ROWS]


def _postmix(oa, ys, x2, wglu, woa, wob, ga, gs, gf, wr, rb, *, tm):
    t, d = x2.shape
    da = oa.shape[1]
    dsm = ys.shape[1]
    e = wr.shape[1]
    row = lambda i: (i, 0)
    col = lambda i: (0, i)
    fix = lambda i: (0, 0)
    return pl.pallas_call(
        functools.partial(_postmix_kernel, n_experts=e),
        grid=(t // tm,),
        in_specs=[
            pl.BlockSpec((tm, da), row), pl.BlockSpec((tm, dsm), row), pl.BlockSpec((tm, d), row),
            pl.BlockSpec((dsm, dsm), fix), pl.BlockSpec((da, d), fix), pl.BlockSpec((dsm, d), fix),
            pl.BlockSpec((1, da), fix), pl.BlockSpec((1, dsm), fix), pl.BlockSpec((1, d), fix),
            pl.BlockSpec((d, e), fix), pl.BlockSpec((1, e), fix),
        ],
        out_specs=[
            pl.BlockSpec((tm, d), row), pl.BlockSpec((IDX_ROWS, tm), col), pl.BlockSpec((tm, IDX_ROWS), row),
            pl.BlockSpec((IDX_ROWS, tm), col), pl.BlockSpec((1, e), fix),
        ],
        out_shape=[
            jax.ShapeDtypeStruct((t, d), F32), jax.ShapeDtypeStruct((IDX_ROWS, t), jnp.int32),
            jax.ShapeDtypeStruct((t, IDX_ROWS), F32), jax.ShapeDtypeStruct((IDX_ROWS, t), jnp.int32),
            jax.ShapeDtypeStruct((1, e), F32),
        ],
        scratch_shapes=[pltpu.VMEM((1, e), F32)],
        compiler_params=_params(("arbitrary",)),
        name="postmix_router",
    )(oa, ys, x2, wglu, woa, wob, ga, gs, gf, wr, rb)


def _rows_to_tiles(ref, packed, sub):
    rows = packed.shape[0]
    for s in range(sub):
        ref[pl.ds(s, rows, stride=sub), :] = packed[:, s * LANES:(s + 1) * LANES]


def _tiles_to_rows(ref, rows, sub):
    return jnp.concatenate([ref[pl.ds(s, rows, stride=sub), :] for s in range(sub)], axis=1)


def _tile_rows(ref, r, sub):
    return ref.at[pl.ds(pl.multiple_of(r * sub, sub), sub)]


def _dispatch_kernel(dest_ref, zflag_ref, h1_ref, gf_ref, wg_ref, wu_ref, wd_ref, h1s_ref, xs_ref,
                     m_ref, zero_ref, sem, zsem, *, sub):
    tm = h1_ref.shape[0]
    zrows = zero_ref.shape[0]
    n_blocks = zflag_ref.shape[0]

    @pl.when(pl.program_id(0) == 0)
    def _():
        zero_ref[...] = jnp.zeros(zero_ref.shape, zero_ref.dtype)

        def zcopy(b):
            return pltpu.make_async_copy(zero_ref, xs_ref.at[pl.ds(pl.multiple_of(b * zrows, zrows), zrows)], zsem)

        def zstart(b, carry):
            @pl.when(zflag_ref[b] != 0)
            def _():
                zcopy(b).start()
            return carry

        def zwait(b, carry):
            @pl.when(zflag_ref[b] != 0)
            def _():
                zcopy(b).wait()
            return carry

        lax.fori_loop(0, n_blocks, zstart, 0)
        lax.fori_loop(0, n_blocks, zwait, 0)

    step = pl.program_id(0)
    slot = lax.rem(step, 2)
    mslot = m_ref.at[slot]
    h1 = h1_ref[...]
    m = _rms(h1, gf_ref[...])
    _rows_to_tiles(mslot, _pack_halves(m), sub)

    def issue(t, carry):
        src = _tile_rows(mslot, t, sub)
        for k in range(TOP_K):
            pltpu.make_async_copy(src, _tile_rows(xs_ref, dest_ref[k * tm + t], sub), sem.at[slot]).start()
        return carry

    lax.fori_loop(0, tm, issue, 0)
    mb = m.astype(BF16)
    a = jnp.dot(mb, wg_ref[...], preferred_element_type=F32)
    b = jnp.dot(mb, wu_ref[...], preferred_element_type=F32)
    hid = (a * _sigmoid(a) * b).astype(BF16)
    h1s_ref[...] = h1 + jnp.dot(hid, wd_ref[...], preferred_element_type=F32)

    def drain(s):
        for _ in range(TOP_K):
            pltpu.make_async_copy(m_ref.at[s], xs_ref.at[pl.ds(0, tm * sub)], sem.at[s]).wait()

    @pl.when(step > 0)
    def _():
        drain(1 - slot)

    @pl.when(step == pl.num_programs(0) - 1)
    def _():
        drain(slot)


def _dispatch(dest, zflag, h1, gf, wg, wu, wd, *, tm, rb, n_pad):
    t, d = h1.shape
    f = wg.shape[1]
    sub = d // 2 // LANES
    row = lambda i: (i, 0)
    fix = lambda i: (0, 0)
    return pl.pallas_call(
        functools.partial(_dispatch_kernel, sub=sub),
        grid=(t // tm,),
        in_specs=[
            pl.BlockSpec((IDX_ROWS * tm,), lambda i: (i,), memory_space=pltpu.SMEM),
            pl.BlockSpec(memory_space=pltpu.SMEM),
            pl.BlockSpec((tm, d), row), pl.BlockSpec((1, d), fix),
            pl.BlockSpec((d, f), fix), pl.BlockSpec((d, f), fix), pl.BlockSpec((f, d), fix),
        ],
        out_specs=[pl.BlockSpec((tm, d), row), pl.BlockSpec(memory_space=pl.ANY)],
        out_shape=[jax.ShapeDtypeStruct((t, d), F32), jax.ShapeDtypeStruct((n_pad * sub, LANES), jnp.int32)],
        scratch_shapes=[pltpu.VMEM((2, tm * sub, LANES), jnp.int32), pltpu.VMEM((rb * sub, LANES), jnp.int32),
                        pltpu.SemaphoreType.DMA((2,)), pltpu.SemaphoreType.DMA(())],
        compiler_params=_params(("arbitrary",)),
        name="dispatch_shared",
    )(dest, zflag, h1, gf, wg, wu, wd)


def _experts_kernel(be_ref, first_ref, nxt_ref, nu_ref, xs_ref, wg_hbm, wu_hbm, wd_hbm, ys_ref,
                    wgf_ref, wuf_ref, wdf_ref, wgb_ref, wub_ref, wdb_ref, sem, *, sub):
    i = pl.program_id(0)
    tm = xs_ref.shape[0] // sub

    def weight_copies(e):
        return (pltpu.make_async_copy(wg_hbm.at[e], wgf_ref, sem.at[0]),
                pltpu.make_async_copy(wu_hbm.at[e], wuf_ref, sem.at[1]),
                pltpu.make_async_copy(wd_hbm.at[e], wdf_ref, sem.at[2]))

    @pl.when(i == 0)
    def _():
        for c in weight_copies(be_ref[0]):
            c.start()

    @pl.when((i < nu_ref[0]) & (first_ref[i] != 0))
    def _():
        for c in weight_copies(be_ref[i]):
            c.wait()
        wgb_ref[...] = wgf_ref[...].astype(BF16)
        wub_ref[...] = wuf_ref[...].astype(BF16)
        wdb_ref[...] = wdf_ref[...].astype(BF16)

        @pl.when(nxt_ref[i] >= 0)
        def _():
            for c in weight_copies(nxt_ref[i]):
                c.start()

    @pl.when(i < nu_ref[0])
    def _():
        lo, hi = _unpack_halves(_tiles_to_rows(xs_ref, tm, sub))
        lo = lo.astype(BF16)
        hi = hi.astype(BF16)
        n = lo.shape[1]
        a = (jnp.dot(lo, wgb_ref[:n, :], preferred_element_type=F32)
             + jnp.dot(hi, wgb_ref[n:, :], preferred_element_type=F32))
        b = (jnp.dot(lo, wub_ref[:n, :], preferred_element_type=F32)
             + jnp.dot(hi, wub_ref[n:, :], preferred_element_type=F32))
        hid = (a * _sigmoid(a) * b).astype(BF16)
        _rows_to_tiles(ys_ref, _pack_halves(jnp.dot(hid, wdb_ref[...], preferred_element_type=F32)), sub)

    @pl.when(i >= nu_ref[0])
    def _():
        ys_ref[...] = jnp.zeros(ys_ref.shape, ys_ref.dtype)


def _experts(blk_e, first, nxt, n_used, xs, wg, wu, wd, *, tm):
    rows_sub, _ = xs.shape
    _, d, f = wg.shape
    sub = d // 2 // LANES
    n_blocks = rows_sub // (tm * sub)
    rows = lambda i, be, fi, nx, nu: (jnp.maximum(jnp.minimum(i, nu[0] - 1), 0), 0)
    orow = lambda i, be, fi, nx, nu: (i, 0)
    return pl.pallas_call(
        functools.partial(_experts_kernel, sub=sub),
        grid_spec=pltpu.PrefetchScalarGridSpec(
            num_scalar_prefetch=4,
            grid=(n_blocks,),
            in_specs=[
                pl.BlockSpec((tm * sub, LANES), rows),
                pl.BlockSpec(memory_space=pl.ANY), pl.BlockSpec(memory_space=pl.ANY),
                pl.BlockSpec(memory_space=pl.ANY),
            ],
            out_specs=pl.BlockSpec((tm * sub, LANES), orow),
            scratch_shapes=[pltpu.VMEM((d, f), F32), pltpu.VMEM((d, f), F32), pltpu.VMEM((f, d), F32),
                            pltpu.VMEM((d, f), BF16), pltpu.VMEM((d, f), BF16), pltpu.VMEM((f, d), BF16),
                            pltpu.SemaphoreType.DMA((3,))],
        ),
        out_shape=jax.ShapeDtypeStruct(xs.shape, jnp.int32),
        compiler_params=_params(("arbitrary",)),
        name="routed_experts",
    )(blk_e, first, nxt, n_used, xs, wg, wu, wd)


def _final_kernel(dest_ref, dnext_ref, h_ref, gate_ref, p_ref, gp_ref, wgate_ref, wproj_ref, ys_ref, o_ref,
                  buf_ref, sem, *, sub):
    tm = h_ref.shape[0]
    step = pl.program_id(0)
    slot = lax.rem(step, 2)

    def gather_tile(idx_ref, s):
        def issue(t, carry):
            for k in range(TOP_K):
                pltpu.make_async_copy(_tile_rows(ys_ref, idx_ref[k * tm + t], sub),
                                      _tile_rows(buf_ref.at[s, k], t, sub), sem.at[s]).start()
            return carry

        lax.fori_loop(0, tm, issue, 0)

    @pl.when(step == 0)
    def _():
        gather_tile(dest_ref, slot)

    @pl.when(step + 1 < pl.num_programs(0))
    def _():
        gather_tile(dnext_ref, 1 - slot)

    pp = jnp.dot(p_ref[...].astype(BF16), wproj_ref[...], preferred_element_type=F32)
    for k in range(TOP_K):
        pltpu.make_async_copy(ys_ref.at[pl.ds(0, tm * sub)], buf_ref.at[slot, k], sem.at[slot]).wait()
    gate = gate_ref[...]
    h = h_ref[...]
    n = h.shape[1] // 2
    acc_lo = h[:, :n]
    acc_hi = h[:, n:]
    for k in range(TOP_K):
        lo, hi = _unpack_halves(_tiles_to_rows(buf_ref.at[slot, k], tm, sub))
        acc_lo = acc_lo + gate[:, k:k + 1] * lo
        acc_hi = acc_hi + gate[:, k:k + 1] * hi
    h2 = jnp.concatenate([acc_lo, acc_hi], axis=1)
    nrm = _rms(h2, gp_ref[...]).astype(BF16)
    o_ref[...] = h2 + _sigmoid(jnp.dot(nrm, wgate_ref[...], preferred_element_type=F32)) * pp


def _final(dest, h1s, gate, p2, gp, wgate, wproj, ys, *, tm):
    t, d = h1s.shape
    dp = p2.shape[1]
    sub = d // 2 // LANES
    row = lambda i: (i, 0)
    fix = lambda i: (0, 0)
    return pl.pallas_call(
        functools.partial(_final_kernel, sub=sub),
        grid=(t // tm,),
        in_specs=[
            pl.BlockSpec((IDX_ROWS * tm,), lambda i: (i,), memory_space=pltpu.SMEM),
            pl.BlockSpec((IDX_ROWS * tm,), lambda i: (jnp.minimum(i + 1, t // tm - 1),), memory_space=pltpu.SMEM),
            pl.BlockSpec((tm, d), row), pl.BlockSpec((tm, IDX_ROWS), row), pl.BlockSpec((tm, dp), row),
            pl.BlockSpec((1, d), fix), pl.BlockSpec((d, d), fix), pl.BlockSpec((dp, d), fix),
            pl.BlockSpec(memory_space=pl.ANY),
        ],
        out_specs=pl.BlockSpec((tm, d), row),
        out_shape=jax.ShapeDtypeStruct((t, d), F32),
        scratch_shapes=[pltpu.VMEM((2, TOP_K, tm * sub, LANES), jnp.int32), pltpu.SemaphoreType.DMA((2,))],
        compiler_params=_params(("arbitrary",)),
        name="combine_ple",
    )(dest, dest, h1s, gate, p2, gp, wgate, wproj, ys)


def _largest_tile(n, cap):
    t = min(n, cap)
    while n % t:
        t //= 2
    return t


def _layer(h, p_l, norm_mix, w_in, q_norm, k_norm, lam_re, lam_im, log_dt, b_re, b_im, c_re, c_im, d_skip,
           w_glu, attn_out_norm, ssm_out_norm, w_out, norm_ffn, w_router, router_bias, w_exp_gate,
           w_exp_up, w_exp_down, w_sh_gate, w_sh_up, w_sh_down, norm_ple, w_ple_gate, w_ple_proj):
    bsz, seq, d = h.shape
    t = bsz * seq
    head_dim = q_norm.shape[-1]
    d_attn = attn_out_norm.shape[-1]
    n_heads = d_attn // head_dim
    n_experts = w_router.shape[-1]
    x2 = h.reshape(t, d)

    tn = d_attn if d_attn % 128 == 0 else head_dim
    scale = 1.0 / math.sqrt(head_dim)
    hg = jnp.concatenate([jnp.tile(q_norm * scale, n_heads), jnp.tile(k_norm, n_heads),
                          jnp.ones((w_in.shape[1] - 2 * d_attn,), F32)])[None, :]
    z, u = _inproj(x2, norm_mix[None, :], w_in.astype(BF16), hg, d_qkv=3 * d_attn,
                   n_norm_tiles=2 * d_attn // tn, head_dim=head_dim, tm=_largest_tile(t, 1024), tn=tn)

    o_attn = _attention(z, batch=bsz, seq=seq, n_heads=n_heads, head_dim=head_dim, blk=_largest_tile(seq, 256))

    n_chunks = seq // S5_CHUNK
    mats = _s5_prepare(lam_re, lam_im, log_dt, b_re, b_im, c_re, c_im, d_skip, n_chunks)
    y_ssm = _s5(u, *mats, n_chunks=n_chunks)

    tm = _largest_tile(t, 256)
    w_out_b = w_out.astype(BF16)
    h1, idx, gate, loc, counts = _postmix(
        o_attn, y_ssm, x2, w_glu.astype(BF16), w_out_b[:d_attn], w_out_b[d_attn:],
        attn_out_norm[None, :], ssm_out_norm[None, :], norm_ffn[None, :], w_router, router_bias[None, :], tm=tm)

    rb = _largest_tile(t, 256)
    n_pad = t * TOP_K + n_experts * rb
    n_blocks = n_pad // rb
    cnt = counts[0].astype(jnp.int32)
    pcnt = (cnt + rb - 1) // rb * rb
    pend = jnp.cumsum(pcnt)
    pstart = pend - pcnt
    onehot = idx[:, :, None] == jnp.arange(n_experts, dtype=jnp.int32)[None, None, :]
    dest = jnp.sum(jnp.where(onehot, pstart[None, None, :], 0), axis=-1) + loc
    dest = dest.reshape(IDX_ROWS, t // tm, tm).transpose(1, 0, 2).reshape(-1)
    n_used = (pend[-1] // rb).astype(jnp.int32)
    blk_ids = jnp.arange(n_blocks, dtype=jnp.int32)
    starts = (blk_ids * rb)[:, None]
    owner = (starts >= pstart[None, :]) & (starts < pend[None, :])
    blk_e = jnp.sum(jnp.where(owner, jnp.arange(n_experts, dtype=jnp.int32)[None, :], 0), axis=1)
    is_last = jnp.any(owner & (starts + rb == pend[None, :]), axis=1)
    is_first = jnp.any(owner & (starts == pstart[None, :]), axis=1).astype(jnp.int32)
    zflag = ((blk_ids >= n_used) | is_last).astype(jnp.int32)
    first_pos = jnp.where(is_first > 0, blk_ids, n_blocks)
    nxt_pos = jnp.concatenate([lax.cummin(first_pos, axis=0, reverse=True)[1:],
                               jnp.full((1,), n_blocks, jnp.int32)])
    nxt_hit = nxt_pos[:, None] == blk_ids[None, :]
    nxt = jnp.where(nxt_pos < n_blocks, jnp.sum(jnp.where(nxt_hit, blk_e[None, :], 0), axis=1), -1)
    last_e = jnp.sum(jnp.where(blk_ids == n_used - 1, blk_e, 0))
    blk_e = jnp.where(blk_ids < n_used, blk_e, last_e).astype(jnp.int32)

    h1s, xs = _dispatch(dest, zflag, h1, norm_ffn[None, :], w_sh_gate.astype(BF16), w_sh_up.astype(BF16),
                        w_sh_down.astype(BF16), tm=tm, rb=rb, n_pad=n_pad)
    ys = _experts(blk_e, is_first, nxt.astype(jnp.int32), n_used[None], xs, w_exp_gate, w_exp_up, w_exp_down,
                  tm=rb)
    out = _final(dest, h1s, gate, p_l.reshape(t, -1), norm_ple[None, :], w_ple_gate.astype(BF16),
                 w_ple_proj.astype(BF16), ys, tm=tm)
    return out.reshape(bsz, seq, d)


def kernel(x, p, norm_mix, w_in, q_norm, k_norm, ssm_lam_re, ssm_lam_im, ssm_log_dt, ssm_b_re, ssm_b_im,
           ssm_c_re, ssm_c_im, ssm_d, w_glu, attn_out_norm, ssm_out_norm, w_out, norm_ffn, w_router,
           router_bias, w_exp_gate, w_exp_up, w_exp_down, w_sh_gate, w_sh_up, w_sh_down, norm_ple,
           w_ple_gate, w_ple_proj):
    h = x
    for i in range(p.shape[0]):
        h = _layer(h, p[i], norm_mix[i], w_in[i], q_norm[i], k_norm[i], ssm_lam_re[i], ssm_lam_im[i],
                   ssm_log_dt[i], ssm_b_re[i], ssm_b_im[i], ssm_c_re[i], ssm_c_im[i], ssm_d[i], w_glu[i],
                   attn_out_norm[i], ssm_out_norm[i], w_out[i], norm_ffn[i], w_router[i], router_bias[i],
                   w_exp_gate[i], w_exp_up[i], w_exp_down[i], w_sh_gate[i], w_sh_up[i], w_sh_down[i],
                   norm_ple[i], w_ple_gate[i], w_ple_proj[i])
    return h
```

```python
import functools
import math

import jax
import jax.numpy as jnp
from jax import lax
from jax.experimental import pallas as pl
from jax.experimental.pallas import tpu as pltpu

NORM_EPS = 1e-6
TOP_K = 6
ROUTED_SCALE = 2.5
IDX_ROWS = 8
LANES = 128
S5_CHUNK = 8
VMEM_LIMIT = 56 * 1024 * 1024

F32 = jnp.float32
BF16 = jnp.bfloat16
HIGH_HALF = -65536


def _rms(x, g):
    return x * lax.rsqrt(jnp.mean(x * x, axis=-1, keepdims=True) + NORM_EPS) * g


def _sigmoid(x):
    return 1.0 / (1.0 + jnp.exp(-x))


def _params(sem):
    return pltpu.CompilerParams(dimension_semantics=sem, vmem_limit_bytes=VMEM_LIMIT)


def _pack_halves(x):
    n = x.shape[1] // 2
    lo = lax.bitcast_convert_type(x[:, :n].astype(BF16).astype(F32), jnp.int32)
    hi = lax.bitcast_convert_type(x[:, n:].astype(BF16).astype(F32), jnp.int32)
    return (hi & HIGH_HALF) | lax.shift_right_logical(lo, 16)


def _unpack_halves(p):
    lo = lax.bitcast_convert_type(lax.shift_left(p, 16), F32)
    hi = lax.bitcast_convert_type(p & HIGH_HALF, F32)
    return lo, hi


def _inproj_kernel(x_ref, g_ref, w_ref, hg_ref, o_ref, u_ref, xn_ref, *, n_norm_tiles, n_qkv_tiles, head_dim):
    j = pl.program_id(1)

    @pl.when(j == 0)
    def _():
        xn_ref[...] = _rms(x_ref[...], g_ref[...]).astype(BF16)

    acc = jnp.dot(xn_ref[...], w_ref[...], preferred_element_type=F32)
    tn = acc.shape[1]

    @pl.when(j < n_norm_tiles)
    def _():
        hg = hg_ref[...]
        for h in range(tn // head_dim):
            sl = slice(h * head_dim, (h + 1) * head_dim)
            o_ref[:, sl] = _rms(acc[:, sl], hg[:, sl]).astype(o_ref.dtype)

    @pl.when((j >= n_norm_tiles) & (j < n_qkv_tiles))
    def _():
        o_ref[...] = acc.astype(o_ref.dtype)

    @pl.when(j >= n_qkv_tiles)
    def _():
        u_ref[...] = acc


def _inproj(x2, g, w, hg, *, d_qkv, n_norm_tiles, head_dim, tm, tn):
    t, d = x2.shape
    n = w.shape[1]
    nq = d_qkv // tn
    return pl.pallas_call(
        functools.partial(_inproj_kernel, n_norm_tiles=n_norm_tiles, n_qkv_tiles=nq, head_dim=head_dim),
        grid=(t // tm, n // tn),
        in_specs=[
            pl.BlockSpec((tm, d), lambda i, j: (i, 0)),
            pl.BlockSpec((1, d), lambda i, j: (0, 0)),
            pl.BlockSpec((d, tn), lambda i, j: (0, j)),
            pl.BlockSpec((1, tn), lambda i, j: (0, j)),
        ],
        out_specs=[
            pl.BlockSpec((tm, tn), lambda i, j: (i, jnp.minimum(j, nq - 1))),
            pl.BlockSpec((tm, tn), lambda i, j: (i, jnp.maximum(j - nq, 0))),
        ],
        out_shape=[jax.ShapeDtypeStruct((t, d_qkv), BF16), jax.ShapeDtypeStruct((t, n - d_qkv), F32)],
        scratch_shapes=[pltpu.VMEM((tm, d), BF16)],
        compiler_params=_params(("parallel", "arbitrary")),
        name="inproj",
    )(x2, g, w, hg)


SKIP_AFTER = 110.0


def _attn_kernel(q_ref, k_ref, v_ref, o_ref, *, blk, hd, heads):
    i = pl.program_id(2)
    row = lax.broadcasted_iota(jnp.int32, (blk, blk), 0)
    col = lax.broadcasted_iota(jnp.int32, (blk, blk), 1)
    causal = col < row
    later = jnp.where(row > col, 1.0, 0.0).astype(BF16)
    lanes = [slice(h * hd, (h + 1) * hd) for h in range(heads)]
    qs = [q_ref[:, sl] for sl in lanes]

    def step(kb, accs, runs, diag):
        start = pl.multiple_of(kb * blk, blk)
        new_accs, new_runs = [], []
        for h, sl in enumerate(lanes):
            k = k_ref[pl.ds(start, blk), sl]
            v = v_ref[pl.ds(start, blk), sl]
            z = lax.dot_general(qs[h], k, (((1,), (1,)), ((), ())), preferred_element_type=F32)
            sp = jnp.maximum(z, 0.0) + jnp.log(1.0 + jnp.exp(-jnp.abs(z)))
            if diag:
                sp = jnp.where(causal, sp, 0.0)
            hi = sp.astype(BF16)
            lo = (sp - hi.astype(F32)).astype(BF16)
            inner = (jnp.dot(hi, later, preferred_element_type=F32)
                     + jnp.dot(lo, later, preferred_element_type=F32))
            w = jnp.exp(z - sp - (inner + runs[h]))
            if diag:
                w = jnp.where(causal, w, 0.0)
            new_accs.append(accs[h] + jnp.dot(w.astype(BF16), v, preferred_element_type=F32))
            new_runs.append(runs[h] + jnp.sum(sp, axis=-1, keepdims=True))
        return tuple(new_accs), tuple(new_runs)

    def keep_going(runs):
        low = runs[0]
        for r in runs[1:]:
            low = jnp.minimum(low, r)
        return (jnp.min(low) < SKIP_AFTER).astype(jnp.int32)

    accs = tuple(jnp.zeros((blk, hd), F32) for _ in lanes)
    runs = tuple(jnp.zeros((blk, 1), F32) for _ in lanes)
    accs, runs = step(i, accs, runs, True)

    def cond(c):
        n, go, _, _ = c
        return (n < i) & (go > 0)

    def body(c):
        n, _, accs, runs = c
        accs, runs = step(i - 1 - n, accs, runs, False)
        return n + 1, keep_going(runs), accs, runs

    _, _, accs, _ = lax.while_loop(cond, body, (jnp.int32(0), keep_going(runs), accs, runs))
    for h, sl in enumerate(lanes):
        o_ref[:, sl] = accs[h].astype(o_ref.dtype)


def _attention(z, *, batch, seq, n_heads, head_dim, blk):
    nq = seq // blk
    heads = 2 if n_heads % 2 == 0 else 1
    hw = heads * head_dim
    ng = n_heads // heads
    return pl.pallas_call(
        functools.partial(_attn_kernel, blk=blk, hd=head_dim, heads=heads),
        grid=(batch, ng, nq),
        in_specs=[
            pl.BlockSpec((blk, hw), lambda b, h, i: (b * nq + i, h)),
            pl.BlockSpec((seq, hw), lambda b, h, i: (b, ng + h)),
            pl.BlockSpec((seq, hw), lambda b, h, i: (b, 2 * ng + h)),
        ],
        out_specs=pl.BlockSpec((blk, hw), lambda b, h, i: (b * nq + i, h)),
        out_shape=jax.ShapeDtypeStruct((batch * seq, n_heads * head_dim), BF16),
        compiler_params=_params(("parallel", "parallel", "arbitrary")),
        name="sb_attention",
    )(z, z, z)


def _s5_prepare(lam_re, lam_im, log_dt, b_re, b_im, c_re, c_im, d_skip, n_chunks):
    hp = lax.Precision.HIGHEST
    chunk = S5_CHUNK
    g, p = lam_re.shape
    h = b_re.shape[-1]
    gg = LANES // h
    o = g // gg
    dt = jnp.exp(log_dt)[:, None]
    ks = jnp.arange(chunk + 1, dtype=F32)[None, :, None]
    mag = jnp.exp(lam_re[:, None, :] * dt[:, None, :] * ks)
    ang = lam_im[:, None, :] * dt[:, None, :] * ks
    ak_re = mag * jnp.cos(ang)
    ak_im = mag * jnp.sin(ang)
    nr = ak_re[:, 1] - 1.0
    ni = ak_im[:, 1]
    den = lam_re * lam_re + lam_im * lam_im
    coef_re = ((nr * lam_re + ni * lam_im) / den)[..., None]
    coef_im = ((ni * lam_re - nr * lam_im) / den)[..., None]
    bbar_re = coef_re * b_re - coef_im * b_im
    bbar_im = coef_re * b_im + coef_im * b_re
    ca_re = c_re[:, None] * ak_re[:, :, None, :] - c_im[:, None] * ak_im[:, :, None, :]
    ca_im = c_re[:, None] * ak_im[:, :, None, :] + c_im[:, None] * ak_re[:, :, None, :]

    cam_re = ca_re[:, :chunk].transpose(0, 3, 1, 2).reshape(g, p, chunk * h)
    cam_im = ca_im[:, :chunk].transpose(0, 3, 1, 2).reshape(g, p, chunk * h)
    kt = (jnp.einsum("gph,gpm->ghm", bbar_re, cam_re, precision=hp)
          - jnp.einsum("gph,gpm->ghm", bbar_im, cam_im, precision=hp))
    def same_group(n_rows, row_block, n_cols, col_block):
        r = (jnp.arange(n_rows) // row_block) % gg
        c = (jnp.arange(n_cols) // col_block) % gg
        return (r[:, None] == c[None, :]).astype(F32)

    tile_h = jnp.tile(jnp.eye(h, dtype=F32), (1, gg))
    tile_p = jnp.tile(jnp.eye(p, dtype=F32), (1, gg))
    kt = kt.reshape(o, gg, h, chunk, h).transpose(0, 3, 1, 2, 4).reshape(o, chunk, LANES, h)
    bd = jnp.einsum("okrh,hc->okrc", kt, tile_h, precision=hp) * same_group(LANES, h, LANES, h)
    dvec = d_skip.reshape(o, LANES)
    lags = [bd[:, k] for k in range(chunk)]
    lags[0] = lags[0] + dvec[:, :, None] * jnp.eye(LANES, dtype=F32)[None]
    zero_blk = jnp.zeros_like(lags[0])
    m_intra = jnp.concatenate(
        [jnp.concatenate([lags[j - i] if j >= i else zero_blk for j in range(chunk)], axis=2)
         for i in range(chunk)], axis=1)

    def expand_in(ak_part_a, ak_part_b, b_a, b_b, sign):
        rk_a = jnp.stack([ak_part_a[:, chunk - 1 - i] for i in range(chunk)], axis=1)
        rk_b = jnp.stack([ak_part_b[:, chunk - 1 - i] for i in range(chunk)], axis=1)
        val = (rk_a[:, :, None, :] * b_a.transpose(0, 2, 1)[:, None]
               + sign * rk_b[:, :, None, :] * b_b.transpose(0, 2, 1)[:, None])
        val = val.reshape(o, gg, chunk, h, p).transpose(0, 2, 1, 3, 4).reshape(o, chunk * LANES, p)
        return jnp.einsum("orp,pc->orc", val, tile_p, precision=hp) * same_group(chunk * LANES, h, gg * p, p)

    m_in = jnp.concatenate([expand_in(ak_re, ak_im, bbar_re, bbar_im, -1.0),
                            expand_in(ak_re, ak_im, bbar_im, bbar_re, 1.0)], axis=-1)

    def expand_out(ca):
        val = ca[:, 1:].reshape(o, gg, chunk, h, p).transpose(0, 4, 2, 1, 3).reshape(o, p, chunk * LANES)
        return jnp.einsum("pc,opr->ocr", tile_p, val, precision=hp) * same_group(gg * p, p, chunk * LANES, h)

    m_out = jnp.concatenate([expand_out(ca_re), -expand_out(ca_im)], axis=1)

    steps = max(1, int(math.ceil(math.log2(max(n_chunks, 2)))))
    cr, ci = ak_re[:, chunk].reshape(o, gg * p), ak_im[:, chunk].reshape(o, gg * p)
    sc_a, sc_b = [], []
    for _ in range(steps):
        sc_a.append(jnp.concatenate([cr, cr], axis=-1))
        sc_b.append(jnp.concatenate([-ci, ci], axis=-1))
        cr, ci = cr * cr - ci * ci, 2.0 * cr * ci
    return (m_intra.astype(BF16), m_in.astype(BF16), m_out.astype(BF16),
            jnp.stack(sc_a, axis=1), jnp.stack(sc_b, axis=1))


def _s5_kernel(u_ref, mi_ref, min_ref, mout_ref, sa_ref, sb_ref, y_ref, *, n_chunks, steps):
    chunk = S5_CHUNK
    n = u_ref.shape[0] // chunk
    u = jnp.concatenate([u_ref[pl.ds(i, n, stride=chunk), :] for i in range(chunk)], axis=1).astype(BF16)
    y = jnp.dot(u, mi_ref[...], preferred_element_type=F32)
    x = jnp.dot(u, min_ref[...], preferred_element_type=F32)
    half = x.shape[1] // 2
    c = lax.rem(lax.broadcasted_iota(jnp.int32, x.shape, 0), n_chunks)
    sa = sa_ref[...]
    sb = sb_ref[...]
    for k in range(steps):
        sh = 1 << k
        xs = jnp.where(c >= sh, pltpu.roll(x, sh, axis=0), 0.0)
        x = x + xs * sa[k:k + 1, :] + pltpu.roll(xs, half, axis=1) * sb[k:k + 1, :]
    s_in = jnp.where(c >= 1, pltpu.roll(x, 1, axis=0), 0.0)
    y = y + jnp.dot(s_in.astype(BF16), mout_ref[...], preferred_element_type=F32)
    for i in range(chunk):
        y_ref[pl.ds(i, n, stride=chunk), :] = y[:, i * LANES:(i + 1) * LANES]


def _s5(u, m_intra, m_in, m_out, sc_a, sc_b, *, n_chunks):
    t, d_ssm = u.shape
    o, lh, st = m_in.shape
    steps = sc_a.shape[1]
    mat = lambda i: (i, 0, 0)
    return pl.pallas_call(
        functools.partial(_s5_kernel, n_chunks=n_chunks, steps=steps),
        grid=(o,),
        in_specs=[
            pl.BlockSpec((t, LANES), lambda i: (0, i)),
            pl.BlockSpec((None, lh, lh), mat), pl.BlockSpec((None, lh, st), mat), pl.BlockSpec((None, st, lh), mat),
            pl.BlockSpec((None, steps, st), mat), pl.BlockSpec((None, steps, st), mat),
        ],
        out_specs=pl.BlockSpec((t, LANES), lambda i: (0, i)),
        out_shape=jax.ShapeDtypeStruct((t, d_ssm), F32),
        compiler_params=_params(("parallel",)),
        name="s5_chunked",
    )(u, m_intra, m_in, m_out, sc_a, sc_b)


def _postmix_kernel(oa_ref, ys_ref, x_ref, wglu_ref, woa_ref, wob_ref, ga_ref, gs_ref, gf_ref,
                    wr_ref, rb_ref, h1_ref, idx_ref, gate_ref, loc_ref, cnt_ref, run_ref, *, n_experts):
    step = pl.program_id(0)

    @pl.when(step == 0)
    def _():
        run_ref[...] = jnp.zeros(run_ref.shape, F32)

    y = ys_ref[...]
    y = 0.5 * y * (1.0 + jnp.tanh(math.sqrt(2.0 / math.pi) * (y + 0.044715 * (y * y * y))))
    y = y * _sigmoid(jnp.dot(y.astype(BF16), wglu_ref[...], preferred_element_type=F32))
    na = _rms(oa_ref[...].astype(F32), ga_ref[...]).astype(BF16)
    ns = _rms(y, gs_ref[...]).astype(BF16)
    h1 = (x_ref[...] + jnp.dot(na, woa_ref[...], preferred_element_type=F32)
          + jnp.dot(ns, wob_ref[...], preferred_element_type=F32))
    h1_ref[...] = h1

    m = _rms(h1, gf_ref[...])
    m_hi = m.astype(BF16)
    m_lo = (m - m_hi.astype(F32)).astype(BF16)
    r_hi = jnp.dot(m_hi, wr_ref[...], preferred_element_type=F32)
    r_lo = jnp.dot(m_lo, wr_ref[...], preferred_element_type=F32)
    logits = r_hi[:, :n_experts] + r_hi[:, n_experts:] + r_lo[:, :n_experts]
    scores = _sigmoid(logits)
    vals = scores + rb_ref[...]
    tm = vals.shape[0]
    lane = lax.broadcasted_iota(jnp.int32, vals.shape, 1).astype(F32)
    picks, firsts, raws = [], [], []
    sel = jnp.zeros(vals.shape, F32)
    for _ in range(TOP_K):
        mx = jnp.max(vals, axis=1, keepdims=True)
        first = jnp.min(jnp.where(vals == mx, lane, float(n_experts)), axis=1, keepdims=True)
        pick = lane == first
        picks.append(pick)
        firsts.append(first)
        raws.append(jnp.sum(jnp.where(pick, scores, 0.0), axis=1, keepdims=True))
        vals = jnp.where(pick, -jnp.inf, vals)
        sel = jnp.where(pick, 1.0, sel)
    denom = raws[0]
    for r in raws[1:]:
        denom = denom + r

    r_i = lax.broadcasted_iota(jnp.int32, (tm, tm), 0)
    c_i = lax.broadcasted_iota(jnp.int32, (tm, tm), 1)
    before = jnp.where(c_i < r_i, 1.0, 0.0).astype(BF16)
    rank = jnp.dot(before, sel.astype(BF16), preferred_element_type=F32) + run_ref[...]
    run_ref[...] = run_ref[...] + jnp.sum(sel, axis=0, keepdims=True)
    cnt_ref[...] = run_ref[...]

    slot = lax.broadcasted_iota(jnp.int32, (tm, LANES), 1)
    idx_w = jnp.zeros((tm, LANES), F32)
    loc_w = jnp.zeros((tm, LANES), F32)
    gate_w = jnp.zeros((tm, LANES), F32)
    for k in range(TOP_K):
        loc = jnp.sum(jnp.where(picks[k], rank, 0.0), axis=1, keepdims=True)
        idx_w = jnp.where(slot == k, firsts[k], idx_w)
        loc_w = jnp.where(slot == k, loc, loc_w)
        gate_w = jnp.where(slot == k, raws[k] / denom * ROUTED_SCALE, gate_w)
    idx_ref[...] = idx_w.T[:IDX_ROWS].astype(jnp.int32)
    loc_ref[...] = loc_w.T[:IDX_ROWS].astype(jnp.int32)
    gate_ref[...] = gate_w[:, :IDX_ROWS]


def _postmix(oa, ys, x2, wglu, woa, wob, ga, gs, gf, wr, rb, *, tm):
    t, d = x2.shape
    da = oa.shape[1]
    dsm = ys.shape[1]
    e = rb.shape[1]
    row = lambda i: (i, 0)
    col = lambda i: (0, i)
    fix = lambda i: (0, 0)
    return pl.pallas_call(
        functools.partial(_postmix_kernel, n_experts=e),
        grid=(t // tm,),
        in_specs=[
            pl.BlockSpec((tm, da), row), pl.BlockSpec((tm, dsm), row), pl.BlockSpec((tm, d), row),
            pl.BlockSpec((dsm, dsm), fix), pl.BlockSpec((da, d), fix), pl.BlockSpec((dsm, d), fix),
            pl.BlockSpec((1, da), fix), pl.BlockSpec((1, dsm), fix), pl.BlockSpec((1, d), fix),
            pl.BlockSpec((d, 2 * e), fix), pl.BlockSpec((1, e), fix),
        ],
        out_specs=[
            pl.BlockSpec((tm, d), row), pl.BlockSpec((IDX_ROWS, tm), col), pl.BlockSpec((tm, IDX_ROWS), row),
            pl.BlockSpec((IDX_ROWS, tm), col), pl.BlockSpec((1, e), fix),
        ],
        out_shape=[
            jax.ShapeDtypeStruct((t, d), F32), jax.ShapeDtypeStruct((IDX_ROWS, t), jnp.int32),
            jax.ShapeDtypeStruct((t, IDX_ROWS), F32), jax.ShapeDtypeStruct((IDX_ROWS, t), jnp.int32),
            jax.ShapeDtypeStruct((1, e), F32),
        ],
        scratch_shapes=[pltpu.VMEM((1, e), F32)],
        compiler_params=_params(("arbitrary",)),
        name="postmix_router",
    )(oa, ys, x2, wglu, woa, wob, ga, gs, gf, wr, rb)


def _rows_to_tiles(ref, packed, sub):
    rows = packed.shape[0]
    for s in range(sub):
        ref[pl.ds(s, rows, stride=sub), :] = packed[:, s * LANES:(s + 1) * LANES]


def _tiles_to_rows(ref, rows, sub):
    return jnp.concatenate([ref[pl.ds(s, rows, stride=sub), :] for s in range(sub)], axis=1)


def _tile_rows(ref, r, sub):
    return ref.at[pl.ds(pl.multiple_of(r * sub, sub), sub)]


def _dispatch_kernel(dest_ref, zflag_ref, h1_ref, gf_ref, wg_ref, wu_ref, wd_ref, h1s_ref, xs_ref,
                     m_ref, zero_ref, sem, zsem, *, sub):
    tm = h1_ref.shape[0]
    zrows = zero_ref.shape[0]
    n_blocks = zflag_ref.shape[0]

    @pl.when(pl.program_id(0) == 0)
    def _():
        zero_ref[...] = jnp.zeros(zero_ref.shape, zero_ref.dtype)

        def zcopy(b):
            return pltpu.make_async_copy(zero_ref, xs_ref.at[pl.ds(pl.multiple_of(b * zrows, zrows), zrows)], zsem)

        def zstart(b, carry):
            @pl.when(zflag_ref[b] != 0)
            def _():
                zcopy(b).start()
            return carry

        def zwait(b, carry):
            @pl.when(zflag_ref[b] != 0)
            def _():
                zcopy(b).wait()
            return carry

        lax.fori_loop(0, n_blocks, zstart, 0)
        lax.fori_loop(0, n_blocks, zwait, 0)

    step = pl.program_id(0)
    slot = lax.rem(step, 2)
    mslot = m_ref.at[slot]
    h1 = h1_ref[...]
    m = _rms(h1, gf_ref[...])
    _rows_to_tiles(mslot, _pack_halves(m), sub)

    def issue(t, carry):
        src = _tile_rows(mslot, t, sub)
        for k in range(TOP_K):
            pltpu.make_async_copy(src, _tile_rows(xs_ref, dest_ref[k * tm + t], sub), sem.at[slot]).start()
        return carry

    lax.fori_loop(0, tm, issue, 0)
    mb = m.astype(BF16)
    a = jnp.dot(mb, wg_ref[...], preferred_element_type=F32)
    b = jnp.dot(mb, wu_ref[...], preferred_element_type=F32)
    hid = (a * _sigmoid(a) * b).astype(BF16)
    h1s_ref[...] = h1 + jnp.dot(hid, wd_ref[...], preferred_element_type=F32)

    def drain(s):
        for _ in range(TOP_K):
            pltpu.make_async_copy(m_ref.at[s], xs_ref.at[pl.ds(0, tm * sub)], sem.at[s]).wait()

    @pl.when(step > 0)
    def _():
        drain(1 - slot)

    @pl.when(step == pl.num_programs(0) - 1)
    def _():
        drain(slot)


def _dispatch(dest, zflag, h1, gf, wg, wu, wd, *, tm, rb, n_pad):
    t, d = h1.shape
    f = wg.shape[1]
    sub = d // 2 // LANES
    row = lambda i: (i, 0)
    fix = lambda i: (0, 0)
    return pl.pallas_call(
        functools.partial(_dispatch_kernel, sub=sub),
        grid=(t // tm,),
        in_specs=[
            pl.BlockSpec((IDX_ROWS * tm,), lambda i: (i,), memory_space=pltpu.SMEM),
            pl.BlockSpec(memory_space=pltpu.SMEM),
            pl.BlockSpec((tm, d), row), pl.BlockSpec((1, d), fix),
            pl.BlockSpec((d, f), fix), pl.BlockSpec((d, f), fix), pl.BlockSpec((f, d), fix),
        ],
        out_specs=[pl.BlockSpec((tm, d), row), pl.BlockSpec(memory_space=pl.ANY)],
        out_shape=[jax.ShapeDtypeStruct((t, d), F32), jax.ShapeDtypeStruct((n_pad * sub, LANES), jnp.int32)],
        scratch_shapes=[pltpu.VMEM((2, tm * sub, LANES), jnp.int32), pltpu.VMEM((rb * sub, LANES), jnp.int32),
                        pltpu.SemaphoreType.DMA((2,)), pltpu.SemaphoreType.DMA(())],
        compiler_params=_params(("arbitrary",)),
        name="dispatch_shared",
    )(dest, zflag, h1, gf, wg, wu, wd)


def _experts_kernel(be_ref, first_ref, nxt_ref, nu_ref, xs_ref, wg_hbm, wu_hbm, wd_hbm, ys_ref,
                    wgf_ref, wuf_ref, wdf_ref, wgb_ref, wub_ref, wdb_ref, sem, *, sub):
    i = pl.program_id(0)
    tm = xs_ref.shape[0] // sub

    def weight_copies(e):
        return (pltpu.make_async_copy(wg_hbm.at[e], wgf_ref, sem.at[0]),
                pltpu.make_async_copy(wu_hbm.at[e], wuf_ref, sem.at[1]),
                pltpu.make_async_copy(wd_hbm.at[e], wdf_ref, sem.at[2]))

    @pl.when(i == 0)
    def _():
        for c in weight_copies(be_ref[0]):
            c.start()

    @pl.when((i < nu_ref[0]) & (first_ref[i] != 0))
    def _():
        for c in weight_copies(be_ref[i]):
            c.wait()
        wgb_ref[...] = wgf_ref[...].astype(BF16)
        wub_ref[...] = wuf_ref[...].astype(BF16)
        wdb_ref[...] = wdf_ref[...].astype(BF16)

        @pl.when(nxt_ref[i] >= 0)
        def _():
            for c in weight_copies(nxt_ref[i]):
                c.start()

    @pl.when(i < nu_ref[0])
    def _():
        lo, hi = _unpack_halves(_tiles_to_rows(xs_ref, tm, sub))
        lo = lo.astype(BF16)
        hi = hi.astype(BF16)
        n = lo.shape[1]
        a = (jnp.dot(lo, wgb_ref[:n, :], preferred_element_type=F32)
             + jnp.dot(hi, wgb_ref[n:, :], preferred_element_type=F32))
        b = (jnp.dot(lo, wub_ref[:n, :], preferred_element_type=F32)
             + jnp.dot(hi, wub_ref[n:, :], preferred_element_type=F32))
        hid = (a * _sigmoid(a) * b).astype(BF16)
        _rows_to_tiles(ys_ref, _pack_halves(jnp.dot(hid, wdb_ref[...], preferred_element_type=F32)), sub)

    @pl.when(i >= nu_ref[0])
    def _():
        ys_ref[...] = jnp.zeros(ys_ref.shape, ys_ref.dtype)


def _experts(blk_e, first, nxt, n_used, xs, wg, wu, wd, *, tm):
    rows_sub, _ = xs.shape
    _, d, f = wg.shape
    sub = d // 2 // LANES
    n_blocks = rows_sub // (tm * sub)
    rows = lambda i, be, fi, nx, nu: (jnp.maximum(jnp.minimum(i, nu[0] - 1), 0), 0)
    orow = lambda i, be, fi, nx, nu: (i, 0)
    return pl.pallas_call(
        functools.partial(_experts_kernel, sub=sub),
        grid_spec=pltpu.PrefetchScalarGridSpec(
            num_scalar_prefetch=4,
            grid=(n_blocks,),
            in_specs=[
                pl.BlockSpec((tm * sub, LANES), rows),
                pl.BlockSpec(memory_space=pl.ANY), pl.BlockSpec(memory_space=pl.ANY),
                pl.BlockSpec(memory_space=pl.ANY),
            ],
            out_specs=pl.BlockSpec((tm * sub, LANES), orow),
            scratch_shapes=[pltpu.VMEM((d, f), F32), pltpu.VMEM((d, f), F32), pltpu.VMEM((f, d), F32),
                            pltpu.VMEM((d, f), BF16), pltpu.VMEM((d, f), BF16), pltpu.VMEM((f, d), BF16),
                            pltpu.SemaphoreType.DMA((3,))],
        ),
        out_shape=jax.ShapeDtypeStruct(xs.shape, jnp.int32),
        compiler_params=_params(("arbitrary",)),
        name="routed_experts",
    )(blk_e, first, nxt, n_used, xs, wg, wu, wd)


def _final_kernel(dest_ref, dnext_ref, h_ref, gate_ref, p_ref, gp_ref, wgate_ref, wproj_ref, ys_ref, o_ref,
                  buf_ref, sem, *, sub):
    tm = h_ref.shape[0]
    step = pl.program_id(0)
    slot = lax.rem(step, 2)

    def gather_tile(idx_ref, s):
        def issue(t, carry):
            for k in range(TOP_K):
                pltpu.make_async_copy(_tile_rows(ys_ref, idx_ref[k * tm + t], sub),
                                      _tile_rows(buf_ref.at[s, k], t, sub), sem.at[s]).start()
            return carry

        lax.fori_loop(0, tm, issue, 0)

    @pl.when(step == 0)
    def _():
        gather_tile(dest_ref, slot)

    @pl.when(step + 1 < pl.num_programs(0))
    def _():
        gather_tile(dnext_ref, 1 - slot)

    pp = jnp.dot(p_ref[...].astype(BF16), wproj_ref[...], preferred_element_type=F32)
    for k in range(TOP_K):
        pltpu.make_async_copy(ys_ref.at[pl.ds(0, tm * sub)], buf_ref.at[slot, k], sem.at[slot]).wait()
    gate = gate_ref[...]
    h = h_ref[...]
    n = h.shape[1] // 2
    acc_lo = h[:, :n]
    acc_hi = h[:, n:]
    for k in range(TOP_K):
        lo, hi = _unpack_halves(_tiles_to_rows(buf_ref.at[slot, k], tm, sub))
        acc_lo = acc_lo + gate[:, k:k + 1] * lo
        acc_hi = acc_hi + gate[:, k:k + 1] * hi
    h2 = jnp.concatenate([acc_lo, acc_hi], axis=1)
    nrm = _rms(h2, gp_ref[...]).astype(BF16)
    o_ref[...] = h2 + _sigmoid(jnp.dot(nrm, wgate_ref[...], preferred_element_type=F32)) * pp


def _final(dest, h1s, gate, p2, gp, wgate, wproj, ys, *, tm):
    t, d = h1s.shape
    dp = p2.shape[1]
    sub = d // 2 // LANES
    row = lambda i: (i, 0)
    fix = lambda i: (0, 0)
    return pl.pallas_call(
        functools.partial(_final_kernel, sub=sub),
        grid=(t // tm,),
        in_specs=[
            pl.BlockSpec((IDX_ROWS * tm,), lambda i: (i,), memory_space=pltpu.SMEM),
            pl.BlockSpec((IDX_ROWS * tm,), lambda i: (jnp.minimum(i + 1, t // tm - 1),), memory_space=pltpu.SMEM),
            pl.BlockSpec((tm, d), row), pl.BlockSpec((tm, IDX_ROWS), row), pl.BlockSpec((tm, dp), row),
            pl.BlockSpec((1, d), fix), pl.BlockSpec((d, d), fix), pl.BlockSpec((dp, d), fix),
            pl.BlockSpec(memory_space=pl.ANY),
        ],
        out_specs=pl.BlockSpec((tm, d), row),
        out_shape=jax.ShapeDtypeStruct((t, d), F32),
        scratch_shapes=[pltpu.VMEM((2, TOP_K, tm * sub, LANES), jnp.int32), pltpu.SemaphoreType.DMA((2,))],
        compiler_params=_params(("arbitrary",)),
        name="combine_ple",
    )(dest, dest, h1s, gate, p2, gp, wgate, wproj, ys)


def _largest_tile(n, cap):
    t = min(n, cap)
    while n % t:
        t //= 2
    return t


def _layer(h, p_l, norm_mix, w_in, q_norm, k_norm, lam_re, lam_im, log_dt, b_re, b_im, c_re, c_im, d_skip,
           w_glu, attn_out_norm, ssm_out_norm, w_out, norm_ffn, w_router, router_bias, w_exp_gate,
           w_exp_up, w_exp_down, w_sh_gate, w_sh_up, w_sh_down, norm_ple, w_ple_gate, w_ple_proj):
    bsz, seq, d = h.shape
    t = bsz * seq
    head_dim = q_norm.shape[-1]
    d_attn = attn_out_norm.shape[-1]
    n_heads = d_attn // head_dim
    n_experts = w_router.shape[-1]
    x2 = h.reshape(t, d)

    tn = d_attn if d_attn % 128 == 0 else head_dim
    scale = 1.0 / math.sqrt(head_dim)
    hg = jnp.concatenate([jnp.tile(q_norm * scale, n_heads), jnp.tile(k_norm, n_heads),
                          jnp.ones((w_in.shape[1] - 2 * d_attn,), F32)])[None, :]
    z, u = _inproj(x2, norm_mix[None, :], w_in.astype(BF16), hg, d_qkv=3 * d_attn,
                   n_norm_tiles=2 * d_attn // tn, head_dim=head_dim, tm=_largest_tile(t, 1024), tn=tn)

    o_attn = _attention(z, batch=bsz, seq=seq, n_heads=n_heads, head_dim=head_dim, blk=_largest_tile(seq, 256))

    n_chunks = seq // S5_CHUNK
    mats = _s5_prepare(lam_re, lam_im, log_dt, b_re, b_im, c_re, c_im, d_skip, n_chunks)
    y_ssm = _s5(u, *mats, n_chunks=n_chunks)

    tm = _largest_tile(t, 256)
    w_out_b = w_out.astype(BF16)
    wr_hi = w_router.astype(BF16)
    wr_lo = (w_router - wr_hi.astype(F32)).astype(BF16)
    h1, idx, gate, loc, counts = _postmix(
        o_attn, y_ssm, x2, w_glu.astype(BF16), w_out_b[:d_attn], w_out_b[d_attn:],
        attn_out_norm[None, :], ssm_out_norm[None, :], norm_ffn[None, :],
        jnp.concatenate([wr_hi, wr_lo], axis=1), router_bias[None, :], tm=_largest_tile(t, 512))

    rb = _largest_tile(t, 256)
    n_pad = t * TOP_K + n_experts * rb
    n_blocks = n_pad // rb
    cnt = counts[0].astype(jnp.int32)
    pcnt = (cnt + rb - 1) // rb * rb
    pend = jnp.cumsum(pcnt)
    pstart = pend - pcnt
    onehot = idx[:, :, None] == jnp.arange(n_experts, dtype=jnp.int32)[None, None, :]
    dest = jnp.sum(jnp.where(onehot, pstart[None, None, :], 0), axis=-1) + loc
    dest = dest.reshape(IDX_ROWS, t // tm, tm).transpose(1, 0, 2).reshape(-1)
    n_used = (pend[-1] // rb).astype(jnp.int32)
    blk_ids = jnp.arange(n_blocks, dtype=jnp.int32)
    starts = (blk_ids * rb)[:, None]
    owner = (starts >= pstart[None, :]) & (starts < pend[None, :])
    blk_e = jnp.sum(jnp.where(owner, jnp.arange(n_experts, dtype=jnp.int32)[None, :], 0), axis=1)
    is_last = jnp.any(owner & (starts + rb == pend[None, :]), axis=1)
    is_first = jnp.any(owner & (starts == pstart[None, :]), axis=1).astype(jnp.int32)
    zflag = ((blk_ids >= n_used) | is_last).astype(jnp.int32)
    first_pos = jnp.where(is_first > 0, blk_ids, n_blocks)
    nxt_pos = jnp.concatenate([lax.cummin(first_pos, axis=0, reverse=True)[1:],
                               jnp.full((1,), n_blocks, jnp.int32)])
    nxt_hit = nxt_pos[:, None] == blk_ids[None, :]
    nxt = jnp.where(nxt_pos < n_blocks, jnp.sum(jnp.where(nxt_hit, blk_e[None, :], 0), axis=1), -1)
    last_e = jnp.sum(jnp.where(blk_ids == n_used - 1, blk_e, 0))
    blk_e = jnp.where(blk_ids < n_used, blk_e, last_e).astype(jnp.int32)

    h1s, xs = _dispatch(dest, zflag, h1, norm_ffn[None, :], w_sh_gate.astype(BF16), w_sh_up.astype(BF16),
                        w_sh_down.astype(BF16), tm=tm, rb=rb, n_pad=n_pad)
    ys = _experts(blk_e, is_first, nxt.astype(jnp.int32), n_used[None], xs, w_exp_gate, w_exp_up, w_exp_down,
                  tm=rb)
    out = _final(dest, h1s, gate, p_l.reshape(t, -1), norm_ple[None, :], w_ple_gate.astype(BF16),
                 w_ple_proj.astype(BF16), ys, tm=tm)
    return out.reshape(bsz, seq, d)


def kernel(x, p, norm_mix, w_in, q_norm, k_norm, ssm_lam_re, ssm_lam_im, ssm_log_dt, ssm_b_re, ssm_b_im,
           ssm_c_re, ssm_c_im, ssm_d, w_glu, attn_out_norm, ssm_out_norm, w_out, norm_ffn, w_router,
           router_bias, w_exp_gate, w_exp_up, w_exp_down, w_sh_gate, w_sh_up, w_sh_down, norm_ple,
           w_ple_gate, w_ple_proj):
    h = x
    for i in range(p.shape[0]):
        h = _layer(h, p[i], norm_mix[i], w_in[i], q_norm[i], k_norm[i], ssm_lam_re[i], ssm_lam_im[i],
                   ssm_log_dt[i], ssm_b_re[i], ssm_b_im[i], ssm_c_re[i], ssm_c_im[i], ssm_d[i], w_glu[i],
                   attn_out_norm[i], ssm_out_norm[i], w_out[i], norm_ffn[i], w_router[i], router_bias[i],
                   w_exp_gate[i], w_exp_up[i], w_exp_down[i], w_sh_gate[i], w_sh_up[i], w_sh_down[i],
                   norm_ple[i], w_ple_gate[i], w_ple_proj[i])
    return h
```

```python
import functools
import math

import jax
import jax.numpy as jnp
from jax import lax
from jax.experimental import pallas as pl
from jax.experimental.pallas import tpu as pltpu

NORM_EPS = 1e-6
TOP_K = 6
ROUTED_SCALE = 2.5
IDX_ROWS = 8
LANES = 128
S5_CHUNK = 8
VMEM_LIMIT = 56 * 1024 * 1024

F32 = jnp.float32
BF16 = jnp.bfloat16
HIGH_HALF = -65536


def _rms(x, g):
    return x * lax.rsqrt(jnp.mean(x * x, axis=-1, keepdims=True) + NORM_EPS) * g


def _sigmoid(x):
    return 1.0 / (1.0 + jnp.exp(-x))


def _params(sem):
    return pltpu.CompilerParams(dimension_semantics=sem, vmem_limit_bytes=VMEM_LIMIT)


def _pack_halves(x):
    n = x.shape[1] // 2
    lo = lax.bitcast_convert_type(x[:, :n].astype(BF16).astype(F32), jnp.int32)
    hi = lax.bitcast_convert_type(x[:, n:].astype(BF16).astype(F32), jnp.int32)
    return (hi & HIGH_HALF) | lax.shift_right_logical(lo, 16)


def _unpack_halves(p):
    lo = lax.bitcast_convert_type(lax.shift_left(p, 16), F32)
    hi = lax.bitcast_convert_type(p & HIGH_HALF, F32)
    return lo, hi


def _inproj_kernel(x_ref, g_ref, w_ref, hg_ref, o_ref, u_ref, xn_ref, *, n_norm_tiles, n_qkv_tiles, head_dim):
    j = pl.program_id(1)

    @pl.when(j == 0)
    def _():
        xn_ref[...] = _rms(x_ref[...], g_ref[...]).astype(BF16)

    acc = jnp.dot(xn_ref[...], w_ref[...], preferred_element_type=F32)
    tn = acc.shape[1]

    @pl.when(j < n_norm_tiles)
    def _():
        hg = hg_ref[...]
        for h in range(tn // head_dim):
            sl = slice(h * head_dim, (h + 1) * head_dim)
            o_ref[:, sl] = _rms(acc[:, sl], hg[:, sl]).astype(o_ref.dtype)

    @pl.when((j >= n_norm_tiles) & (j < n_qkv_tiles))
    def _():
        o_ref[...] = acc.astype(o_ref.dtype)

    @pl.when(j >= n_qkv_tiles)
    def _():
        u_ref[...] = acc


def _inproj(x2, g, w, hg, *, d_qkv, n_norm_tiles, head_dim, tm, tn):
    t, d = x2.shape
    n = w.shape[1]
    nq = d_qkv // tn
    return pl.pallas_call(
        functools.partial(_inproj_kernel, n_norm_tiles=n_norm_tiles, n_qkv_tiles=nq, head_dim=head_dim),
        grid=(t // tm, n // tn),
        in_specs=[
            pl.BlockSpec((tm, d), lambda i, j: (i, 0)),
            pl.BlockSpec((1, d), lambda i, j: (0, 0)),
            pl.BlockSpec((d, tn), lambda i, j: (0, j)),
            pl.BlockSpec((1, tn), lambda i, j: (0, j)),
        ],
        out_specs=[
            pl.BlockSpec((tm, tn), lambda i, j: (i, jnp.minimum(j, nq - 1))),
            pl.BlockSpec((tm, tn), lambda i, j: (i, jnp.maximum(j - nq, 0))),
        ],
        out_shape=[jax.ShapeDtypeStruct((t, d_qkv), BF16), jax.ShapeDtypeStruct((t, n - d_qkv), F32)],
        scratch_shapes=[pltpu.VMEM((tm, d), BF16)],
        compiler_params=_params(("parallel", "arbitrary")),
        name="inproj",
    )(x2, g, w, hg)


SKIP_AFTER = 110.0


def _attn_kernel(q_ref, k_ref, v_ref, o_ref, *, blk, hd, heads):
    i = pl.program_id(2)
    row = lax.broadcasted_iota(jnp.int32, (blk, blk), 0)
    col = lax.broadcasted_iota(jnp.int32, (blk, blk), 1)
    causal = col < row
    later = jnp.where(row > col, 1.0, 0.0).astype(BF16)
    lanes = [slice(h * hd, (h + 1) * hd) for h in range(heads)]
    qs = [q_ref[:, sl] for sl in lanes]

    def step(kb, accs, runs, diag):
        start = pl.multiple_of(kb * blk, blk)
        zs, sps, inners, new_accs, new_runs = [], [], [], [], []
        for h, sl in enumerate(lanes):
            k = k_ref[pl.ds(start, blk), sl]
            zs.append(lax.dot_general(qs[h], k, (((1,), (1,)), ((), ())), preferred_element_type=F32))
        for z in zs:
            sp = jnp.maximum(z, 0.0) + jnp.log(1.0 + jnp.exp(-jnp.abs(z)))
            sps.append(jnp.where(causal, sp, 0.0) if diag else sp)
        for sp in sps:
            hi = sp.astype(BF16)
            lo = (sp - hi.astype(F32)).astype(BF16)
            inners.append(jnp.dot(hi, later, preferred_element_type=F32)
                          + jnp.dot(lo, later, preferred_element_type=F32))
        for h, sl in enumerate(lanes):
            w = jnp.exp(zs[h] - sps[h] - (inners[h] + runs[h]))
            if diag:
                w = jnp.where(causal, w, 0.0)
            v = v_ref[pl.ds(start, blk), sl]
            new_accs.append(accs[h] + jnp.dot(w.astype(BF16), v, preferred_element_type=F32))
            new_runs.append(runs[h] + jnp.sum(sps[h], axis=-1, keepdims=True))
        return tuple(new_accs), tuple(new_runs)

    def keep_going(runs):
        low = runs[0]
        for r in runs[1:]:
            low = jnp.minimum(low, r)
        return (jnp.min(low) < SKIP_AFTER).astype(jnp.int32)

    accs = tuple(jnp.zeros((blk, hd), F32) for _ in lanes)
    runs = tuple(jnp.zeros((blk, 1), F32) for _ in lanes)
    accs, runs = step(i, accs, runs, True)

    def cond(c):
        n, go, _, _ = c
        return (n < i) & (go > 0)

    def body(c):
        n, _, accs, runs = c
        accs, runs = step(i - 1 - n, accs, runs, False)
        return n + 1, keep_going(runs), accs, runs

    _, _, accs, _ = lax.while_loop(cond, body, (jnp.int32(0), keep_going(runs), accs, runs))
    for h, sl in enumerate(lanes):
        o_ref[:, sl] = accs[h].astype(o_ref.dtype)


def _attention(z, *, batch, seq, n_heads, head_dim, blk):
    nq = seq // blk
    heads = 2 if n_heads % 2 == 0 else 1
    hw = heads * head_dim
    ng = n_heads // heads
    return pl.pallas_call(
        functools.partial(_attn_kernel, blk=blk, hd=head_dim, heads=heads),
        grid=(batch, ng, nq),
        in_specs=[
            pl.BlockSpec((blk, hw), lambda b, h, i: (b * nq + i, h)),
            pl.BlockSpec((seq, hw), lambda b, h, i: (b, ng + h)),
            pl.BlockSpec((seq, hw), lambda b, h, i: (b, 2 * ng + h)),
        ],
        out_specs=pl.BlockSpec((blk, hw), lambda b, h, i: (b * nq + i, h)),
        out_shape=jax.ShapeDtypeStruct((batch * seq, n_heads * head_dim), BF16),
        compiler_params=_params(("parallel", "parallel", "arbitrary")),
        name="sb_attention",
    )(z, z, z)


def _s5_prepare(lam_re, lam_im, log_dt, b_re, b_im, c_re, c_im, d_skip, n_chunks):
    hp = lax.Precision.HIGHEST
    chunk = S5_CHUNK
    g, p = lam_re.shape
    h = b_re.shape[-1]
    gg = LANES // h
    o = g // gg
    dt = jnp.exp(log_dt)[:, None]
    ks = jnp.arange(chunk + 1, dtype=F32)[None, :, None]
    mag = jnp.exp(lam_re[:, None, :] * dt[:, None, :] * ks)
    ang = lam_im[:, None, :] * dt[:, None, :] * ks
    ak_re = mag * jnp.cos(ang)
    ak_im = mag * jnp.sin(ang)
    nr = ak_re[:, 1] - 1.0
    ni = ak_im[:, 1]
    den = lam_re * lam_re + lam_im * lam_im
    coef_re = ((nr * lam_re + ni * lam_im) / den)[..., None]
    coef_im = ((ni * lam_re - nr * lam_im) / den)[..., None]
    bbar_re = coef_re * b_re - coef_im * b_im
    bbar_im = coef_re * b_im + coef_im * b_re
    ca_re = c_re[:, None] * ak_re[:, :, None, :] - c_im[:, None] * ak_im[:, :, None, :]
    ca_im = c_re[:, None] * ak_im[:, :, None, :] + c_im[:, None] * ak_re[:, :, None, :]

    cam_re = ca_re[:, :chunk].transpose(0, 3, 1, 2).reshape(g, p, chunk * h)
    cam_im = ca_im[:, :chunk].transpose(0, 3, 1, 2).reshape(g, p, chunk * h)
    kt = (jnp.einsum("gph,gpm->ghm", bbar_re, cam_re, precision=hp)
          - jnp.einsum("gph,gpm->ghm", bbar_im, cam_im, precision=hp))
    def same_group(n_rows, row_block, n_cols, col_block):
        r = (jnp.arange(n_rows) // row_block) % gg
        c = (jnp.arange(n_cols) // col_block) % gg
        return (r[:, None] == c[None, :]).astype(F32)

    tile_h = jnp.tile(jnp.eye(h, dtype=F32), (1, gg))
    tile_p = jnp.tile(jnp.eye(p, dtype=F32), (1, gg))
    kt = kt.reshape(o, gg, h, chunk, h).transpose(0, 3, 1, 2, 4).reshape(o, chunk, LANES, h)
    bd = jnp.einsum("okrh,hc->okrc", kt, tile_h, precision=hp) * same_group(LANES, h, LANES, h)
    dvec = d_skip.reshape(o, LANES)
    lags = [bd[:, k] for k in range(chunk)]
    lags[0] = lags[0] + dvec[:, :, None] * jnp.eye(LANES, dtype=F32)[None]
    zero_blk = jnp.zeros_like(lags[0])
    m_intra = jnp.concatenate(
        [jnp.concatenate([lags[j - i] if j >= i else zero_blk for j in range(chunk)], axis=2)
         for i in range(chunk)], axis=1)

    def expand_in(ak_part_a, ak_part_b, b_a, b_b, sign):
        rk_a = jnp.stack([ak_part_a[:, chunk - 1 - i] for i in range(chunk)], axis=1)
        rk_b = jnp.stack([ak_part_b[:, chunk - 1 - i] for i in range(chunk)], axis=1)
        val = (rk_a[:, :, None, :] * b_a.transpose(0, 2, 1)[:, None]
               + sign * rk_b[:, :, None, :] * b_b.transpose(0, 2, 1)[:, None])
        val = val.reshape(o, gg, chunk, h, p).transpose(0, 2, 1, 3, 4).reshape(o, chunk * LANES, p)
        return jnp.einsum("orp,pc->orc", val, tile_p, precision=hp) * same_group(chunk * LANES, h, gg * p, p)

    m_in = jnp.concatenate([expand_in(ak_re, ak_im, bbar_re, bbar_im, -1.0),
                            expand_in(ak_re, ak_im, bbar_im, bbar_re, 1.0)], axis=-1)

    def expand_out(ca):
        val = ca[:, 1:].reshape(o, gg, chunk, h, p).transpose(0, 4, 2, 1, 3).reshape(o, p, chunk * LANES)
        return jnp.einsum("pc,opr->ocr", tile_p, val, precision=hp) * same_group(gg * p, p, chunk * LANES, h)

    m_out = jnp.concatenate([expand_out(ca_re), -expand_out(ca_im)], axis=1)

    steps = max(1, int(math.ceil(math.log2(max(n_chunks, 2)))))
    cr, ci = ak_re[:, chunk].reshape(o, gg * p), ak_im[:, chunk].reshape(o, gg * p)
    sc_a, sc_b = [], []
    for _ in range(steps):
        sc_a.append(jnp.concatenate([cr, cr], axis=-1))
        sc_b.append(jnp.concatenate([-ci, ci], axis=-1))
        cr, ci = cr * cr - ci * ci, 2.0 * cr * ci
    return (m_intra.astype(BF16), m_in.astype(BF16), m_out.astype(BF16),
            jnp.stack(sc_a, axis=1), jnp.stack(sc_b, axis=1))


def _s5_kernel(u_ref, mi_ref, min_ref, mout_ref, sa_ref, sb_ref, y_ref, *, n_chunks, steps):
    chunk = S5_CHUNK
    n = u_ref.shape[0] // chunk
    u = jnp.concatenate([u_ref[pl.ds(i, n, stride=chunk), :] for i in range(chunk)], axis=1).astype(BF16)
    y = jnp.dot(u, mi_ref[...], preferred_element_type=F32)
    x = jnp.dot(u, min_ref[...], preferred_element_type=F32)
    half = x.shape[1] // 2
    c = lax.rem(lax.broadcasted_iota(jnp.int32, x.shape, 0), n_chunks)
    sa = sa_ref[...]
    sb = sb_ref[...]
    for k in range(steps):
        sh = 1 << k
        xs = jnp.where(c >= sh, pltpu.roll(x, sh, axis=0), 0.0)
        x = x + xs * sa[k:k + 1, :] + pltpu.roll(xs, half, axis=1) * sb[k:k + 1, :]
    s_in = jnp.where(c >= 1, pltpu.roll(x, 1, axis=0), 0.0)
    y = y + jnp.dot(s_in.astype(BF16), mout_ref[...], preferred_element_type=F32)
    for i in range(chunk):
        y_ref[pl.ds(i, n, stride=chunk), :] = y[:, i * LANES:(i + 1) * LANES]


def _s5(u, m_intra, m_in, m_out, sc_a, sc_b, *, n_chunks):
    t, d_ssm = u.shape
    o, lh, st = m_in.shape
    steps = sc_a.shape[1]
    mat = lambda i: (i, 0, 0)
    return pl.pallas_call(
        functools.partial(_s5_kernel, n_chunks=n_chunks, steps=steps),
        grid=(o,),
        in_specs=[
            pl.BlockSpec((t, LANES), lambda i: (0, i)),
            pl.BlockSpec((None, lh, lh), mat), pl.BlockSpec((None, lh, st), mat), pl.BlockSpec((None, st, lh), mat),
            pl.BlockSpec((None, steps, st), mat), pl.BlockSpec((None, steps, st), mat),
        ],
        out_specs=pl.BlockSpec((t, LANES), lambda i: (0, i)),
        out_shape=jax.ShapeDtypeStruct((t, d_ssm), F32),
        compiler_params=_params(("parallel",)),
        name="s5_chunked",
    )(u, m_intra, m_in, m_out, sc_a, sc_b)


def _postmix_kernel(oa_ref, ys_ref, x_ref, wglu_ref, woa_ref, wob_ref, ga_ref, gs_ref, gf_ref,
                    wr_ref, rb_ref, h1_ref, idx_ref, gate_ref, loc_ref, cnt_ref, run_ref, *, n_experts):
    step = pl.program_id(0)

    @pl.when(step == 0)
    def _():
        run_ref[...] = jnp.zeros(run_ref.shape, F32)

    y = ys_ref[...]
    y = 0.5 * y * (1.0 + jnp.tanh(math.sqrt(2.0 / math.pi) * (y + 0.044715 * (y * y * y))))
    y = y * _sigmoid(jnp.dot(y.astype(BF16), wglu_ref[...], preferred_element_type=F32))
    na = _rms(oa_ref[...].astype(F32), ga_ref[...]).astype(BF16)
    ns = _rms(y, gs_ref[...]).astype(BF16)
    h1 = (x_ref[...] + jnp.dot(na, woa_ref[...], preferred_element_type=F32)
          + jnp.dot(ns, wob_ref[...], preferred_element_type=F32))
    h1_ref[...] = h1

    m = _rms(h1, gf_ref[...])
    m_hi = m.astype(BF16)
    m_lo = (m - m_hi.astype(F32)).astype(BF16)
    r_hi = jnp.dot(m_hi, wr_ref[...], preferred_element_type=F32)
    r_lo = jnp.dot(m_lo, wr_ref[...], preferred_element_type=F32)
    logits = r_hi[:, :n_experts] + r_hi[:, n_experts:] + r_lo[:, :n_experts]
    scores = _sigmoid(logits)
    vals = scores + rb_ref[...]
    tm = vals.shape[0]
    lane = lax.broadcasted_iota(jnp.int32, vals.shape, 1).astype(F32)
    picks, firsts, raws = [], [], []
    sel = jnp.zeros(vals.shape, F32)
    for _ in range(TOP_K):
        mx = jnp.max(vals, axis=1, keepdims=True)
        first = jnp.min(jnp.where(vals == mx, lane, float(n_experts)), axis=1, keepdims=True)
        pick = lane == first
        picks.append(pick)
        firsts.append(first)
        raws.append(jnp.sum(jnp.where(pick, scores, 0.0), axis=1, keepdims=True))
        vals = jnp.where(pick, -jnp.inf, vals)
        sel = jnp.where(pick, 1.0, sel)
    denom = raws[0]
    for r in raws[1:]:
        denom = denom + r

    r_i = lax.broadcasted_iota(jnp.int32, (tm, tm), 0)
    c_i = lax.broadcasted_iota(jnp.int32, (tm, tm), 1)
    before = jnp.where(c_i < r_i, 1.0, 0.0).astype(BF16)
    rank = jnp.dot(before, sel.astype(BF16), preferred_element_type=F32) + run_ref[...]
    run_ref[...] = run_ref[...] + jnp.sum(sel, axis=0, keepdims=True)
    cnt_ref[...] = run_ref[...]

    slot = lax.broadcasted_iota(jnp.int32, (tm, LANES), 1)
    idx_w = jnp.zeros((tm, LANES), F32)
    loc_w = jnp.zeros((tm, LANES), F32)
    gate_w = jnp.zeros((tm, LANES), F32)
    for k in range(TOP_K):
        loc = jnp.sum(jnp.where(picks[k], rank, 0.0), axis=1, keepdims=True)
        idx_w = jnp.where(slot == k, firsts[k], idx_w)
        loc_w = jnp.where(slot == k, loc, loc_w)
        gate_w = jnp.where(slot == k, raws[k] / denom * ROUTED_SCALE, gate_w)
    idx_ref[...] = idx_w.T[:IDX_ROWS].astype(jnp.int32)
    loc_ref[...] = loc_w.T[:IDX_ROWS].astype(jnp.int32)
    gate_ref[...] = gate_w[:, :IDX_ROWS]


def _postmix(oa, ys, x2, wglu, woa, wob, ga, gs, gf, wr, rb, *, tm):
    t, d = x2.shape
    da = oa.shape[1]
    dsm = ys.shape[1]
    e = rb.shape[1]
    row = lambda i: (i, 0)
    col = lambda i: (0, i)
    fix = lambda i: (0, 0)
    return pl.pallas_call(
        functools.partial(_postmix_kernel, n_experts=e),
        grid=(t // tm,),
        in_specs=[
            pl.BlockSpec((tm, da), row), pl.BlockSpec((tm, dsm), row), pl.BlockSpec((tm, d), row),
            pl.BlockSpec((dsm, dsm), fix), pl.BlockSpec((da, d), fix), pl.BlockSpec((dsm, d), fix),
            pl.BlockSpec((1, da), fix), pl.BlockSpec((1, dsm), fix), pl.BlockSpec((1, d), fix),
            pl.BlockSpec((d, 2 * e), fix), pl.BlockSpec((1, e), fix),
        ],
        out_specs=[
            pl.BlockSpec((tm, d), row), pl.BlockSpec((IDX_ROWS, tm), col), pl.BlockSpec((tm, IDX_ROWS), row),
            pl.BlockSpec((IDX_ROWS, tm), col), pl.BlockSpec((1, e), fix),
        ],
        out_shape=[
            jax.ShapeDtypeStruct((t, d), F32), jax.ShapeDtypeStruct((IDX_ROWS, t), jnp.int32),
            jax.ShapeDtypeStruct((t, IDX_ROWS), F32), jax.ShapeDtypeStruct((IDX_ROWS, t), jnp.int32),
            jax.ShapeDtypeStruct((1, e), F32),
        ],
        scratch_shapes=[pltpu.VMEM((1, e), F32)],
        compiler_params=_params(("arbitrary",)),
        name="postmix_router",
    )(oa, ys, x2, wglu, woa, wob, ga, gs, gf, wr, rb)


def _rows_to_tiles(ref, packed, sub):
    rows = packed.shape[0]
    for s in range(sub):
        ref[pl.ds(s, rows, stride=sub), :] = packed[:, s * LANES:(s + 1) * LANES]


def _tiles_to_rows(ref, rows, sub):
    return jnp.concatenate([ref[pl.ds(s, rows, stride=sub), :] for s in range(sub)], axis=1)


def _tile_rows(ref, r, sub):
    start = r * sub if isinstance(r, int) else pl.multiple_of(r * sub, sub)
    return ref.at[pl.ds(start, sub)]


def _dispatch_kernel(dest_ref, zflag_ref, h1_ref, gf_ref, wg_ref, wu_ref, wd_ref, h1s_ref, xs_ref,
                     m0_ref, m1_ref, zero_ref, sem, zsem, *, sub):
    tm = h1_ref.shape[0]
    zrows = zero_ref.shape[0]
    n_blocks = zflag_ref.shape[0]

    @pl.when(pl.program_id(0) == 0)
    def _():
        zero_ref[...] = jnp.zeros(zero_ref.shape, zero_ref.dtype)

        def zcopy(b):
            return pltpu.make_async_copy(zero_ref, xs_ref.at[pl.ds(pl.multiple_of(b * zrows, zrows), zrows)], zsem)

        def zstart(b, carry):
            @pl.when(zflag_ref[b] != 0)
            def _():
                zcopy(b).start()
            return carry

        def zwait(b, carry):
            @pl.when(zflag_ref[b] != 0)
            def _():
                zcopy(b).wait()
            return carry

        lax.fori_loop(0, n_blocks, zstart, 0)
        lax.fori_loop(0, n_blocks, zwait, 0)

    step = pl.program_id(0)
    slot = lax.rem(step, 2)
    ms = (m0_ref, m1_ref)

    def drain(s):
        for _ in range(TOP_K):
            pltpu.make_async_copy(ms[s], xs_ref.at[pl.ds(0, tm * sub)], sem.at[s]).wait()

    def tile(cur):
        h1 = h1_ref[...]
        m = _rms(h1, gf_ref[...])
        _rows_to_tiles(ms[cur], _pack_halves(m), sub)
        for t in range(tm):
            src = _tile_rows(ms[cur], t, sub)
            for k in range(TOP_K):
                pltpu.make_async_copy(src, _tile_rows(xs_ref, dest_ref[k * tm + t], sub), sem.at[cur]).start()
        mb = m.astype(BF16)
        a = jnp.dot(mb, wg_ref[...], preferred_element_type=F32)
        b = jnp.dot(mb, wu_ref[...], preferred_element_type=F32)
        hid = (a * _sigmoid(a) * b).astype(BF16)
        h1s_ref[...] = h1 + jnp.dot(hid, wd_ref[...], preferred_element_type=F32)

        @pl.when(step > 0)
        def _():
            drain(1 - cur)

        @pl.when(step == pl.num_programs(0) - 1)
        def _():
            drain(cur)

    for cur in range(2):
        @pl.when(slot == cur)
        def _(cur=cur):
            tile(cur)


def _dispatch(dest, zflag, h1, gf, wg, wu, wd, *, tm, rb, n_pad):
    t, d = h1.shape
    f = wg.shape[1]
    sub = d // 2 // LANES
    row = lambda i: (i, 0)
    fix = lambda i: (0, 0)
    return pl.pallas_call(
        functools.partial(_dispatch_kernel, sub=sub),
        grid=(t // tm,),
        in_specs=[
            pl.BlockSpec((IDX_ROWS * tm,), lambda i: (i,), memory_space=pltpu.SMEM),
            pl.BlockSpec(memory_space=pltpu.SMEM),
            pl.BlockSpec((tm, d), row), pl.BlockSpec((1, d), fix),
            pl.BlockSpec((d, f), fix), pl.BlockSpec((d, f), fix), pl.BlockSpec((f, d), fix),
        ],
        out_specs=[pl.BlockSpec((tm, d), row), pl.BlockSpec(memory_space=pl.ANY)],
        out_shape=[jax.ShapeDtypeStruct((t, d), F32), jax.ShapeDtypeStruct((n_pad * sub, LANES), jnp.int32)],
        scratch_shapes=[pltpu.VMEM((tm * sub, LANES), jnp.int32), pltpu.VMEM((tm * sub, LANES), jnp.int32),
                        pltpu.VMEM((rb * sub, LANES), jnp.int32),
                        pltpu.SemaphoreType.DMA((2,)), pltpu.SemaphoreType.DMA(())],
        compiler_params=_params(("arbitrary",)),
        name="dispatch_shared",
    )(dest, zflag, h1, gf, wg, wu, wd)


def _experts_kernel(be_ref, first_ref, nxt_ref, nu_ref, xs_ref, wg_hbm, wu_hbm, wd_hbm, ys_ref,
                    wgf_ref, wuf_ref, wdf_ref, wgb_ref, wub_ref, wdb_ref, sem, *, sub):
    i = pl.program_id(0)
    tm = xs_ref.shape[0] // sub

    def weight_copies(e, s):
        return (pltpu.make_async_copy(wg_hbm.at[e], wgf_ref.at[s], sem.at[s, 0]),
                pltpu.make_async_copy(wu_hbm.at[e], wuf_ref.at[s], sem.at[s, 1]),
                pltpu.make_async_copy(wd_hbm.at[e], wdf_ref.at[s], sem.at[s, 2]))

    @pl.when(i == 0)
    def _():
        for c in weight_copies(be_ref[0], 0):
            c.start()

    @pl.when((i < nu_ref[0]) & (first_ref[i] != 0))
    def _():
        s = first_ref[i] - 1
        for c in weight_copies(be_ref[i], s):
            c.wait()

        @pl.when(nxt_ref[i] >= 0)
        def _():
            for c in weight_copies(nxt_ref[i], 1 - s):
                c.start()

        wgb_ref[...] = wgf_ref[s].astype(BF16)
        wub_ref[...] = wuf_ref[s].astype(BF16)
        wdb_ref[...] = wdf_ref[s].astype(BF16)

    @pl.when(i < nu_ref[0])
    def _():
        lo, hi = _unpack_halves(_tiles_to_rows(xs_ref, tm, sub))
        lo = lo.astype(BF16)
        hi = hi.astype(BF16)
        n = lo.shape[1]
        a = (jnp.dot(lo, wgb_ref[:n, :], preferred_element_type=F32)
             + jnp.dot(hi, wgb_ref[n:, :], preferred_element_type=F32))
        b = (jnp.dot(lo, wub_ref[:n, :], preferred_element_type=F32)
             + jnp.dot(hi, wub_ref[n:, :], preferred_element_type=F32))
        hid = (a * _sigmoid(a) * b).astype(BF16)
        _rows_to_tiles(ys_ref, _pack_halves(jnp.dot(hid, wdb_ref[...], preferred_element_type=F32)), sub)

    @pl.when(i >= nu_ref[0])
    def _():
        ys_ref[...] = jnp.zeros(ys_ref.shape, ys_ref.dtype)


def _experts(blk_e, first, nxt, n_used, xs, wg, wu, wd, *, tm):
    rows_sub, _ = xs.shape
    _, d, f = wg.shape
    sub = d // 2 // LANES
    n_blocks = rows_sub // (tm * sub)
    rows = lambda i, be, fi, nx, nu: (jnp.maximum(jnp.minimum(i, nu[0] - 1), 0), 0)
    orow = lambda i, be, fi, nx, nu: (i, 0)
    return pl.pallas_call(
        functools.partial(_experts_kernel, sub=sub),
        grid_spec=pltpu.PrefetchScalarGridSpec(
            num_scalar_prefetch=4,
            grid=(n_blocks,),
            in_specs=[
                pl.BlockSpec((tm * sub, LANES), rows),
                pl.BlockSpec(memory_space=pl.ANY), pl.BlockSpec(memory_space=pl.ANY),
                pl.BlockSpec(memory_space=pl.ANY),
            ],
            out_specs=pl.BlockSpec((tm * sub, LANES), orow),
            scratch_shapes=[pltpu.VMEM((2, d, f), F32), pltpu.VMEM((2, d, f), F32), pltpu.VMEM((2, f, d), F32),
                            pltpu.VMEM((d, f), BF16), pltpu.VMEM((d, f), BF16), pltpu.VMEM((f, d), BF16),
                            pltpu.SemaphoreType.DMA((2, 3))],
        ),
        out_shape=jax.ShapeDtypeStruct(xs.shape, jnp.int32),
        compiler_params=_params(("arbitrary",)),
        name="routed_experts",
    )(blk_e, first, nxt, n_used, xs, wg, wu, wd)


def _final_kernel(dest_ref, dnext_ref, h_ref, gate_ref, p_ref, gp_ref, wgate_ref, wproj_ref, ys_ref, o_ref,
                  buf0_ref, buf1_ref, sem, *, sub):
    tm = h_ref.shape[0]
    step = pl.program_id(0)
    slot = lax.rem(step, 2)
    bufs = (buf0_ref, buf1_ref)

    def row_copy(idx_ref, t, k, s):
        return pltpu.make_async_copy(_tile_rows(ys_ref, idx_ref[k * tm + t], sub),
                                     _tile_rows(bufs[s].at[k], t, sub), sem.at[s])

    def drain(s):
        for k in range(TOP_K):
            pltpu.make_async_copy(ys_ref.at[pl.ds(0, tm * sub)], bufs[s].at[k], sem.at[s]).wait()

    @pl.when(step == 0)
    def _():
        def issue(t, carry):
            for k in range(TOP_K):
                row_copy(dest_ref, t, k, 0).start()
            return carry

        lax.fori_loop(0, tm, issue, 0)

    def tile(cur, nxt):
        drain(cur)
        for t in range(tm):
            for k in range(TOP_K):
                row_copy(dnext_ref, t, k, nxt).start()
        pp = jnp.dot(p_ref[...].astype(BF16), wproj_ref[...], preferred_element_type=F32)
        gate = gate_ref[...]
        h = h_ref[...]
        n = h.shape[1] // 2
        acc_lo = h[:, :n]
        acc_hi = h[:, n:]
        for k in range(TOP_K):
            lo, hi = _unpack_halves(_tiles_to_rows(bufs[cur].at[k], tm, sub))
            acc_lo = acc_lo + gate[:, k:k + 1] * lo
            acc_hi = acc_hi + gate[:, k:k + 1] * hi
        h2 = jnp.concatenate([acc_lo, acc_hi], axis=1)
        nrm = _rms(h2, gp_ref[...]).astype(BF16)
        o_ref[...] = h2 + _sigmoid(jnp.dot(nrm, wgate_ref[...], preferred_element_type=F32)) * pp

    for cur in range(2):
        @pl.when(slot == cur)
        def _(cur=cur):
            tile(cur, 1 - cur)

            @pl.when(step == pl.num_programs(0) - 1)
            def _():
                drain(1 - cur)


def _final(dest, h1s, gate, p2, gp, wgate, wproj, ys, *, tm):
    t, d = h1s.shape
    dp = p2.shape[1]
    sub = d // 2 // LANES
    row = lambda i: (i, 0)
    fix = lambda i: (0, 0)
    return pl.pallas_call(
        functools.partial(_final_kernel, sub=sub),
        grid=(t // tm,),
        in_specs=[
            pl.BlockSpec((IDX_ROWS * tm,), lambda i: (i,), memory_space=pltpu.SMEM),
            pl.BlockSpec((IDX_ROWS * tm,), lambda i: (jnp.minimum(i + 1, t // tm - 1),), memory_space=pltpu.SMEM),
            pl.BlockSpec((tm, d), row), pl.BlockSpec((tm, IDX_ROWS), row), pl.BlockSpec((tm, dp), row),
            pl.BlockSpec((1, d), fix), pl.BlockSpec((d, d), fix), pl.BlockSpec((dp, d), fix),
            pl.BlockSpec(memory_space=pl.ANY),
        ],
        out_specs=pl.BlockSpec((tm, d), row),
        out_shape=jax.ShapeDtypeStruct((t, d), F32),
        scratch_shapes=[pltpu.VMEM((TOP_K, tm * sub, LANES), jnp.int32),
                        pltpu.VMEM((TOP_K, tm * sub, LANES), jnp.int32), pltpu.SemaphoreType.DMA((2,))],
        compiler_params=_params(("arbitrary",)),
        name="combine_ple",
    )(dest, dest, h1s, gate, p2, gp, wgate, wproj, ys)


def _largest_tile(n, cap):
    t = min(n, cap)
    while n % t:
        t //= 2
    return t


def _layer(h, p_l, norm_mix, w_in, q_norm, k_norm, lam_re, lam_im, log_dt, b_re, b_im, c_re, c_im, d_skip,
           w_glu, attn_out_norm, ssm_out_norm, w_out, norm_ffn, w_router, router_bias, w_exp_gate,
           w_exp_up, w_exp_down, w_sh_gate, w_sh_up, w_sh_down, norm_ple, w_ple_gate, w_ple_proj):
    bsz, seq, d = h.shape
    t = bsz * seq
    head_dim = q_norm.shape[-1]
    d_attn = attn_out_norm.shape[-1]
    n_heads = d_attn // head_dim
    n_experts = w_router.shape[-1]
    x2 = h.reshape(t, d)

    tn = d_attn if d_attn % 128 == 0 else head_dim
    scale = 1.0 / math.sqrt(head_dim)
    hg = jnp.concatenate([jnp.tile(q_norm * scale, n_heads), jnp.tile(k_norm, n_heads),
                          jnp.ones((w_in.shape[1] - 2 * d_attn,), F32)])[None, :]
    z, u = _inproj(x2, norm_mix[None, :], w_in.astype(BF16), hg, d_qkv=3 * d_attn,
                   n_norm_tiles=2 * d_attn // tn, head_dim=head_dim, tm=_largest_tile(t, 1024), tn=tn)

    o_attn = _attention(z, batch=bsz, seq=seq, n_heads=n_heads, head_dim=head_dim, blk=_largest_tile(seq, 256))

    n_chunks = seq // S5_CHUNK
    mats = _s5_prepare(lam_re, lam_im, log_dt, b_re, b_im, c_re, c_im, d_skip, n_chunks)
    y_ssm = _s5(u, *mats, n_chunks=n_chunks)

    tm = _largest_tile(t, 256)
    w_out_b = w_out.astype(BF16)
    wr_hi = w_router.astype(BF16)
    wr_lo = (w_router - wr_hi.astype(F32)).astype(BF16)
    h1, idx, gate, loc, counts = _postmix(
        o_attn, y_ssm, x2, w_glu.astype(BF16), w_out_b[:d_attn], w_out_b[d_attn:],
        attn_out_norm[None, :], ssm_out_norm[None, :], norm_ffn[None, :],
        jnp.concatenate([wr_hi, wr_lo], axis=1), router_bias[None, :], tm=_largest_tile(t, 512))

    rb = _largest_tile(t, 256)
    n_pad = t * TOP_K + n_experts * rb
    n_blocks = n_pad // rb
    cnt = counts[0].astype(jnp.int32)
    pcnt = (cnt + rb - 1) // rb * rb
    pend = jnp.cumsum(pcnt)
    pstart = pend - pcnt
    onehot = idx[:, :, None] == jnp.arange(n_experts, dtype=jnp.int32)[None, None, :]
    dest = jnp.sum(jnp.where(onehot, pstart[None, None, :], 0), axis=-1) + loc
    dest = dest.reshape(IDX_ROWS, t // tm, tm).transpose(1, 0, 2).reshape(-1)
    n_used = (pend[-1] // rb).astype(jnp.int32)
    blk_ids = jnp.arange(n_blocks, dtype=jnp.int32)
    starts = (blk_ids * rb)[:, None]
    owner = (starts >= pstart[None, :]) & (starts < pend[None, :])
    blk_e = jnp.sum(jnp.where(owner, jnp.arange(n_experts, dtype=jnp.int32)[None, :], 0), axis=1)
    is_last = jnp.any(owner & (starts + rb == pend[None, :]), axis=1)
    is_first = jnp.any(owner & (starts == pstart[None, :]), axis=1).astype(jnp.int32)
    zflag = ((blk_ids >= n_used) | is_last).astype(jnp.int32)
    first_pos = jnp.where(is_first > 0, blk_ids, n_blocks)
    nxt_pos = jnp.concatenate([lax.cummin(first_pos, axis=0, reverse=True)[1:],
                               jnp.full((1,), n_blocks, jnp.int32)])
    nxt_hit = nxt_pos[:, None] == blk_ids[None, :]
    nxt = jnp.where(nxt_pos < n_blocks, jnp.sum(jnp.where(nxt_hit, blk_e[None, :], 0), axis=1), -1)
    last_e = jnp.sum(jnp.where(blk_ids == n_used - 1, blk_e, 0))
    blk_e = jnp.where(blk_ids < n_used, blk_e, last_e).astype(jnp.int32)
    first_code = jnp.where(is_first > 0, 1 + (jnp.cumsum(is_first) - 1) % 2, 0).astype(jnp.int32)

    h1s, xs = _dispatch(dest, zflag, h1, norm_ffn[None, :], w_sh_gate.astype(BF16), w_sh_up.astype(BF16),
                        w_sh_down.astype(BF16), tm=tm, rb=rb, n_pad=n_pad)
    ys = _experts(blk_e, first_code, nxt.astype(jnp.int32), n_used[None], xs, w_exp_gate, w_exp_up, w_exp_down,
                  tm=rb)
    out = _final(dest, h1s, gate, p_l.reshape(t, -1), norm_ple[None, :], w_ple_gate.astype(BF16),
                 w_ple_proj.astype(BF16), ys, tm=tm)
    return out.reshape(bsz, seq, d)


def kernel(x, p, norm_mix, w_in, q_norm, k_norm, ssm_lam_re, ssm_lam_im, ssm_log_dt, ssm_b_re, ssm_b_im,
           ssm_c_re, ssm_c_im, ssm_d, w_glu, attn_out_norm, ssm_out_norm, w_out, norm_ffn, w_router,
           router_bias, w_exp_gate, w_exp_up, w_exp_down, w_sh_gate, w_sh_up, w_sh_down, norm_ple,
           w_ple_gate, w_ple_proj):
    h = x
    for i in range(p.shape[0]):
        h = _layer(h, p[i], norm_mix[i], w_in[i], q_norm[i], k_norm[i], ssm_lam_re[i], ssm_lam_im[i],
                   ssm_log_dt[i], ssm_b_re[i], ssm_b_im[i], ssm_c_re[i], ssm_c_im[i], ssm_d[i], w_glu[i],
                   attn_out_norm[i], ssm_out_norm[i], w_out[i], norm_ffn[i], w_router[i], router_bias[i],
                   w_exp_gate[i], w_exp_up[i], w_exp_down[i], w_sh_gate[i], w_sh_up[i], w_sh_down[i],
                   norm_ple[i], w_ple_gate[i], w_ple_proj[i])
    return h
```

```python
import functools
import math

import jax
import jax.numpy as jnp
from jax import lax
from jax.experimental import pallas as pl
from jax.experimental.pallas import tpu as pltpu

NORM_EPS = 1e-6
TOP_K = 6
ROUTED_SCALE = 2.5
IDX_ROWS = 8
LANES = 128
S5_CHUNK = 8
WEIGHT_DMA_SPLIT = 4
VMEM_LIMIT = 56 * 1024 * 1024

F32 = jnp.float32
BF16 = jnp.bfloat16
HIGH_HALF = -65536


def _rms(x, g):
    return x * lax.rsqrt(jnp.mean(x * x, axis=-1, keepdims=True) + NORM_EPS) * g


def _sigmoid(x):
    return 1.0 / (1.0 + jnp.exp(-x))


def _params(sem):
    return pltpu.CompilerParams(dimension_semantics=sem, vmem_limit_bytes=VMEM_LIMIT)


def _pack_halves(x):
    n = x.shape[1] // 2
    lo = lax.bitcast_convert_type(x[:, :n].astype(BF16).astype(F32), jnp.int32)
    hi = lax.bitcast_convert_type(x[:, n:].astype(BF16).astype(F32), jnp.int32)
    return (hi & HIGH_HALF) | lax.shift_right_logical(lo, 16)


def _unpack_halves(p):
    lo = lax.bitcast_convert_type(lax.shift_left(p, 16), F32)
    hi = lax.bitcast_convert_type(p & HIGH_HALF, F32)
    return lo, hi


def _inproj_kernel(x_ref, g_ref, w_ref, hg_ref, o_ref, u_ref, xn_ref, *, n_norm_tiles, n_qkv_tiles, head_dim):
    j = pl.program_id(1)

    @pl.when(j == 0)
    def _():
        xn_ref[...] = _rms(x_ref[...], g_ref[...]).astype(BF16)

    acc = jnp.dot(xn_ref[...], w_ref[...], preferred_element_type=F32)
    tn = acc.shape[1]

    @pl.when(j < n_norm_tiles)
    def _():
        hg = hg_ref[...]
        for h in range(tn // head_dim):
            sl = slice(h * head_dim, (h + 1) * head_dim)
            o_ref[:, sl] = _rms(acc[:, sl], hg[:, sl]).astype(o_ref.dtype)

    @pl.when((j >= n_norm_tiles) & (j < n_qkv_tiles))
    def _():
        o_ref[...] = acc.astype(o_ref.dtype)

    @pl.when(j >= n_qkv_tiles)
    def _():
        u_ref[...] = acc


def _inproj(x2, g, w, hg, *, d_qkv, n_norm_tiles, head_dim, tm, tn):
    t, d = x2.shape
    n = w.shape[1]
    nq = d_qkv // tn
    return pl.pallas_call(
        functools.partial(_inproj_kernel, n_norm_tiles=n_norm_tiles, n_qkv_tiles=nq, head_dim=head_dim),
        grid=(t // tm, n // tn),
        in_specs=[
            pl.BlockSpec((tm, d), lambda i, j: (i, 0)),
            pl.BlockSpec((1, d), lambda i, j: (0, 0)),
            pl.BlockSpec((d, tn), lambda i, j: (0, j)),
            pl.BlockSpec((1, tn), lambda i, j: (0, j)),
        ],
        out_specs=[
            pl.BlockSpec((tm, tn), lambda i, j: (i, jnp.minimum(j, nq - 1))),
            pl.BlockSpec((tm, tn), lambda i, j: (i, jnp.maximum(j - nq, 0))),
        ],
        out_shape=[jax.ShapeDtypeStruct((t, d_qkv), BF16), jax.ShapeDtypeStruct((t, n - d_qkv), F32)],
        scratch_shapes=[pltpu.VMEM((tm, d), BF16)],
        compiler_params=_params(("parallel", "arbitrary")),
        name="inproj",
    )(x2, g, w, hg)


SKIP_AFTER = 110.0


def _attn_kernel(q_ref, k_ref, v_ref, o_ref, *, blk, hd, heads):
    i = pl.program_id(2)
    row = lax.broadcasted_iota(jnp.int32, (blk, blk), 0)
    col = lax.broadcasted_iota(jnp.int32, (blk, blk), 1)
    causal = col < row
    later = jnp.where(row > col, 1.0, 0.0).astype(BF16)
    lanes = [slice(h * hd, (h + 1) * hd) for h in range(heads)]
    qs = [q_ref[:, sl] for sl in lanes]

    def step(kb, accs, runs, diag):
        start = pl.multiple_of(kb * blk, blk)
        zs, sps, inners, new_accs, new_runs = [], [], [], [], []
        for h, sl in enumerate(lanes):
            k = k_ref[pl.ds(start, blk), sl]
            zs.append(lax.dot_general(qs[h], k, (((1,), (1,)), ((), ())), preferred_element_type=F32))
        for z in zs:
            sp = jnp.maximum(z, 0.0) + jnp.log(1.0 + jnp.exp(-jnp.abs(z)))
            sps.append(jnp.where(causal, sp, 0.0) if diag else sp)
        for sp in sps:
            hi = sp.astype(BF16)
            lo = (sp - hi.astype(F32)).astype(BF16)
            inners.append(jnp.dot(hi, later, preferred_element_type=F32)
                          + jnp.dot(lo, later, preferred_element_type=F32))
        for h, sl in enumerate(lanes):
            w = jnp.exp(zs[h] - sps[h] - (inners[h] + runs[h]))
            if diag:
                w = jnp.where(causal, w, 0.0)
            v = v_ref[pl.ds(start, blk), sl]
            new_accs.append(accs[h] + jnp.dot(w.astype(BF16), v, preferred_element_type=F32))
            new_runs.append(runs[h] + jnp.sum(sps[h], axis=-1, keepdims=True))
        return tuple(new_accs), tuple(new_runs)

    def keep_going(runs):
        low = runs[0]
        for r in runs[1:]:
            low = jnp.minimum(low, r)
        return (jnp.min(low) < SKIP_AFTER).astype(jnp.int32)

    accs = tuple(jnp.zeros((blk, hd), F32) for _ in lanes)
    runs = tuple(jnp.zeros((blk, 1), F32) for _ in lanes)
    accs, runs = step(i, accs, runs, True)

    def cond(c):
        n, go, _, _ = c
        return (n < i) & (go > 0)

    def body(c):
        n, _, accs, runs = c
        accs, runs = step(i - 1 - n, accs, runs, False)
        return n + 1, keep_going(runs), accs, runs

    _, _, accs, _ = lax.while_loop(cond, body, (jnp.int32(0), keep_going(runs), accs, runs))
    for h, sl in enumerate(lanes):
        o_ref[:, sl] = accs[h].astype(o_ref.dtype)


def _attention(z, *, batch, seq, n_heads, head_dim, blk):
    nq = seq // blk
    heads = next(c for c in (4, 2, 1) if n_heads % c == 0)
    hw = heads * head_dim
    ng = n_heads // heads
    return pl.pallas_call(
        functools.partial(_attn_kernel, blk=blk, hd=head_dim, heads=heads),
        grid=(batch, ng, nq),
        in_specs=[
            pl.BlockSpec((blk, hw), lambda b, h, i: (b * nq + i, h)),
            pl.BlockSpec((seq, hw), lambda b, h, i: (b, ng + h)),
            pl.BlockSpec((seq, hw), lambda b, h, i: (b, 2 * ng + h)),
        ],
        out_specs=pl.BlockSpec((blk, hw), lambda b, h, i: (b * nq + i, h)),
        out_shape=jax.ShapeDtypeStruct((batch * seq, n_heads * head_dim), BF16),
        compiler_params=_params(("parallel", "parallel", "arbitrary")),
        name="sb_attention",
    )(z, z, z)


def _s5_prepare(lam_re, lam_im, log_dt, b_re, b_im, c_re, c_im, d_skip, n_chunks):
    hp = lax.Precision.HIGHEST
    chunk = S5_CHUNK
    g, p = lam_re.shape
    h = b_re.shape[-1]
    gg = LANES // h
    o = g // gg
    dt = jnp.exp(log_dt)[:, None]
    ks = jnp.arange(chunk + 1, dtype=F32)[None, :, None]
    mag = jnp.exp(lam_re[:, None, :] * dt[:, None, :] * ks)
    ang = lam_im[:, None, :] * dt[:, None, :] * ks
    ak_re = mag * jnp.cos(ang)
    ak_im = mag * jnp.sin(ang)
    nr = ak_re[:, 1] - 1.0
    ni = ak_im[:, 1]
    den = lam_re * lam_re + lam_im * lam_im
    coef_re = ((nr * lam_re + ni * lam_im) / den)[..., None]
    coef_im = ((ni * lam_re - nr * lam_im) / den)[..., None]
    bbar_re = coef_re * b_re - coef_im * b_im
    bbar_im = coef_re * b_im + coef_im * b_re
    ca_re = c_re[:, None] * ak_re[:, :, None, :] - c_im[:, None] * ak_im[:, :, None, :]
    ca_im = c_re[:, None] * ak_im[:, :, None, :] + c_im[:, None] * ak_re[:, :, None, :]

    cam_re = ca_re[:, :chunk].transpose(0, 3, 1, 2).reshape(g, p, chunk * h)
    cam_im = ca_im[:, :chunk].transpose(0, 3, 1, 2).reshape(g, p, chunk * h)
    kt = (jnp.einsum("gph,gpm->ghm", bbar_re, cam_re, precision=hp)
          - jnp.einsum("gph,gpm->ghm", bbar_im, cam_im, precision=hp))
    def same_group(n_rows, row_block, n_cols, col_block):
        r = (jnp.arange(n_rows) // row_block) % gg
        c = (jnp.arange(n_cols) // col_block) % gg
        return (r[:, None] == c[None, :]).astype(F32)

    tile_h = jnp.tile(jnp.eye(h, dtype=F32), (1, gg))
    tile_p = jnp.tile(jnp.eye(p, dtype=F32), (1, gg))
    kt = kt.reshape(o, gg, h, chunk, h).transpose(0, 3, 1, 2, 4).reshape(o, chunk, LANES, h)
    bd = jnp.einsum("okrh,hc->okrc", kt, tile_h, precision=hp) * same_group(LANES, h, LANES, h)
    dvec = d_skip.reshape(o, LANES)
    lags = [bd[:, k] for k in range(chunk)]
    lags[0] = lags[0] + dvec[:, :, None] * jnp.eye(LANES, dtype=F32)[None]
    zero_blk = jnp.zeros_like(lags[0])
    m_intra = jnp.concatenate(
        [jnp.concatenate([lags[j - i] if j >= i else zero_blk for j in range(chunk)], axis=2)
         for i in range(chunk)], axis=1)

    def expand_in(ak_part_a, ak_part_b, b_a, b_b, sign):
        rk_a = jnp.stack([ak_part_a[:, chunk - 1 - i] for i in range(chunk)], axis=1)
        rk_b = jnp.stack([ak_part_b[:, chunk - 1 - i] for i in range(chunk)], axis=1)
        val = (rk_a[:, :, None, :] * b_a.transpose(0, 2, 1)[:, None]
               + sign * rk_b[:, :, None, :] * b_b.transpose(0, 2, 1)[:, None])
        val = val.reshape(o, gg, chunk, h, p).transpose(0, 2, 1, 3, 4).reshape(o, chunk * LANES, p)
        return jnp.einsum("orp,pc->orc", val, tile_p, precision=hp) * same_group(chunk * LANES, h, gg * p, p)

    m_in = jnp.concatenate([expand_in(ak_re, ak_im, bbar_re, bbar_im, -1.0),
                            expand_in(ak_re, ak_im, bbar_im, bbar_re, 1.0)], axis=-1)

    def expand_out(ca):
        val = ca[:, 1:].reshape(o, gg, chunk, h, p).transpose(0, 4, 2, 1, 3).reshape(o, p, chunk * LANES)
        return jnp.einsum("pc,opr->ocr", tile_p, val, precision=hp) * same_group(gg * p, p, chunk * LANES, h)

    m_out = jnp.concatenate([expand_out(ca_re), -expand_out(ca_im)], axis=1)

    steps = max(1, int(math.ceil(math.log2(max(n_chunks, 2)))))
    cr, ci = ak_re[:, chunk].reshape(o, gg * p), ak_im[:, chunk].reshape(o, gg * p)
    sc_a, sc_b = [], []
    for _ in range(steps):
        sc_a.append(jnp.concatenate([cr, cr], axis=-1))
        sc_b.append(jnp.concatenate([-ci, ci], axis=-1))
        cr, ci = cr * cr - ci * ci, 2.0 * cr * ci
    return (m_intra.astype(BF16), m_in.astype(BF16), m_out.astype(BF16),
            jnp.stack(sc_a, axis=1), jnp.stack(sc_b, axis=1))


def _s5_kernel(u_ref, mi_ref, min_ref, mout_ref, sa_ref, sb_ref, y_ref, *, n_chunks, steps):
    chunk = S5_CHUNK
    n = u_ref.shape[0] // chunk
    u = jnp.concatenate([u_ref[pl.ds(i, n, stride=chunk), :] for i in range(chunk)], axis=1).astype(BF16)
    y = jnp.dot(u, mi_ref[...], preferred_element_type=F32)
    x = jnp.dot(u, min_ref[...], preferred_element_type=F32)
    half = x.shape[1] // 2
    c = lax.rem(lax.broadcasted_iota(jnp.int32, x.shape, 0), n_chunks)
    sa = sa_ref[...]
    sb = sb_ref[...]
    for k in range(steps):
        sh = 1 << k
        xs = jnp.where(c >= sh, pltpu.roll(x, sh, axis=0), 0.0)
        x = x + xs * sa[k:k + 1, :] + pltpu.roll(xs, half, axis=1) * sb[k:k + 1, :]
    s_in = jnp.where(c >= 1, pltpu.roll(x, 1, axis=0), 0.0)
    y = y + jnp.dot(s_in.astype(BF16), mout_ref[...], preferred_element_type=F32)
    for i in range(chunk):
        y_ref[pl.ds(i, n, stride=chunk), :] = y[:, i * LANES:(i + 1) * LANES]


def _s5(u, m_intra, m_in, m_out, sc_a, sc_b, *, n_chunks):
    t, d_ssm = u.shape
    o, lh, st = m_in.shape
    steps = sc_a.shape[1]
    mat = lambda i: (i, 0, 0)
    return pl.pallas_call(
        functools.partial(_s5_kernel, n_chunks=n_chunks, steps=steps),
        grid=(o,),
        in_specs=[
            pl.BlockSpec((t, LANES), lambda i: (0, i)),
            pl.BlockSpec((None, lh, lh), mat), pl.BlockSpec((None, lh, st), mat), pl.BlockSpec((None, st, lh), mat),
            pl.BlockSpec((None, steps, st), mat), pl.BlockSpec((None, steps, st), mat),
        ],
        out_specs=pl.BlockSpec((t, LANES), lambda i: (0, i)),
        out_shape=jax.ShapeDtypeStruct((t, d_ssm), F32),
        compiler_params=_params(("parallel",)),
        name="s5_chunked",
    )(u, m_intra, m_in, m_out, sc_a, sc_b)


def _postmix_kernel(oa_ref, ys_ref, x_ref, wglu_ref, woa_ref, wob_ref, ga_ref, gs_ref, gf_ref,
                    wr_ref, rb_ref, h1_ref, idx_ref, gate_ref, loc_ref, cnt_ref, run_ref, *, n_experts):
    step = pl.program_id(0)

    @pl.when(step == 0)
    def _():
        run_ref[...] = jnp.zeros(run_ref.shape, F32)

    y = ys_ref[...]
    y = 0.5 * y * (1.0 + jnp.tanh(math.sqrt(2.0 / math.pi) * (y + 0.044715 * (y * y * y))))
    y = y * _sigmoid(jnp.dot(y.astype(BF16), wglu_ref[...], preferred_element_type=F32))
    na = _rms(oa_ref[...].astype(F32), ga_ref[...]).astype(BF16)
    ns = _rms(y, gs_ref[...]).astype(BF16)
    h1 = (x_ref[...] + jnp.dot(na, woa_ref[...], preferred_element_type=F32)
          + jnp.dot(ns, wob_ref[...], preferred_element_type=F32))
    h1_ref[...] = h1

    m = _rms(h1, gf_ref[...])
    m_hi = m.astype(BF16)
    m_lo = (m - m_hi.astype(F32)).astype(BF16)
    r_hi = jnp.dot(m_hi, wr_ref[...], preferred_element_type=F32)
    r_lo = jnp.dot(m_lo, wr_ref[...], preferred_element_type=F32)
    logits = r_hi[:, :n_experts] + r_hi[:, n_experts:] + r_lo[:, :n_experts]
    scores = _sigmoid(logits)
    vals = scores + rb_ref[...]
    tm = vals.shape[0]
    lane = lax.broadcasted_iota(jnp.int32, vals.shape, 1).astype(F32)
    picks, firsts, raws = [], [], []
    sel = jnp.zeros(vals.shape, F32)
    for _ in range(TOP_K):
        mx = jnp.max(vals, axis=1, keepdims=True)
        first = jnp.min(jnp.where(vals == mx, lane, float(n_experts)), axis=1, keepdims=True)
        pick = lane == first
        picks.append(pick)
        firsts.append(first)
        raws.append(jnp.sum(jnp.where(pick, scores, 0.0), axis=1, keepdims=True))
        vals = jnp.where(pick, -jnp.inf, vals)
        sel = jnp.where(pick, 1.0, sel)
    denom = raws[0]
    for r in raws[1:]:
        denom = denom + r

    r_i = lax.broadcasted_iota(jnp.int32, (tm, tm), 0)
    c_i = lax.broadcasted_iota(jnp.int32, (tm, tm), 1)
    before = jnp.where(c_i < r_i, 1.0, 0.0).astype(BF16)
    rank = jnp.dot(before, sel.astype(BF16), preferred_element_type=F32) + run_ref[...]
    run_ref[...] = run_ref[...] + jnp.sum(sel, axis=0, keepdims=True)
    cnt_ref[...] = run_ref[...]

    slot = lax.broadcasted_iota(jnp.int32, (tm, LANES), 1)
    idx_w = jnp.zeros((tm, LANES), F32)
    loc_w = jnp.zeros((tm, LANES), F32)
    gate_w = jnp.zeros((tm, LANES), F32)
    for k in range(TOP_K):
        loc = jnp.sum(jnp.where(picks[k], rank, 0.0), axis=1, keepdims=True)
        idx_w = jnp.where(slot == k, firsts[k], idx_w)
        loc_w = jnp.where(slot == k, loc, loc_w)
        gate_w = jnp.where(slot == k, raws[k] / denom * ROUTED_SCALE, gate_w)
    idx_ref[...] = idx_w.T[:IDX_ROWS].astype(jnp.int32)
    loc_ref[...] = loc_w.T[:IDX_ROWS].astype(jnp.int32)
    gate_ref[...] = gate_w[:, :IDX_ROWS]


def _postmix(oa, ys, x2, wglu, woa, wob, ga, gs, gf, wr, rb, *, tm):
    t, d = x2.shape
    da = oa.shape[1]
    dsm = ys.shape[1]
    e = rb.shape[1]
    row = lambda i: (i, 0)
    col = lambda i: (0, i)
    fix = lambda i: (0, 0)
    return pl.pallas_call(
        functools.partial(_postmix_kernel, n_experts=e),
        grid=(t // tm,),
        in_specs=[
            pl.BlockSpec((tm, da), row), pl.BlockSpec((tm, dsm), row), pl.BlockSpec((tm, d), row),
            pl.BlockSpec((dsm, dsm), fix), pl.BlockSpec((da, d), fix), pl.BlockSpec((dsm, d), fix),
            pl.BlockSpec((1, da), fix), pl.BlockSpec((1, dsm), fix), pl.BlockSpec((1, d), fix),
            pl.BlockSpec((d, 2 * e), fix), pl.BlockSpec((1, e), fix),
        ],
        out_specs=[
            pl.BlockSpec((tm, d), row), pl.BlockSpec((IDX_ROWS, tm), col), pl.BlockSpec((tm, IDX_ROWS), row),
            pl.BlockSpec((IDX_ROWS, tm), col), pl.BlockSpec((1, e), fix),
        ],
        out_shape=[
            jax.ShapeDtypeStruct((t, d), F32), jax.ShapeDtypeStruct((IDX_ROWS, t), jnp.int32),
            jax.ShapeDtypeStruct((t, IDX_ROWS), F32), jax.ShapeDtypeStruct((IDX_ROWS, t), jnp.int32),
            jax.ShapeDtypeStruct((1, e), F32),
        ],
        scratch_shapes=[pltpu.VMEM((1, e), F32)],
        compiler_params=_params(("arbitrary",)),
        name="postmix_router",
    )(oa, ys, x2, wglu, woa, wob, ga, gs, gf, wr, rb)


def _rows_to_tiles(ref, packed, sub):
    rows = packed.shape[0]
    for s in range(sub):
        ref[pl.ds(s, rows, stride=sub), :] = packed[:, s * LANES:(s + 1) * LANES]


def _tiles_to_rows(ref, rows, sub):
    return jnp.concatenate([ref[pl.ds(s, rows, stride=sub), :] for s in range(sub)], axis=1)


def _tile_rows(ref, r, sub):
    start = r * sub if isinstance(r, int) else pl.multiple_of(r * sub, sub)
    return ref.at[pl.ds(start, sub)]


def _dispatch_kernel(dest_ref, zflag_ref, h1_ref, gf_ref, wg_ref, wu_ref, wd_ref, h1s_ref, xs_ref,
                     m0_ref, m1_ref, zero_ref, sem, zsem, *, sub):
    tm = h1_ref.shape[0]
    zrows = zero_ref.shape[0]
    n_blocks = zflag_ref.shape[0]

    @pl.when(pl.program_id(0) == 0)
    def _():
        zero_ref[...] = jnp.zeros(zero_ref.shape, zero_ref.dtype)

        def zcopy(b):
            return pltpu.make_async_copy(zero_ref, xs_ref.at[pl.ds(pl.multiple_of(b * zrows, zrows), zrows)], zsem)

        def zstart(b, carry):
            @pl.when(zflag_ref[b] != 0)
            def _():
                zcopy(b).start()
            return carry

        def zwait(b, carry):
            @pl.when(zflag_ref[b] != 0)
            def _():
                zcopy(b).wait()
            return carry

        lax.fori_loop(0, n_blocks, zstart, 0)
        lax.fori_loop(0, n_blocks, zwait, 0)

    step = pl.program_id(0)
    slot = lax.rem(step, 2)
    ms = (m0_ref, m1_ref)

    def drain(s):
        for _ in range(TOP_K):
            pltpu.make_async_copy(ms[s], xs_ref.at[pl.ds(0, tm * sub)], sem.at[s]).wait()

    def tile(cur):
        h1 = h1_ref[...]
        m = _rms(h1, gf_ref[...])
        _rows_to_tiles(ms[cur], _pack_halves(m), sub)
        for t in range(tm):
            src = _tile_rows(ms[cur], t, sub)
            for k in range(TOP_K):
                pltpu.make_async_copy(src, _tile_rows(xs_ref, dest_ref[k * tm + t], sub), sem.at[cur]).start()
        mb = m.astype(BF16)
        a = jnp.dot(mb, wg_ref[...], preferred_element_type=F32)
        b = jnp.dot(mb, wu_ref[...], preferred_element_type=F32)
        hid = (a * _sigmoid(a) * b).astype(BF16)
        h1s_ref[...] = h1 + jnp.dot(hid, wd_ref[...], preferred_element_type=F32)

        @pl.when(step > 0)
        def _():
            drain(1 - cur)

        @pl.when(step == pl.num_programs(0) - 1)
        def _():
            drain(cur)

    for cur in range(2):
        @pl.when(slot == cur)
        def _(cur=cur):
            tile(cur)


def _dispatch(dest, zflag, h1, gf, wg, wu, wd, *, tm, rb, n_pad):
    t, d = h1.shape
    f = wg.shape[1]
    sub = d // 2 // LANES
    row = lambda i: (i, 0)
    fix = lambda i: (0, 0)
    return pl.pallas_call(
        functools.partial(_dispatch_kernel, sub=sub),
        grid=(t // tm,),
        in_specs=[
            pl.BlockSpec((IDX_ROWS * tm,), lambda i: (i,), memory_space=pltpu.SMEM),
            pl.BlockSpec(memory_space=pltpu.SMEM),
            pl.BlockSpec((tm, d), row), pl.BlockSpec((1, d), fix),
            pl.BlockSpec((d, f), fix), pl.BlockSpec((d, f), fix), pl.BlockSpec((f, d), fix),
        ],
        out_specs=[pl.BlockSpec((tm, d), row), pl.BlockSpec(memory_space=pl.ANY)],
        out_shape=[jax.ShapeDtypeStruct((t, d), F32), jax.ShapeDtypeStruct((n_pad * sub, LANES), jnp.int32)],
        scratch_shapes=[pltpu.VMEM((tm * sub, LANES), jnp.int32), pltpu.VMEM((tm * sub, LANES), jnp.int32),
                        pltpu.VMEM((rb * sub, LANES), jnp.int32),
                        pltpu.SemaphoreType.DMA((2,)), pltpu.SemaphoreType.DMA(())],
        compiler_params=_params(("arbitrary",)),
        name="dispatch_shared",
    )(dest, zflag, h1, gf, wg, wu, wd)


def _experts_kernel(be_ref, first_ref, nxt_ref, nu_ref, xs_ref, wg_hbm, wu_hbm, wd_hbm, ys_ref,
                    wgf_ref, wuf_ref, wdf_ref, wgb_ref, wub_ref, wdb_ref, sem, *, sub):
    i = pl.program_id(0)
    tm = xs_ref.shape[0] // sub

    def weight_copies(e, s):
        copies = []
        for j, (src, dst) in enumerate(((wg_hbm, wgf_ref), (wu_hbm, wuf_ref), (wd_hbm, wdf_ref))):
            rows = dst.shape[1] // WEIGHT_DMA_SPLIT
            for c in range(WEIGHT_DMA_SPLIT):
                sl = pl.ds(c * rows, rows)
                copies.append(pltpu.make_async_copy(src.at[e, sl], dst.at[s, sl], sem.at[s, j]))
        return copies

    @pl.when(i == 0)
    def _():
        for c in weight_copies(be_ref[0], 0):
            c.start()

    @pl.when((i < nu_ref[0]) & (first_ref[i] != 0))
    def _():
        s = first_ref[i] - 1
        for c in weight_copies(be_ref[i], s):
            c.wait()

        @pl.when(nxt_ref[i] >= 0)
        def _():
            for c in weight_copies(nxt_ref[i], 1 - s):
                c.start()

        wgb_ref[...] = wgf_ref[s].astype(BF16)
        wub_ref[...] = wuf_ref[s].astype(BF16)
        wdb_ref[...] = wdf_ref[s].astype(BF16)

    @pl.when(i < nu_ref[0])
    def _():
        lo, hi = _unpack_halves(_tiles_to_rows(xs_ref, tm, sub))
        lo = lo.astype(BF16)
        hi = hi.astype(BF16)
        n = lo.shape[1]
        a = (jnp.dot(lo, wgb_ref[:n, :], preferred_element_type=F32)
             + jnp.dot(hi, wgb_ref[n:, :], preferred_element_type=F32))
        b = (jnp.dot(lo, wub_ref[:n, :], preferred_element_type=F32)
             + jnp.dot(hi, wub_ref[n:, :], preferred_element_type=F32))
        hid = (a * _sigmoid(a) * b).astype(BF16)
        _rows_to_tiles(ys_ref, _pack_halves(jnp.dot(hid, wdb_ref[...], preferred_element_type=F32)), sub)

    @pl.when(i >= nu_ref[0])
    def _():
        ys_ref[...] = jnp.zeros(ys_ref.shape, ys_ref.dtype)


def _experts(blk_e, first, nxt, n_used, xs, wg, wu, wd, *, tm):
    rows_sub, _ = xs.shape
    _, d, f = wg.shape
    sub = d // 2 // LANES
    n_blocks = rows_sub // (tm * sub)
    rows = lambda i, be, fi, nx, nu: (jnp.maximum(jnp.minimum(i, nu[0] - 1), 0), 0)
    orow = lambda i, be, fi, nx, nu: (i, 0)
    return pl.pallas_call(
        functools.partial(_experts_kernel, sub=sub),
        grid_spec=pltpu.PrefetchScalarGridSpec(
            num_scalar_prefetch=4,
            grid=(n_blocks,),
            in_specs=[
                pl.BlockSpec((tm * sub, LANES), rows),
                pl.BlockSpec(memory_space=pl.ANY), pl.BlockSpec(memory_space=pl.ANY),
                pl.BlockSpec(memory_space=pl.ANY),
            ],
            out_specs=pl.BlockSpec((tm * sub, LANES), orow),
            scratch_shapes=[pltpu.VMEM((2, d, f), F32), pltpu.VMEM((2, d, f), F32), pltpu.VMEM((2, f, d), F32),
                            pltpu.VMEM((d, f), BF16), pltpu.VMEM((d, f), BF16), pltpu.VMEM((f, d), BF16),
                            pltpu.SemaphoreType.DMA((2, 3))],
        ),
        out_shape=jax.ShapeDtypeStruct(xs.shape, jnp.int32),
        compiler_params=_params(("arbitrary",)),
        name="routed_experts",
    )(blk_e, first, nxt, n_used, xs, wg, wu, wd)


def _final_kernel(dest_ref, dnext_ref, h_ref, gate_ref, p_ref, gp_ref, wgate_ref, wproj_ref, ys_ref, o_ref,
                  buf0_ref, buf1_ref, sem, *, sub):
    tm = h_ref.shape[0]
    step = pl.program_id(0)
    slot = lax.rem(step, 2)
    bufs = (buf0_ref, buf1_ref)

    def row_copy(idx_ref, t, k, s):
        return pltpu.make_async_copy(_tile_rows(ys_ref, idx_ref[k * tm + t], sub),
                                     _tile_rows(bufs[s].at[k], t, sub), sem.at[s])

    def drain(s):
        for k in range(TOP_K):
            pltpu.make_async_copy(ys_ref.at[pl.ds(0, tm * sub)], bufs[s].at[k], sem.at[s]).wait()

    @pl.when(step == 0)
    def _():
        def issue(t, carry):
            for k in range(TOP_K):
                row_copy(dest_ref, t, k, 0).start()
            return carry

        lax.fori_loop(0, tm, issue, 0)

    def tile(cur, nxt):
        drain(cur)
        for t in range(tm):
            for k in range(TOP_K):
                row_copy(dnext_ref, t, k, nxt).start()
        pp = jnp.dot(p_ref[...].astype(BF16), wproj_ref[...], preferred_element_type=F32)
        gate = gate_ref[...]
        h = h_ref[...]
        n = h.shape[1] // 2
        acc_lo = h[:, :n]
        acc_hi = h[:, n:]
        for k in range(TOP_K):
            lo, hi = _unpack_halves(_tiles_to_rows(bufs[cur].at[k], tm, sub))
            acc_lo = acc_lo + gate[:, k:k + 1] * lo
            acc_hi = acc_hi + gate[:, k:k + 1] * hi
        h2 = jnp.concatenate([acc_lo, acc_hi], axis=1)
        nrm = _rms(h2, gp_ref[...]).astype(BF16)
        o_ref[...] = h2 + _sigmoid(jnp.dot(nrm, wgate_ref[...], preferred_element_type=F32)) * pp

    for cur in range(2):
        @pl.when(slot == cur)
        def _(cur=cur):
            tile(cur, 1 - cur)

            @pl.when(step == pl.num_programs(0) - 1)
            def _():
                drain(1 - cur)


def _final(dest, h1s, gate, p2, gp, wgate, wproj, ys, *, tm):
    t, d = h1s.shape
    dp = p2.shape[1]
    sub = d // 2 // LANES
    row = lambda i: (i, 0)
    fix = lambda i: (0, 0)
    return pl.pallas_call(
        functools.partial(_final_kernel, sub=sub),
        grid=(t // tm,),
        in_specs=[
            pl.BlockSpec((IDX_ROWS * tm,), lambda i: (i,), memory_space=pltpu.SMEM),
            pl.BlockSpec((IDX_ROWS * tm,), lambda i: (jnp.minimum(i + 1, t // tm - 1),), memory_space=pltpu.SMEM),
            pl.BlockSpec((tm, d), row), pl.BlockSpec((tm, IDX_ROWS), row), pl.BlockSpec((tm, dp), row),
            pl.BlockSpec((1, d), fix), pl.BlockSpec((d, d), fix), pl.BlockSpec((dp, d), fix),
            pl.BlockSpec(memory_space=pl.ANY),
        ],
        out_specs=pl.BlockSpec((tm, d), row),
        out_shape=jax.ShapeDtypeStruct((t, d), F32),
        scratch_shapes=[pltpu.VMEM((TOP_K, tm * sub, LANES), jnp.int32),
                        pltpu.VMEM((TOP_K, tm * sub, LANES), jnp.int32), pltpu.SemaphoreType.DMA((2,))],
        compiler_params=_params(("arbitrary",)),
        name="combine_ple",
    )(dest, dest, h1s, gate, p2, gp, wgate, wproj, ys)


def _largest_tile(n, cap):
    t = min(n, cap)
    while n % t:
        t //= 2
    return t


def _layer(h, p_l, norm_mix, w_in, q_norm, k_norm, lam_re, lam_im, log_dt, b_re, b_im, c_re, c_im, d_skip,
           w_glu, attn_out_norm, ssm_out_norm, w_out, norm_ffn, w_router, router_bias, w_exp_gate,
           w_exp_up, w_exp_down, w_sh_gate, w_sh_up, w_sh_down, norm_ple, w_ple_gate, w_ple_proj):
    bsz, seq, d = h.shape
    t = bsz * seq
    head_dim = q_norm.shape[-1]
    d_attn = attn_out_norm.shape[-1]
    n_heads = d_attn // head_dim
    n_experts = w_router.shape[-1]
    x2 = h.reshape(t, d)

    tn = d_attn if d_attn % 128 == 0 else head_dim
    scale = 1.0 / math.sqrt(head_dim)
    hg = jnp.concatenate([jnp.tile(q_norm * scale, n_heads), jnp.tile(k_norm, n_heads),
                          jnp.ones((w_in.shape[1] - 2 * d_attn,), F32)])[None, :]
    z, u = _inproj(x2, norm_mix[None, :], w_in.astype(BF16), hg, d_qkv=3 * d_attn,
                   n_norm_tiles=2 * d_attn // tn, head_dim=head_dim, tm=_largest_tile(t, 1024), tn=tn)

    o_attn = _attention(z, batch=bsz, seq=seq, n_heads=n_heads, head_dim=head_dim, blk=_largest_tile(seq, 256))

    n_chunks = seq // S5_CHUNK
    mats = _s5_prepare(lam_re, lam_im, log_dt, b_re, b_im, c_re, c_im, d_skip, n_chunks)
    y_ssm = _s5(u, *mats, n_chunks=n_chunks)

    tm = _largest_tile(t, 256)
    w_out_b = w_out.astype(BF16)
    wr_hi = w_router.astype(BF16)
    wr_lo = (w_router - wr_hi.astype(F32)).astype(BF16)
    h1, idx, gate, loc, counts = _postmix(
        o_attn, y_ssm, x2, w_glu.astype(BF16), w_out_b[:d_attn], w_out_b[d_attn:],
        attn_out_norm[None, :], ssm_out_norm[None, :], norm_ffn[None, :],
        jnp.concatenate([wr_hi, wr_lo], axis=1), router_bias[None, :], tm=_largest_tile(t, 512))

    rb = _largest_tile(t, 256)
    n_pad = t * TOP_K + n_experts * rb
    n_blocks = n_pad // rb
    cnt = counts[0].astype(jnp.int32)
    pcnt = (cnt + rb - 1) // rb * rb
    pend = jnp.cumsum(pcnt)
    pstart = pend - pcnt
    onehot = idx[:, :, None] == jnp.arange(n_experts, dtype=jnp.int32)[None, None, :]
    dest = jnp.sum(jnp.where(onehot, pstart[None, None, :], 0), axis=-1) + loc
    dest = dest.reshape(IDX_ROWS, t // tm, tm).transpose(1, 0, 2).reshape(-1)
    n_used = (pend[-1] // rb).astype(jnp.int32)
    blk_ids = jnp.arange(n_blocks, dtype=jnp.int32)
    starts = (blk_ids * rb)[:, None]
    owner = (starts >= pstart[None, :]) & (starts < pend[None, :])
    blk_e = jnp.sum(jnp.where(owner, jnp.arange(n_experts, dtype=jnp.int32)[None, :], 0), axis=1)
    is_last = jnp.any(owner & (starts + rb == pend[None, :]), axis=1)
    is_first = jnp.any(owner & (starts == pstart[None, :]), axis=1).astype(jnp.int32)
    zflag = ((blk_ids >= n_used) | is_last).astype(jnp.int32)
    first_pos = jnp.where(is_first > 0, blk_ids, n_blocks)
    nxt_pos = jnp.concatenate([lax.cummin(first_pos, axis=0, reverse=True)[1:],
                               jnp.full((1,), n_blocks, jnp.int32)])
    nxt_hit = nxt_pos[:, None] == blk_ids[None, :]
    nxt = jnp.where(nxt_pos < n_blocks, jnp.sum(jnp.where(nxt_hit, blk_e[None, :], 0), axis=1), -1)
    last_e = jnp.sum(jnp.where(blk_ids == n_used - 1, blk_e, 0))
    blk_e = jnp.where(blk_ids < n_used, blk_e, last_e).astype(jnp.int32)
    first_code = jnp.where(is_first > 0, 1 + (jnp.cumsum(is_first) - 1) % 2, 0).astype(jnp.int32)

    h1s, xs = _dispatch(dest, zflag, h1, norm_ffn[None, :], w_sh_gate.astype(BF16), w_sh_up.astype(BF16),
                        w_sh_down.astype(BF16), tm=tm, rb=rb, n_pad=n_pad)
    ys = _experts(blk_e, first_code, nxt.astype(jnp.int32), n_used[None], xs, w_exp_gate, w_exp_up, w_exp_down,
                  tm=rb)
    out = _final(dest, h1s, gate, p_l.reshape(t, -1), norm_ple[None, :], w_ple_gate.astype(BF16),
                 w_ple_proj.astype(BF16), ys, tm=tm)
    return out.reshape(bsz, seq, d)


def kernel(x, p, norm_mix, w_in, q_norm, k_norm, ssm_lam_re, ssm_lam_im, ssm_log_dt, ssm_b_re, ssm_b_im,
           ssm_c_re, ssm_c_im, ssm_d, w_glu, attn_out_norm, ssm_out_norm, w_out, norm_ffn, w_router,
           router_bias, w_exp_gate, w_exp_up, w_exp_down, w_sh_gate, w_sh_up, w_sh_down, norm_ple,
           w_ple_gate, w_ple_proj):
    h = x
    for i in range(p.shape[0]):
        h = _layer(h, p[i], norm_mix[i], w_in[i], q_norm[i], k_norm[i], ssm_lam_re[i], ssm_lam_im[i],
                   ssm_log_dt[i], ssm_b_re[i], ssm_b_im[i], ssm_c_re[i], ssm_c_im[i], ssm_d[i], w_glu[i],
                   attn_out_norm[i], ssm_out_norm[i], w_out[i], norm_ffn[i], w_router[i], router_bias[i],
                   w_exp_gate[i], w_exp_up[i], w_exp_down[i], w_sh_gate[i], w_sh_up[i], w_sh_down[i],
                   norm_ple[i], w_ple_gate[i], w_ple_proj[i])
    return h
```

```python
import functools
import math

import jax
import jax.numpy as jnp
from jax import lax
from jax.experimental import pallas as pl
from jax.experimental.pallas import tpu as pltpu

NORM_EPS = 1e-6
TOP_K = 6
ROUTED_SCALE = 2.5
IDX_ROWS = 8
LANES = 128
S5_CHUNK = 8
WEIGHT_DMA_SPLIT = 4
POSTMIX_ROW_GROUPS = 2
VMEM_LIMIT = 56 * 1024 * 1024

F32 = jnp.float32
BF16 = jnp.bfloat16
HIGH_HALF = -65536


def _rms(x, g):
    return x * lax.rsqrt(jnp.mean(x * x, axis=-1, keepdims=True) + NORM_EPS) * g


def _sigmoid(x):
    return 1.0 / (1.0 + jnp.exp(-x))


def _params(sem):
    return pltpu.CompilerParams(dimension_semantics=sem, vmem_limit_bytes=VMEM_LIMIT)


def _pack_halves(x):
    n = x.shape[1] // 2
    lo = lax.bitcast_convert_type(x[:, :n].astype(BF16).astype(F32), jnp.int32)
    hi = lax.bitcast_convert_type(x[:, n:].astype(BF16).astype(F32), jnp.int32)
    return (hi & HIGH_HALF) | lax.shift_right_logical(lo, 16)


def _unpack_halves(p):
    lo = lax.bitcast_convert_type(lax.shift_left(p, 16), F32)
    hi = lax.bitcast_convert_type(p & HIGH_HALF, F32)
    return lo, hi


def _inproj_kernel(x_ref, g_ref, w_ref, hg_ref, o_ref, u_ref, xn_ref, *, n_norm_tiles, n_qkv_tiles, head_dim):
    j = pl.program_id(1)

    @pl.when(j == 0)
    def _():
        xn_ref[...] = _rms(x_ref[...], g_ref[...]).astype(BF16)

    acc = jnp.dot(xn_ref[...], w_ref[...], preferred_element_type=F32)
    tn = acc.shape[1]

    @pl.when(j < n_norm_tiles)
    def _():
        hg = hg_ref[...]
        for h in range(tn // head_dim):
            sl = slice(h * head_dim, (h + 1) * head_dim)
            o_ref[:, sl] = _rms(acc[:, sl], hg[:, sl]).astype(o_ref.dtype)

    @pl.when((j >= n_norm_tiles) & (j < n_qkv_tiles))
    def _():
        o_ref[...] = acc.astype(o_ref.dtype)

    @pl.when(j >= n_qkv_tiles)
    def _():
        u_ref[...] = acc


def _inproj(x2, g, w, hg, *, d_qkv, n_norm_tiles, head_dim, tm, tn):
    t, d = x2.shape
    n = w.shape[1]
    nq = d_qkv // tn
    return pl.pallas_call(
        functools.partial(_inproj_kernel, n_norm_tiles=n_norm_tiles, n_qkv_tiles=nq, head_dim=head_dim),
        grid=(t // tm, n // tn),
        in_specs=[
            pl.BlockSpec((tm, d), lambda i, j: (i, 0)),
            pl.BlockSpec((1, d), lambda i, j: (0, 0)),
            pl.BlockSpec((d, tn), lambda i, j: (0, j)),
            pl.BlockSpec((1, tn), lambda i, j: (0, j)),
        ],
        out_specs=[
            pl.BlockSpec((tm, tn), lambda i, j: (i, jnp.minimum(j, nq - 1))),
            pl.BlockSpec((tm, tn), lambda i, j: (i, jnp.maximum(j - nq, 0))),
        ],
        out_shape=[jax.ShapeDtypeStruct((t, d_qkv), BF16), jax.ShapeDtypeStruct((t, n - d_qkv), F32)],
        scratch_shapes=[pltpu.VMEM((tm, d), BF16)],
        compiler_params=_params(("parallel", "arbitrary")),
        name="inproj",
    )(x2, g, w, hg)


SKIP_AFTER = 110.0


def _attn_kernel(q_ref, k_ref, v_ref, o_ref, *, blk, hd, heads):
    i = pl.program_id(2)
    row = lax.broadcasted_iota(jnp.int32, (blk, blk), 0)
    col = lax.broadcasted_iota(jnp.int32, (blk, blk), 1)
    causal = col < row
    later = jnp.where(row > col, 1.0, 0.0).astype(BF16)
    lanes = [slice(h * hd, (h + 1) * hd) for h in range(heads)]
    qs = [q_ref[:, sl] for sl in lanes]

    def step(kb, accs, runs, diag):
        start = pl.multiple_of(kb * blk, blk)
        zs, sps, inners, new_accs, new_runs = [], [], [], [], []
        for h, sl in enumerate(lanes):
            k = k_ref[pl.ds(start, blk), sl]
            zs.append(lax.dot_general(qs[h], k, (((1,), (1,)), ((), ())), preferred_element_type=F32))
        for z in zs:
            sp = jnp.maximum(z, 0.0) + jnp.log(1.0 + jnp.exp(-jnp.abs(z)))
            sps.append(jnp.where(causal, sp, 0.0) if diag else sp)
        for sp in sps:
            hi = sp.astype(BF16)
            lo = (sp - hi.astype(F32)).astype(BF16)
            inners.append(jnp.dot(hi, later, preferred_element_type=F32)
                          + jnp.dot(lo, later, preferred_element_type=F32))
        for h, sl in enumerate(lanes):
            w = jnp.exp(zs[h] - sps[h] - (inners[h] + runs[h]))
            if diag:
                w = jnp.where(causal, w, 0.0)
            v = v_ref[pl.ds(start, blk), sl]
            new_accs.append(accs[h] + jnp.dot(w.astype(BF16), v, preferred_element_type=F32))
            new_runs.append(runs[h] + jnp.sum(sps[h], axis=-1, keepdims=True))
        return tuple(new_accs), tuple(new_runs)

    def keep_going(runs):
        low = runs[0]
        for r in runs[1:]:
            low = jnp.minimum(low, r)
        return (jnp.min(low) < SKIP_AFTER).astype(jnp.int32)

    accs = tuple(jnp.zeros((blk, hd), F32) for _ in lanes)
    runs = tuple(jnp.zeros((blk, 1), F32) for _ in lanes)
    accs, runs = step(i, accs, runs, True)

    def cond(c):
        n, go, _, _ = c
        return (n < i) & (go > 0)

    def body(c):
        n, _, accs, runs = c
        accs, runs = step(i - 1 - n, accs, runs, False)
        return n + 1, keep_going(runs), accs, runs

    _, _, accs, _ = lax.while_loop(cond, body, (jnp.int32(0), keep_going(runs), accs, runs))
    for h, sl in enumerate(lanes):
        o_ref[:, sl] = accs[h].astype(o_ref.dtype)


def _attention(z, *, batch, seq, n_heads, head_dim, blk):
    nq = seq // blk
    heads = next(c for c in (4, 2, 1) if n_heads % c == 0)
    hw = heads * head_dim
    ng = n_heads // heads
    return pl.pallas_call(
        functools.partial(_attn_kernel, blk=blk, hd=head_dim, heads=heads),
        grid=(batch, ng, nq),
        in_specs=[
            pl.BlockSpec((blk, hw), lambda b, h, i: (b * nq + i, h)),
            pl.BlockSpec((seq, hw), lambda b, h, i: (b, ng + h)),
            pl.BlockSpec((seq, hw), lambda b, h, i: (b, 2 * ng + h)),
        ],
        out_specs=pl.BlockSpec((blk, hw), lambda b, h, i: (b * nq + i, h)),
        out_shape=jax.ShapeDtypeStruct((batch * seq, n_heads * head_dim), BF16),
        compiler_params=_params(("parallel", "parallel", "arbitrary")),
        name="sb_attention",
    )(z, z, z)


def _s5_prepare(lam_re, lam_im, log_dt, b_re, b_im, c_re, c_im, d_skip, n_chunks):
    hp = lax.Precision.HIGHEST
    chunk = S5_CHUNK
    g, p = lam_re.shape
    h = b_re.shape[-1]
    gg = LANES // h
    o = g // gg
    dt = jnp.exp(log_dt)[:, None]
    ks = jnp.arange(chunk + 1, dtype=F32)[None, :, None]
    mag = jnp.exp(lam_re[:, None, :] * dt[:, None, :] * ks)
    ang = lam_im[:, None, :] * dt[:, None, :] * ks
    ak_re = mag * jnp.cos(ang)
    ak_im = mag * jnp.sin(ang)
    nr = ak_re[:, 1] - 1.0
    ni = ak_im[:, 1]
    den = lam_re * lam_re + lam_im * lam_im
    coef_re = ((nr * lam_re + ni * lam_im) / den)[..., None]
    coef_im = ((ni * lam_re - nr * lam_im) / den)[..., None]
    bbar_re = coef_re * b_re - coef_im * b_im
    bbar_im = coef_re * b_im + coef_im * b_re
    ca_re = c_re[:, None] * ak_re[:, :, None, :] - c_im[:, None] * ak_im[:, :, None, :]
    ca_im = c_re[:, None] * ak_im[:, :, None, :] + c_im[:, None] * ak_re[:, :, None, :]

    cam_re = ca_re[:, :chunk].transpose(0, 3, 1, 2).reshape(g, p, chunk * h)
    cam_im = ca_im[:, :chunk].transpose(0, 3, 1, 2).reshape(g, p, chunk * h)
    kt = (jnp.einsum("gph,gpm->ghm", bbar_re, cam_re, precision=hp)
          - jnp.einsum("gph,gpm->ghm", bbar_im, cam_im, precision=hp))
    def same_group(n_rows, row_block, n_cols, col_block):
        r = (jnp.arange(n_rows) // row_block) % gg
        c = (jnp.arange(n_cols) // col_block) % gg
        return (r[:, None] == c[None, :]).astype(F32)

    tile_h = jnp.tile(jnp.eye(h, dtype=F32), (1, gg))
    tile_p = jnp.tile(jnp.eye(p, dtype=F32), (1, gg))
    kt = kt.reshape(o, gg, h, chunk, h).transpose(0, 3, 1, 2, 4).reshape(o, chunk, LANES, h)
    bd = jnp.einsum("okrh,hc->okrc", kt, tile_h, precision=hp) * same_group(LANES, h, LANES, h)
    dvec = d_skip.reshape(o, LANES)
    lags = [bd[:, k] for k in range(chunk)]
    lags[0] = lags[0] + dvec[:, :, None] * jnp.eye(LANES, dtype=F32)[None]
    zero_blk = jnp.zeros_like(lags[0])
    m_intra = jnp.concatenate(
        [jnp.concatenate([lags[j - i] if j >= i else zero_blk for j in range(chunk)], axis=2)
         for i in range(chunk)], axis=1)

    def expand_in(ak_part_a, ak_part_b, b_a, b_b, sign):
        rk_a = jnp.stack([ak_part_a[:, chunk - 1 - i] for i in range(chunk)], axis=1)
        rk_b = jnp.stack([ak_part_b[:, chunk - 1 - i] for i in range(chunk)], axis=1)
        val = (rk_a[:, :, None, :] * b_a.transpose(0, 2, 1)[:, None]
               + sign * rk_b[:, :, None, :] * b_b.transpose(0, 2, 1)[:, None])
        val = val.reshape(o, gg, chunk, h, p).transpose(0, 2, 1, 3, 4).reshape(o, chunk * LANES, p)
        wide = jnp.einsum("orp,pc->orc", val.astype(BF16), tile_p.astype(BF16), preferred_element_type=BF16)
        return wide * same_group(chunk * LANES, h, gg * p, p).astype(BF16)

    m_in = jnp.concatenate([expand_in(ak_re, ak_im, bbar_re, bbar_im, -1.0),
                            expand_in(ak_re, ak_im, bbar_im, bbar_re, 1.0)], axis=-1)

    def expand_out(ca):
        val = ca[:, 1:].reshape(o, gg, chunk, h, p).transpose(0, 4, 2, 1, 3).reshape(o, p, chunk * LANES)
        wide = jnp.einsum("pc,opr->ocr", tile_p.astype(BF16), val.astype(BF16), preferred_element_type=BF16)
        return wide * same_group(gg * p, p, chunk * LANES, h).astype(BF16)

    m_out = jnp.concatenate([expand_out(ca_re), -expand_out(ca_im)], axis=1)

    steps = max(1, int(math.ceil(math.log2(max(n_chunks, 2)))))
    cr, ci = ak_re[:, chunk].reshape(o, gg * p), ak_im[:, chunk].reshape(o, gg * p)
    sc_a, sc_b = [], []
    for _ in range(steps):
        sc_a.append(jnp.concatenate([cr, cr], axis=-1))
        sc_b.append(jnp.concatenate([-ci, ci], axis=-1))
        cr, ci = cr * cr - ci * ci, 2.0 * cr * ci
    return m_intra.astype(BF16), m_in, m_out, jnp.stack(sc_a, axis=1), jnp.stack(sc_b, axis=1)


def _s5_kernel(u_ref, mi_ref, min_ref, mout_ref, sa_ref, sb_ref, y_ref, *, n_chunks, steps):
    chunk = S5_CHUNK
    n = u_ref.shape[0] // chunk
    u = jnp.concatenate([u_ref[pl.ds(i, n, stride=chunk), :] for i in range(chunk)], axis=1).astype(BF16)
    y = jnp.dot(u, mi_ref[...], preferred_element_type=F32)
    x = jnp.dot(u, min_ref[...], preferred_element_type=F32)
    half = x.shape[1] // 2
    c = lax.rem(lax.broadcasted_iota(jnp.int32, x.shape, 0), n_chunks)
    sa = sa_ref[...]
    sb = sb_ref[...]
    for k in range(steps):
        sh = 1 << k
        xs = jnp.where(c >= sh, pltpu.roll(x, sh, axis=0), 0.0)
        x = x + xs * sa[k:k + 1, :] + pltpu.roll(xs, half, axis=1) * sb[k:k + 1, :]
    s_in = jnp.where(c >= 1, pltpu.roll(x, 1, axis=0), 0.0)
    y = y + jnp.dot(s_in.astype(BF16), mout_ref[...], preferred_element_type=F32)
    for i in range(chunk):
        y_ref[pl.ds(i, n, stride=chunk), :] = y[:, i * LANES:(i + 1) * LANES]


def _s5(u, m_intra, m_in, m_out, sc_a, sc_b, *, n_chunks):
    t, d_ssm = u.shape
    o, lh, st = m_in.shape
    steps = sc_a.shape[1]
    mat = lambda i: (i, 0, 0)
    return pl.pallas_call(
        functools.partial(_s5_kernel, n_chunks=n_chunks, steps=steps),
        grid=(o,),
        in_specs=[
            pl.BlockSpec((t, LANES), lambda i: (0, i)),
            pl.BlockSpec((None, lh, lh), mat), pl.BlockSpec((None, lh, st), mat), pl.BlockSpec((None, st, lh), mat),
            pl.BlockSpec((None, steps, st), mat), pl.BlockSpec((None, steps, st), mat),
        ],
        out_specs=pl.BlockSpec((t, LANES), lambda i: (0, i)),
        out_shape=jax.ShapeDtypeStruct((t, d_ssm), F32),
        compiler_params=_params(("parallel",)),
        name="s5_chunked",
    )(u, m_intra, m_in, m_out, sc_a, sc_b)


def _postmix_kernel(oa_ref, ys_ref, x_ref, wglu_ref, woa_ref, wob_ref, ga_ref, gs_ref, gf_ref,
                    wr_ref, rb_ref, h1_ref, idx_ref, gate_ref, loc_ref, cnt_ref, run_ref, *, n_experts):
    step = pl.program_id(0)

    @pl.when(step == 0)
    def _():
        run_ref[...] = jnp.zeros(run_ref.shape, F32)

    tm = x_ref.shape[0]
    n_groups = POSTMIX_ROW_GROUPS if tm % (8 * POSTMIX_ROW_GROUPS) == 0 else 1
    rows = [pl.ds(g * (tm // n_groups), tm // n_groups) for g in range(n_groups)]

    def dot(a, w_ref):
        return jnp.dot(a, w_ref[...], preferred_element_type=F32)

    ys = [ys_ref[r, :] for r in rows]
    ys = [0.5 * y * (1.0 + jnp.tanh(math.sqrt(2.0 / math.pi) * (y + 0.044715 * (y * y * y)))) for y in ys]
    glu = [dot(y.astype(BF16), wglu_ref) for y in ys]
    ys = [y * _sigmoid(g) for y, g in zip(ys, glu)]
    nas = [_rms(oa_ref[r, :].astype(F32), ga_ref[...]).astype(BF16) for r in rows]
    nss = [_rms(y, gs_ref[...]).astype(BF16) for y in ys]
    h1s = [x_ref[r, :] + dot(na, woa_ref) + dot(ns, wob_ref) for r, na, ns in zip(rows, nas, nss)]
    for r, h1 in zip(rows, h1s):
        h1_ref[r, :] = h1

    ms = [_rms(h1, gf_ref[...]) for h1 in h1s]
    m_his = [m.astype(BF16) for m in ms]
    m_los = [(m - m_hi.astype(F32)).astype(BF16) for m, m_hi in zip(ms, m_his)]
    r_his = [dot(m_hi, wr_ref) for m_hi in m_his]
    r_los = [dot(m_lo, wr_ref) for m_lo in m_los]
    logits = jnp.concatenate([r_hi[:, :n_experts] + r_hi[:, n_experts:] + r_lo[:, :n_experts]
                              for r_hi, r_lo in zip(r_his, r_los)], axis=0)
    scores = _sigmoid(logits)
    vals = scores + rb_ref[...]
    lane = lax.broadcasted_iota(jnp.int32, vals.shape, 1).astype(F32)
    picks, firsts, raws = [], [], []
    sel = jnp.zeros(vals.shape, F32)
    for _ in range(TOP_K):
        mx = jnp.max(vals, axis=1, keepdims=True)
        first = jnp.min(jnp.where(vals == mx, lane, float(n_experts)), axis=1, keepdims=True)
        pick = lane == first
        picks.append(pick)
        firsts.append(first)
        raws.append(jnp.sum(jnp.where(pick, scores, 0.0), axis=1, keepdims=True))
        vals = jnp.where(pick, -jnp.inf, vals)
        sel = jnp.where(pick, 1.0, sel)
    denom = raws[0]
    for r in raws[1:]:
        denom = denom + r

    r_i = lax.broadcasted_iota(jnp.int32, (tm, tm), 0)
    c_i = lax.broadcasted_iota(jnp.int32, (tm, tm), 1)
    before = jnp.where(c_i < r_i, 1.0, 0.0).astype(BF16)
    rank = jnp.dot(before, sel.astype(BF16), preferred_element_type=F32) + run_ref[...]
    run_ref[...] = run_ref[...] + jnp.sum(sel, axis=0, keepdims=True)
    cnt_ref[...] = run_ref[...]

    slot = lax.broadcasted_iota(jnp.int32, (tm, LANES), 1)
    idx_w = jnp.zeros((tm, LANES), F32)
    loc_w = jnp.zeros((tm, LANES), F32)
    gate_w = jnp.zeros((tm, LANES), F32)
    for k in range(TOP_K):
        loc = jnp.sum(jnp.where(picks[k], rank, 0.0), axis=1, keepdims=True)
        idx_w = jnp.where(slot == k, firsts[k], idx_w)
        loc_w = jnp.where(slot == k, loc, loc_w)
        gate_w = jnp.where(slot == k, raws[k] / denom * ROUTED_SCALE, gate_w)
    idx_ref[...] = idx_w.T[:IDX_ROWS].astype(jnp.int32)
    loc_ref[...] = loc_w.T[:IDX_ROWS].astype(jnp.int32)
    gate_ref[...] = gate_w[:, :IDX_ROWS]


def _postmix(oa, ys, x2, wglu, woa, wob, ga, gs, gf, wr, rb, *, tm):
    t, d = x2.shape
    da = oa.shape[1]
    dsm = ys.shape[1]
    e = rb.shape[1]
    row = lambda i: (i, 0)
    col = lambda i: (0, i)
    fix = lambda i: (0, 0)
    return pl.pallas_call(
        functools.partial(_postmix_kernel, n_experts=e),
        grid=(t // tm,),
        in_specs=[
            pl.BlockSpec((tm, da), row), pl.BlockSpec((tm, dsm), row), pl.BlockSpec((tm, d), row),
            pl.BlockSpec((dsm, dsm), fix), pl.BlockSpec((da, d), fix), pl.BlockSpec((dsm, d), fix),
            pl.BlockSpec((1, da), fix), pl.BlockSpec((1, dsm), fix), pl.BlockSpec((1, d), fix),
            pl.BlockSpec((d, 2 * e), fix), pl.BlockSpec((1, e), fix),
        ],
        out_specs=[
            pl.BlockSpec((tm, d), row), pl.BlockSpec((IDX_ROWS, tm), col), pl.BlockSpec((tm, IDX_ROWS), row),
            pl.BlockSpec((IDX_ROWS, tm), col), pl.BlockSpec((1, e), fix),
        ],
        out_shape=[
            jax.ShapeDtypeStruct((t, d), F32), jax.ShapeDtypeStruct((IDX_ROWS, t), jnp.int32),
            jax.ShapeDtypeStruct((t, IDX_ROWS), F32), jax.ShapeDtypeStruct((IDX_ROWS, t), jnp.int32),
            jax.ShapeDtypeStruct((1, e), F32),
        ],
        scratch_shapes=[pltpu.VMEM((1, e), F32)],
        compiler_params=_params(("arbitrary",)),
        name="postmix_router",
    )(oa, ys, x2, wglu, woa, wob, ga, gs, gf, wr, rb)


def _rows_to_tiles(ref, packed, sub):
    rows = packed.shape[0]
    for s in range(sub):
        ref[pl.ds(s, rows, stride=sub), :] = packed[:, s * LANES:(s + 1) * LANES]


def _tiles_to_rows(ref, rows, sub):
    return jnp.concatenate([ref[pl.ds(s, rows, stride=sub), :] for s in range(sub)], axis=1)


def _tile_rows(ref, r, sub):
    start = r * sub if isinstance(r, int) else pl.multiple_of(r * sub, sub)
    return ref.at[pl.ds(start, sub)]


def _dispatch_kernel(dest_ref, zflag_ref, h1_ref, gf_ref, wg_ref, wu_ref, wd_ref, h1s_ref, xs_ref,
                     m0_ref, m1_ref, zero_ref, sem, zsem, *, sub):
    tm = h1_ref.shape[0]
    zrows = zero_ref.shape[0]
    n_blocks = zflag_ref.shape[0]

    @pl.when(pl.program_id(0) == 0)
    def _():
        zero_ref[...] = jnp.zeros(zero_ref.shape, zero_ref.dtype)

        def zcopy(b):
            return pltpu.make_async_copy(zero_ref, xs_ref.at[pl.ds(pl.multiple_of(b * zrows, zrows), zrows)], zsem)

        def zstart(b, carry):
            @pl.when(zflag_ref[b] != 0)
            def _():
                zcopy(b).start()
            return carry

        def zwait(b, carry):
            @pl.when(zflag_ref[b] != 0)
            def _():
                zcopy(b).wait()
            return carry

        lax.fori_loop(0, n_blocks, zstart, 0)
        lax.fori_loop(0, n_blocks, zwait, 0)

    step = pl.program_id(0)
    slot = lax.rem(step, 2)
    ms = (m0_ref, m1_ref)

    def drain(s):
        for _ in range(TOP_K):
            pltpu.make_async_copy(ms[s], xs_ref.at[pl.ds(0, tm * sub)], sem.at[s]).wait()

    def tile(cur):
        h1 = h1_ref[...]
        m = _rms(h1, gf_ref[...])
        _rows_to_tiles(ms[cur], _pack_halves(m), sub)
        for t in range(tm):
            src = _tile_rows(ms[cur], t, sub)
            for k in range(TOP_K):
                pltpu.make_async_copy(src, _tile_rows(xs_ref, dest_ref[k * tm + t], sub), sem.at[cur]).start()
        mb = m.astype(BF16)
        a = jnp.dot(mb, wg_ref[...], preferred_element_type=F32)
        b = jnp.dot(mb, wu_ref[...], preferred_element_type=F32)
        hid = (a * _sigmoid(a) * b).astype(BF16)
        h1s_ref[...] = h1 + jnp.dot(hid, wd_ref[...], preferred_element_type=F32)

        @pl.when(step > 0)
        def _():
            drain(1 - cur)

        @pl.when(step == pl.num_programs(0) - 1)
        def _():
            drain(cur)

    for cur in range(2):
        @pl.when(slot == cur)
        def _(cur=cur):
            tile(cur)


def _dispatch(dest, zflag, h1, gf, wg, wu, wd, *, tm, rb, n_pad):
    t, d = h1.shape
    f = wg.shape[1]
    sub = d // 2 // LANES
    row = lambda i: (i, 0)
    fix = lambda i: (0, 0)
    return pl.pallas_call(
        functools.partial(_dispatch_kernel, sub=sub),
        grid=(t // tm,),
        in_specs=[
            pl.BlockSpec((IDX_ROWS * tm,), lambda i: (i,), memory_space=pltpu.SMEM),
            pl.BlockSpec(memory_space=pltpu.SMEM),
            pl.BlockSpec((tm, d), row), pl.BlockSpec((1, d), fix),
            pl.BlockSpec((d, f), fix), pl.BlockSpec((d, f), fix), pl.BlockSpec((f, d), fix),
        ],
        out_specs=[pl.BlockSpec((tm, d), row), pl.BlockSpec(memory_space=pl.ANY)],
        out_shape=[jax.ShapeDtypeStruct((t, d), F32), jax.ShapeDtypeStruct((n_pad * sub, LANES), jnp.int32)],
        scratch_shapes=[pltpu.VMEM((tm * sub, LANES), jnp.int32), pltpu.VMEM((tm * sub, LANES), jnp.int32),
                        pltpu.VMEM((rb * sub, LANES), jnp.int32),
                        pltpu.SemaphoreType.DMA((2,)), pltpu.SemaphoreType.DMA(())],
        compiler_params=_params(("arbitrary",)),
        name="dispatch_shared",
    )(dest, zflag, h1, gf, wg, wu, wd)


def _experts_kernel(be_ref, first_ref, nxt_ref, nu_ref, xs_ref, wg_hbm, wu_hbm, wd_hbm, ys_ref,
                    wgf_ref, wuf_ref, wdf_ref, wgb_ref, wub_ref, wdb_ref, sem, *, sub):
    i = pl.program_id(0)
    tm = xs_ref.shape[0] // sub

    def weight_copies(e, s):
        copies = []
        for j, (src, dst) in enumerate(((wg_hbm, wgf_ref), (wu_hbm, wuf_ref), (wd_hbm, wdf_ref))):
            rows = dst.shape[1] // WEIGHT_DMA_SPLIT
            for c in range(WEIGHT_DMA_SPLIT):
                sl = pl.ds(c * rows, rows)
                copies.append(pltpu.make_async_copy(src.at[e, sl], dst.at[s, sl], sem.at[s, j]))
        return copies

    @pl.when(i == 0)
    def _():
        for c in weight_copies(be_ref[0], 0):
            c.start()

    @pl.when((i < nu_ref[0]) & (first_ref[i] != 0))
    def _():
        s = first_ref[i] - 1
        for c in weight_copies(be_ref[i], s):
            c.wait()

        @pl.when(nxt_ref[i] >= 0)
        def _():
            for c in weight_copies(nxt_ref[i], 1 - s):
                c.start()

        wgb_ref[...] = wgf_ref[s].astype(BF16)
        wub_ref[...] = wuf_ref[s].astype(BF16)
        wdb_ref[...] = wdf_ref[s].astype(BF16)

    @pl.when(i < nu_ref[0])
    def _():
        lo, hi = _unpack_halves(_tiles_to_rows(xs_ref, tm, sub))
        lo = lo.astype(BF16)
        hi = hi.astype(BF16)
        n = lo.shape[1]
        a = (jnp.dot(lo, wgb_ref[:n, :], preferred_element_type=F32)
             + jnp.dot(hi, wgb_ref[n:, :], preferred_element_type=F32))
        b = (jnp.dot(lo, wub_ref[:n, :], preferred_element_type=F32)
             + jnp.dot(hi, wub_ref[n:, :], preferred_element_type=F32))
        hid = (a * _sigmoid(a) * b).astype(BF16)
        _rows_to_tiles(ys_ref, _pack_halves(jnp.dot(hid, wdb_ref[...], preferred_element_type=F32)), sub)

    @pl.when(i >= nu_ref[0])
    def _():
        ys_ref[...] = jnp.zeros(ys_ref.shape, ys_ref.dtype)


def _experts(blk_e, first, nxt, n_used, xs, wg, wu, wd, *, tm):
    rows_sub, _ = xs.shape
    _, d, f = wg.shape
    sub = d // 2 // LANES
    n_blocks = rows_sub // (tm * sub)
    rows = lambda i, be, fi, nx, nu: (jnp.maximum(jnp.minimum(i, nu[0] - 1), 0), 0)
    orow = lambda i, be, fi, nx, nu: (i, 0)
    return pl.pallas_call(
        functools.partial(_experts_kernel, sub=sub),
        grid_spec=pltpu.PrefetchScalarGridSpec(
            num_scalar_prefetch=4,
            grid=(n_blocks,),
            in_specs=[
                pl.BlockSpec((tm * sub, LANES), rows),
                pl.BlockSpec(memory_space=pl.ANY), pl.BlockSpec(memory_space=pl.ANY),
                pl.BlockSpec(memory_space=pl.ANY),
            ],
            out_specs=pl.BlockSpec((tm * sub, LANES), orow),
            scratch_shapes=[pltpu.VMEM((2, d, f), F32), pltpu.VMEM((2, d, f), F32), pltpu.VMEM((2, f, d), F32),
                            pltpu.VMEM((d, f), BF16), pltpu.VMEM((d, f), BF16), pltpu.VMEM((f, d), BF16),
                            pltpu.SemaphoreType.DMA((2, 3))],
        ),
        out_shape=jax.ShapeDtypeStruct(xs.shape, jnp.int32),
        compiler_params=_params(("arbitrary",)),
        name="routed_experts",
    )(blk_e, first, nxt, n_used, xs, wg, wu, wd)


def _final_kernel(dest_ref, dnext_ref, h_ref, gate_ref, p_ref, gp_ref, wgate_ref, wproj_ref, ys_ref, o_ref,
                  buf0_ref, buf1_ref, sem, *, sub):
    tm = h_ref.shape[0]
    step = pl.program_id(0)
    slot = lax.rem(step, 2)
    bufs = (buf0_ref, buf1_ref)

    def row_copy(idx_ref, t, k, s):
        return pltpu.make_async_copy(_tile_rows(ys_ref, idx_ref[k * tm + t], sub),
                                     _tile_rows(bufs[s].at[k], t, sub), sem.at[s])

    def drain(s):
        for k in range(TOP_K):
            pltpu.make_async_copy(ys_ref.at[pl.ds(0, tm * sub)], bufs[s].at[k], sem.at[s]).wait()

    @pl.when(step == 0)
    def _():
        def issue(t, carry):
            for k in range(TOP_K):
                row_copy(dest_ref, t, k, 0).start()
            return carry

        lax.fori_loop(0, tm, issue, 0)

    def tile(cur, nxt):
        drain(cur)
        for t in range(tm):
            for k in range(TOP_K):
                row_copy(dnext_ref, t, k, nxt).start()
        pp = jnp.dot(p_ref[...].astype(BF16), wproj_ref[...], preferred_element_type=F32)
        gate = gate_ref[...]
        h = h_ref[...]
        n = h.shape[1] // 2
        acc_lo = h[:, :n]
        acc_hi = h[:, n:]
        for k in range(TOP_K):
            lo, hi = _unpack_halves(_tiles_to_rows(bufs[cur].at[k], tm, sub))
            acc_lo = acc_lo + gate[:, k:k + 1] * lo
            acc_hi = acc_hi + gate[:, k:k + 1] * hi
        h2 = jnp.concatenate([acc_lo, acc_hi], axis=1)
        nrm = _rms(h2, gp_ref[...]).astype(BF16)
        o_ref[...] = h2 + _sigmoid(jnp.dot(nrm, wgate_ref[...], preferred_element_type=F32)) * pp

    for cur in range(2):
        @pl.when(slot == cur)
        def _(cur=cur):
            tile(cur, 1 - cur)

            @pl.when(step == pl.num_programs(0) - 1)
            def _():
                drain(1 - cur)


def _final(dest, h1s, gate, p2, gp, wgate, wproj, ys, *, tm):
    t, d = h1s.shape
    dp = p2.shape[1]
    sub = d // 2 // LANES
    row = lambda i: (i, 0)
    fix = lambda i: (0, 0)
    return pl.pallas_call(
        functools.partial(_final_kernel, sub=sub),
        grid=(t // tm,),
        in_specs=[
            pl.BlockSpec((IDX_ROWS * tm,), lambda i: (i,), memory_space=pltpu.SMEM),
            pl.BlockSpec((IDX_ROWS * tm,), lambda i: (jnp.minimum(i + 1, t // tm - 1),), memory_space=pltpu.SMEM),
            pl.BlockSpec((tm, d), row), pl.BlockSpec((tm, IDX_ROWS), row), pl.BlockSpec((tm, dp), row),
            pl.BlockSpec((1, d), fix), pl.BlockSpec((d, d), fix), pl.BlockSpec((dp, d), fix),
            pl.BlockSpec(memory_space=pl.ANY),
        ],
        out_specs=pl.BlockSpec((tm, d), row),
        out_shape=jax.ShapeDtypeStruct((t, d), F32),
        scratch_shapes=[pltpu.VMEM((TOP_K, tm * sub, LANES), jnp.int32),
                        pltpu.VMEM((TOP_K, tm * sub, LANES), jnp.int32), pltpu.SemaphoreType.DMA((2,))],
        compiler_params=_params(("arbitrary",)),
        name="combine_ple",
    )(dest, dest, h1s, gate, p2, gp, wgate, wproj, ys)


def _largest_tile(n, cap):
    t = min(n, cap)
    while n % t:
        t //= 2
    return t


def _layer(h, p_l, norm_mix, w_in, q_norm, k_norm, lam_re, lam_im, log_dt, b_re, b_im, c_re, c_im, d_skip,
           w_glu, attn_out_norm, ssm_out_norm, w_out, norm_ffn, w_router, router_bias, w_exp_gate,
           w_exp_up, w_exp_down, w_sh_gate, w_sh_up, w_sh_down, norm_ple, w_ple_gate, w_ple_proj):
    bsz, seq, d = h.shape
    t = bsz * seq
    head_dim = q_norm.shape[-1]
    d_attn = attn_out_norm.shape[-1]
    n_heads = d_attn // head_dim
    n_experts = w_router.shape[-1]
    x2 = h.reshape(t, d)

    tn = d_attn if d_attn % 128 == 0 else head_dim
    scale = 1.0 / math.sqrt(head_dim)
    hg = jnp.concatenate([jnp.tile(q_norm * scale, n_heads), jnp.tile(k_norm, n_heads),
                          jnp.ones((w_in.shape[1] - 2 * d_attn,), F32)])[None, :]
    z, u = _inproj(x2, norm_mix[None, :], w_in.astype(BF16), hg, d_qkv=3 * d_attn,
                   n_norm_tiles=2 * d_attn // tn, head_dim=head_dim, tm=_largest_tile(t, 1024), tn=tn)

    o_attn = _attention(z, batch=bsz, seq=seq, n_heads=n_heads, head_dim=head_dim, blk=_largest_tile(seq, 256))

    n_chunks = seq // S5_CHUNK
    mats = _s5_prepare(lam_re, lam_im, log_dt, b_re, b_im, c_re, c_im, d_skip, n_chunks)
    y_ssm = _s5(u, *mats, n_chunks=n_chunks)

    tm = _largest_tile(t, 256)
    w_out_b = w_out.astype(BF16)
    wr_hi = w_router.astype(BF16)
    wr_lo = (w_router - wr_hi.astype(F32)).astype(BF16)
    h1, idx, gate, loc, counts = _postmix(
        o_attn, y_ssm, x2, w_glu.astype(BF16), w_out_b[:d_attn], w_out_b[d_attn:],
        attn_out_norm[None, :], ssm_out_norm[None, :], norm_ffn[None, :],
        jnp.concatenate([wr_hi, wr_lo], axis=1), router_bias[None, :], tm=_largest_tile(t, 512))

    rb = _largest_tile(t, 256)
    n_pad = t * TOP_K + n_experts * rb
    n_blocks = n_pad // rb
    cnt = counts[0].astype(jnp.int32)
    pcnt = (cnt + rb - 1) // rb * rb
    pend = jnp.cumsum(pcnt)
    pstart = pend - pcnt
    onehot = idx[:, :, None] == jnp.arange(n_experts, dtype=jnp.int32)[None, None, :]
    dest = jnp.sum(jnp.where(onehot, pstart[None, None, :], 0), axis=-1) + loc
    dest = dest.reshape(IDX_ROWS, t // tm, tm).transpose(1, 0, 2).reshape(-1)
    n_used = (pend[-1] // rb).astype(jnp.int32)
    blk_ids = jnp.arange(n_blocks, dtype=jnp.int32)
    starts = (blk_ids * rb)[:, None]
    owner = (starts >= pstart[None, :]) & (starts < pend[None, :])
    blk_e = jnp.sum(jnp.where(owner, jnp.arange(n_experts, dtype=jnp.int32)[None, :], 0), axis=1)
    is_last = jnp.any(owner & (starts + rb == pend[None, :]), axis=1)
    is_first = jnp.any(owner & (starts == pstart[None, :]), axis=1).astype(jnp.int32)
    zflag = ((blk_ids >= n_used) | is_last).astype(jnp.int32)
    first_pos = jnp.where(is_first > 0, blk_ids, n_blocks)
    nxt_pos = jnp.concatenate([lax.cummin(first_pos, axis=0, reverse=True)[1:],
                               jnp.full((1,), n_blocks, jnp.int32)])
    nxt_hit = nxt_pos[:, None] == blk_ids[None, :]
    nxt = jnp.where(nxt_pos < n_blocks, jnp.sum(jnp.where(nxt_hit, blk_e[None, :], 0), axis=1), -1)
    last_e = jnp.sum(jnp.where(blk_ids == n_used - 1, blk_e, 0))
    blk_e = jnp.where(blk_ids < n_used, blk_e, last_e).astype(jnp.int32)
    first_code = jnp.where(is_first > 0, 1 + (jnp.cumsum(is_first) - 1) % 2, 0).astype(jnp.int32)

    h1s, xs = _dispatch(dest, zflag, h1, norm_ffn[None, :], w_sh_gate.astype(BF16), w_sh_up.astype(BF16),
                        w_sh_down.astype(BF16), tm=tm, rb=rb, n_pad=n_pad)
    ys = _experts(blk_e, first_code, nxt.astype(jnp.int32), n_used[None], xs, w_exp_gate, w_exp_up, w_exp_down,
                  tm=rb)
    out = _final(dest, h1s, gate, p_l.reshape(t, -1), norm_ple[None, :], w_ple_gate.astype(BF16),
                 w_ple_proj.astype(BF16), ys, tm=tm)
    return out.reshape(bsz, seq, d)


def kernel(x, p, norm_mix, w_in, q_norm, k_norm, ssm_lam_re, ssm_lam_im, ssm_log_dt, ssm_b_re, ssm_b_im,
           ssm_c_re, ssm_c_im, ssm_d, w_glu, attn_out_norm, ssm_out_norm, w_out, norm_ffn, w_router,
           router_bias, w_exp_gate, w_exp_up, w_exp_down, w_sh_gate, w_sh_up, w_sh_down, norm_ple,
           w_ple_gate, w_ple_proj):
    h = x
    for i in range(p.shape[0]):
        h = _layer(h, p[i], norm_mix[i], w_in[i], q_norm[i], k_norm[i], ssm_lam_re[i], ssm_lam_im[i],
                   ssm_log_dt[i], ssm_b_re[i], ssm_b_im[i], ssm_c_re[i], ssm_c_im[i], ssm_d[i], w_glu[i],
                   attn_out_norm[i], ssm_out_norm[i], w_out[i], norm_ffn[i], w_router[i], router_bias[i],
                   w_exp_gate[i], w_exp_up[i], w_exp_down[i], w_sh_gate[i], w_sh_up[i], w_sh_down[i],
                   norm_ple[i], w_ple_gate[i], w_ple_proj[i])
    return h
```

```python
import functools
import math

import jax
import jax.numpy as jnp
from jax import lax
from jax.experimental import pallas as pl
from jax.experimental.pallas import tpu as pltpu

NORM_EPS = 1e-6
TOP_K = 6
ROUTED_SCALE = 2.5
IDX_ROWS = 8
LANES = 128
S5_CHUNK = 8
WEIGHT_DMA_SPLIT = 4
POSTMIX_ROW_GROUPS = 2
VMEM_LIMIT = 56 * 1024 * 1024

F32 = jnp.float32
BF16 = jnp.bfloat16
HIGH_HALF = -65536


def _rms(x, g):
    return x * lax.rsqrt(jnp.mean(x * x, axis=-1, keepdims=True) + NORM_EPS) * g


def _sigmoid(x):
    return 1.0 / (1.0 + jnp.exp(-x))


def _params(sem):
    return pltpu.CompilerParams(dimension_semantics=sem, vmem_limit_bytes=VMEM_LIMIT)


def _pack_halves(x):
    n = x.shape[1] // 2
    lo = lax.bitcast_convert_type(x[:, :n].astype(BF16).astype(F32), jnp.int32)
    hi = lax.bitcast_convert_type(x[:, n:].astype(BF16).astype(F32), jnp.int32)
    return (hi & HIGH_HALF) | lax.shift_right_logical(lo, 16)


def _unpack_halves(p):
    lo = lax.bitcast_convert_type(lax.shift_left(p, 16), F32)
    hi = lax.bitcast_convert_type(p & HIGH_HALF, F32)
    return lo, hi


def _inproj_kernel(x_ref, g_ref, w_ref, hg_ref, o_ref, u_ref, xn_ref, *, n_norm_tiles, n_qkv_tiles, head_dim):
    j = pl.program_id(1)

    @pl.when(j == 0)
    def _():
        xn_ref[...] = _rms(x_ref[...], g_ref[...]).astype(BF16)

    acc = jnp.dot(xn_ref[...], w_ref[...], preferred_element_type=F32)
    tn = acc.shape[1]

    @pl.when(j < n_norm_tiles)
    def _():
        hg = hg_ref[...]
        for h in range(tn // head_dim):
            sl = slice(h * head_dim, (h + 1) * head_dim)
            o_ref[:, sl] = _rms(acc[:, sl], hg[:, sl]).astype(o_ref.dtype)

    @pl.when((j >= n_norm_tiles) & (j < n_qkv_tiles))
    def _():
        o_ref[...] = acc.astype(o_ref.dtype)

    @pl.when(j >= n_qkv_tiles)
    def _():
        u_ref[...] = acc


def _inproj(x2, g, w, hg, *, d_qkv, n_norm_tiles, head_dim, tm, tn):
    t, d = x2.shape
    n = w.shape[1]
    nq = d_qkv // tn
    return pl.pallas_call(
        functools.partial(_inproj_kernel, n_norm_tiles=n_norm_tiles, n_qkv_tiles=nq, head_dim=head_dim),
        grid=(t // tm, n // tn),
        in_specs=[
            pl.BlockSpec((tm, d), lambda i, j: (i, 0)),
            pl.BlockSpec((1, d), lambda i, j: (0, 0)),
            pl.BlockSpec((d, tn), lambda i, j: (0, j)),
            pl.BlockSpec((1, tn), lambda i, j: (0, j)),
        ],
        out_specs=[
            pl.BlockSpec((tm, tn), lambda i, j: (i, jnp.minimum(j, nq - 1))),
            pl.BlockSpec((tm, tn), lambda i, j: (i, jnp.maximum(j - nq, 0))),
        ],
        out_shape=[jax.ShapeDtypeStruct((t, d_qkv), BF16), jax.ShapeDtypeStruct((t, n - d_qkv), F32)],
        scratch_shapes=[pltpu.VMEM((tm, d), BF16)],
        compiler_params=_params(("parallel", "arbitrary")),
        name="inproj",
    )(x2, g, w, hg)


SKIP_AFTER = 110.0


def _attn_kernel(q_ref, k_ref, v_ref, o_ref, *, blk, hd, heads):
    i = pl.program_id(2)
    row = lax.broadcasted_iota(jnp.int32, (blk, blk), 0)
    col = lax.broadcasted_iota(jnp.int32, (blk, blk), 1)
    causal = col < row
    later = jnp.where(row > col, 1.0, 0.0).astype(BF16)
    lanes = [slice(h * hd, (h + 1) * hd) for h in range(heads)]
    qs = [q_ref[:, sl] for sl in lanes]

    def step(kb, accs, runs, diag):
        start = pl.multiple_of(kb * blk, blk)
        zs, sps, inners, new_accs, new_runs = [], [], [], [], []
        for h, sl in enumerate(lanes):
            k = k_ref[pl.ds(start, blk), sl]
            zs.append(lax.dot_general(qs[h], k, (((1,), (1,)), ((), ())), preferred_element_type=F32))
        for z in zs:
            sp = jnp.maximum(z, 0.0) + jnp.log(1.0 + jnp.exp(-jnp.abs(z)))
            sps.append(jnp.where(causal, sp, 0.0) if diag else sp)
        for sp in sps:
            hi = sp.astype(BF16)
            lo = (sp - hi.astype(F32)).astype(BF16)
            inners.append(jnp.dot(hi, later, preferred_element_type=F32)
                          + jnp.dot(lo, later, preferred_element_type=F32))
        for h, sl in enumerate(lanes):
            w = jnp.exp(zs[h] - sps[h] - (inners[h] + runs[h]))
            if diag:
                w = jnp.where(causal, w, 0.0)
            v = v_ref[pl.ds(start, blk), sl]
            new_accs.append(accs[h] + jnp.dot(w.astype(BF16), v, preferred_element_type=F32))
            new_runs.append(runs[h] + jnp.sum(sps[h], axis=-1, keepdims=True))
        return tuple(new_accs), tuple(new_runs)

    def keep_going(runs):
        low = runs[0]
        for r in runs[1:]:
            low = jnp.minimum(low, r)
        return (jnp.min(low) < SKIP_AFTER).astype(jnp.int32)

    accs = tuple(jnp.zeros((blk, hd), F32) for _ in lanes)
    runs = tuple(jnp.zeros((blk, 1), F32) for _ in lanes)
    accs, runs = step(i, accs, runs, True)

    def cond(c):
        n, go, _, _ = c
        return (n < i) & (go > 0)

    def body(c):
        n, _, accs, runs = c
        accs, runs = step(i - 1 - n, accs, runs, False)
        return n + 1, keep_going(runs), accs, runs

    _, _, accs, _ = lax.while_loop(cond, body, (jnp.int32(0), keep_going(runs), accs, runs))
    for h, sl in enumerate(lanes):
        o_ref[:, sl] = accs[h].astype(o_ref.dtype)


def _attention(z, *, batch, seq, n_heads, head_dim, blk):
    nq = seq // blk
    heads = next(c for c in (4, 2, 1) if n_heads % c == 0)
    hw = heads * head_dim
    ng = n_heads // heads
    return pl.pallas_call(
        functools.partial(_attn_kernel, blk=blk, hd=head_dim, heads=heads),
        grid=(batch, ng, nq),
        in_specs=[
            pl.BlockSpec((blk, hw), lambda b, h, i: (b * nq + i, h)),
            pl.BlockSpec((seq, hw), lambda b, h, i: (b, ng + h)),
            pl.BlockSpec((seq, hw), lambda b, h, i: (b, 2 * ng + h)),
        ],
        out_specs=pl.BlockSpec((blk, hw), lambda b, h, i: (b * nq + i, h)),
        out_shape=jax.ShapeDtypeStruct((batch * seq, n_heads * head_dim), BF16),
        compiler_params=_params(("parallel", "parallel", "arbitrary")),
        name="sb_attention",
    )(z, z, z)


def _s5_prepare(lam_re, lam_im, log_dt, b_re, b_im, c_re, c_im, d_skip, n_chunks):
    hp = lax.Precision.HIGHEST
    chunk = S5_CHUNK
    g, p = lam_re.shape
    h = b_re.shape[-1]
    gg = LANES // h
    o = g // gg
    dt = jnp.exp(log_dt)[:, None]
    ks = jnp.arange(chunk + 1, dtype=F32)[None, :, None]
    mag = jnp.exp(lam_re[:, None, :] * dt[:, None, :] * ks)
    ang = lam_im[:, None, :] * dt[:, None, :] * ks
    ak_re = mag * jnp.cos(ang)
    ak_im = mag * jnp.sin(ang)
    nr = ak_re[:, 1] - 1.0
    ni = ak_im[:, 1]
    den = lam_re * lam_re + lam_im * lam_im
    coef_re = ((nr * lam_re + ni * lam_im) / den)[..., None]
    coef_im = ((ni * lam_re - nr * lam_im) / den)[..., None]
    bbar_re = coef_re * b_re - coef_im * b_im
    bbar_im = coef_re * b_im + coef_im * b_re
    ca_re = c_re[:, None] * ak_re[:, :, None, :] - c_im[:, None] * ak_im[:, :, None, :]
    ca_im = c_re[:, None] * ak_im[:, :, None, :] + c_im[:, None] * ak_re[:, :, None, :]

    cam_re = ca_re[:, :chunk].transpose(0, 3, 1, 2).reshape(g, p, chunk * h)
    cam_im = ca_im[:, :chunk].transpose(0, 3, 1, 2).reshape(g, p, chunk * h)
    kt = (jnp.einsum("gph,gpm->ghm", bbar_re, cam_re, precision=hp)
          - jnp.einsum("gph,gpm->ghm", bbar_im, cam_im, precision=hp))
    def same_group(n_rows, row_block, n_cols, col_block):
        r = (jnp.arange(n_rows) // row_block) % gg
        c = (jnp.arange(n_cols) // col_block) % gg
        return (r[:, None] == c[None, :]).astype(F32)

    tile_h = jnp.tile(jnp.eye(h, dtype=F32), (1, gg))
    tile_p = jnp.tile(jnp.eye(p, dtype=F32), (1, gg))
    kt = kt.reshape(o, gg, h, chunk, h).transpose(0, 3, 1, 2, 4).reshape(o, chunk, LANES, h)
    bd = jnp.einsum("okrh,hc->okrc", kt, tile_h, precision=hp) * same_group(LANES, h, LANES, h)
    dvec = d_skip.reshape(o, LANES)
    lags = [bd[:, k] for k in range(chunk)]
    lags[0] = lags[0] + dvec[:, :, None] * jnp.eye(LANES, dtype=F32)[None]
    zero_blk = jnp.zeros_like(lags[0])
    m_intra = jnp.concatenate(
        [jnp.concatenate([lags[j - i] if j >= i else zero_blk for j in range(chunk)], axis=2)
         for i in range(chunk)], axis=1)

    def expand_in(ak_part_a, ak_part_b, b_a, b_b, sign):
        rk_a = jnp.stack([ak_part_a[:, chunk - 1 - i] for i in range(chunk)], axis=1)
        rk_b = jnp.stack([ak_part_b[:, chunk - 1 - i] for i in range(chunk)], axis=1)
        val = (rk_a[:, :, None, :] * b_a.transpose(0, 2, 1)[:, None]
               + sign * rk_b[:, :, None, :] * b_b.transpose(0, 2, 1)[:, None])
        val = val.reshape(o, gg, chunk, h, p).transpose(0, 2, 1, 3, 4).reshape(o, chunk * LANES, p)
        wide = jnp.einsum("orp,pc->orc", val.astype(BF16), tile_p.astype(BF16), preferred_element_type=BF16)
        return wide * same_group(chunk * LANES, h, gg * p, p).astype(BF16)

    m_in = jnp.concatenate([expand_in(ak_re, ak_im, bbar_re, bbar_im, -1.0),
                            expand_in(ak_re, ak_im, bbar_im, bbar_re, 1.0)], axis=-1)

    def expand_out(ca):
        val = ca[:, 1:].reshape(o, gg, chunk, h, p).transpose(0, 4, 2, 1, 3).reshape(o, p, chunk * LANES)
        wide = jnp.einsum("pc,opr->ocr", tile_p.astype(BF16), val.astype(BF16), preferred_element_type=BF16)
        return wide * same_group(gg * p, p, chunk * LANES, h).astype(BF16)

    m_out = jnp.concatenate([expand_out(ca_re), -expand_out(ca_im)], axis=1)

    steps = max(1, int(math.ceil(math.log2(max(n_chunks, 2)))))
    cr, ci = ak_re[:, chunk].reshape(o, gg * p), ak_im[:, chunk].reshape(o, gg * p)
    sc_a, sc_b = [], []
    for _ in range(steps):
        sc_a.append(jnp.concatenate([cr, cr], axis=-1))
        sc_b.append(jnp.concatenate([-ci, ci], axis=-1))
        cr, ci = cr * cr - ci * ci, 2.0 * cr * ci
    return m_intra.astype(BF16), m_in, m_out, jnp.stack(sc_a, axis=1), jnp.stack(sc_b, axis=1)


def _s5_kernel(u_ref, mi_ref, min_ref, mout_ref, sa_ref, sb_ref, y_ref, *, n_chunks, steps):
    chunk = S5_CHUNK
    n = u_ref.shape[0] // chunk
    u = jnp.concatenate([u_ref[pl.ds(i, n, stride=chunk), :] for i in range(chunk)], axis=1).astype(BF16)
    y = jnp.dot(u, mi_ref[...], preferred_element_type=F32)
    x = jnp.dot(u, min_ref[...], preferred_element_type=F32)
    half = x.shape[1] // 2
    c = lax.rem(lax.broadcasted_iota(jnp.int32, x.shape, 0), n_chunks)
    sa = sa_ref[...]
    sb = sb_ref[...]
    for k in range(steps):
        sh = 1 << k
        xs = jnp.where(c >= sh, pltpu.roll(x, sh, axis=0), 0.0)
        x = x + xs * sa[k:k + 1, :] + pltpu.roll(xs, half, axis=1) * sb[k:k + 1, :]
    s_in = jnp.where(c >= 1, pltpu.roll(x, 1, axis=0), 0.0)
    y = y + jnp.dot(s_in.astype(BF16), mout_ref[...], preferred_element_type=F32)
    for i in range(chunk):
        y_ref[pl.ds(i, n, stride=chunk), :] = y[:, i * LANES:(i + 1) * LANES]


def _s5(u, m_intra, m_in, m_out, sc_a, sc_b, *, n_chunks):
    t, d_ssm = u.shape
    o, lh, st = m_in.shape
    steps = sc_a.shape[1]
    mat = lambda i: (i, 0, 0)
    return pl.pallas_call(
        functools.partial(_s5_kernel, n_chunks=n_chunks, steps=steps),
        grid=(o,),
        in_specs=[
            pl.BlockSpec((t, LANES), lambda i: (0, i)),
            pl.BlockSpec((None, lh, lh), mat), pl.BlockSpec((None, lh, st), mat), pl.BlockSpec((None, st, lh), mat),
            pl.BlockSpec((None, steps, st), mat), pl.BlockSpec((None, steps, st), mat),
        ],
        out_specs=pl.BlockSpec((t, LANES), lambda i: (0, i)),
        out_shape=jax.ShapeDtypeStruct((t, d_ssm), F32),
        compiler_params=_params(("parallel",)),
        name="s5_chunked",
    )(u, m_intra, m_in, m_out, sc_a, sc_b)


def _postmix_kernel(oa_ref, ys_ref, x_ref, wglu_ref, woa_ref, wob_ref, ga_ref, gs_ref, gf_ref,
                    wr_ref, rb_ref, h1_ref, idx_ref, gate_ref, loc_ref, cnt_ref, run_ref, *, n_experts):
    step = pl.program_id(0)

    @pl.when(step == 0)
    def _():
        run_ref[...] = jnp.zeros(run_ref.shape, F32)

    tm = x_ref.shape[0]
    n_groups = POSTMIX_ROW_GROUPS if tm % (8 * POSTMIX_ROW_GROUPS) == 0 else 1
    rows = [pl.ds(g * (tm // n_groups), tm // n_groups) for g in range(n_groups)]

    def dot(a, w_ref):
        return jnp.dot(a, w_ref[...], preferred_element_type=F32)

    ys = [ys_ref[r, :] for r in rows]
    ys = [0.5 * y * (1.0 + jnp.tanh(math.sqrt(2.0 / math.pi) * (y + 0.044715 * (y * y * y)))) for y in ys]
    glu = [dot(y.astype(BF16), wglu_ref) for y in ys]
    ys = [y * _sigmoid(g) for y, g in zip(ys, glu)]
    nas = [_rms(oa_ref[r, :].astype(F32), ga_ref[...]).astype(BF16) for r in rows]
    nss = [_rms(y, gs_ref[...]).astype(BF16) for y in ys]
    h1s = [x_ref[r, :] + dot(na, woa_ref) + dot(ns, wob_ref) for r, na, ns in zip(rows, nas, nss)]
    for r, h1 in zip(rows, h1s):
        h1_ref[r, :] = h1

    ms = [_rms(h1, gf_ref[...]) for h1 in h1s]
    m_his = [m.astype(BF16) for m in ms]
    m_los = [(m - m_hi.astype(F32)).astype(BF16) for m, m_hi in zip(ms, m_his)]
    r_his = [dot(m_hi, wr_ref) for m_hi in m_his]
    r_los = [dot(m_lo, wr_ref) for m_lo in m_los]
    logits = jnp.concatenate([r_hi[:, :n_experts] + r_hi[:, n_experts:] + r_lo[:, :n_experts]
                              for r_hi, r_lo in zip(r_his, r_los)], axis=0)
    scores = _sigmoid(logits)
    vals = scores + rb_ref[...]
    lane = lax.broadcasted_iota(jnp.int32, vals.shape, 1).astype(F32)
    picks, firsts, raws = [], [], []
    sel = jnp.zeros(vals.shape, F32)
    for _ in range(TOP_K):
        mx = jnp.max(vals, axis=1, keepdims=True)
        first = jnp.min(jnp.where(vals == mx, lane, float(n_experts)), axis=1, keepdims=True)
        pick = lane == first
        picks.append(pick)
        firsts.append(first)
        raws.append(jnp.sum(jnp.where(pick, scores, 0.0), axis=1, keepdims=True))
        vals = jnp.where(pick, -jnp.inf, vals)
        sel = jnp.where(pick, 1.0, sel)
    denom = raws[0]
    for r in raws[1:]:
        denom = denom + r

    r_i = lax.broadcasted_iota(jnp.int32, (tm, tm), 0)
    c_i = lax.broadcasted_iota(jnp.int32, (tm, tm), 1)
    before = jnp.where(c_i < r_i, 1.0, 0.0).astype(BF16)
    rank = jnp.dot(before, sel.astype(BF16), preferred_element_type=F32) + run_ref[...]
    run_ref[...] = run_ref[...] + jnp.sum(sel, axis=0, keepdims=True)
    cnt_ref[...] = run_ref[...]

    slot = lax.broadcasted_iota(jnp.int32, (tm, LANES), 1)
    idx_w = jnp.zeros((tm, LANES), F32)
    loc_w = jnp.zeros((tm, LANES), F32)
    gate_w = jnp.zeros((tm, LANES), F32)
    for k in range(TOP_K):
        loc = jnp.sum(jnp.where(picks[k], rank, 0.0), axis=1, keepdims=True)
        idx_w = jnp.where(slot == k, firsts[k], idx_w)
        loc_w = jnp.where(slot == k, loc, loc_w)
        gate_w = jnp.where(slot == k, raws[k] / denom * ROUTED_SCALE, gate_w)
    idx_ref[...] = idx_w.T[:IDX_ROWS].astype(jnp.int32)
    loc_ref[...] = loc_w.T[:IDX_ROWS].astype(jnp.int32)
    gate_ref[...] = gate_w[:, :IDX_ROWS]


def _postmix(oa, ys, x2, wglu, woa, wob, ga, gs, gf, wr, rb, *, tm):
    t, d = x2.shape
    da = oa.shape[1]
    dsm = ys.shape[1]
    e = rb.shape[1]
    row = lambda i: (i, 0)
    col = lambda i: (0, i)
    fix = lambda i: (0, 0)
    return pl.pallas_call(
        functools.partial(_postmix_kernel, n_experts=e),
        grid=(t // tm,),
        in_specs=[
            pl.BlockSpec((tm, da), row), pl.BlockSpec((tm, dsm), row), pl.BlockSpec((tm, d), row),
            pl.BlockSpec((dsm, dsm), fix), pl.BlockSpec((da, d), fix), pl.BlockSpec((dsm, d), fix),
            pl.BlockSpec((1, da), fix), pl.BlockSpec((1, dsm), fix), pl.BlockSpec((1, d), fix),
            pl.BlockSpec((d, 2 * e), fix), pl.BlockSpec((1, e), fix),
        ],
        out_specs=[
            pl.BlockSpec((tm, d), row), pl.BlockSpec((IDX_ROWS, tm), col), pl.BlockSpec((tm, IDX_ROWS), row),
            pl.BlockSpec((IDX_ROWS, tm), col), pl.BlockSpec((1, e), fix),
        ],
        out_shape=[
            jax.ShapeDtypeStruct((t, d), F32), jax.ShapeDtypeStruct((IDX_ROWS, t), jnp.int32),
            jax.ShapeDtypeStruct((t, IDX_ROWS), F32), jax.ShapeDtypeStruct((IDX_ROWS, t), jnp.int32),
            jax.ShapeDtypeStruct((1, e), F32),
        ],
        scratch_shapes=[pltpu.VMEM((1, e), F32)],
        compiler_params=_params(("arbitrary",)),
        name="postmix_router",
    )(oa, ys, x2, wglu, woa, wob, ga, gs, gf, wr, rb)


def _rows_to_tiles(ref, packed, sub):
    rows = packed.shape[0]
    for s in range(sub):
        ref[pl.ds(s, rows, stride=sub), :] = packed[:, s * LANES:(s + 1) * LANES]


def _tiles_to_rows(ref, rows, sub):
    return jnp.concatenate([ref[pl.ds(s, rows, stride=sub), :] for s in range(sub)], axis=1)


def _tile_rows(ref, r, sub):
    start = r * sub if isinstance(r, int) else pl.multiple_of(r * sub, sub)
    return ref.at[pl.ds(start, sub)]


def _shared_kernel(dest_ref, tab_init_hbm, h1_ref, gf_ref, wg_ref, wu_ref, wd_ref, h1s_ref, mp_ref, tab_ref,
                   sem, *, sub, n_tokens):
    tm = h1_ref.shape[0]
    base = pl.program_id(0) * tm

    @pl.when(pl.program_id(0) == 0)
    def _():
        init = pltpu.make_async_copy(tab_init_hbm, tab_ref, sem)
        init.start()
        init.wait()

    for t in range(tm):
        for k in range(TOP_K):
            tab_ref[dest_ref[k * tm + t]] = k * n_tokens + base + t
    h1 = h1_ref[...]
    m = _rms(h1, gf_ref[...])
    _rows_to_tiles(mp_ref, _pack_halves(m), sub)
    mb = m.astype(BF16)
    a = jnp.dot(mb, wg_ref[...], preferred_element_type=F32)
    b = jnp.dot(mb, wu_ref[...], preferred_element_type=F32)
    hid = (a * _sigmoid(a) * b).astype(BF16)
    h1s_ref[...] = h1 + jnp.dot(hid, wd_ref[...], preferred_element_type=F32)


def _shared(dest, table0, h1, gf, wg, wu, wd, *, tm):
    t, d = h1.shape
    f = wg.shape[1]
    sub = d // 2 // LANES
    row = lambda i: (i, 0)
    fix = lambda i: (0, 0)
    return pl.pallas_call(
        functools.partial(_shared_kernel, sub=sub, n_tokens=t),
        grid=(t // tm,),
        in_specs=[
            pl.BlockSpec((IDX_ROWS * tm,), lambda i: (i,), memory_space=pltpu.SMEM),
            pl.BlockSpec(memory_space=pl.ANY),
            pl.BlockSpec((tm, d), row), pl.BlockSpec((1, d), fix),
            pl.BlockSpec((d, f), fix), pl.BlockSpec((d, f), fix), pl.BlockSpec((f, d), fix),
        ],
        out_specs=[pl.BlockSpec((tm, d), row), pl.BlockSpec((tm * sub, LANES), row),
                   pl.BlockSpec(memory_space=pltpu.SMEM)],
        out_shape=[jax.ShapeDtypeStruct((t, d), F32), jax.ShapeDtypeStruct((t * sub, LANES), jnp.int32),
                   jax.ShapeDtypeStruct(table0.shape, jnp.int32)],
        scratch_shapes=[pltpu.SemaphoreType.DMA(())],
        compiler_params=_params(("arbitrary",)),
        name="shared_table",
    )(dest, table0, h1, gf, wg, wu, wd)


def _experts_kernel(be_ref, first_ref, nxt_ref, nu_ref, tab_ref, mp_hbm, wg_hbm, wu_hbm, wd_hbm, yb_hbm,
                    wgf_ref, wuf_ref, wdf_ref, wgb_ref, wub_ref, wdb_ref, x0_ref, x1_ref, y0_ref, y1_ref,
                    wsem, gsem, ssem, *, sub, tm, n_tokens):
    i = pl.program_id(0)
    nu = nu_ref[0]
    xs = (x0_ref, x1_ref)
    ys = (y0_ref, y1_ref)
    spare = TOP_K * n_tokens

    def token_of(e):
        return e & (n_tokens - 1) if n_tokens & (n_tokens - 1) == 0 else lax.rem(e, n_tokens)

    def gather(blk, r, s):
        return pltpu.make_async_copy(_tile_rows(mp_hbm, token_of(tab_ref[blk * tm + r]), sub),
                                     _tile_rows(xs[s], r, sub), gsem.at[s])

    def scatter(entry, r, s):
        return pltpu.make_async_copy(_tile_rows(ys[s], r, sub), _tile_rows(yb_hbm, entry, sub), ssem.at[s])

    def gather_wait(s):
        pltpu.make_async_copy(mp_hbm.at[pl.ds(0, tm * sub)], xs[s], gsem.at[s]).wait()

    def scatter_wait(s):
        pltpu.make_async_copy(ys[s], yb_hbm.at[pl.ds(0, tm * sub)], ssem.at[s]).wait()

    def weight_copies(e, s):
        copies = []
        for j, (src, dst) in enumerate(((wg_hbm, wgf_ref), (wu_hbm, wuf_ref), (wd_hbm, wdf_ref))):
            rows = dst.shape[1] // WEIGHT_DMA_SPLIT
            for c in range(WEIGHT_DMA_SPLIT):
                sl = pl.ds(c * rows, rows)
                copies.append(pltpu.make_async_copy(src.at[e, sl], dst.at[s, sl], wsem.at[s, j]))
        return copies

    @pl.when(i == 0)
    def _():
        for c in weight_copies(be_ref[0], 0):
            c.start()

        def first_rows(r, carry):
            gather(0, r, 0).start()
            return carry

        lax.fori_loop(0, tm, first_rows, 0)
        y1_ref[...] = jnp.zeros(y1_ref.shape, y1_ref.dtype)

    @pl.when((i < nu) & (first_ref[jnp.minimum(i, first_ref.shape[0] - 1)] != 0))
    def _():
        s = first_ref[i] - 1
        for c in weight_copies(be_ref[i], s):
            c.wait()

        @pl.when(nxt_ref[i] >= 0)
        def _():
            for c in weight_copies(nxt_ref[i], 1 - s):
                c.start()

        wgb_ref[...] = wgf_ref[s].astype(BF16)
        wub_ref[...] = wuf_ref[s].astype(BF16)
        wdb_ref[...] = wdf_ref[s].astype(BF16)

    def block(p):
        gather_wait(p)

        @pl.when(i >= 1)
        def _():
            scatter_wait(p)

        nb = jnp.minimum(i + 1, nu - 1)
        pb = jnp.maximum(i - 1, 0)
        for r in range(tm):
            gather(nb, r, 1 - p).start()
        for r in range(tm):
            entry = jnp.where(i >= 1, tab_ref[pb * tm + r], spare + r)
            scatter(entry, r, 1 - p).start()
        lo, hi = _unpack_halves(_tiles_to_rows(xs[p], tm, sub))
        lo = lo.astype(BF16)
        hi = hi.astype(BF16)
        n = lo.shape[1]
        a = (jnp.dot(lo, wgb_ref[:n, :], preferred_element_type=F32)
             + jnp.dot(hi, wgb_ref[n:, :], preferred_element_type=F32))
        b = (jnp.dot(lo, wub_ref[:n, :], preferred_element_type=F32)
             + jnp.dot(hi, wub_ref[n:, :], preferred_element_type=F32))
        hid = (a * _sigmoid(a) * b).astype(BF16)
        _rows_to_tiles(ys[p], _pack_halves(jnp.dot(hid, wdb_ref[...], preferred_element_type=F32)), sub)

    def flush(p):
        def last_rows(r, carry):
            scatter(tab_ref[(nu - 1) * tm + r], r, 1 - p).start()
            return carry

        scatter_wait(p)
        lax.fori_loop(0, tm, last_rows, 0)
        scatter_wait(1 - p)
        gather_wait(p)

    for p in range(2):
        @pl.when((i < nu) & (lax.rem(i, 2) == p))
        def _(p=p):
            block(p)

        @pl.when((i == nu) & (lax.rem(i, 2) == p))
        def _(p=p):
            flush(p)


def _experts(blk_e, first, nxt, n_used, table, mp, wg, wu, wd, *, tm):
    _, d, f = wg.shape
    sub = d // 2 // LANES
    n_tokens = mp.shape[0] // sub
    n_blocks = table.shape[0] // tm
    return pl.pallas_call(
        functools.partial(_experts_kernel, sub=sub, tm=tm, n_tokens=n_tokens),
        grid_spec=pltpu.PrefetchScalarGridSpec(
            num_scalar_prefetch=5,
            grid=(n_blocks + 1,),
            in_specs=[pl.BlockSpec(memory_space=pl.ANY)] * 4,
            out_specs=pl.BlockSpec(memory_space=pl.ANY),
            scratch_shapes=[pltpu.VMEM((2, d, f), F32), pltpu.VMEM((2, d, f), F32), pltpu.VMEM((2, f, d), F32),
                            pltpu.VMEM((d, f), BF16), pltpu.VMEM((d, f), BF16), pltpu.VMEM((f, d), BF16),
                            pltpu.VMEM((tm * sub, LANES), jnp.int32), pltpu.VMEM((tm * sub, LANES), jnp.int32),
                            pltpu.VMEM((tm * sub, LANES), jnp.int32), pltpu.VMEM((tm * sub, LANES), jnp.int32),
                            pltpu.SemaphoreType.DMA((2, 3)), pltpu.SemaphoreType.DMA((2,)),
                            pltpu.SemaphoreType.DMA((2,))],
        ),
        out_shape=jax.ShapeDtypeStruct(((TOP_K * n_tokens + tm) * sub, LANES), jnp.int32),
        compiler_params=_params(("arbitrary",)),
        name="routed_experts",
    )(blk_e, first, nxt, n_used, table, mp, wg, wu, wd)


def _final_kernel(h_ref, gate_ref, p_ref, gp_ref, wgate_ref, wproj_ref, *rest, sub):
    y_refs, o_ref = rest[:TOP_K], rest[TOP_K]
    tm = h_ref.shape[0]
    pp = jnp.dot(p_ref[...].astype(BF16), wproj_ref[...], preferred_element_type=F32)
    gate = gate_ref[...]
    h = h_ref[...]
    n = h.shape[1] // 2
    acc_lo = h[:, :n]
    acc_hi = h[:, n:]
    for k in range(TOP_K):
        lo, hi = _unpack_halves(_tiles_to_rows(y_refs[k], tm, sub))
        acc_lo = acc_lo + gate[:, k:k + 1] * lo
        acc_hi = acc_hi + gate[:, k:k + 1] * hi
    h2 = jnp.concatenate([acc_lo, acc_hi], axis=1)
    nrm = _rms(h2, gp_ref[...]).astype(BF16)
    o_ref[...] = h2 + _sigmoid(jnp.dot(nrm, wgate_ref[...], preferred_element_type=F32)) * pp


def _final(h1s, gate, p2, gp, wgate, wproj, yb, *, tm):
    t, d = h1s.shape
    dp = p2.shape[1]
    sub = d // 2 // LANES
    nt = t // tm
    row = lambda i: (i, 0)
    fix = lambda i: (0, 0)
    y_specs = [pl.BlockSpec((tm * sub, LANES), lambda i, k=k: (k * nt + i, 0)) for k in range(TOP_K)]
    return pl.pallas_call(
        functools.partial(_final_kernel, sub=sub),
        grid=(nt,),
        in_specs=[
            pl.BlockSpec((tm, d), row), pl.BlockSpec((tm, IDX_ROWS), row), pl.BlockSpec((tm, dp), row),
            pl.BlockSpec((1, d), fix), pl.BlockSpec((d, d), fix), pl.BlockSpec((dp, d), fix),
        ] + y_specs,
        out_specs=pl.BlockSpec((tm, d), row),
        out_shape=jax.ShapeDtypeStruct((t, d), F32),
        compiler_params=_params(("parallel",)),
        name="combine_ple",
    )(h1s, gate, p2, gp, wgate, wproj, *([yb] * TOP_K))


def _largest_tile(n, cap):
    t = min(n, cap)
    while n % t:
        t //= 2
    return t


def _layer(h, p_l, norm_mix, w_in, q_norm, k_norm, lam_re, lam_im, log_dt, b_re, b_im, c_re, c_im, d_skip,
           w_glu, attn_out_norm, ssm_out_norm, w_out, norm_ffn, w_router, router_bias, w_exp_gate,
           w_exp_up, w_exp_down, w_sh_gate, w_sh_up, w_sh_down, norm_ple, w_ple_gate, w_ple_proj):
    bsz, seq, d = h.shape
    t = bsz * seq
    head_dim = q_norm.shape[-1]
    d_attn = attn_out_norm.shape[-1]
    n_heads = d_attn // head_dim
    n_experts = w_router.shape[-1]
    x2 = h.reshape(t, d)

    tn = d_attn if d_attn % 128 == 0 else head_dim
    scale = 1.0 / math.sqrt(head_dim)
    hg = jnp.concatenate([jnp.tile(q_norm * scale, n_heads), jnp.tile(k_norm, n_heads),
                          jnp.ones((w_in.shape[1] - 2 * d_attn,), F32)])[None, :]
    z, u = _inproj(x2, norm_mix[None, :], w_in.astype(BF16), hg, d_qkv=3 * d_attn,
                   n_norm_tiles=2 * d_attn // tn, head_dim=head_dim, tm=_largest_tile(t, 1024), tn=tn)

    o_attn = _attention(z, batch=bsz, seq=seq, n_heads=n_heads, head_dim=head_dim, blk=_largest_tile(seq, 256))

    n_chunks = seq // S5_CHUNK
    mats = _s5_prepare(lam_re, lam_im, log_dt, b_re, b_im, c_re, c_im, d_skip, n_chunks)
    y_ssm = _s5(u, *mats, n_chunks=n_chunks)

    tm = _largest_tile(t, 256)
    w_out_b = w_out.astype(BF16)
    wr_hi = w_router.astype(BF16)
    wr_lo = (w_router - wr_hi.astype(F32)).astype(BF16)
    h1, idx, gate, loc, counts = _postmix(
        o_attn, y_ssm, x2, w_glu.astype(BF16), w_out_b[:d_attn], w_out_b[d_attn:],
        attn_out_norm[None, :], ssm_out_norm[None, :], norm_ffn[None, :],
        jnp.concatenate([wr_hi, wr_lo], axis=1), router_bias[None, :], tm=_largest_tile(t, 512))

    rb = _largest_tile(t, 256)
    n_pad = t * TOP_K + n_experts * rb
    n_blocks = n_pad // rb
    cnt = counts[0].astype(jnp.int32)
    pcnt = (cnt + rb - 1) // rb * rb
    pend = jnp.cumsum(pcnt)
    pstart = pend - pcnt
    onehot = idx[:, :, None] == jnp.arange(n_experts, dtype=jnp.int32)[None, None, :]
    dest = jnp.sum(jnp.where(onehot, pstart[None, None, :], 0), axis=-1) + loc
    dest = dest.reshape(IDX_ROWS, t // tm, tm).transpose(1, 0, 2).reshape(-1)
    n_used = (pend[-1] // rb).astype(jnp.int32)
    blk_ids = jnp.arange(n_blocks, dtype=jnp.int32)
    starts = (blk_ids * rb)[:, None]
    owner = (starts >= pstart[None, :]) & (starts < pend[None, :])
    blk_e = jnp.sum(jnp.where(owner, jnp.arange(n_experts, dtype=jnp.int32)[None, :], 0), axis=1)
    is_first = jnp.any(owner & (starts == pstart[None, :]), axis=1).astype(jnp.int32)
    first_pos = jnp.where(is_first > 0, blk_ids, n_blocks)
    nxt_pos = jnp.concatenate([lax.cummin(first_pos, axis=0, reverse=True)[1:],
                               jnp.full((1,), n_blocks, jnp.int32)])
    nxt_hit = nxt_pos[:, None] == blk_ids[None, :]
    nxt = jnp.where(nxt_pos < n_blocks, jnp.sum(jnp.where(nxt_hit, blk_e[None, :], 0), axis=1), -1)
    last_e = jnp.sum(jnp.where(blk_ids == n_used - 1, blk_e, 0))
    blk_e = jnp.where(blk_ids < n_used, blk_e, last_e).astype(jnp.int32)
    first_code = jnp.where(is_first > 0, 1 + (jnp.cumsum(is_first) - 1) % 2, 0).astype(jnp.int32)

    table0 = TOP_K * t + jnp.arange(n_pad, dtype=jnp.int32) % rb
    h1s, mp, table = _shared(dest, table0, h1, norm_ffn[None, :], w_sh_gate.astype(BF16), w_sh_up.astype(BF16),
                             w_sh_down.astype(BF16), tm=tm)
    yb = _experts(blk_e, first_code, nxt.astype(jnp.int32), n_used[None], table, mp, w_exp_gate, w_exp_up,
                  w_exp_down, tm=rb)
    out = _final(h1s, gate, p_l.reshape(t, -1), norm_ple[None, :], w_ple_gate.astype(BF16),
                 w_ple_proj.astype(BF16), yb, tm=tm)
    return out.reshape(bsz, seq, d)


def kernel(x, p, norm_mix, w_in, q_norm, k_norm, ssm_lam_re, ssm_lam_im, ssm_log_dt, ssm_b_re, ssm_b_im,
           ssm_c_re, ssm_c_im, ssm_d, w_glu, attn_out_norm, ssm_out_norm, w_out, norm_ffn, w_router,
           router_bias, w_exp_gate, w_exp_up, w_exp_down, w_sh_gate, w_sh_up, w_sh_down, norm_ple,
           w_ple_gate, w_ple_proj):
    h = x
    for i in range(p.shape[0]):
        h = _layer(h, p[i], norm_mix[i], w_in[i], q_norm[i], k_norm[i], ssm_lam_re[i], ssm_lam_im[i],
                   ssm_log_dt[i], ssm_b_re[i], ssm_b_im[i], ssm_c_re[i], ssm_c_im[i], ssm_d[i], w_glu[i],
                   attn_out_norm[i], ssm_out_norm[i], w_out[i], norm_ffn[i], w_router[i], router_bias[i],
                   w_exp_gate[i], w_exp_up[i], w_exp_down[i], w_sh_gate[i], w_sh_up[i], w_sh_down[i],
                   norm_ple[i], w_ple_gate[i], w_ple_proj[i])
    return h
```

```python
import functools
import math

import jax
import jax.numpy as jnp
from jax import lax
from jax.experimental import pallas as pl
from jax.experimental.pallas import tpu as pltpu

NORM_EPS = 1e-6
TOP_K = 6
ROUTED_SCALE = 2.5
IDX_ROWS = 8
LANES = 128
S5_CHUNK = 8
WEIGHT_DMA_SPLIT = 4
POSTMIX_ROW_GROUPS = 2
ROW_DMA_GROUP = 64
VMEM_LIMIT = 56 * 1024 * 1024

F32 = jnp.float32
BF16 = jnp.bfloat16
HIGH_HALF = -65536


def _rms(x, g):
    return x * lax.rsqrt(jnp.mean(x * x, axis=-1, keepdims=True) + NORM_EPS) * g


def _sigmoid(x):
    return 1.0 / (1.0 + jnp.exp(-x))


def _params(sem):
    return pltpu.CompilerParams(dimension_semantics=sem, vmem_limit_bytes=VMEM_LIMIT)


def _pack_halves(x):
    n = x.shape[1] // 2
    lo = lax.bitcast_convert_type(x[:, :n].astype(BF16).astype(F32), jnp.int32)
    hi = lax.bitcast_convert_type(x[:, n:].astype(BF16).astype(F32), jnp.int32)
    return (hi & HIGH_HALF) | lax.shift_right_logical(lo, 16)


def _unpack_halves(p):
    lo = lax.bitcast_convert_type(lax.shift_left(p, 16), F32)
    hi = lax.bitcast_convert_type(p & HIGH_HALF, F32)
    return lo, hi


def _inproj_kernel(x_ref, g_ref, w_ref, hg_ref, o_ref, u_ref, xn_ref, *, n_norm_tiles, n_qkv_tiles, head_dim):
    j = pl.program_id(1)

    @pl.when(j == 0)
    def _():
        xn_ref[...] = _rms(x_ref[...], g_ref[...]).astype(BF16)

    acc = jnp.dot(xn_ref[...], w_ref[...], preferred_element_type=F32)
    tn = acc.shape[1]

    @pl.when(j < n_norm_tiles)
    def _():
        hg = hg_ref[...]
        for h in range(tn // head_dim):
            sl = slice(h * head_dim, (h + 1) * head_dim)
            o_ref[:, sl] = _rms(acc[:, sl], hg[:, sl]).astype(o_ref.dtype)

    @pl.when((j >= n_norm_tiles) & (j < n_qkv_tiles))
    def _():
        o_ref[...] = acc.astype(o_ref.dtype)

    @pl.when(j >= n_qkv_tiles)
    def _():
        u_ref[...] = acc


def _inproj(x2, g, w, hg, *, d_qkv, n_norm_tiles, head_dim, tm, tn):
    t, d = x2.shape
    n = w.shape[1]
    nq = d_qkv // tn
    return pl.pallas_call(
        functools.partial(_inproj_kernel, n_norm_tiles=n_norm_tiles, n_qkv_tiles=nq, head_dim=head_dim),
        grid=(t // tm, n // tn),
        in_specs=[
            pl.BlockSpec((tm, d), lambda i, j: (i, 0)),
            pl.BlockSpec((1, d), lambda i, j: (0, 0)),
            pl.BlockSpec((d, tn), lambda i, j: (0, j)),
            pl.BlockSpec((1, tn), lambda i, j: (0, j)),
        ],
        out_specs=[
            pl.BlockSpec((tm, tn), lambda i, j: (i, jnp.minimum(j, nq - 1))),
            pl.BlockSpec((tm, tn), lambda i, j: (i, jnp.maximum(j - nq, 0))),
        ],
        out_shape=[jax.ShapeDtypeStruct((t, d_qkv), BF16), jax.ShapeDtypeStruct((t, n - d_qkv), F32)],
        scratch_shapes=[pltpu.VMEM((tm, d), BF16)],
        compiler_params=_params(("parallel", "arbitrary")),
        name="inproj",
    )(x2, g, w, hg)


SKIP_AFTER = 110.0


def _attn_kernel(q_ref, k_ref, v_ref, o_ref, *, blk, hd, heads):
    i = pl.program_id(2)
    row = lax.broadcasted_iota(jnp.int32, (blk, blk), 0)
    col = lax.broadcasted_iota(jnp.int32, (blk, blk), 1)
    causal = col < row
    later = jnp.where(row > col, 1.0, 0.0).astype(BF16)
    lanes = [slice(h * hd, (h + 1) * hd) for h in range(heads)]
    qs = [q_ref[:, sl] for sl in lanes]

    def step(kb, accs, runs, diag):
        start = pl.multiple_of(kb * blk, blk)
        zs, sps, inners, new_accs, new_runs = [], [], [], [], []
        for h, sl in enumerate(lanes):
            k = k_ref[pl.ds(start, blk), sl]
            zs.append(lax.dot_general(qs[h], k, (((1,), (1,)), ((), ())), preferred_element_type=F32))
        for z in zs:
            sp = jnp.maximum(z, 0.0) + jnp.log(1.0 + jnp.exp(-jnp.abs(z)))
            sps.append(jnp.where(causal, sp, 0.0) if diag else sp)
        for sp in sps:
            hi = sp.astype(BF16)
            lo = (sp - hi.astype(F32)).astype(BF16)
            inners.append(jnp.dot(hi, later, preferred_element_type=F32)
                          + jnp.dot(lo, later, preferred_element_type=F32))
        for h, sl in enumerate(lanes):
            w = jnp.exp(zs[h] - sps[h] - (inners[h] + runs[h]))
            if diag:
                w = jnp.where(causal, w, 0.0)
            v = v_ref[pl.ds(start, blk), sl]
            new_accs.append(accs[h] + jnp.dot(w.astype(BF16), v, preferred_element_type=F32))
            new_runs.append(runs[h] + jnp.sum(sps[h], axis=-1, keepdims=True))
        return tuple(new_accs), tuple(new_runs)

    def keep_going(runs):
        low = runs[0]
        for r in runs[1:]:
            low = jnp.minimum(low, r)
        return (jnp.min(low) < SKIP_AFTER).astype(jnp.int32)

    accs = tuple(jnp.zeros((blk, hd), F32) for _ in lanes)
    runs = tuple(jnp.zeros((blk, 1), F32) for _ in lanes)
    accs, runs = step(i, accs, runs, True)

    def cond(c):
        n, go, _, _ = c
        return (n < i) & (go > 0)

    def body(c):
        n, _, accs, runs = c
        accs, runs = step(i - 1 - n, accs, runs, False)
        return n + 1, keep_going(runs), accs, runs

    _, _, accs, _ = lax.while_loop(cond, body, (jnp.int32(0), keep_going(runs), accs, runs))
    for h, sl in enumerate(lanes):
        o_ref[:, sl] = accs[h].astype(o_ref.dtype)


def _attention(z, *, batch, seq, n_heads, head_dim, blk):
    nq = seq // blk
    heads = next(c for c in (4, 2, 1) if n_heads % c == 0)
    hw = heads * head_dim
    ng = n_heads // heads
    return pl.pallas_call(
        functools.partial(_attn_kernel, blk=blk, hd=head_dim, heads=heads),
        grid=(batch, ng, nq),
        in_specs=[
            pl.BlockSpec((blk, hw), lambda b, h, i: (b * nq + i, h)),
            pl.BlockSpec((seq, hw), lambda b, h, i: (b, ng + h)),
            pl.BlockSpec((seq, hw), lambda b, h, i: (b, 2 * ng + h)),
        ],
        out_specs=pl.BlockSpec((blk, hw), lambda b, h, i: (b * nq + i, h)),
        out_shape=jax.ShapeDtypeStruct((batch * seq, n_heads * head_dim), BF16),
        compiler_params=_params(("parallel", "parallel", "arbitrary")),
        name="sb_attention",
    )(z, z, z)


def _s5_prepare(lam_re, lam_im, log_dt, b_re, b_im, c_re, c_im, d_skip, n_chunks):
    hp = lax.Precision.HIGHEST
    chunk = S5_CHUNK
    g, p = lam_re.shape
    h = b_re.shape[-1]
    gg = LANES // h
    o = g // gg
    dt = jnp.exp(log_dt)[:, None]
    ks = jnp.arange(chunk + 1, dtype=F32)[None, :, None]
    mag = jnp.exp(lam_re[:, None, :] * dt[:, None, :] * ks)
    ang = lam_im[:, None, :] * dt[:, None, :] * ks
    ak_re = mag * jnp.cos(ang)
    ak_im = mag * jnp.sin(ang)
    nr = ak_re[:, 1] - 1.0
    ni = ak_im[:, 1]
    den = lam_re * lam_re + lam_im * lam_im
    coef_re = ((nr * lam_re + ni * lam_im) / den)[..., None]
    coef_im = ((ni * lam_re - nr * lam_im) / den)[..., None]
    bbar_re = coef_re * b_re - coef_im * b_im
    bbar_im = coef_re * b_im + coef_im * b_re
    ca_re = c_re[:, None] * ak_re[:, :, None, :] - c_im[:, None] * ak_im[:, :, None, :]
    ca_im = c_re[:, None] * ak_im[:, :, None, :] + c_im[:, None] * ak_re[:, :, None, :]

    cam_re = ca_re[:, :chunk].transpose(0, 3, 1, 2).reshape(g, p, chunk * h)
    cam_im = ca_im[:, :chunk].transpose(0, 3, 1, 2).reshape(g, p, chunk * h)
    kt = (jnp.einsum("gph,gpm->ghm", bbar_re, cam_re, precision=hp)
          - jnp.einsum("gph,gpm->ghm", bbar_im, cam_im, precision=hp))
    def same_group(n_rows, row_block, n_cols, col_block):
        r = (jnp.arange(n_rows) // row_block) % gg
        c = (jnp.arange(n_cols) // col_block) % gg
        return (r[:, None] == c[None, :]).astype(F32)

    tile_h = jnp.tile(jnp.eye(h, dtype=F32), (1, gg))
    tile_p = jnp.tile(jnp.eye(p, dtype=F32), (1, gg))
    kt = kt.reshape(o, gg, h, chunk, h).transpose(0, 3, 1, 2, 4).reshape(o, chunk, LANES, h)
    bd = jnp.einsum("okrh,hc->okrc", kt, tile_h, precision=hp) * same_group(LANES, h, LANES, h)
    dvec = d_skip.reshape(o, LANES)
    lags = [bd[:, k] for k in range(chunk)]
    lags[0] = lags[0] + dvec[:, :, None] * jnp.eye(LANES, dtype=F32)[None]
    zero_blk = jnp.zeros_like(lags[0])
    m_intra = jnp.concatenate(
        [jnp.concatenate([lags[j - i] if j >= i else zero_blk for j in range(chunk)], axis=2)
         for i in range(chunk)], axis=1)

    def expand_in(ak_part_a, ak_part_b, b_a, b_b, sign):
        rk_a = jnp.stack([ak_part_a[:, chunk - 1 - i] for i in range(chunk)], axis=1)
        rk_b = jnp.stack([ak_part_b[:, chunk - 1 - i] for i in range(chunk)], axis=1)
        val = (rk_a[:, :, None, :] * b_a.transpose(0, 2, 1)[:, None]
               + sign * rk_b[:, :, None, :] * b_b.transpose(0, 2, 1)[:, None])
        val = val.reshape(o, gg, chunk, h, p).transpose(0, 2, 1, 3, 4).reshape(o, chunk * LANES, p)
        wide = jnp.einsum("orp,pc->orc", val.astype(BF16), tile_p.astype(BF16), preferred_element_type=BF16)
        return wide * same_group(chunk * LANES, h, gg * p, p).astype(BF16)

    m_in = jnp.concatenate([expand_in(ak_re, ak_im, bbar_re, bbar_im, -1.0),
                            expand_in(ak_re, ak_im, bbar_im, bbar_re, 1.0)], axis=-1)

    def expand_out(ca):
        val = ca[:, 1:].reshape(o, gg, chunk, h, p).transpose(0, 4, 2, 1, 3).reshape(o, p, chunk * LANES)
        wide = jnp.einsum("pc,opr->ocr", tile_p.astype(BF16), val.astype(BF16), preferred_element_type=BF16)
        return wide * same_group(gg * p, p, chunk * LANES, h).astype(BF16)

    m_out = jnp.concatenate([expand_out(ca_re), -expand_out(ca_im)], axis=1)

    steps = max(1, int(math.ceil(math.log2(max(n_chunks, 2)))))
    cr, ci = ak_re[:, chunk].reshape(o, gg * p), ak_im[:, chunk].reshape(o, gg * p)
    sc_a, sc_b = [], []
    for _ in range(steps):
        sc_a.append(jnp.concatenate([cr, cr], axis=-1))
        sc_b.append(jnp.concatenate([-ci, ci], axis=-1))
        cr, ci = cr * cr - ci * ci, 2.0 * cr * ci
    return m_intra.astype(BF16), m_in, m_out, jnp.stack(sc_a, axis=1), jnp.stack(sc_b, axis=1)


def _s5_kernel(u_ref, mi_ref, min_ref, mout_ref, sa_ref, sb_ref, y_ref, *, n_chunks, steps):
    chunk = S5_CHUNK
    n = u_ref.shape[0] // chunk
    u = jnp.concatenate([u_ref[pl.ds(i, n, stride=chunk), :] for i in range(chunk)], axis=1).astype(BF16)
    y = jnp.dot(u, mi_ref[...], preferred_element_type=F32)
    x = jnp.dot(u, min_ref[...], preferred_element_type=F32)
    half = x.shape[1] // 2
    c = lax.rem(lax.broadcasted_iota(jnp.int32, x.shape, 0), n_chunks)
    sa = sa_ref[...]
    sb = sb_ref[...]
    for k in range(steps):
        sh = 1 << k
        xs = jnp.where(c >= sh, pltpu.roll(x, sh, axis=0), 0.0)
        x = x + xs * sa[k:k + 1, :] + pltpu.roll(xs, half, axis=1) * sb[k:k + 1, :]
    s_in = jnp.where(c >= 1, pltpu.roll(x, 1, axis=0), 0.0)
    y = y + jnp.dot(s_in.astype(BF16), mout_ref[...], preferred_element_type=F32)
    for i in range(chunk):
        y_ref[pl.ds(i, n, stride=chunk), :] = y[:, i * LANES:(i + 1) * LANES]


def _s5(u, m_intra, m_in, m_out, sc_a, sc_b, *, n_chunks):
    t, d_ssm = u.shape
    o, lh, st = m_in.shape
    steps = sc_a.shape[1]
    mat = lambda i: (i, 0, 0)
    return pl.pallas_call(
        functools.partial(_s5_kernel, n_chunks=n_chunks, steps=steps),
        grid=(o,),
        in_specs=[
            pl.BlockSpec((t, LANES), lambda i: (0, i)),
            pl.BlockSpec((None, lh, lh), mat), pl.BlockSpec((None, lh, st), mat), pl.BlockSpec((None, st, lh), mat),
            pl.BlockSpec((None, steps, st), mat), pl.BlockSpec((None, steps, st), mat),
        ],
        out_specs=pl.BlockSpec((t, LANES), lambda i: (0, i)),
        out_shape=jax.ShapeDtypeStruct((t, d_ssm), F32),
        compiler_params=_params(("parallel",)),
        name="s5_chunked",
    )(u, m_intra, m_in, m_out, sc_a, sc_b)


def _postmix_kernel(oa_ref, ys_ref, x_ref, wglu_ref, woa_ref, wob_ref, ga_ref, gs_ref, gf_ref,
                    wr_ref, rb_ref, h1_ref, idx_ref, gate_ref, loc_ref, cnt_ref, run_ref, *, n_experts):
    step = pl.program_id(0)

    @pl.when(step == 0)
    def _():
        run_ref[...] = jnp.zeros(run_ref.shape, F32)

    tm = x_ref.shape[0]
    n_groups = POSTMIX_ROW_GROUPS if tm % (8 * POSTMIX_ROW_GROUPS) == 0 else 1
    rows = [pl.ds(g * (tm // n_groups), tm // n_groups) for g in range(n_groups)]

    def dot(a, w_ref):
        return jnp.dot(a, w_ref[...], preferred_element_type=F32)

    ys = [ys_ref[r, :] for r in rows]
    ys = [0.5 * y * (1.0 + jnp.tanh(math.sqrt(2.0 / math.pi) * (y + 0.044715 * (y * y * y)))) for y in ys]
    glu = [dot(y.astype(BF16), wglu_ref) for y in ys]
    ys = [y * _sigmoid(g) for y, g in zip(ys, glu)]
    nas = [_rms(oa_ref[r, :].astype(F32), ga_ref[...]).astype(BF16) for r in rows]
    nss = [_rms(y, gs_ref[...]).astype(BF16) for y in ys]
    h1s = [x_ref[r, :] + dot(na, woa_ref) + dot(ns, wob_ref) for r, na, ns in zip(rows, nas, nss)]
    for r, h1 in zip(rows, h1s):
        h1_ref[r, :] = h1

    ms = [_rms(h1, gf_ref[...]) for h1 in h1s]
    m_his = [m.astype(BF16) for m in ms]
    m_los = [(m - m_hi.astype(F32)).astype(BF16) for m, m_hi in zip(ms, m_his)]
    r_his = [dot(m_hi, wr_ref) for m_hi in m_his]
    r_los = [dot(m_lo, wr_ref) for m_lo in m_los]
    logits = jnp.concatenate([r_hi[:, :n_experts] + r_hi[:, n_experts:] + r_lo[:, :n_experts]
                              for r_hi, r_lo in zip(r_his, r_los)], axis=0)
    scores = _sigmoid(logits)
    vals = scores + rb_ref[...]
    lane = lax.broadcasted_iota(jnp.int32, vals.shape, 1).astype(F32)
    picks, firsts, raws = [], [], []
    sel = jnp.zeros(vals.shape, F32)
    for _ in range(TOP_K):
        mx = jnp.max(vals, axis=1, keepdims=True)
        first = jnp.min(jnp.where(vals == mx, lane, float(n_experts)), axis=1, keepdims=True)
        pick = lane == first
        picks.append(pick)
        firsts.append(first)
        raws.append(jnp.sum(jnp.where(pick, scores, 0.0), axis=1, keepdims=True))
        vals = jnp.where(pick, -jnp.inf, vals)
        sel = jnp.where(pick, 1.0, sel)
    denom = raws[0]
    for r in raws[1:]:
        denom = denom + r

    r_i = lax.broadcasted_iota(jnp.int32, (tm, tm), 0)
    c_i = lax.broadcasted_iota(jnp.int32, (tm, tm), 1)
    before = jnp.where(c_i < r_i, 1.0, 0.0).astype(BF16)
    rank = jnp.dot(before, sel.astype(BF16), preferred_element_type=F32) + run_ref[...]
    run_ref[...] = run_ref[...] + jnp.sum(sel, axis=0, keepdims=True)
    cnt_ref[...] = run_ref[...]

    slot = lax.broadcasted_iota(jnp.int32, (tm, LANES), 1)
    idx_w = jnp.zeros((tm, LANES), F32)
    loc_w = jnp.zeros((tm, LANES), F32)
    gate_w = jnp.zeros((tm, LANES), F32)
    for k in range(TOP_K):
        loc = jnp.sum(jnp.where(picks[k], rank, 0.0), axis=1, keepdims=True)
        idx_w = jnp.where(slot == k, firsts[k], idx_w)
        loc_w = jnp.where(slot == k, loc, loc_w)
        gate_w = jnp.where(slot == k, raws[k] / denom * ROUTED_SCALE, gate_w)
    idx_ref[...] = idx_w.T[:IDX_ROWS].astype(jnp.int32)
    loc_ref[...] = loc_w.T[:IDX_ROWS].astype(jnp.int32)
    gate_ref[...] = gate_w[:, :IDX_ROWS]


def _postmix(oa, ys, x2, wglu, woa, wob, ga, gs, gf, wr, rb, *, tm):
    t, d = x2.shape
    da = oa.shape[1]
    dsm = ys.shape[1]
    e = rb.shape[1]
    row = lambda i: (i, 0)
    col = lambda i: (0, i)
    fix = lambda i: (0, 0)
    return pl.pallas_call(
        functools.partial(_postmix_kernel, n_experts=e),
        grid=(t // tm,),
        in_specs=[
            pl.BlockSpec((tm, da), row), pl.BlockSpec((tm, dsm), row), pl.BlockSpec((tm, d), row),
            pl.BlockSpec((dsm, dsm), fix), pl.BlockSpec((da, d), fix), pl.BlockSpec((dsm, d), fix),
            pl.BlockSpec((1, da), fix), pl.BlockSpec((1, dsm), fix), pl.BlockSpec((1, d), fix),
            pl.BlockSpec((d, 2 * e), fix), pl.BlockSpec((1, e), fix),
        ],
        out_specs=[
            pl.BlockSpec((tm, d), row), pl.BlockSpec((IDX_ROWS, tm), col), pl.BlockSpec((tm, IDX_ROWS), row),
            pl.BlockSpec((IDX_ROWS, tm), col), pl.BlockSpec((1, e), fix),
        ],
        out_shape=[
            jax.ShapeDtypeStruct((t, d), F32), jax.ShapeDtypeStruct((IDX_ROWS, t), jnp.int32),
            jax.ShapeDtypeStruct((t, IDX_ROWS), F32), jax.ShapeDtypeStruct((IDX_ROWS, t), jnp.int32),
            jax.ShapeDtypeStruct((1, e), F32),
        ],
        scratch_shapes=[pltpu.VMEM((1, e), F32)],
        compiler_params=_params(("arbitrary",)),
        name="postmix_router",
    )(oa, ys, x2, wglu, woa, wob, ga, gs, gf, wr, rb)


def _rows_to_tiles(ref, packed, sub):
    rows = packed.shape[0]
    for s in range(sub):
        ref[pl.ds(s, rows, stride=sub), :] = packed[:, s * LANES:(s + 1) * LANES]


def _tiles_to_rows(ref, rows, sub):
    return jnp.concatenate([ref[pl.ds(s, rows, stride=sub), :] for s in range(sub)], axis=1)


def _tile_rows(ref, r, sub):
    start = r * sub if isinstance(r, int) else pl.multiple_of(r * sub, sub)
    return ref.at[pl.ds(start, sub)]


def _shared_kernel(dest_ref, tab_init_hbm, h1_ref, gf_ref, wg_ref, wu_ref, wd_ref, h1s_ref, mp_ref, tab_ref,
                   sem, *, sub, n_tokens):
    tm = h1_ref.shape[0]
    base = pl.program_id(0) * tm

    @pl.when(pl.program_id(0) == 0)
    def _():
        init = pltpu.make_async_copy(tab_init_hbm, tab_ref, sem)
        init.start()
        init.wait()

    for t in range(tm):
        for k in range(TOP_K):
            tab_ref[dest_ref[k * tm + t]] = k * n_tokens + base + t
    h1 = h1_ref[...]
    m = _rms(h1, gf_ref[...])
    _rows_to_tiles(mp_ref, _pack_halves(m), sub)
    mb = m.astype(BF16)
    a = jnp.dot(mb, wg_ref[...], preferred_element_type=F32)
    b = jnp.dot(mb, wu_ref[...], preferred_element_type=F32)
    hid = (a * _sigmoid(a) * b).astype(BF16)
    h1s_ref[...] = h1 + jnp.dot(hid, wd_ref[...], preferred_element_type=F32)


def _shared(dest, table0, h1, gf, wg, wu, wd, *, tm):
    t, d = h1.shape
    f = wg.shape[1]
    sub = d // 2 // LANES
    row = lambda i: (i, 0)
    fix = lambda i: (0, 0)
    return pl.pallas_call(
        functools.partial(_shared_kernel, sub=sub, n_tokens=t),
        grid=(t // tm,),
        in_specs=[
            pl.BlockSpec((IDX_ROWS * tm,), lambda i: (i,), memory_space=pltpu.SMEM),
            pl.BlockSpec(memory_space=pl.ANY),
            pl.BlockSpec((tm, d), row), pl.BlockSpec((1, d), fix),
            pl.BlockSpec((d, f), fix), pl.BlockSpec((d, f), fix), pl.BlockSpec((f, d), fix),
        ],
        out_specs=[pl.BlockSpec((tm, d), row), pl.BlockSpec((tm * sub, LANES), row),
                   pl.BlockSpec(memory_space=pltpu.SMEM)],
        out_shape=[jax.ShapeDtypeStruct((t, d), F32), jax.ShapeDtypeStruct((t * sub, LANES), jnp.int32),
                   jax.ShapeDtypeStruct(table0.shape, jnp.int32)],
        scratch_shapes=[pltpu.SemaphoreType.DMA(())],
        compiler_params=_params(("arbitrary",)),
        name="shared_table",
    )(dest, table0, h1, gf, wg, wu, wd)


def _experts_kernel(be_ref, first_ref, nxt_ref, nu_ref, nv_ref, tab_ref, mp_hbm, wg_hbm, wu_hbm, wd_hbm, yb_hbm,
                    wgf_ref, wuf_ref, wdf_ref, wgb_ref, wub_ref, wdb_ref, x0_ref, x1_ref, y0_ref, y1_ref,
                    wsem, gsem, ssem, *, sub, tm, n_tokens):
    i = pl.program_id(0)
    nu = nu_ref[0]
    xs = (x0_ref, x1_ref)
    ys = (y0_ref, y1_ref)
    spare = TOP_K * n_tokens
    grp = ROW_DMA_GROUP
    n_grp = tm // grp

    def for_groups(n_rows, fn):
        fn(0)
        for g in range(1, n_grp):
            @pl.when(n_rows > g * grp)
            def _(g=g):
                fn(g)

    def token_of(e):
        return e & (n_tokens - 1) if n_tokens & (n_tokens - 1) == 0 else lax.rem(e, n_tokens)

    def gather(blk, r, s):
        return pltpu.make_async_copy(_tile_rows(mp_hbm, token_of(tab_ref[blk * tm + r]), sub),
                                     _tile_rows(xs[s], r, sub), gsem.at[s])

    def scatter(entry, r, s):
        return pltpu.make_async_copy(_tile_rows(ys[s], r, sub), _tile_rows(yb_hbm, entry, sub), ssem.at[s])

    def gather_wait(s, g):
        rows = pl.ds(g * grp * sub, grp * sub)
        pltpu.make_async_copy(mp_hbm.at[pl.ds(0, grp * sub)], xs[s].at[rows], gsem.at[s]).wait()

    def scatter_wait(s, g):
        rows = pl.ds(g * grp * sub, grp * sub)
        pltpu.make_async_copy(ys[s].at[rows], yb_hbm.at[pl.ds(0, grp * sub)], ssem.at[s]).wait()

    def whole_groups(n_rows):
        return (n_rows + grp - 1) // grp * grp

    def weight_copies(e, s):
        copies = []
        for j, (src, dst) in enumerate(((wg_hbm, wgf_ref), (wu_hbm, wuf_ref), (wd_hbm, wdf_ref))):
            rows = dst.shape[1] // WEIGHT_DMA_SPLIT
            for c in range(WEIGHT_DMA_SPLIT):
                sl = pl.ds(c * rows, rows)
                copies.append(pltpu.make_async_copy(src.at[e, sl], dst.at[s, sl], wsem.at[s, j]))
        return copies

    @pl.when(i == 0)
    def _():
        for c in weight_copies(be_ref[0], 0):
            c.start()

        for ref in (x0_ref, x1_ref, y1_ref):
            ref[...] = jnp.zeros(ref.shape, ref.dtype)

        def first_rows(r, carry):
            gather(0, r, 0).start()
            return carry

        lax.fori_loop(0, whole_groups(nv_ref[0]), first_rows, 0)

    @pl.when((i < nu) & (first_ref[jnp.minimum(i, first_ref.shape[0] - 1)] != 0))
    def _():
        s = first_ref[i] - 1
        for c in weight_copies(be_ref[i], s):
            c.wait()

        @pl.when(nxt_ref[i] >= 0)
        def _():
            for c in weight_copies(nxt_ref[i], 1 - s):
                c.start()

        wgb_ref[...] = wgf_ref[s].astype(BF16)
        wub_ref[...] = wuf_ref[s].astype(BF16)
        wdb_ref[...] = wdf_ref[s].astype(BF16)

    def block(p):
        for_groups(nv_ref[i], lambda g: gather_wait(p, g))

        @pl.when(i >= 1)
        def _():
            for_groups(jnp.where(i >= 2, nv_ref[jnp.maximum(i - 2, 0)], tm), lambda g: scatter_wait(p, g))

        nb = jnp.minimum(i + 1, nu - 1)
        pb = jnp.maximum(i - 1, 0)

        def gather_group(g):
            for r in range(g * grp, (g + 1) * grp):
                gather(nb, r, 1 - p).start()

        def scatter_group(g):
            for r in range(g * grp, (g + 1) * grp):
                entry = jnp.where(i >= 1, tab_ref[pb * tm + r], spare + r)
                scatter(entry, r, 1 - p).start()

        for g in range(1, n_grp):
            @pl.when(nv_ref[nb] > g * grp)
            def _(g=g):
                gather_group(g)

            @pl.when(jnp.where(i >= 1, nv_ref[pb], tm) > g * grp)
            def _(g=g):
                scatter_group(g)

        gather_group(0)
        scatter_group(0)
        lo, hi = _unpack_halves(_tiles_to_rows(xs[p], tm, sub))
        lo = lo.astype(BF16)
        hi = hi.astype(BF16)
        n = lo.shape[1]
        a = (jnp.dot(lo, wgb_ref[:n, :], preferred_element_type=F32)
             + jnp.dot(hi, wgb_ref[n:, :], preferred_element_type=F32))
        b = (jnp.dot(lo, wub_ref[:n, :], preferred_element_type=F32)
             + jnp.dot(hi, wub_ref[n:, :], preferred_element_type=F32))
        hid = (a * _sigmoid(a) * b).astype(BF16)
        _rows_to_tiles(ys[p], _pack_halves(jnp.dot(hid, wdb_ref[...], preferred_element_type=F32)), sub)

    def flush(p):
        def last_rows(r, carry):
            scatter(tab_ref[(nu - 1) * tm + r], r, 1 - p).start()
            return carry

        last = nv_ref[nu - 1]
        for_groups(jnp.where(nu >= 2, nv_ref[jnp.maximum(nu - 2, 0)], tm), lambda g: scatter_wait(p, g))
        lax.fori_loop(0, whole_groups(last), last_rows, 0)
        for_groups(last, lambda g: scatter_wait(1 - p, g))
        for_groups(last, lambda g: gather_wait(p, g))

    for p in range(2):
        @pl.when((i < nu) & (lax.rem(i, 2) == p))
        def _(p=p):
            block(p)

        @pl.when((i == nu) & (lax.rem(i, 2) == p))
        def _(p=p):
            flush(p)


def _experts(blk_e, first, nxt, n_used, n_valid, table, mp, wg, wu, wd, *, tm):
    _, d, f = wg.shape
    sub = d // 2 // LANES
    n_tokens = mp.shape[0] // sub
    n_blocks = table.shape[0] // tm
    return pl.pallas_call(
        functools.partial(_experts_kernel, sub=sub, tm=tm, n_tokens=n_tokens),
        grid_spec=pltpu.PrefetchScalarGridSpec(
            num_scalar_prefetch=6,
            grid=(n_blocks + 1,),
            in_specs=[pl.BlockSpec(memory_space=pl.ANY)] * 4,
            out_specs=pl.BlockSpec(memory_space=pl.ANY),
            scratch_shapes=[pltpu.VMEM((2, d, f), F32), pltpu.VMEM((2, d, f), F32), pltpu.VMEM((2, f, d), F32),
                            pltpu.VMEM((d, f), BF16), pltpu.VMEM((d, f), BF16), pltpu.VMEM((f, d), BF16),
                            pltpu.VMEM((tm * sub, LANES), jnp.int32), pltpu.VMEM((tm * sub, LANES), jnp.int32),
                            pltpu.VMEM((tm * sub, LANES), jnp.int32), pltpu.VMEM((tm * sub, LANES), jnp.int32),
                            pltpu.SemaphoreType.DMA((2, 3)), pltpu.SemaphoreType.DMA((2,)),
                            pltpu.SemaphoreType.DMA((2,))],
        ),
        out_shape=jax.ShapeDtypeStruct(((TOP_K * n_tokens + tm) * sub, LANES), jnp.int32),
        compiler_params=_params(("arbitrary",)),
        name="routed_experts",
    )(blk_e, first, nxt, n_used, n_valid, table, mp, wg, wu, wd)


def _final_kernel(h_ref, gate_ref, p_ref, gp_ref, wgate_ref, wproj_ref, *rest, sub):
    y_refs, o_ref = rest[:TOP_K], rest[TOP_K]
    tm = h_ref.shape[0]
    pp = jnp.dot(p_ref[...].astype(BF16), wproj_ref[...], preferred_element_type=F32)
    gate = gate_ref[...]
    h = h_ref[...]
    n = h.shape[1] // 2
    acc_lo = h[:, :n]
    acc_hi = h[:, n:]
    for k in range(TOP_K):
        lo, hi = _unpack_halves(_tiles_to_rows(y_refs[k], tm, sub))
        acc_lo = acc_lo + gate[:, k:k + 1] * lo
        acc_hi = acc_hi + gate[:, k:k + 1] * hi
    h2 = jnp.concatenate([acc_lo, acc_hi], axis=1)
    nrm = _rms(h2, gp_ref[...]).astype(BF16)
    o_ref[...] = h2 + _sigmoid(jnp.dot(nrm, wgate_ref[...], preferred_element_type=F32)) * pp


def _final(h1s, gate, p2, gp, wgate, wproj, yb, *, tm):
    t, d = h1s.shape
    dp = p2.shape[1]
    sub = d // 2 // LANES
    nt = t // tm
    row = lambda i: (i, 0)
    fix = lambda i: (0, 0)
    y_specs = [pl.BlockSpec((tm * sub, LANES), lambda i, k=k: (k * nt + i, 0)) for k in range(TOP_K)]
    return pl.pallas_call(
        functools.partial(_final_kernel, sub=sub),
        grid=(nt,),
        in_specs=[
            pl.BlockSpec((tm, d), row), pl.BlockSpec((tm, IDX_ROWS), row), pl.BlockSpec((tm, dp), row),
            pl.BlockSpec((1, d), fix), pl.BlockSpec((d, d), fix), pl.BlockSpec((dp, d), fix),
        ] + y_specs,
        out_specs=pl.BlockSpec((tm, d), row),
        out_shape=jax.ShapeDtypeStruct((t, d), F32),
        compiler_params=_params(("parallel",)),
        name="combine_ple",
    )(h1s, gate, p2, gp, wgate, wproj, *([yb] * TOP_K))


def _largest_tile(n, cap):
    t = min(n, cap)
    while n % t:
        t //= 2
    return t


def _layer(h, p_l, norm_mix, w_in, q_norm, k_norm, lam_re, lam_im, log_dt, b_re, b_im, c_re, c_im, d_skip,
           w_glu, attn_out_norm, ssm_out_norm, w_out, norm_ffn, w_router, router_bias, w_exp_gate,
           w_exp_up, w_exp_down, w_sh_gate, w_sh_up, w_sh_down, norm_ple, w_ple_gate, w_ple_proj):
    bsz, seq, d = h.shape
    t = bsz * seq
    head_dim = q_norm.shape[-1]
    d_attn = attn_out_norm.shape[-1]
    n_heads = d_attn // head_dim
    n_experts = w_router.shape[-1]
    x2 = h.reshape(t, d)

    tn = d_attn if d_attn % 128 == 0 else head_dim
    scale = 1.0 / math.sqrt(head_dim)
    hg = jnp.concatenate([jnp.tile(q_norm * scale, n_heads), jnp.tile(k_norm, n_heads),
                          jnp.ones((w_in.shape[1] - 2 * d_attn,), F32)])[None, :]
    z, u = _inproj(x2, norm_mix[None, :], w_in.astype(BF16), hg, d_qkv=3 * d_attn,
                   n_norm_tiles=2 * d_attn // tn, head_dim=head_dim, tm=_largest_tile(t, 1024), tn=tn)

    o_attn = _attention(z, batch=bsz, seq=seq, n_heads=n_heads, head_dim=head_dim, blk=_largest_tile(seq, 256))

    n_chunks = seq // S5_CHUNK
    mats = _s5_prepare(lam_re, lam_im, log_dt, b_re, b_im, c_re, c_im, d_skip, n_chunks)
    y_ssm = _s5(u, *mats, n_chunks=n_chunks)

    tm = _largest_tile(t, 256)
    w_out_b = w_out.astype(BF16)
    wr_hi = w_router.astype(BF16)
    wr_lo = (w_router - wr_hi.astype(F32)).astype(BF16)
    h1, idx, gate, loc, counts = _postmix(
        o_attn, y_ssm, x2, w_glu.astype(BF16), w_out_b[:d_attn], w_out_b[d_attn:],
        attn_out_norm[None, :], ssm_out_norm[None, :], norm_ffn[None, :],
        jnp.concatenate([wr_hi, wr_lo], axis=1), router_bias[None, :], tm=_largest_tile(t, 512))

    rb = _largest_tile(t, 256)
    n_pad = t * TOP_K + n_experts * rb
    n_blocks = n_pad // rb
    cnt = counts[0].astype(jnp.int32)
    pcnt = (cnt + rb - 1) // rb * rb
    pend = jnp.cumsum(pcnt)
    pstart = pend - pcnt
    onehot = idx[:, :, None] == jnp.arange(n_experts, dtype=jnp.int32)[None, None, :]
    dest = jnp.sum(jnp.where(onehot, pstart[None, None, :], 0), axis=-1) + loc
    dest = dest.reshape(IDX_ROWS, t // tm, tm).transpose(1, 0, 2).reshape(-1)
    n_used = (pend[-1] // rb).astype(jnp.int32)
    blk_ids = jnp.arange(n_blocks, dtype=jnp.int32)
    starts = (blk_ids * rb)[:, None]
    owner = (starts >= pstart[None, :]) & (starts < pend[None, :])
    blk_e = jnp.sum(jnp.where(owner, jnp.arange(n_experts, dtype=jnp.int32)[None, :], 0), axis=1)
    is_first = jnp.any(owner & (starts == pstart[None, :]), axis=1).astype(jnp.int32)
    n_valid = jnp.sum(jnp.where(owner, jnp.clip((pstart + cnt)[None, :] - starts, 0, rb), 0), axis=1)
    first_pos = jnp.where(is_first > 0, blk_ids, n_blocks)
    nxt_pos = jnp.concatenate([lax.cummin(first_pos, axis=0, reverse=True)[1:],
                               jnp.full((1,), n_blocks, jnp.int32)])
    nxt_hit = nxt_pos[:, None] == blk_ids[None, :]
    nxt = jnp.where(nxt_pos < n_blocks, jnp.sum(jnp.where(nxt_hit, blk_e[None, :], 0), axis=1), -1)
    last_e = jnp.sum(jnp.where(blk_ids == n_used - 1, blk_e, 0))
    blk_e = jnp.where(blk_ids < n_used, blk_e, last_e).astype(jnp.int32)
    first_code = jnp.where(is_first > 0, 1 + (jnp.cumsum(is_first) - 1) % 2, 0).astype(jnp.int32)

    table0 = TOP_K * t + jnp.arange(n_pad, dtype=jnp.int32) % rb
    h1s, mp, table = _shared(dest, table0, h1, norm_ffn[None, :], w_sh_gate.astype(BF16), w_sh_up.astype(BF16),
                             w_sh_down.astype(BF16), tm=tm)
    yb = _experts(blk_e, first_code, nxt.astype(jnp.int32), n_used[None], n_valid.astype(jnp.int32), table, mp,
                  w_exp_gate, w_exp_up, w_exp_down, tm=rb)
    out = _final(h1s, gate, p_l.reshape(t, -1), norm_ple[None, :], w_ple_gate.astype(BF16),
                 w_ple_proj.astype(BF16), yb, tm=tm)
    return out.reshape(bsz, seq, d)


def kernel(x, p, norm_mix, w_in, q_norm, k_norm, ssm_lam_re, ssm_lam_im, ssm_log_dt, ssm_b_re, ssm_b_im,
           ssm_c_re, ssm_c_im, ssm_d, w_glu, attn_out_norm, ssm_out_norm, w_out, norm_ffn, w_router,
           router_bias, w_exp_gate, w_exp_up, w_exp_down, w_sh_gate, w_sh_up, w_sh_down, norm_ple,
           w_ple_gate, w_ple_proj):
    h = x
    for i in range(p.shape[0]):
        h = _layer(h, p[i], norm_mix[i], w_in[i], q_norm[i], k_norm[i], ssm_lam_re[i], ssm_lam_im[i],
                   ssm_log_dt[i], ssm_b_re[i], ssm_b_im[i], ssm_c_re[i], ssm_c_im[i], ssm_d[i], w_glu[i],
                   attn_out_norm[i], ssm_out_norm[i], w_out[i], norm_ffn[i], w_router[i], router_bias[i],
                   w_exp_gate[i], w_exp_up[i], w_exp_down[i], w_sh_gate[i], w_sh_up[i], w_sh_down[i],
                   norm_ple[i], w_ple_gate[i], w_ple_proj[i])
    return h
```

```python
import functools
import math

import jax
import jax.numpy as jnp
from jax import lax
from jax.experimental import pallas as pl
from jax.experimental.pallas import tpu as pltpu

NORM_EPS = 1e-6
TOP_K = 6
ROUTED_SCALE = 2.5
IDX_ROWS = 8
LANES = 128
S5_CHUNK = 8
POSTMIX_ROW_GROUPS = 2
VMEM_LIMIT = 56 * 1024 * 1024
EXPERTS_VMEM_LIMIT = 62 * 1024 * 1024

F32 = jnp.float32
BF16 = jnp.bfloat16
HIGH_HALF = -65536


def _rms(x, g):
    return x * lax.rsqrt(jnp.mean(x * x, axis=-1, keepdims=True) + NORM_EPS) * g


def _sigmoid(x):
    return 1.0 / (1.0 + jnp.exp(-x))


def _params(sem):
    return pltpu.CompilerParams(dimension_semantics=sem, vmem_limit_bytes=VMEM_LIMIT)


def _pack_halves(x):
    n = x.shape[1] // 2
    lo = lax.bitcast_convert_type(x[:, :n].astype(BF16).astype(F32), jnp.int32)
    hi = lax.bitcast_convert_type(x[:, n:].astype(BF16).astype(F32), jnp.int32)
    return (hi & HIGH_HALF) | lax.shift_right_logical(lo, 16)


def _unpack_halves(p):
    lo = lax.bitcast_convert_type(lax.shift_left(p, 16), F32)
    hi = lax.bitcast_convert_type(p & HIGH_HALF, F32)
    return lo, hi


def _inproj_kernel(x_ref, g_ref, w_ref, hg_ref, o_ref, u_ref, xn_ref, *, n_norm_tiles, n_qkv_tiles, head_dim):
    j = pl.program_id(1)

    @pl.when(j == 0)
    def _():
        xn_ref[...] = _rms(x_ref[...], g_ref[...]).astype(BF16)

    acc = jnp.dot(xn_ref[...], w_ref[...], preferred_element_type=F32)
    tn = acc.shape[1]

    @pl.when(j < n_norm_tiles)
    def _():
        hg = hg_ref[...]
        for h in range(tn // head_dim):
            sl = slice(h * head_dim, (h + 1) * head_dim)
            o_ref[:, sl] = _rms(acc[:, sl], hg[:, sl]).astype(o_ref.dtype)

    @pl.when((j >= n_norm_tiles) & (j < n_qkv_tiles))
    def _():
        o_ref[...] = acc.astype(o_ref.dtype)

    @pl.when(j >= n_qkv_tiles)
    def _():
        u_ref[...] = acc


def _inproj(x2, g, w, hg, *, d_qkv, n_norm_tiles, head_dim, tm, tn):
    t, d = x2.shape
    n = w.shape[1]
    nq = d_qkv // tn
    return pl.pallas_call(
        functools.partial(_inproj_kernel, n_norm_tiles=n_norm_tiles, n_qkv_tiles=nq, head_dim=head_dim),
        grid=(t // tm, n // tn),
        in_specs=[
            pl.BlockSpec((tm, d), lambda i, j: (i, 0)),
            pl.BlockSpec((1, d), lambda i, j: (0, 0)),
            pl.BlockSpec((d, tn), lambda i, j: (0, j)),
            pl.BlockSpec((1, tn), lambda i, j: (0, j)),
        ],
        out_specs=[
            pl.BlockSpec((tm, tn), lambda i, j: (i, jnp.minimum(j, nq - 1))),
            pl.BlockSpec((tm, tn), lambda i, j: (i, jnp.maximum(j - nq, 0))),
        ],
        out_shape=[jax.ShapeDtypeStruct((t, d_qkv), BF16), jax.ShapeDtypeStruct((t, n - d_qkv), F32)],
        scratch_shapes=[pltpu.VMEM((tm, d), BF16)],
        compiler_params=_params(("parallel", "arbitrary")),
        name="inproj",
    )(x2, g, w, hg)


SKIP_AFTER = 110.0


def _attn_kernel(q_ref, k_ref, v_ref, o_ref, *, blk, hd, heads):
    i = pl.program_id(2)
    row = lax.broadcasted_iota(jnp.int32, (blk, blk), 0)
    col = lax.broadcasted_iota(jnp.int32, (blk, blk), 1)
    causal = col < row
    later = jnp.where(row > col, 1.0, 0.0).astype(BF16)
    lanes = [slice(h * hd, (h + 1) * hd) for h in range(heads)]
    qs = [q_ref[:, sl] for sl in lanes]

    def step(kb, accs, runs, diag):
        start = pl.multiple_of(kb * blk, blk)
        zs, sps, inners, new_accs, new_runs = [], [], [], [], []
        for h, sl in enumerate(lanes):
            k = k_ref[pl.ds(start, blk), sl]
            zs.append(lax.dot_general(qs[h], k, (((1,), (1,)), ((), ())), preferred_element_type=F32))
        for z in zs:
            sp = jnp.maximum(z, 0.0) + jnp.log(1.0 + jnp.exp(-jnp.abs(z)))
            sps.append(jnp.where(causal, sp, 0.0) if diag else sp)
        for sp in sps:
            hi = sp.astype(BF16)
            lo = (sp - hi.astype(F32)).astype(BF16)
            inners.append(jnp.dot(hi, later, preferred_element_type=F32)
                          + jnp.dot(lo, later, preferred_element_type=F32))
        for h, sl in enumerate(lanes):
            w = jnp.exp(zs[h] - sps[h] - (inners[h] + runs[h]))
            if diag:
                w = jnp.where(causal, w, 0.0)
            v = v_ref[pl.ds(start, blk), sl]
            new_accs.append(accs[h] + jnp.dot(w.astype(BF16), v, preferred_element_type=F32))
            new_runs.append(runs[h] + jnp.sum(sps[h], axis=-1, keepdims=True))
        return tuple(new_accs), tuple(new_runs)

    def keep_going(runs):
        low = runs[0]
        for r in runs[1:]:
            low = jnp.minimum(low, r)
        return (jnp.min(low) < SKIP_AFTER).astype(jnp.int32)

    accs = tuple(jnp.zeros((blk, hd), F32) for _ in lanes)
    runs = tuple(jnp.zeros((blk, 1), F32) for _ in lanes)
    accs, runs = step(i, accs, runs, True)

    def cond(c):
        n, go, _, _ = c
        return (n < i) & (go > 0)

    def body(c):
        n, _, accs, runs = c
        accs, runs = step(i - 1 - n, accs, runs, False)
        return n + 1, keep_going(runs), accs, runs

    _, _, accs, _ = lax.while_loop(cond, body, (jnp.int32(0), keep_going(runs), accs, runs))
    for h, sl in enumerate(lanes):
        o_ref[:, sl] = accs[h].astype(o_ref.dtype)


def _attention(z, *, batch, seq, n_heads, head_dim, blk):
    nq = seq // blk
    heads = next(c for c in (4, 2, 1) if n_heads % c == 0)
    hw = heads * head_dim
    ng = n_heads // heads
    return pl.pallas_call(
        functools.partial(_attn_kernel, blk=blk, hd=head_dim, heads=heads),
        grid=(batch, ng, nq),
        in_specs=[
            pl.BlockSpec((blk, hw), lambda b, h, i: (b * nq + i, h)),
            pl.BlockSpec((seq, hw), lambda b, h, i: (b, ng + h)),
            pl.BlockSpec((seq, hw), lambda b, h, i: (b, 2 * ng + h)),
        ],
        out_specs=pl.BlockSpec((blk, hw), lambda b, h, i: (b * nq + i, h)),
        out_shape=jax.ShapeDtypeStruct((batch * seq, n_heads * head_dim), BF16),
        compiler_params=_params(("parallel", "parallel", "arbitrary")),
        name="sb_attention",
    )(z, z, z)


def _s5_prepare(lam_re, lam_im, log_dt, b_re, b_im, c_re, c_im, d_skip, n_chunks):
    hp = lax.Precision.HIGHEST
    chunk = S5_CHUNK
    g, p = lam_re.shape
    h = b_re.shape[-1]
    gg = LANES // h
    o = g // gg
    dt = jnp.exp(log_dt)[:, None]
    ks = jnp.arange(chunk + 1, dtype=F32)[None, :, None]
    mag = jnp.exp(lam_re[:, None, :] * dt[:, None, :] * ks)
    ang = lam_im[:, None, :] * dt[:, None, :] * ks
    ak_re = mag * jnp.cos(ang)
    ak_im = mag * jnp.sin(ang)
    nr = ak_re[:, 1] - 1.0
    ni = ak_im[:, 1]
    den = lam_re * lam_re + lam_im * lam_im
    coef_re = ((nr * lam_re + ni * lam_im) / den)[..., None]
    coef_im = ((ni * lam_re - nr * lam_im) / den)[..., None]
    bbar_re = coef_re * b_re - coef_im * b_im
    bbar_im = coef_re * b_im + coef_im * b_re
    ca_re = c_re[:, None] * ak_re[:, :, None, :] - c_im[:, None] * ak_im[:, :, None, :]
    ca_im = c_re[:, None] * ak_im[:, :, None, :] + c_im[:, None] * ak_re[:, :, None, :]

    cam_re = ca_re[:, :chunk].transpose(0, 3, 1, 2).reshape(g, p, chunk * h)
    cam_im = ca_im[:, :chunk].transpose(0, 3, 1, 2).reshape(g, p, chunk * h)
    kt = (jnp.einsum("gph,gpm->ghm", bbar_re, cam_re, precision=hp)
          - jnp.einsum("gph,gpm->ghm", bbar_im, cam_im, precision=hp))
    def same_group(n_rows, row_block, n_cols, col_block):
        r = (jnp.arange(n_rows) // row_block) % gg
        c = (jnp.arange(n_cols) // col_block) % gg
        return (r[:, None] == c[None, :]).astype(F32)

    tile_h = jnp.tile(jnp.eye(h, dtype=F32), (1, gg))
    tile_p = jnp.tile(jnp.eye(p, dtype=F32), (1, gg))
    kt = kt.reshape(o, gg, h, chunk, h).transpose(0, 3, 1, 2, 4).reshape(o, chunk, LANES, h)
    bd = jnp.einsum("okrh,hc->okrc", kt, tile_h, precision=hp) * same_group(LANES, h, LANES, h)
    dvec = d_skip.reshape(o, LANES)
    lags = [bd[:, k] for k in range(chunk)]
    lags[0] = lags[0] + dvec[:, :, None] * jnp.eye(LANES, dtype=F32)[None]
    zero_blk = jnp.zeros_like(lags[0])
    m_intra = jnp.concatenate(
        [jnp.concatenate([lags[j - i] if j >= i else zero_blk for j in range(chunk)], axis=2)
         for i in range(chunk)], axis=1)

    def expand_in(ak_part_a, ak_part_b, b_a, b_b, sign):
        rk_a = jnp.stack([ak_part_a[:, chunk - 1 - i] for i in range(chunk)], axis=1)
        rk_b = jnp.stack([ak_part_b[:, chunk - 1 - i] for i in range(chunk)], axis=1)
        val = (rk_a[:, :, None, :] * b_a.transpose(0, 2, 1)[:, None]
               + sign * rk_b[:, :, None, :] * b_b.transpose(0, 2, 1)[:, None])
        val = val.reshape(o, gg, chunk, h, p).transpose(0, 2, 1, 3, 4).reshape(o, chunk * LANES, p)
        wide = jnp.einsum("orp,pc->orc", val.astype(BF16), tile_p.astype(BF16), preferred_element_type=BF16)
        return wide * same_group(chunk * LANES, h, gg * p, p).astype(BF16)

    m_in = jnp.concatenate([expand_in(ak_re, ak_im, bbar_re, bbar_im, -1.0),
                            expand_in(ak_re, ak_im, bbar_im, bbar_re, 1.0)], axis=-1)

    def expand_out(ca):
        val = ca[:, 1:].reshape(o, gg, chunk, h, p).transpose(0, 4, 2, 1, 3).reshape(o, p, chunk * LANES)
        wide = jnp.einsum("pc,opr->ocr", tile_p.astype(BF16), val.astype(BF16), preferred_element_type=BF16)
        return wide * same_group(gg * p, p, chunk * LANES, h).astype(BF16)

    m_out = jnp.concatenate([expand_out(ca_re), -expand_out(ca_im)], axis=1)

    steps = max(1, int(math.ceil(math.log2(max(n_chunks, 2)))))
    cr, ci = ak_re[:, chunk].reshape(o, gg * p), ak_im[:, chunk].reshape(o, gg * p)
    sc_a, sc_b = [], []
    for _ in range(steps):
        sc_a.append(jnp.concatenate([cr, cr], axis=-1))
        sc_b.append(jnp.concatenate([-ci, ci], axis=-1))
        cr, ci = cr * cr - ci * ci, 2.0 * cr * ci
    return m_intra.astype(BF16), m_in, m_out, jnp.stack(sc_a, axis=1), jnp.stack(sc_b, axis=1)


def _s5_kernel(u_ref, mi_ref, min_ref, mout_ref, sa_ref, sb_ref, y_ref, *, n_chunks, steps):
    chunk = S5_CHUNK
    n = u_ref.shape[0] // chunk
    u = jnp.concatenate([u_ref[pl.ds(i, n, stride=chunk), :] for i in range(chunk)], axis=1).astype(BF16)
    y = jnp.dot(u, mi_ref[...], preferred_element_type=F32)
    x = jnp.dot(u, min_ref[...], preferred_element_type=F32)
    half = x.shape[1] // 2
    c = lax.rem(lax.broadcasted_iota(jnp.int32, x.shape, 0), n_chunks)
    sa = sa_ref[...]
    sb = sb_ref[...]
    for k in range(steps):
        sh = 1 << k
        xs = jnp.where(c >= sh, pltpu.roll(x, sh, axis=0), 0.0)
        x = x + xs * sa[k:k + 1, :] + pltpu.roll(xs, half, axis=1) * sb[k:k + 1, :]
    s_in = jnp.where(c >= 1, pltpu.roll(x, 1, axis=0), 0.0)
    y = y + jnp.dot(s_in.astype(BF16), mout_ref[...], preferred_element_type=F32)
    for i in range(chunk):
        y_ref[pl.ds(i, n, stride=chunk), :] = y[:, i * LANES:(i + 1) * LANES]


def _s5(u, m_intra, m_in, m_out, sc_a, sc_b, *, n_chunks):
    t, d_ssm = u.shape
    o, lh, st = m_in.shape
    steps = sc_a.shape[1]
    mat = lambda i: (i, 0, 0)
    return pl.pallas_call(
        functools.partial(_s5_kernel, n_chunks=n_chunks, steps=steps),
        grid=(o,),
        in_specs=[
            pl.BlockSpec((t, LANES), lambda i: (0, i)),
            pl.BlockSpec((None, lh, lh), mat), pl.BlockSpec((None, lh, st), mat), pl.BlockSpec((None, st, lh), mat),
            pl.BlockSpec((None, steps, st), mat), pl.BlockSpec((None, steps, st), mat),
        ],
        out_specs=pl.BlockSpec((t, LANES), lambda i: (0, i)),
        out_shape=jax.ShapeDtypeStruct((t, d_ssm), F32),
        compiler_params=_params(("parallel",)),
        name="s5_chunked",
    )(u, m_intra, m_in, m_out, sc_a, sc_b)


def _postmix_kernel(oa_ref, ys_ref, x_ref, wglu_ref, woa_ref, wob_ref, ga_ref, gs_ref, gf_ref,
                    wr_ref, rb_ref, h1_ref, idx_ref, gate_ref, loc_ref, cnt_ref, run_ref, *, n_experts):
    step = pl.program_id(0)

    @pl.when(step == 0)
    def _():
        run_ref[...] = jnp.zeros(run_ref.shape, F32)

    tm = x_ref.shape[0]
    n_groups = POSTMIX_ROW_GROUPS if tm % (8 * POSTMIX_ROW_GROUPS) == 0 else 1
    rows = [pl.ds(g * (tm // n_groups), tm // n_groups) for g in range(n_groups)]

    def dot(a, w_ref):
        return jnp.dot(a, w_ref[...], preferred_element_type=F32)

    ys = [ys_ref[r, :] for r in rows]
    ys = [0.5 * y * (1.0 + jnp.tanh(math.sqrt(2.0 / math.pi) * (y + 0.044715 * (y * y * y)))) for y in ys]
    glu = [dot(y.astype(BF16), wglu_ref) for y in ys]
    ys = [y * _sigmoid(g) for y, g in zip(ys, glu)]
    nas = [_rms(oa_ref[r, :].astype(F32), ga_ref[...]).astype(BF16) for r in rows]
    nss = [_rms(y, gs_ref[...]).astype(BF16) for y in ys]
    h1s = [x_ref[r, :] + dot(na, woa_ref) + dot(ns, wob_ref) for r, na, ns in zip(rows, nas, nss)]
    for r, h1 in zip(rows, h1s):
        h1_ref[r, :] = h1

    ms = [_rms(h1, gf_ref[...]) for h1 in h1s]
    m_his = [m.astype(BF16) for m in ms]
    m_los = [(m - m_hi.astype(F32)).astype(BF16) for m, m_hi in zip(ms, m_his)]
    r_his = [dot(m_hi, wr_ref) for m_hi in m_his]
    r_los = [dot(m_lo, wr_ref) for m_lo in m_los]
    logits = jnp.concatenate([r_hi[:, :n_experts] + r_hi[:, n_experts:] + r_lo[:, :n_experts]
                              for r_hi, r_lo in zip(r_his, r_los)], axis=0)
    scores = _sigmoid(logits)
    vals = scores + rb_ref[...]
    lane = lax.broadcasted_iota(jnp.int32, vals.shape, 1).astype(F32)
    picks, firsts, raws = [], [], []
    sel = jnp.zeros(vals.shape, F32)
    for _ in range(TOP_K):
        mx = jnp.max(vals, axis=1, keepdims=True)
        first = jnp.min(jnp.where(vals == mx, lane, float(n_experts)), axis=1, keepdims=True)
        pick = lane == first
        picks.append(pick)
        firsts.append(first)
        raws.append(jnp.sum(jnp.where(pick, scores, 0.0), axis=1, keepdims=True))
        vals = jnp.where(pick, -jnp.inf, vals)
        sel = jnp.where(pick, 1.0, sel)
    denom = raws[0]
    for r in raws[1:]:
        denom = denom + r

    r_i = lax.broadcasted_iota(jnp.int32, (tm, tm), 0)
    c_i = lax.broadcasted_iota(jnp.int32, (tm, tm), 1)
    before = jnp.where(c_i < r_i, 1.0, 0.0).astype(BF16)
    rank = jnp.dot(before, sel.astype(BF16), preferred_element_type=F32) + run_ref[...]
    run_ref[...] = run_ref[...] + jnp.sum(sel, axis=0, keepdims=True)
    cnt_ref[...] = run_ref[...]

    slot = lax.broadcasted_iota(jnp.int32, (tm, LANES), 1)
    idx_w = jnp.zeros((tm, LANES), F32)
    loc_w = jnp.zeros((tm, LANES), F32)
    gate_w = jnp.zeros((tm, LANES), F32)
    for k in range(TOP_K):
        loc = jnp.sum(jnp.where(picks[k], rank, 0.0), axis=1, keepdims=True)
        idx_w = jnp.where(slot == k, firsts[k], idx_w)
        loc_w = jnp.where(slot == k, loc, loc_w)
        gate_w = jnp.where(slot == k, raws[k] / denom * ROUTED_SCALE, gate_w)
    idx_ref[...] = idx_w.T[:IDX_ROWS].astype(jnp.int32)
    loc_ref[...] = loc_w.T[:IDX_ROWS].astype(jnp.int32)
    gate_ref[...] = gate_w[:, :IDX_ROWS]


def _postmix(oa, ys, x2, wglu, woa, wob, ga, gs, gf, wr, rb, *, tm):
    t, d = x2.shape
    da = oa.shape[1]
    dsm = ys.shape[1]
    e = rb.shape[1]
    row = lambda i: (i, 0)
    col = lambda i: (0, i)
    fix = lambda i: (0, 0)
    return pl.pallas_call(
        functools.partial(_postmix_kernel, n_experts=e),
        grid=(t // tm,),
        in_specs=[
            pl.BlockSpec((tm, da), row), pl.BlockSpec((tm, dsm), row), pl.BlockSpec((tm, d), row),
            pl.BlockSpec((dsm, dsm), fix), pl.BlockSpec((da, d), fix), pl.BlockSpec((dsm, d), fix),
            pl.BlockSpec((1, da), fix), pl.BlockSpec((1, dsm), fix), pl.BlockSpec((1, d), fix),
            pl.BlockSpec((d, 2 * e), fix), pl.BlockSpec((1, e), fix),
        ],
        out_specs=[
            pl.BlockSpec((tm, d), row), pl.BlockSpec((IDX_ROWS, tm), col), pl.BlockSpec((tm, IDX_ROWS), row),
            pl.BlockSpec((IDX_ROWS, tm), col), pl.BlockSpec((1, e), fix),
        ],
        out_shape=[
            jax.ShapeDtypeStruct((t, d), F32), jax.ShapeDtypeStruct((IDX_ROWS, t), jnp.int32),
            jax.ShapeDtypeStruct((t, IDX_ROWS), F32), jax.ShapeDtypeStruct((IDX_ROWS, t), jnp.int32),
            jax.ShapeDtypeStruct((1, e), F32),
        ],
        scratch_shapes=[pltpu.VMEM((1, e), F32)],
        compiler_params=_params(("arbitrary",)),
        name="postmix_router",
    )(oa, ys, x2, wglu, woa, wob, ga, gs, gf, wr, rb)


def _rows_to_tiles(ref, packed, sub):
    rows = packed.shape[0]
    for s in range(sub):
        ref[pl.ds(s, rows, stride=sub), :] = packed[:, s * LANES:(s + 1) * LANES]


def _tiles_to_rows(ref, rows, sub):
    return jnp.concatenate([ref[pl.ds(s, rows, stride=sub), :] for s in range(sub)], axis=1)


def _tile_rows(ref, r, sub):
    start = r * sub if isinstance(r, int) else pl.multiple_of(r * sub, sub)
    return ref.at[pl.ds(start, sub)]


def _shared_kernel(dest_ref, tab_init_hbm, h1_ref, gf_ref, wg_ref, wu_ref, wd_ref, h1s_ref, mp_ref, tab_ref,
                   sem, *, sub, n_tokens):
    tm = h1_ref.shape[0]
    base = pl.program_id(0) * tm

    @pl.when(pl.program_id(0) == 0)
    def _():
        init = pltpu.make_async_copy(tab_init_hbm, tab_ref, sem)
        init.start()
        init.wait()

    for t in range(tm):
        for k in range(TOP_K):
            tab_ref[dest_ref[k * tm + t]] = k * n_tokens + base + t
    h1 = h1_ref[...]
    m = _rms(h1, gf_ref[...])
    _rows_to_tiles(mp_ref, _pack_halves(m), sub)
    mb = m.astype(BF16)
    a = jnp.dot(mb, wg_ref[...], preferred_element_type=F32)
    b = jnp.dot(mb, wu_ref[...], preferred_element_type=F32)
    hid = (a * _sigmoid(a) * b).astype(BF16)
    h1s_ref[...] = h1 + jnp.dot(hid, wd_ref[...], preferred_element_type=F32)


def _shared(dest, table0, h1, gf, wg, wu, wd, *, tm):
    t, d = h1.shape
    f = wg.shape[1]
    sub = d // 2 // LANES
    row = lambda i: (i, 0)
    fix = lambda i: (0, 0)
    return pl.pallas_call(
        functools.partial(_shared_kernel, sub=sub, n_tokens=t),
        grid=(t // tm,),
        in_specs=[
            pl.BlockSpec((IDX_ROWS * tm,), lambda i: (i,), memory_space=pltpu.SMEM),
            pl.BlockSpec(memory_space=pl.ANY),
            pl.BlockSpec((tm, d), row), pl.BlockSpec((1, d), fix),
            pl.BlockSpec((d, f), fix), pl.BlockSpec((d, f), fix), pl.BlockSpec((f, d), fix),
        ],
        out_specs=[pl.BlockSpec((tm, d), row), pl.BlockSpec((tm * sub, LANES), row),
                   pl.BlockSpec(memory_space=pltpu.SMEM)],
        out_shape=[jax.ShapeDtypeStruct((t, d), F32), jax.ShapeDtypeStruct((t * sub, LANES), jnp.int32),
                   jax.ShapeDtypeStruct(table0.shape, jnp.int32)],
        scratch_shapes=[pltpu.SemaphoreType.DMA(())],
        compiler_params=_params(("arbitrary",)),
        name="shared_table",
    )(dest, table0, h1, gf, wg, wu, wd)


def _experts_kernel(be_ref, first_ref, nxt_ref, nu_ref, tab_ref, mp_hbm, wg_hbm, wu_hbm, wd_hbm, yb_hbm,
                    mp_ref, wgf_ref, wuf_ref, wdf_ref, wgb_ref, wub_ref, wdb_ref, x0_ref, x1_ref, y0_ref, y1_ref,
                    wsem, msem, ssem, *, sub, tm, n_tokens):
    i = pl.program_id(0)
    nu = nu_ref[0]
    xs = (x0_ref, x1_ref)
    ys = (y0_ref, y1_ref)
    spare = TOP_K * n_tokens

    def token_of(e):
        return e & (n_tokens - 1) if n_tokens & (n_tokens - 1) == 0 else lax.rem(e, n_tokens)

    def gather(blk, r, s):
        tok = token_of(tab_ref[blk * tm + r])
        dst = r * sub if isinstance(r, int) else pl.multiple_of(r * sub, sub)
        xs[s][pl.ds(dst, sub), :] = mp_ref[pl.ds(pl.multiple_of(tok * sub, sub), sub), :]

    def scatter(entry, r, s):
        return pltpu.make_async_copy(_tile_rows(ys[s], r, sub), _tile_rows(yb_hbm, entry, sub), ssem.at[s])

    def scatter_wait(s):
        pltpu.make_async_copy(ys[s], yb_hbm.at[pl.ds(0, tm * sub)], ssem.at[s]).wait()

    def weight_copies(e):
        return (pltpu.make_async_copy(wg_hbm.at[e], wgf_ref, wsem.at[0]),
                pltpu.make_async_copy(wu_hbm.at[e], wuf_ref, wsem.at[1]),
                pltpu.make_async_copy(wd_hbm.at[e], wdf_ref, wsem.at[2]))

    @pl.when(i == 0)
    def _():
        for c in weight_copies(be_ref[0]):
            c.start()
        resident = pltpu.make_async_copy(mp_hbm, mp_ref, msem)
        resident.start()
        resident.wait()

        def first_rows(r, carry):
            gather(0, r, 0)
            return carry

        lax.fori_loop(0, tm, first_rows, 0)
        y1_ref[...] = jnp.zeros(y1_ref.shape, y1_ref.dtype)

    @pl.when((i < nu) & (first_ref[jnp.minimum(i, first_ref.shape[0] - 1)] != 0))
    def _():
        for c in weight_copies(be_ref[i]):
            c.wait()
        wgb_ref[...] = wgf_ref[...].astype(BF16)
        wub_ref[...] = wuf_ref[...].astype(BF16)
        wdb_ref[...] = wdf_ref[...].astype(BF16)

        @pl.when(nxt_ref[i] >= 0)
        def _():
            for c in weight_copies(nxt_ref[i]):
                c.start()

    def block(p):
        @pl.when(i >= 1)
        def _():
            scatter_wait(p)

        nb = jnp.minimum(i + 1, nu - 1)
        pb = jnp.maximum(i - 1, 0)
        for r in range(tm):
            gather(nb, r, 1 - p)
        for r in range(tm):
            entry = jnp.where(i >= 1, tab_ref[pb * tm + r], spare + r)
            scatter(entry, r, 1 - p).start()
        lo, hi = _unpack_halves(_tiles_to_rows(xs[p], tm, sub))
        lo = lo.astype(BF16)
        hi = hi.astype(BF16)
        n = lo.shape[1]
        a = (jnp.dot(lo, wgb_ref[:n, :], preferred_element_type=F32)
             + jnp.dot(hi, wgb_ref[n:, :], preferred_element_type=F32))
        b = (jnp.dot(lo, wub_ref[:n, :], preferred_element_type=F32)
             + jnp.dot(hi, wub_ref[n:, :], preferred_element_type=F32))
        hid = (a * _sigmoid(a) * b).astype(BF16)
        _rows_to_tiles(ys[p], _pack_halves(jnp.dot(hid, wdb_ref[...], preferred_element_type=F32)), sub)

    def flush(p):
        def last_rows(r, carry):
            scatter(tab_ref[(nu - 1) * tm + r], r, 1 - p).start()
            return carry

        scatter_wait(p)
        lax.fori_loop(0, tm, last_rows, 0)
        scatter_wait(1 - p)

    for p in range(2):
        @pl.when((i < nu) & (lax.rem(i, 2) == p))
        def _(p=p):
            block(p)

        @pl.when((i == nu) & (lax.rem(i, 2) == p))
        def _(p=p):
            flush(p)


def _experts(blk_e, first, nxt, n_used, table, mp, wg, wu, wd, *, tm):
    _, d, f = wg.shape
    sub = d // 2 // LANES
    n_tokens = mp.shape[0] // sub
    n_blocks = table.shape[0] // tm
    return pl.pallas_call(
        functools.partial(_experts_kernel, sub=sub, tm=tm, n_tokens=n_tokens),
        grid_spec=pltpu.PrefetchScalarGridSpec(
            num_scalar_prefetch=5,
            grid=(n_blocks + 1,),
            in_specs=[pl.BlockSpec(memory_space=pl.ANY)] * 4,
            out_specs=pl.BlockSpec(memory_space=pl.ANY),
            scratch_shapes=[pltpu.VMEM(mp.shape, jnp.int32),
                            pltpu.VMEM((d, f), F32), pltpu.VMEM((d, f), F32), pltpu.VMEM((f, d), F32),
                            pltpu.VMEM((d, f), BF16), pltpu.VMEM((d, f), BF16), pltpu.VMEM((f, d), BF16),
                            pltpu.VMEM((tm * sub, LANES), jnp.int32), pltpu.VMEM((tm * sub, LANES), jnp.int32),
                            pltpu.VMEM((tm * sub, LANES), jnp.int32), pltpu.VMEM((tm * sub, LANES), jnp.int32),
                            pltpu.SemaphoreType.DMA((3,)), pltpu.SemaphoreType.DMA(()),
                            pltpu.SemaphoreType.DMA((2,))],
        ),
        out_shape=jax.ShapeDtypeStruct(((TOP_K * n_tokens + tm) * sub, LANES), jnp.int32),
        compiler_params=pltpu.CompilerParams(dimension_semantics=("arbitrary",),
                                             vmem_limit_bytes=EXPERTS_VMEM_LIMIT),
        name="routed_experts",
    )(blk_e, first, nxt, n_used, table, mp, wg, wu, wd)


def _final_kernel(h_ref, gate_ref, p_ref, gp_ref, wgate_ref, wproj_ref, *rest, sub):
    y_refs, o_ref = rest[:TOP_K], rest[TOP_K]
    tm = h_ref.shape[0]
    pp = jnp.dot(p_ref[...].astype(BF16), wproj_ref[...], preferred_element_type=F32)
    gate = gate_ref[...]
    h = h_ref[...]
    n = h.shape[1] // 2
    acc_lo = h[:, :n]
    acc_hi = h[:, n:]
    for k in range(TOP_K):
        lo, hi = _unpack_halves(_tiles_to_rows(y_refs[k], tm, sub))
        acc_lo = acc_lo + gate[:, k:k + 1] * lo
        acc_hi = acc_hi + gate[:, k:k + 1] * hi
    h2 = jnp.concatenate([acc_lo, acc_hi], axis=1)
    nrm = _rms(h2, gp_ref[...]).astype(BF16)
    o_ref[...] = h2 + _sigmoid(jnp.dot(nrm, wgate_ref[...], preferred_element_type=F32)) * pp


def _final(h1s, gate, p2, gp, wgate, wproj, yb, *, tm):
    t, d = h1s.shape
    dp = p2.shape[1]
    sub = d // 2 // LANES
    nt = t // tm
    row = lambda i: (i, 0)
    fix = lambda i: (0, 0)
    y_specs = [pl.BlockSpec((tm * sub, LANES), lambda i, k=k: (k * nt + i, 0)) for k in range(TOP_K)]
    return pl.pallas_call(
        functools.partial(_final_kernel, sub=sub),
        grid=(nt,),
        in_specs=[
            pl.BlockSpec((tm, d), row), pl.BlockSpec((tm, IDX_ROWS), row), pl.BlockSpec((tm, dp), row),
            pl.BlockSpec((1, d), fix), pl.BlockSpec((d, d), fix), pl.BlockSpec((dp, d), fix),
        ] + y_specs,
        out_specs=pl.BlockSpec((tm, d), row),
        out_shape=jax.ShapeDtypeStruct((t, d), F32),
        compiler_params=_params(("parallel",)),
        name="combine_ple",
    )(h1s, gate, p2, gp, wgate, wproj, *([yb] * TOP_K))


def _largest_tile(n, cap):
    t = min(n, cap)
    while n % t:
        t //= 2
    return t


def _layer(h, p_l, norm_mix, w_in, q_norm, k_norm, lam_re, lam_im, log_dt, b_re, b_im, c_re, c_im, d_skip,
           w_glu, attn_out_norm, ssm_out_norm, w_out, norm_ffn, w_router, router_bias, w_exp_gate,
           w_exp_up, w_exp_down, w_sh_gate, w_sh_up, w_sh_down, norm_ple, w_ple_gate, w_ple_proj):
    bsz, seq, d = h.shape
    t = bsz * seq
    head_dim = q_norm.shape[-1]
    d_attn = attn_out_norm.shape[-1]
    n_heads = d_attn // head_dim
    n_experts = w_router.shape[-1]
    x2 = h.reshape(t, d)

    tn = d_attn if d_attn % 128 == 0 else head_dim
    scale = 1.0 / math.sqrt(head_dim)
    hg = jnp.concatenate([jnp.tile(q_norm * scale, n_heads), jnp.tile(k_norm, n_heads),
                          jnp.ones((w_in.shape[1] - 2 * d_attn,), F32)])[None, :]
    z, u = _inproj(x2, norm_mix[None, :], w_in.astype(BF16), hg, d_qkv=3 * d_attn,
                   n_norm_tiles=2 * d_attn // tn, head_dim=head_dim, tm=_largest_tile(t, 1024), tn=tn)

    o_attn = _attention(z, batch=bsz, seq=seq, n_heads=n_heads, head_dim=head_dim, blk=_largest_tile(seq, 256))

    n_chunks = seq // S5_CHUNK
    mats = _s5_prepare(lam_re, lam_im, log_dt, b_re, b_im, c_re, c_im, d_skip, n_chunks)
    y_ssm = _s5(u, *mats, n_chunks=n_chunks)

    tm = _largest_tile(t, 256)
    w_out_b = w_out.astype(BF16)
    wr_hi = w_router.astype(BF16)
    wr_lo = (w_router - wr_hi.astype(F32)).astype(BF16)
    h1, idx, gate, loc, counts = _postmix(
        o_attn, y_ssm, x2, w_glu.astype(BF16), w_out_b[:d_attn], w_out_b[d_attn:],
        attn_out_norm[None, :], ssm_out_norm[None, :], norm_ffn[None, :],
        jnp.concatenate([wr_hi, wr_lo], axis=1), router_bias[None, :], tm=_largest_tile(t, 512))

    rb = _largest_tile(t, 256)
    n_pad = t * TOP_K + n_experts * rb
    n_blocks = n_pad // rb
    cnt = counts[0].astype(jnp.int32)
    pcnt = (cnt + rb - 1) // rb * rb
    pend = jnp.cumsum(pcnt)
    pstart = pend - pcnt
    onehot = idx[:, :, None] == jnp.arange(n_experts, dtype=jnp.int32)[None, None, :]
    dest = jnp.sum(jnp.where(onehot, pstart[None, None, :], 0), axis=-1) + loc
    dest = dest.reshape(IDX_ROWS, t // tm, tm).transpose(1, 0, 2).reshape(-1)
    n_used = (pend[-1] // rb).astype(jnp.int32)
    blk_ids = jnp.arange(n_blocks, dtype=jnp.int32)
    starts = (blk_ids * rb)[:, None]
    owner = (starts >= pstart[None, :]) & (starts < pend[None, :])
    blk_e = jnp.sum(jnp.where(owner, jnp.arange(n_experts, dtype=jnp.int32)[None, :], 0), axis=1)
    is_first = jnp.any(owner & (starts == pstart[None, :]), axis=1).astype(jnp.int32)
    first_pos = jnp.where(is_first > 0, blk_ids, n_blocks)
    nxt_pos = jnp.concatenate([lax.cummin(first_pos, axis=0, reverse=True)[1:],
                               jnp.full((1,), n_blocks, jnp.int32)])
    nxt_hit = nxt_pos[:, None] == blk_ids[None, :]
    nxt = jnp.where(nxt_pos < n_blocks, jnp.sum(jnp.where(nxt_hit, blk_e[None, :], 0), axis=1), -1)
    last_e = jnp.sum(jnp.where(blk_ids == n_used - 1, blk_e, 0))
    blk_e = jnp.where(blk_ids < n_used, blk_e, last_e).astype(jnp.int32)

    table0 = TOP_K * t + jnp.arange(n_pad, dtype=jnp.int32) % rb
    h1s, mp, table = _shared(dest, table0, h1, norm_ffn[None, :], w_sh_gate.astype(BF16), w_sh_up.astype(BF16),
                             w_sh_down.astype(BF16), tm=tm)
    yb = _experts(blk_e, is_first, nxt.astype(jnp.int32), n_used[None], table, mp, w_exp_gate, w_exp_up,
                  w_exp_down, tm=rb)
    out = _final(h1s, gate, p_l.reshape(t, -1), norm_ple[None, :], w_ple_gate.astype(BF16),
                 w_ple_proj.astype(BF16), yb, tm=tm)
    return out.reshape(bsz, seq, d)


def kernel(x, p, norm_mix, w_in, q_norm, k_norm, ssm_lam_re, ssm_lam_im, ssm_log_dt, ssm_b_re, ssm_b_im,
           ssm_c_re, ssm_c_im, ssm_d, w_glu, attn_out_norm, ssm_out_norm, w_out, norm_ffn, w_router,
           router_bias, w_exp_gate, w_exp_up, w_exp_down, w_sh_gate, w_sh_up, w_sh_down, norm_ple,
           w_ple_gate, w_ple_proj):
    h = x
    for i in range(p.shape[0]):
        h = _layer(h, p[i], norm_mix[i], w_in[i], q_norm[i], k_norm[i], ssm_lam_re[i], ssm_lam_im[i],
                   ssm_log_dt[i], ssm_b_re[i], ssm_b_im[i], ssm_c_re[i], ssm_c_im[i], ssm_d[i], w_glu[i],
                   attn_out_norm[i], ssm_out_norm[i], w_out[i], norm_ffn[i], w_router[i], router_bias[i],
                   w_exp_gate[i], w_exp_up[i], w_exp_down[i], w_sh_gate[i], w_sh_up[i], w_sh_down[i],
                   norm_ple[i], w_ple_gate[i], w_ple_proj[i])
    return h
```

```python
import functools
import math

import jax
import jax.numpy as jnp
from jax import lax
from jax.experimental import pallas as pl
from jax.experimental.pallas import tpu as pltpu

NORM_EPS = 1e-6
TOP_K = 6
ROUTED_SCALE = 2.5
IDX_ROWS = 8
LANES = 128
S5_CHUNK = 8
POSTMIX_ROW_GROUPS = 2
S5_ROW_GROUPS = 2
COMBINE_ROW_GROUPS = 2
VMEM_LIMIT = 56 * 1024 * 1024
EXPERTS_VMEM_LIMIT = 62 * 1024 * 1024

F32 = jnp.float32
BF16 = jnp.bfloat16
HIGH_HALF = -65536


def _rms(x, g):
    return x * lax.rsqrt(jnp.mean(x * x, axis=-1, keepdims=True) + NORM_EPS) * g


def _sigmoid(x):
    return 1.0 / (1.0 + jnp.exp(-x))


def _params(sem):
    return pltpu.CompilerParams(dimension_semantics=sem, vmem_limit_bytes=VMEM_LIMIT)


def _pack_halves(x):
    n = x.shape[1] // 2
    lo = lax.bitcast_convert_type(x[:, :n].astype(BF16).astype(F32), jnp.int32)
    hi = lax.bitcast_convert_type(x[:, n:].astype(BF16).astype(F32), jnp.int32)
    return (hi & HIGH_HALF) | lax.shift_right_logical(lo, 16)


def _unpack_halves(p):
    lo = lax.bitcast_convert_type(lax.shift_left(p, 16), F32)
    hi = lax.bitcast_convert_type(p & HIGH_HALF, F32)
    return lo, hi


def _inproj_kernel(x_ref, g_ref, w_ref, hg_ref, o_ref, u_ref, xn_ref, *, n_norm_tiles, n_qkv_tiles, head_dim):
    j = pl.program_id(1)

    @pl.when(j == 0)
    def _():
        xn_ref[...] = _rms(x_ref[...], g_ref[...]).astype(BF16)

    acc = jnp.dot(xn_ref[...], w_ref[...], preferred_element_type=F32)
    tn = acc.shape[1]

    @pl.when(j < n_norm_tiles)
    def _():
        hg = hg_ref[...]
        for h in range(tn // head_dim):
            sl = slice(h * head_dim, (h + 1) * head_dim)
            o_ref[:, sl] = _rms(acc[:, sl], hg[:, sl]).astype(o_ref.dtype)

    @pl.when((j >= n_norm_tiles) & (j < n_qkv_tiles))
    def _():
        o_ref[...] = acc.astype(o_ref.dtype)

    @pl.when(j >= n_qkv_tiles)
    def _():
        u_ref[...] = acc


def _inproj(x2, g, w, hg, *, d_qkv, n_norm_tiles, head_dim, tm, tn):
    t, d = x2.shape
    n = w.shape[1]
    nq = d_qkv // tn
    return pl.pallas_call(
        functools.partial(_inproj_kernel, n_norm_tiles=n_norm_tiles, n_qkv_tiles=nq, head_dim=head_dim),
        grid=(t // tm, n // tn),
        in_specs=[
            pl.BlockSpec((tm, d), lambda i, j: (i, 0)),
            pl.BlockSpec((1, d), lambda i, j: (0, 0)),
            pl.BlockSpec((d, tn), lambda i, j: (0, j)),
            pl.BlockSpec((1, tn), lambda i, j: (0, j)),
        ],
        out_specs=[
            pl.BlockSpec((tm, tn), lambda i, j: (i, jnp.minimum(j, nq - 1))),
            pl.BlockSpec((tm, tn), lambda i, j: (i, jnp.maximum(j - nq, 0))),
        ],
        out_shape=[jax.ShapeDtypeStruct((t, d_qkv), BF16), jax.ShapeDtypeStruct((t, n - d_qkv), F32)],
        scratch_shapes=[pltpu.VMEM((tm, d), BF16)],
        compiler_params=_params(("parallel", "arbitrary")),
        name="inproj",
    )(x2, g, w, hg)


SKIP_AFTER = 110.0


def _attn_kernel(q_ref, k_ref, v_ref, o_ref, *, blk, hd, heads):
    i = pl.program_id(2)
    row = lax.broadcasted_iota(jnp.int32, (blk, blk), 0)
    col = lax.broadcasted_iota(jnp.int32, (blk, blk), 1)
    causal = col < row
    later = jnp.where(row > col, 1.0, 0.0).astype(BF16)
    lanes = [slice(h * hd, (h + 1) * hd) for h in range(heads)]
    qs = [q_ref[:, sl] for sl in lanes]

    def step(kb, accs, runs, diag):
        start = pl.multiple_of(kb * blk, blk)
        zs, sps, inners, new_accs, new_runs = [], [], [], [], []
        for h, sl in enumerate(lanes):
            k = k_ref[pl.ds(start, blk), sl]
            zs.append(lax.dot_general(qs[h], k, (((1,), (1,)), ((), ())), preferred_element_type=F32))
        for z in zs:
            sp = jnp.maximum(z, 0.0) + jnp.log(1.0 + jnp.exp(-jnp.abs(z)))
            sps.append(jnp.where(causal, sp, 0.0) if diag else sp)
        for sp in sps:
            hi = sp.astype(BF16)
            lo = (sp - hi.astype(F32)).astype(BF16)
            inners.append(jnp.dot(hi, later, preferred_element_type=F32)
                          + jnp.dot(lo, later, preferred_element_type=F32))
        for h, sl in enumerate(lanes):
            w = jnp.exp(zs[h] - sps[h] - (inners[h] + runs[h]))
            if diag:
                w = jnp.where(causal, w, 0.0)
            v = v_ref[pl.ds(start, blk), sl]
            new_accs.append(accs[h] + jnp.dot(w.astype(BF16), v, preferred_element_type=F32))
            new_runs.append(runs[h] + jnp.sum(sps[h], axis=-1, keepdims=True))
        return tuple(new_accs), tuple(new_runs)

    def keep_going(runs):
        low = runs[0]
        for r in runs[1:]:
            low = jnp.minimum(low, r)
        return (jnp.min(low) < SKIP_AFTER).astype(jnp.int32)

    accs = tuple(jnp.zeros((blk, hd), F32) for _ in lanes)
    runs = tuple(jnp.zeros((blk, 1), F32) for _ in lanes)
    accs, runs = step(i, accs, runs, True)

    def cond(c):
        n, go, _, _ = c
        return (n < i) & (go > 0)

    def body(c):
        n, _, accs, runs = c
        accs, runs = step(i - 1 - n, accs, runs, False)
        return n + 1, keep_going(runs), accs, runs

    _, _, accs, _ = lax.while_loop(cond, body, (jnp.int32(0), keep_going(runs), accs, runs))
    for h, sl in enumerate(lanes):
        o_ref[:, sl] = accs[h].astype(o_ref.dtype)


def _attention(z, *, batch, seq, n_heads, head_dim, blk):
    nq = seq // blk
    heads = next(c for c in (4, 2, 1) if n_heads % c == 0)
    hw = heads * head_dim
    ng = n_heads // heads
    return pl.pallas_call(
        functools.partial(_attn_kernel, blk=blk, hd=head_dim, heads=heads),
        grid=(batch, ng, nq),
        in_specs=[
            pl.BlockSpec((blk, hw), lambda b, h, i: (b * nq + i, h)),
            pl.BlockSpec((seq, hw), lambda b, h, i: (b, ng + h)),
            pl.BlockSpec((seq, hw), lambda b, h, i: (b, 2 * ng + h)),
        ],
        out_specs=pl.BlockSpec((blk, hw), lambda b, h, i: (b * nq + i, h)),
        out_shape=jax.ShapeDtypeStruct((batch * seq, n_heads * head_dim), BF16),
        compiler_params=_params(("parallel", "parallel", "arbitrary")),
        name="sb_attention",
    )(z, z, z)


def _s5_prepare(lam_re, lam_im, log_dt, b_re, b_im, c_re, c_im, d_skip, n_chunks):
    hp = lax.Precision.HIGHEST
    chunk = S5_CHUNK
    g, p = lam_re.shape
    h = b_re.shape[-1]
    gg = LANES // h
    o = g // gg
    dt = jnp.exp(log_dt)[:, None]
    ks = jnp.arange(chunk + 1, dtype=F32)[None, :, None]
    mag = jnp.exp(lam_re[:, None, :] * dt[:, None, :] * ks)
    ang = lam_im[:, None, :] * dt[:, None, :] * ks
    ak_re = mag * jnp.cos(ang)
    ak_im = mag * jnp.sin(ang)
    nr = ak_re[:, 1] - 1.0
    ni = ak_im[:, 1]
    den = lam_re * lam_re + lam_im * lam_im
    coef_re = ((nr * lam_re + ni * lam_im) / den)[..., None]
    coef_im = ((ni * lam_re - nr * lam_im) / den)[..., None]
    bbar_re = coef_re * b_re - coef_im * b_im
    bbar_im = coef_re * b_im + coef_im * b_re
    ca_re = c_re[:, None] * ak_re[:, :, None, :] - c_im[:, None] * ak_im[:, :, None, :]
    ca_im = c_re[:, None] * ak_im[:, :, None, :] + c_im[:, None] * ak_re[:, :, None, :]

    cam_re = ca_re[:, :chunk].transpose(0, 3, 1, 2).reshape(g, p, chunk * h)
    cam_im = ca_im[:, :chunk].transpose(0, 3, 1, 2).reshape(g, p, chunk * h)
    kt = (jnp.einsum("gph,gpm->ghm", bbar_re, cam_re, precision=hp)
          - jnp.einsum("gph,gpm->ghm", bbar_im, cam_im, precision=hp))
    def same_group(n_rows, row_block, n_cols, col_block):
        r = (jnp.arange(n_rows) // row_block) % gg
        c = (jnp.arange(n_cols) // col_block) % gg
        return (r[:, None] == c[None, :]).astype(F32)

    tile_h = jnp.tile(jnp.eye(h, dtype=F32), (1, gg))
    tile_p = jnp.tile(jnp.eye(p, dtype=F32), (1, gg))
    kt = kt.reshape(o, gg, h, chunk, h).transpose(0, 3, 1, 2, 4).reshape(o, chunk, LANES, h)
    bd = jnp.einsum("okrh,hc->okrc", kt, tile_h, precision=hp) * same_group(LANES, h, LANES, h)
    dvec = d_skip.reshape(o, LANES)
    lags = [bd[:, k] for k in range(chunk)]
    lags[0] = lags[0] + dvec[:, :, None] * jnp.eye(LANES, dtype=F32)[None]
    zero_blk = jnp.zeros_like(lags[0])
    m_intra = jnp.concatenate(
        [jnp.concatenate([lags[j - i] if j >= i else zero_blk for j in range(chunk)], axis=2)
         for i in range(chunk)], axis=1)

    def expand_in(ak_part_a, ak_part_b, b_a, b_b, sign):
        rk_a = jnp.stack([ak_part_a[:, chunk - 1 - i] for i in range(chunk)], axis=1)
        rk_b = jnp.stack([ak_part_b[:, chunk - 1 - i] for i in range(chunk)], axis=1)
        val = (rk_a[:, :, None, :] * b_a.transpose(0, 2, 1)[:, None]
               + sign * rk_b[:, :, None, :] * b_b.transpose(0, 2, 1)[:, None])
        val = val.reshape(o, gg, chunk, h, p).transpose(0, 2, 1, 3, 4).reshape(o, chunk * LANES, p)
        wide = jnp.einsum("orp,pc->orc", val.astype(BF16), tile_p.astype(BF16), preferred_element_type=BF16)
        return wide * same_group(chunk * LANES, h, gg * p, p).astype(BF16)

    m_in = jnp.concatenate([expand_in(ak_re, ak_im, bbar_re, bbar_im, -1.0),
                            expand_in(ak_re, ak_im, bbar_im, bbar_re, 1.0)], axis=-1)

    def expand_out(ca):
        val = ca[:, 1:].reshape(o, gg, chunk, h, p).transpose(0, 4, 2, 1, 3).reshape(o, p, chunk * LANES)
        wide = jnp.einsum("pc,opr->ocr", tile_p.astype(BF16), val.astype(BF16), preferred_element_type=BF16)
        return wide * same_group(gg * p, p, chunk * LANES, h).astype(BF16)

    m_out = jnp.concatenate([expand_out(ca_re), -expand_out(ca_im)], axis=1)

    steps = max(1, int(math.ceil(math.log2(max(n_chunks, 2)))))
    cr, ci = ak_re[:, chunk].reshape(o, gg * p), ak_im[:, chunk].reshape(o, gg * p)
    sc_a, sc_b = [], []
    for _ in range(steps):
        sc_a.append(jnp.concatenate([cr, cr], axis=-1))
        sc_b.append(jnp.concatenate([-ci, ci], axis=-1))
        cr, ci = cr * cr - ci * ci, 2.0 * cr * ci
    return m_intra.astype(BF16), m_in, m_out, jnp.stack(sc_a, axis=1), jnp.stack(sc_b, axis=1)


def _s5_kernel(u_ref, mi_ref, min_ref, mout_ref, sa_ref, sb_ref, y_ref, *, n_chunks, steps):
    chunk = S5_CHUNK
    n_all = u_ref.shape[0] // chunk
    n_grp = S5_ROW_GROUPS if (n_all // n_chunks) % S5_ROW_GROUPS == 0 else 1
    n = n_all // n_grp
    us = [jnp.concatenate([u_ref[pl.ds(g * n * chunk + i, n, stride=chunk), :] for i in range(chunk)],
                          axis=1).astype(BF16) for g in range(n_grp)]
    ys = [jnp.dot(u, mi_ref[...], preferred_element_type=F32) for u in us]
    xs = [jnp.dot(u, min_ref[...], preferred_element_type=F32) for u in us]
    half = xs[0].shape[1] // 2
    c = lax.rem(lax.broadcasted_iota(jnp.int32, xs[0].shape, 0), n_chunks)
    sa = sa_ref[...]
    sb = sb_ref[...]
    for k in range(steps):
        sh = 1 << k
        sh_xs = [jnp.where(c >= sh, pltpu.roll(x, sh, axis=0), 0.0) for x in xs]
        xs = [x + s * sa[k:k + 1, :] + pltpu.roll(s, half, axis=1) * sb[k:k + 1, :] for x, s in zip(xs, sh_xs)]
    s_ins = [jnp.where(c >= 1, pltpu.roll(x, 1, axis=0), 0.0) for x in xs]
    ys = [y + jnp.dot(s_in.astype(BF16), mout_ref[...], preferred_element_type=F32) for y, s_in in zip(ys, s_ins)]
    for g, y in enumerate(ys):
        for i in range(chunk):
            y_ref[pl.ds(g * n * chunk + i, n, stride=chunk), :] = y[:, i * LANES:(i + 1) * LANES]


def _s5(u, m_intra, m_in, m_out, sc_a, sc_b, *, n_chunks):
    t, d_ssm = u.shape
    o, lh, st = m_in.shape
    steps = sc_a.shape[1]
    mat = lambda i: (i, 0, 0)
    return pl.pallas_call(
        functools.partial(_s5_kernel, n_chunks=n_chunks, steps=steps),
        grid=(o,),
        in_specs=[
            pl.BlockSpec((t, LANES), lambda i: (0, i)),
            pl.BlockSpec((None, lh, lh), mat), pl.BlockSpec((None, lh, st), mat), pl.BlockSpec((None, st, lh), mat),
            pl.BlockSpec((None, steps, st), mat), pl.BlockSpec((None, steps, st), mat),
        ],
        out_specs=pl.BlockSpec((t, LANES), lambda i: (0, i)),
        out_shape=jax.ShapeDtypeStruct((t, d_ssm), F32),
        compiler_params=_params(("parallel",)),
        name="s5_chunked",
    )(u, m_intra, m_in, m_out, sc_a, sc_b)


def _postmix_kernel(oa_ref, ys_ref, x_ref, wglu_ref, woa_ref, wob_ref, ga_ref, gs_ref, gf_ref,
                    wr_ref, rb_ref, h1_ref, idx_ref, gate_ref, loc_ref, cnt_ref, run_ref, *, n_experts):
    step = pl.program_id(0)

    @pl.when(step == 0)
    def _():
        run_ref[...] = jnp.zeros(run_ref.shape, F32)

    tm = x_ref.shape[0]
    n_groups = POSTMIX_ROW_GROUPS if tm % (8 * POSTMIX_ROW_GROUPS) == 0 else 1
    rows = [pl.ds(g * (tm // n_groups), tm // n_groups) for g in range(n_groups)]

    def dot(a, w_ref):
        return jnp.dot(a, w_ref[...], preferred_element_type=F32)

    ys = [ys_ref[r, :] for r in rows]
    ys = [0.5 * y * (1.0 + jnp.tanh(math.sqrt(2.0 / math.pi) * (y + 0.044715 * (y * y * y)))) for y in ys]
    glu = [dot(y.astype(BF16), wglu_ref) for y in ys]
    ys = [y * _sigmoid(g) for y, g in zip(ys, glu)]
    nas = [_rms(oa_ref[r, :].astype(F32), ga_ref[...]).astype(BF16) for r in rows]
    nss = [_rms(y, gs_ref[...]).astype(BF16) for y in ys]
    h1s = [x_ref[r, :] + dot(na, woa_ref) + dot(ns, wob_ref) for r, na, ns in zip(rows, nas, nss)]
    for r, h1 in zip(rows, h1s):
        h1_ref[r, :] = h1

    ms = [_rms(h1, gf_ref[...]) for h1 in h1s]
    m_his = [m.astype(BF16) for m in ms]
    m_los = [(m - m_hi.astype(F32)).astype(BF16) for m, m_hi in zip(ms, m_his)]
    r_his = [dot(m_hi, wr_ref) for m_hi in m_his]
    r_los = [dot(m_lo, wr_ref) for m_lo in m_los]
    wide_t = jnp.concatenate([(r_hi + r_lo).T for r_hi, r_lo in zip(r_his, r_los)], axis=1)
    logits = wide_t[:n_experts] + wide_t[n_experts:]
    scores = _sigmoid(logits)
    vals = scores + rb_ref[...]
    eidx = lax.broadcasted_iota(jnp.int32, vals.shape, 0).astype(F32)
    picks, firsts, raws = [], [], []
    sel = jnp.zeros(vals.shape, F32)
    for _ in range(TOP_K):
        mx = jnp.max(vals, axis=0, keepdims=True)
        first = jnp.min(jnp.where(vals == mx, eidx, float(n_experts)), axis=0, keepdims=True)
        pick = eidx == first
        picks.append(pick)
        firsts.append(first)
        raws.append(jnp.sum(jnp.where(pick, scores, 0.0), axis=0, keepdims=True))
        vals = jnp.where(pick, -jnp.inf, vals)
        sel = jnp.where(pick, 1.0, sel)
    denom = raws[0]
    for r in raws[1:]:
        denom = denom + r

    r_i = lax.broadcasted_iota(jnp.int32, (tm, tm), 0)
    c_i = lax.broadcasted_iota(jnp.int32, (tm, tm), 1)
    earlier = jnp.where(r_i < c_i, 1.0, 0.0).astype(BF16)
    rank = jnp.dot(sel.astype(BF16), earlier, preferred_element_type=F32) + run_ref[...]
    run_ref[...] = run_ref[...] + jnp.sum(sel, axis=1, keepdims=True)
    cnt_ref[...] = run_ref[...]

    slot8 = lax.broadcasted_iota(jnp.int32, (IDX_ROWS, tm), 0)
    slot_w = lax.broadcasted_iota(jnp.int32, (LANES, tm), 0)
    idx8 = jnp.zeros((IDX_ROWS, tm), F32)
    loc8 = jnp.zeros((IDX_ROWS, tm), F32)
    gate_w = jnp.zeros((LANES, tm), F32)
    for k in range(TOP_K):
        loc = jnp.sum(jnp.where(picks[k], rank, 0.0), axis=0, keepdims=True)
        idx8 = jnp.where(slot8 == k, firsts[k], idx8)
        loc8 = jnp.where(slot8 == k, loc, loc8)
        gate_w = jnp.where(slot_w == k, raws[k] / denom * ROUTED_SCALE, gate_w)
    idx_ref[...] = idx8.astype(jnp.int32)
    loc_ref[...] = loc8.astype(jnp.int32)
    gate_ref[...] = gate_w.T[:, :IDX_ROWS]


def _postmix(oa, ys, x2, wglu, woa, wob, ga, gs, gf, wr, rb, *, tm):
    t, d = x2.shape
    da = oa.shape[1]
    dsm = ys.shape[1]
    e = rb.shape[0]
    row = lambda i: (i, 0)
    col = lambda i: (0, i)
    fix = lambda i: (0, 0)
    return pl.pallas_call(
        functools.partial(_postmix_kernel, n_experts=e),
        grid=(t // tm,),
        in_specs=[
            pl.BlockSpec((tm, da), row), pl.BlockSpec((tm, dsm), row), pl.BlockSpec((tm, d), row),
            pl.BlockSpec((dsm, dsm), fix), pl.BlockSpec((da, d), fix), pl.BlockSpec((dsm, d), fix),
            pl.BlockSpec((1, da), fix), pl.BlockSpec((1, dsm), fix), pl.BlockSpec((1, d), fix),
            pl.BlockSpec((d, 2 * e), fix), pl.BlockSpec((e, 1), fix),
        ],
        out_specs=[
            pl.BlockSpec((tm, d), row), pl.BlockSpec((IDX_ROWS, tm), col), pl.BlockSpec((tm, IDX_ROWS), row),
            pl.BlockSpec((IDX_ROWS, tm), col), pl.BlockSpec((e, 1), fix),
        ],
        out_shape=[
            jax.ShapeDtypeStruct((t, d), F32), jax.ShapeDtypeStruct((IDX_ROWS, t), jnp.int32),
            jax.ShapeDtypeStruct((t, IDX_ROWS), F32), jax.ShapeDtypeStruct((IDX_ROWS, t), jnp.int32),
            jax.ShapeDtypeStruct((e, 1), F32),
        ],
        scratch_shapes=[pltpu.VMEM((e, 1), F32)],
        compiler_params=_params(("arbitrary",)),
        name="postmix_router",
    )(oa, ys, x2, wglu, woa, wob, ga, gs, gf, wr, rb)


def _rows_to_tiles(ref, packed, sub):
    rows = packed.shape[0]
    for s in range(sub):
        ref[pl.ds(s, rows, stride=sub), :] = packed[:, s * LANES:(s + 1) * LANES]


def _tiles_to_rows(ref, rows, sub):
    return jnp.concatenate([ref[pl.ds(s, rows, stride=sub), :] for s in range(sub)], axis=1)


def _tile_rows(ref, r, sub):
    start = r * sub if isinstance(r, int) else pl.multiple_of(r * sub, sub)
    return ref.at[pl.ds(start, sub)]


def _shared_kernel(dest_ref, tab_init_hbm, h1_ref, gf_ref, wg_ref, wu_ref, wd_ref, h1s_ref, mp_ref, tab_ref,
                   sem, *, sub, n_tokens):
    tm = h1_ref.shape[0]
    base = pl.program_id(0) * tm

    @pl.when(pl.program_id(0) == 0)
    def _():
        init = pltpu.make_async_copy(tab_init_hbm, tab_ref, sem)
        init.start()
        init.wait()

    for t in range(tm):
        for k in range(TOP_K):
            tab_ref[dest_ref[k * tm + t]] = k * n_tokens + base + t
    h1 = h1_ref[...]
    m = _rms(h1, gf_ref[...])
    _rows_to_tiles(mp_ref, _pack_halves(m), sub)
    mb = m.astype(BF16)
    a = jnp.dot(mb, wg_ref[...], preferred_element_type=F32)
    b = jnp.dot(mb, wu_ref[...], preferred_element_type=F32)
    hid = (a * _sigmoid(a) * b).astype(BF16)
    h1s_ref[...] = h1 + jnp.dot(hid, wd_ref[...], preferred_element_type=F32)


def _shared(dest, table0, h1, gf, wg, wu, wd, *, tm):
    t, d = h1.shape
    f = wg.shape[1]
    sub = d // 2 // LANES
    row = lambda i: (i, 0)
    fix = lambda i: (0, 0)
    return pl.pallas_call(
        functools.partial(_shared_kernel, sub=sub, n_tokens=t),
        grid=(t // tm,),
        in_specs=[
            pl.BlockSpec((IDX_ROWS * tm,), lambda i: (i,), memory_space=pltpu.SMEM),
            pl.BlockSpec(memory_space=pl.ANY),
            pl.BlockSpec((tm, d), row), pl.BlockSpec((1, d), fix),
            pl.BlockSpec((d, f), fix), pl.BlockSpec((d, f), fix), pl.BlockSpec((f, d), fix),
        ],
        out_specs=[pl.BlockSpec((tm, d), row), pl.BlockSpec((tm * sub, LANES), row),
                   pl.BlockSpec(memory_space=pltpu.SMEM)],
        out_shape=[jax.ShapeDtypeStruct((t, d), F32), jax.ShapeDtypeStruct((t * sub, LANES), jnp.int32),
                   jax.ShapeDtypeStruct(table0.shape, jnp.int32)],
        scratch_shapes=[pltpu.SemaphoreType.DMA(())],
        compiler_params=_params(("arbitrary",)),
        name="shared_table",
    )(dest, table0, h1, gf, wg, wu, wd)


def _experts_kernel(be_ref, first_ref, nxt_ref, nu_ref, tab_ref, mp_hbm, wg_hbm, wu_hbm, wd_hbm, yb_hbm,
                    mp_ref, wgf_ref, wuf_ref, wdf_ref, wgb_ref, wub_ref, wdb_ref, x0_ref, x1_ref, y0_ref, y1_ref,
                    wsem, msem, ssem, *, sub, tm, n_tokens):
    i = pl.program_id(0)
    nu = nu_ref[0]
    xs = (x0_ref, x1_ref)
    ys = (y0_ref, y1_ref)
    spare = TOP_K * n_tokens

    def token_of(e):
        return e & (n_tokens - 1) if n_tokens & (n_tokens - 1) == 0 else lax.rem(e, n_tokens)

    def gather(blk, r, s):
        tok = token_of(tab_ref[blk * tm + r])
        dst = r * sub if isinstance(r, int) else pl.multiple_of(r * sub, sub)
        xs[s][pl.ds(dst, sub), :] = mp_ref[pl.ds(pl.multiple_of(tok * sub, sub), sub), :]

    def scatter(entry, r, s):
        return pltpu.make_async_copy(_tile_rows(ys[s], r, sub), _tile_rows(yb_hbm, entry, sub), ssem.at[s])

    def scatter_wait(s):
        pltpu.make_async_copy(ys[s], yb_hbm.at[pl.ds(0, tm * sub)], ssem.at[s]).wait()

    def weight_copies(e):
        return (pltpu.make_async_copy(wg_hbm.at[e], wgf_ref, wsem.at[0]),
                pltpu.make_async_copy(wu_hbm.at[e], wuf_ref, wsem.at[1]),
                pltpu.make_async_copy(wd_hbm.at[e], wdf_ref, wsem.at[2]))

    @pl.when(i == 0)
    def _():
        for c in weight_copies(be_ref[0]):
            c.start()
        resident = pltpu.make_async_copy(mp_hbm, mp_ref, msem)
        resident.start()
        resident.wait()

        def first_rows(r, carry):
            gather(0, r, 0)
            return carry

        lax.fori_loop(0, tm, first_rows, 0)
        y1_ref[...] = jnp.zeros(y1_ref.shape, y1_ref.dtype)

    @pl.when((i < nu) & (first_ref[jnp.minimum(i, first_ref.shape[0] - 1)] != 0))
    def _():
        for c in weight_copies(be_ref[i]):
            c.wait()
        wgb_ref[...] = wgf_ref[...].astype(BF16)
        wub_ref[...] = wuf_ref[...].astype(BF16)
        wdb_ref[...] = wdf_ref[...].astype(BF16)

        @pl.when(nxt_ref[i] >= 0)
        def _():
            for c in weight_copies(nxt_ref[i]):
                c.start()

    def block(p):
        @pl.when(i >= 1)
        def _():
            scatter_wait(p)

        nb = jnp.minimum(i + 1, nu - 1)
        pb = jnp.maximum(i - 1, 0)
        for r in range(tm):
            gather(nb, r, 1 - p)
        for r in range(tm):
            entry = jnp.where(i >= 1, tab_ref[pb * tm + r], spare + r)
            scatter(entry, r, 1 - p).start()
        lo, hi = _unpack_halves(_tiles_to_rows(xs[p], tm, sub))
        lo = lo.astype(BF16)
        hi = hi.astype(BF16)
        n = lo.shape[1]
        a = (jnp.dot(lo, wgb_ref[:n, :], preferred_element_type=F32)
             + jnp.dot(hi, wgb_ref[n:, :], preferred_element_type=F32))
        b = (jnp.dot(lo, wub_ref[:n, :], preferred_element_type=F32)
             + jnp.dot(hi, wub_ref[n:, :], preferred_element_type=F32))
        hid = (a * _sigmoid(a) * b).astype(BF16)
        _rows_to_tiles(ys[p], _pack_halves(jnp.dot(hid, wdb_ref[...], preferred_element_type=F32)), sub)

    def flush(p):
        def last_rows(r, carry):
            scatter(tab_ref[(nu - 1) * tm + r], r, 1 - p).start()
            return carry

        scatter_wait(p)
        lax.fori_loop(0, tm, last_rows, 0)
        scatter_wait(1 - p)

    for p in range(2):
        @pl.when((i < nu) & (lax.rem(i, 2) == p))
        def _(p=p):
            block(p)

        @pl.when((i == nu) & (lax.rem(i, 2) == p))
        def _(p=p):
            flush(p)


def _experts(blk_e, first, nxt, n_used, table, mp, wg, wu, wd, *, tm):
    _, d, f = wg.shape
    sub = d // 2 // LANES
    n_tokens = mp.shape[0] // sub
    n_blocks = table.shape[0] // tm
    return pl.pallas_call(
        functools.partial(_experts_kernel, sub=sub, tm=tm, n_tokens=n_tokens),
        grid_spec=pltpu.PrefetchScalarGridSpec(
            num_scalar_prefetch=5,
            grid=(n_blocks + 1,),
            in_specs=[pl.BlockSpec(memory_space=pl.ANY)] * 4,
            out_specs=pl.BlockSpec(memory_space=pl.ANY),
            scratch_shapes=[pltpu.VMEM(mp.shape, jnp.int32),
                            pltpu.VMEM((d, f), F32), pltpu.VMEM((d, f), F32), pltpu.VMEM((f, d), F32),
                            pltpu.VMEM((d, f), BF16), pltpu.VMEM((d, f), BF16), pltpu.VMEM((f, d), BF16),
                            pltpu.VMEM((tm * sub, LANES), jnp.int32), pltpu.VMEM((tm * sub, LANES), jnp.int32),
                            pltpu.VMEM((tm * sub, LANES), jnp.int32), pltpu.VMEM((tm * sub, LANES), jnp.int32),
                            pltpu.SemaphoreType.DMA((3,)), pltpu.SemaphoreType.DMA(()),
                            pltpu.SemaphoreType.DMA((2,))],
        ),
        out_shape=jax.ShapeDtypeStruct(((TOP_K * n_tokens + tm) * sub, LANES), jnp.int32),
        compiler_params=pltpu.CompilerParams(dimension_semantics=("arbitrary",),
                                             vmem_limit_bytes=EXPERTS_VMEM_LIMIT),
        name="routed_experts",
    )(blk_e, first, nxt, n_used, table, mp, wg, wu, wd)


def _final_kernel(h_ref, gate_ref, p_ref, gp_ref, wgate_ref, wproj_ref, *rest, sub):
    y_refs, o_ref = rest[:TOP_K], rest[TOP_K]
    tm = h_ref.shape[0]
    n = h_ref.shape[1] // 2
    n_grp = COMBINE_ROW_GROUPS if tm % (8 * COMBINE_ROW_GROUPS) == 0 else 1
    gr = tm // n_grp
    rows = [pl.ds(g * gr, gr) for g in range(n_grp)]
    pps = [jnp.dot(p_ref[r, :].astype(BF16), wproj_ref[...], preferred_element_type=F32) for r in rows]
    gates = [gate_ref[r, :] for r in rows]
    acc_los = [h_ref[r, :n] for r in rows]
    acc_his = [h_ref[r, n:] for r in rows]
    for k in range(TOP_K):
        for g in range(n_grp):
            packed = jnp.concatenate([y_refs[k][pl.ds(g * gr * sub + s, gr, stride=sub), :] for s in range(sub)],
                                     axis=1)
            lo, hi = _unpack_halves(packed)
            acc_los[g] = acc_los[g] + gates[g][:, k:k + 1] * lo
            acc_his[g] = acc_his[g] + gates[g][:, k:k + 1] * hi
    h2s = [jnp.concatenate([lo, hi], axis=1) for lo, hi in zip(acc_los, acc_his)]
    nrms = [_rms(h2, gp_ref[...]).astype(BF16) for h2 in h2s]
    gts = [_sigmoid(jnp.dot(nrm, wgate_ref[...], preferred_element_type=F32)) for nrm in nrms]
    for r, h2, gt, pp in zip(rows, h2s, gts, pps):
        o_ref[r, :] = h2 + gt * pp


def _final(h1s, gate, p2, gp, wgate, wproj, yb, *, tm):
    t, d = h1s.shape
    dp = p2.shape[1]
    sub = d // 2 // LANES
    nt = t // tm
    row = lambda i: (i, 0)
    fix = lambda i: (0, 0)
    y_specs = [pl.BlockSpec((tm * sub, LANES), lambda i, k=k: (k * nt + i, 0)) for k in range(TOP_K)]
    return pl.pallas_call(
        functools.partial(_final_kernel, sub=sub),
        grid=(nt,),
        in_specs=[
            pl.BlockSpec((tm, d), row), pl.BlockSpec((tm, IDX_ROWS), row), pl.BlockSpec((tm, dp), row),
            pl.BlockSpec((1, d), fix), pl.BlockSpec((d, d), fix), pl.BlockSpec((dp, d), fix),
        ] + y_specs,
        out_specs=pl.BlockSpec((tm, d), row),
        out_shape=jax.ShapeDtypeStruct((t, d), F32),
        compiler_params=_params(("parallel",)),
        name="combine_ple",
    )(h1s, gate, p2, gp, wgate, wproj, *([yb] * TOP_K))


def _largest_tile(n, cap):
    t = min(n, cap)
    while n % t:
        t //= 2
    return t


def _layer(h, p_l, norm_mix, w_in, q_norm, k_norm, lam_re, lam_im, log_dt, b_re, b_im, c_re, c_im, d_skip,
           w_glu, attn_out_norm, ssm_out_norm, w_out, norm_ffn, w_router, router_bias, w_exp_gate,
           w_exp_up, w_exp_down, w_sh_gate, w_sh_up, w_sh_down, norm_ple, w_ple_gate, w_ple_proj):
    bsz, seq, d = h.shape
    t = bsz * seq
    head_dim = q_norm.shape[-1]
    d_attn = attn_out_norm.shape[-1]
    n_heads = d_attn // head_dim
    n_experts = w_router.shape[-1]
    x2 = h.reshape(t, d)

    tn = d_attn if d_attn % 128 == 0 else head_dim
    scale = 1.0 / math.sqrt(head_dim)
    hg = jnp.concatenate([jnp.tile(q_norm * scale, n_heads), jnp.tile(k_norm, n_heads),
                          jnp.ones((w_in.shape[1] - 2 * d_attn,), F32)])[None, :]
    z, u = _inproj(x2, norm_mix[None, :], w_in.astype(BF16), hg, d_qkv=3 * d_attn,
                   n_norm_tiles=2 * d_attn // tn, head_dim=head_dim, tm=_largest_tile(t, 1024), tn=tn)

    o_attn = _attention(z, batch=bsz, seq=seq, n_heads=n_heads, head_dim=head_dim, blk=_largest_tile(seq, 256))

    n_chunks = seq // S5_CHUNK
    mats = _s5_prepare(lam_re, lam_im, log_dt, b_re, b_im, c_re, c_im, d_skip, n_chunks)
    y_ssm = _s5(u, *mats, n_chunks=n_chunks)

    tm = _largest_tile(t, 256)
    w_out_b = w_out.astype(BF16)
    wr_hi = w_router.astype(BF16)
    wr_lo = (w_router - wr_hi.astype(F32)).astype(BF16)
    h1, idx, gate, loc, counts = _postmix(
        o_attn, y_ssm, x2, w_glu.astype(BF16), w_out_b[:d_attn], w_out_b[d_attn:],
        attn_out_norm[None, :], ssm_out_norm[None, :], norm_ffn[None, :],
        jnp.concatenate([wr_hi, wr_lo], axis=1), router_bias[:, None], tm=_largest_tile(t, 512))

    rb = _largest_tile(t, 256)
    n_pad = t * TOP_K + n_experts * rb
    n_blocks = n_pad // rb
    cnt = counts[:, 0].astype(jnp.int32)
    pcnt = (cnt + rb - 1) // rb * rb
    pend = jnp.cumsum(pcnt)
    pstart = pend - pcnt
    onehot = idx[:, :, None] == jnp.arange(n_experts, dtype=jnp.int32)[None, None, :]
    dest = jnp.sum(jnp.where(onehot, pstart[None, None, :], 0), axis=-1) + loc
    dest = dest.reshape(IDX_ROWS, t // tm, tm).transpose(1, 0, 2).reshape(-1)
    n_used = (pend[-1] // rb).astype(jnp.int32)
    blk_ids = jnp.arange(n_blocks, dtype=jnp.int32)
    starts = (blk_ids * rb)[:, None]
    owner = (starts >= pstart[None, :]) & (starts < pend[None, :])
    blk_e = jnp.sum(jnp.where(owner, jnp.arange(n_experts, dtype=jnp.int32)[None, :], 0), axis=1)
    is_first = jnp.any(owner & (starts == pstart[None, :]), axis=1).astype(jnp.int32)
    first_pos = jnp.where(is_first > 0, blk_ids, n_blocks)
    nxt_pos = jnp.concatenate([lax.cummin(first_pos, axis=0, reverse=True)[1:],
                               jnp.full((1,), n_blocks, jnp.int32)])
    nxt_hit = nxt_pos[:, None] == blk_ids[None, :]
    nxt = jnp.where(nxt_pos < n_blocks, jnp.sum(jnp.where(nxt_hit, blk_e[None, :], 0), axis=1), -1)
    last_e = jnp.sum(jnp.where(blk_ids == n_used - 1, blk_e, 0))
    blk_e = jnp.where(blk_ids < n_used, blk_e, last_e).astype(jnp.int32)

    table0 = TOP_K * t + jnp.arange(n_pad, dtype=jnp.int32) % rb
    h1s, mp, table = _shared(dest, table0, h1, norm_ffn[None, :], w_sh_gate.astype(BF16), w_sh_up.astype(BF16),
                             w_sh_down.astype(BF16), tm=tm)
    yb = _experts(blk_e, is_first, nxt.astype(jnp.int32), n_used[None], table, mp, w_exp_gate, w_exp_up,
                  w_exp_down, tm=rb)
    out = _final(h1s, gate, p_l.reshape(t, -1), norm_ple[None, :], w_ple_gate.astype(BF16),
                 w_ple_proj.astype(BF16), yb, tm=tm)
    return out.reshape(bsz, seq, d)


def kernel(x, p, norm_mix, w_in, q_norm, k_norm, ssm_lam_re, ssm_lam_im, ssm_log_dt, ssm_b_re, ssm_b_im,
           ssm_c_re, ssm_c_im, ssm_d, w_glu, attn_out_norm, ssm_out_norm, w_out, norm_ffn, w_router,
           router_bias, w_exp_gate, w_exp_up, w_exp_down, w_sh_gate, w_sh_up, w_sh_down, norm_ple,
           w_ple_gate, w_ple_proj):
    h = x
    for i in range(p.shape[0]):
        h = _layer(h, p[i], norm_mix[i], w_in[i], q_norm[i], k_norm[i], ssm_lam_re[i], ssm_lam_im[i],
                   ssm_log_dt[i], ssm_b_re[i], ssm_b_im[i], ssm_c_re[i], ssm_c_im[i], ssm_d[i], w_glu[i],
                   attn_out_norm[i], ssm_out_norm[i], w_out[i], norm_ffn[i], w_router[i], router_bias[i],
                   w_exp_gate[i], w_exp_up[i], w_exp_down[i], w_sh_gate[i], w_sh_up[i], w_sh_down[i],
                   norm_ple[i], w_ple_gate[i], w_ple_proj[i])
    return h
```

```python
import functools
import math

import jax
import jax.numpy as jnp
from jax import lax
from jax.experimental import pallas as pl
from jax.experimental.pallas import tpu as pltpu

NORM_EPS = 1e-6
TOP_K = 6
ROUTED_SCALE = 2.5
IDX_ROWS = 8
LANES = 128
S5_CHUNK = 8
POSTMIX_ROW_GROUPS = 2
S5_ROW_GROUPS = 2
COMBINE_ROW_GROUPS = 2
VMEM_LIMIT = 56 * 1024 * 1024
EXPERTS_VMEM_LIMIT = 62 * 1024 * 1024

F32 = jnp.float32
BF16 = jnp.bfloat16
HIGH_HALF = -65536


def _rms(x, g):
    return x * lax.rsqrt(jnp.mean(x * x, axis=-1, keepdims=True) + NORM_EPS) * g


def _sigmoid(x):
    return 1.0 / (1.0 + jnp.exp(-x))


def _params(sem):
    return pltpu.CompilerParams(dimension_semantics=sem, vmem_limit_bytes=VMEM_LIMIT)


def _pack_halves(x):
    n = x.shape[1] // 2
    lo = lax.bitcast_convert_type(x[:, :n].astype(BF16).astype(F32), jnp.int32)
    hi = lax.bitcast_convert_type(x[:, n:].astype(BF16).astype(F32), jnp.int32)
    return (hi & HIGH_HALF) | lax.shift_right_logical(lo, 16)


def _unpack_halves(p):
    lo = lax.bitcast_convert_type(lax.shift_left(p, 16), F32)
    hi = lax.bitcast_convert_type(p & HIGH_HALF, F32)
    return lo, hi


def _inproj_kernel(x_ref, g_ref, w_ref, hg_ref, o_ref, u_ref, xn_ref, *, n_norm_tiles, n_qkv_tiles, head_dim):
    j = pl.program_id(1)

    @pl.when(j == 0)
    def _():
        xn_ref[...] = _rms(x_ref[...], g_ref[...]).astype(BF16)

    acc = jnp.dot(xn_ref[...], w_ref[...], preferred_element_type=F32)
    tn = acc.shape[1]

    @pl.when(j < n_norm_tiles)
    def _():
        hg = hg_ref[...]
        for h in range(tn // head_dim):
            sl = slice(h * head_dim, (h + 1) * head_dim)
            o_ref[:, sl] = _rms(acc[:, sl], hg[:, sl]).astype(o_ref.dtype)

    @pl.when((j >= n_norm_tiles) & (j < n_qkv_tiles))
    def _():
        o_ref[...] = acc.astype(o_ref.dtype)

    @pl.when(j >= n_qkv_tiles)
    def _():
        u_ref[...] = acc


def _inproj(x2, g, w, hg, *, d_qkv, n_norm_tiles, head_dim, tm, tn):
    t, d = x2.shape
    n = w.shape[1]
    nq = d_qkv // tn
    return pl.pallas_call(
        functools.partial(_inproj_kernel, n_norm_tiles=n_norm_tiles, n_qkv_tiles=nq, head_dim=head_dim),
        grid=(t // tm, n // tn),
        in_specs=[
            pl.BlockSpec((tm, d), lambda i, j: (i, 0)),
            pl.BlockSpec((1, d), lambda i, j: (0, 0)),
            pl.BlockSpec((d, tn), lambda i, j: (0, j)),
            pl.BlockSpec((1, tn), lambda i, j: (0, j)),
        ],
        out_specs=[
            pl.BlockSpec((tm, tn), lambda i, j: (i, jnp.minimum(j, nq - 1))),
            pl.BlockSpec((tm, tn), lambda i, j: (i, jnp.maximum(j - nq, 0))),
        ],
        out_shape=[jax.ShapeDtypeStruct((t, d_qkv), BF16), jax.ShapeDtypeStruct((t, n - d_qkv), F32)],
        scratch_shapes=[pltpu.VMEM((tm, d), BF16)],
        compiler_params=_params(("parallel", "arbitrary")),
        name="inproj",
    )(x2, g, w, hg)


SKIP_AFTER = 110.0


def _attn_kernel(q_ref, k_ref, v_ref, o_ref, *, blk, hd, heads):
    i = pl.program_id(2)
    row = lax.broadcasted_iota(jnp.int32, (blk, blk), 0)
    col = lax.broadcasted_iota(jnp.int32, (blk, blk), 1)
    causal = col < row
    later = jnp.where(row > col, 1.0, 0.0).astype(BF16)
    lanes = [slice(h * hd, (h + 1) * hd) for h in range(heads)]
    qs = [q_ref[:, sl] for sl in lanes]

    def step(kb, accs, runs, diag):
        start = pl.multiple_of(kb * blk, blk)
        zs, sps, inners, new_accs, new_runs = [], [], [], [], []
        for h, sl in enumerate(lanes):
            k = k_ref[pl.ds(start, blk), sl]
            zs.append(lax.dot_general(qs[h], k, (((1,), (1,)), ((), ())), preferred_element_type=F32))
        for z in zs:
            sp = jnp.maximum(z, 0.0) + jnp.log(1.0 + jnp.exp(-jnp.abs(z)))
            sps.append(jnp.where(causal, sp, 0.0) if diag else sp)
        for sp in sps:
            hi = sp.astype(BF16)
            lo = (sp - hi.astype(F32)).astype(BF16)
            inners.append(jnp.dot(hi, later, preferred_element_type=F32)
                          + jnp.dot(lo, later, preferred_element_type=F32))
        for h, sl in enumerate(lanes):
            w = jnp.exp(zs[h] - sps[h] - (inners[h] + runs[h]))
            if diag:
                w = jnp.where(causal, w, 0.0)
            v = v_ref[pl.ds(start, blk), sl]
            new_accs.append(accs[h] + jnp.dot(w.astype(BF16), v, preferred_element_type=F32))
            new_runs.append(runs[h] + jnp.sum(sps[h], axis=-1, keepdims=True))
        return tuple(new_accs), tuple(new_runs)

    def keep_going(runs):
        low = runs[0]
        for r in runs[1:]:
            low = jnp.minimum(low, r)
        return (jnp.min(low) < SKIP_AFTER).astype(jnp.int32)

    accs = tuple(jnp.zeros((blk, hd), F32) for _ in lanes)
    runs = tuple(jnp.zeros((blk, 1), F32) for _ in lanes)
    accs, runs = step(i, accs, runs, True)

    def cond(c):
        n, go, _, _ = c
        return (n < i) & (go > 0)

    def body(c):
        n, _, accs, runs = c
        accs, runs = step(i - 1 - n, accs, runs, False)
        return n + 1, keep_going(runs), accs, runs

    _, _, accs, _ = lax.while_loop(cond, body, (jnp.int32(0), keep_going(runs), accs, runs))
    for h, sl in enumerate(lanes):
        o_ref[:, sl] = accs[h].astype(o_ref.dtype)


def _attention(z, *, batch, seq, n_heads, head_dim, blk):
    nq = seq // blk
    heads = next(c for c in (4, 2, 1) if n_heads % c == 0)
    hw = heads * head_dim
    ng = n_heads // heads
    return pl.pallas_call(
        functools.partial(_attn_kernel, blk=blk, hd=head_dim, heads=heads),
        grid=(batch, ng, nq),
        in_specs=[
            pl.BlockSpec((blk, hw), lambda b, h, i: (b * nq + i, h)),
            pl.BlockSpec((seq, hw), lambda b, h, i: (b, ng + h)),
            pl.BlockSpec((seq, hw), lambda b, h, i: (b, 2 * ng + h)),
        ],
        out_specs=pl.BlockSpec((blk, hw), lambda b, h, i: (b * nq + i, h)),
        out_shape=jax.ShapeDtypeStruct((batch * seq, n_heads * head_dim), BF16),
        compiler_params=_params(("parallel", "parallel", "arbitrary")),
        name="sb_attention",
    )(z, z, z)


def _s5_prepare(lam_re, lam_im, log_dt, b_re, b_im, c_re, c_im, d_skip, n_chunks):
    hp = lax.Precision.HIGHEST
    chunk = S5_CHUNK
    g, p = lam_re.shape
    h = b_re.shape[-1]
    gg = LANES // h
    o = g // gg
    dt = jnp.exp(log_dt)[:, None]
    ks = jnp.arange(chunk + 1, dtype=F32)[None, :, None]
    mag = jnp.exp(lam_re[:, None, :] * dt[:, None, :] * ks)
    ang = lam_im[:, None, :] * dt[:, None, :] * ks
    ak_re = mag * jnp.cos(ang)
    ak_im = mag * jnp.sin(ang)
    nr = ak_re[:, 1] - 1.0
    ni = ak_im[:, 1]
    den = lam_re * lam_re + lam_im * lam_im
    coef_re = ((nr * lam_re + ni * lam_im) / den)[..., None]
    coef_im = ((ni * lam_re - nr * lam_im) / den)[..., None]
    bbar_re = coef_re * b_re - coef_im * b_im
    bbar_im = coef_re * b_im + coef_im * b_re
    ca_re = c_re[:, None] * ak_re[:, :, None, :] - c_im[:, None] * ak_im[:, :, None, :]
    ca_im = c_re[:, None] * ak_im[:, :, None, :] + c_im[:, None] * ak_re[:, :, None, :]

    cam_re = ca_re[:, :chunk].transpose(0, 3, 1, 2).reshape(g, p, chunk * h)
    cam_im = ca_im[:, :chunk].transpose(0, 3, 1, 2).reshape(g, p, chunk * h)
    kt = (jnp.einsum("gph,gpm->ghm", bbar_re, cam_re, precision=hp)
          - jnp.einsum("gph,gpm->ghm", bbar_im, cam_im, precision=hp))
    def same_group(n_rows, row_block, n_cols, col_block):
        r = (jnp.arange(n_rows) // row_block) % gg
        c = (jnp.arange(n_cols) // col_block) % gg
        return (r[:, None] == c[None, :]).astype(F32)

    tile_h = jnp.tile(jnp.eye(h, dtype=F32), (1, gg))
    tile_p = jnp.tile(jnp.eye(p, dtype=F32), (1, gg))
    kt = kt.reshape(o, gg, h, chunk, h).transpose(0, 3, 1, 2, 4).reshape(o, chunk, LANES, h)
    bd = jnp.einsum("okrh,hc->okrc", kt, tile_h, precision=hp) * same_group(LANES, h, LANES, h)
    dvec = d_skip.reshape(o, LANES)
    lags = [bd[:, k] for k in range(chunk)]
    lags[0] = lags[0] + dvec[:, :, None] * jnp.eye(LANES, dtype=F32)[None]
    m_lags = jnp.stack(lags, axis=1)

    def expand_in(ak_part_a, ak_part_b, b_a, b_b, sign):
        rk_a = jnp.stack([ak_part_a[:, chunk - 1 - i] for i in range(chunk)], axis=1)
        rk_b = jnp.stack([ak_part_b[:, chunk - 1 - i] for i in range(chunk)], axis=1)
        val = (rk_a[:, :, None, :] * b_a.transpose(0, 2, 1)[:, None]
               + sign * rk_b[:, :, None, :] * b_b.transpose(0, 2, 1)[:, None])
        val = val.reshape(o, gg, chunk, h, p).transpose(0, 2, 1, 3, 4).reshape(o, chunk * LANES, p)
        wide = jnp.einsum("orp,pc->orc", val.astype(BF16), tile_p.astype(BF16), preferred_element_type=BF16)
        return wide * same_group(chunk * LANES, h, gg * p, p).astype(BF16)

    m_in = jnp.concatenate([expand_in(ak_re, ak_im, bbar_re, bbar_im, -1.0),
                            expand_in(ak_re, ak_im, bbar_im, bbar_re, 1.0)], axis=-1)

    def expand_out(ca):
        val = ca[:, 1:].reshape(o, gg, chunk, h, p).transpose(0, 4, 2, 1, 3).reshape(o, p, chunk * LANES)
        wide = jnp.einsum("pc,opr->ocr", tile_p.astype(BF16), val.astype(BF16), preferred_element_type=BF16)
        return wide * same_group(gg * p, p, chunk * LANES, h).astype(BF16)

    m_out = jnp.concatenate([expand_out(ca_re), -expand_out(ca_im)], axis=1)

    steps = max(1, int(math.ceil(math.log2(max(n_chunks, 2)))))
    cr, ci = ak_re[:, chunk].reshape(o, gg * p), ak_im[:, chunk].reshape(o, gg * p)
    sc_a, sc_b = [], []
    for _ in range(steps):
        sc_a.append(jnp.concatenate([cr, cr], axis=-1))
        sc_b.append(jnp.concatenate([-ci, ci], axis=-1))
        cr, ci = cr * cr - ci * ci, 2.0 * cr * ci
    return m_lags.astype(BF16), m_in, m_out, jnp.stack(sc_a, axis=1), jnp.stack(sc_b, axis=1)


def _s5_kernel(u_ref, lag_ref, min_ref, mout_ref, sa_ref, sb_ref, y_ref, mi_ref, *, n_chunks, steps):
    chunk = S5_CHUNK
    n_all = u_ref.shape[0] // chunk
    for i in range(chunk):
        for j in range(chunk):
            blk = lag_ref[j - i] if j >= i else jnp.zeros((LANES, LANES), mi_ref.dtype)
            mi_ref[i * LANES:(i + 1) * LANES, j * LANES:(j + 1) * LANES] = blk
    n_grp = S5_ROW_GROUPS if (n_all // n_chunks) % S5_ROW_GROUPS == 0 else 1
    n = n_all // n_grp
    us = [jnp.concatenate([u_ref[pl.ds(g * n * chunk + i, n, stride=chunk), :] for i in range(chunk)],
                          axis=1).astype(BF16) for g in range(n_grp)]
    ys = [jnp.dot(u, mi_ref[...], preferred_element_type=F32) for u in us]
    xs = [jnp.dot(u, min_ref[...], preferred_element_type=F32) for u in us]
    half = xs[0].shape[1] // 2
    c = lax.rem(lax.broadcasted_iota(jnp.int32, xs[0].shape, 0), n_chunks)
    sa = sa_ref[...]
    sb = sb_ref[...]
    for k in range(steps):
        sh = 1 << k
        sh_xs = [jnp.where(c >= sh, pltpu.roll(x, sh, axis=0), 0.0) for x in xs]
        xs = [x + s * sa[k:k + 1, :] + pltpu.roll(s, half, axis=1) * sb[k:k + 1, :] for x, s in zip(xs, sh_xs)]
    s_ins = [jnp.where(c >= 1, pltpu.roll(x, 1, axis=0), 0.0) for x in xs]
    ys = [y + jnp.dot(s_in.astype(BF16), mout_ref[...], preferred_element_type=F32) for y, s_in in zip(ys, s_ins)]
    for g, y in enumerate(ys):
        for i in range(chunk):
            y_ref[pl.ds(g * n * chunk + i, n, stride=chunk), :] = y[:, i * LANES:(i + 1) * LANES]


def _s5(u, m_lags, m_in, m_out, sc_a, sc_b, *, n_chunks):
    t, d_ssm = u.shape
    o, lh, st = m_in.shape
    steps = sc_a.shape[1]
    mat = lambda i: (i, 0, 0)
    return pl.pallas_call(
        functools.partial(_s5_kernel, n_chunks=n_chunks, steps=steps),
        grid=(o,),
        in_specs=[
            pl.BlockSpec((t, LANES), lambda i: (0, i)),
            pl.BlockSpec((None,) + m_lags.shape[1:], lambda i: (i, 0, 0, 0)),
            pl.BlockSpec((None, lh, st), mat), pl.BlockSpec((None, st, lh), mat),
            pl.BlockSpec((None, steps, st), mat), pl.BlockSpec((None, steps, st), mat),
        ],
        out_specs=pl.BlockSpec((t, LANES), lambda i: (0, i)),
        out_shape=jax.ShapeDtypeStruct((t, d_ssm), F32),
        scratch_shapes=[pltpu.VMEM((lh, lh), BF16)],
        compiler_params=_params(("parallel",)),
        name="s5_chunked",
    )(u, m_lags, m_in, m_out, sc_a, sc_b)


def _postmix_kernel(oa_ref, ys_ref, x_ref, wglu_ref, woa_ref, wob_ref, ga_ref, gs_ref, gf_ref,
                    wr_ref, rb_ref, h1_ref, idx_ref, gate_ref, loc_ref, cnt_ref, run_ref, *, n_experts):
    step = pl.program_id(0)

    @pl.when(step == 0)
    def _():
        run_ref[...] = jnp.zeros(run_ref.shape, F32)

    tm = x_ref.shape[0]
    n_groups = POSTMIX_ROW_GROUPS if tm % (8 * POSTMIX_ROW_GROUPS) == 0 else 1
    rows = [pl.ds(g * (tm // n_groups), tm // n_groups) for g in range(n_groups)]

    def dot(a, w_ref):
        return jnp.dot(a, w_ref[...], preferred_element_type=F32)

    ys = [ys_ref[r, :] for r in rows]
    ys = [0.5 * y * (1.0 + jnp.tanh(math.sqrt(2.0 / math.pi) * (y + 0.044715 * (y * y * y)))) for y in ys]
    glu = [dot(y.astype(BF16), wglu_ref) for y in ys]
    ys = [y * _sigmoid(g) for y, g in zip(ys, glu)]
    nas = [_rms(oa_ref[r, :].astype(F32), ga_ref[...]).astype(BF16) for r in rows]
    nss = [_rms(y, gs_ref[...]).astype(BF16) for y in ys]
    h1s = [x_ref[r, :] + dot(na, woa_ref) + dot(ns, wob_ref) for r, na, ns in zip(rows, nas, nss)]
    for r, h1 in zip(rows, h1s):
        h1_ref[r, :] = h1

    ms = [_rms(h1, gf_ref[...]) for h1 in h1s]
    m_his = [m.astype(BF16) for m in ms]
    m_los = [(m - m_hi.astype(F32)).astype(BF16) for m, m_hi in zip(ms, m_his)]
    r_his = [dot(m_hi, wr_ref) for m_hi in m_his]
    r_los = [dot(m_lo, wr_ref) for m_lo in m_los]
    wide_t = jnp.concatenate([(r_hi + r_lo).T for r_hi, r_lo in zip(r_his, r_los)], axis=1)
    logits = wide_t[:n_experts] + wide_t[n_experts:]
    scores = _sigmoid(logits)
    vals = scores + rb_ref[...]
    eidx = lax.broadcasted_iota(jnp.int32, vals.shape, 0).astype(F32)
    picks, firsts, raws = [], [], []
    sel = jnp.zeros(vals.shape, F32)
    for _ in range(TOP_K):
        mx = jnp.max(vals, axis=0, keepdims=True)
        first = jnp.min(jnp.where(vals == mx, eidx, float(n_experts)), axis=0, keepdims=True)
        pick = eidx == first
        picks.append(pick)
        firsts.append(first)
        raws.append(jnp.sum(jnp.where(pick, scores, 0.0), axis=0, keepdims=True))
        vals = jnp.where(pick, -jnp.inf, vals)
        sel = jnp.where(pick, 1.0, sel)
    denom = raws[0]
    for r in raws[1:]:
        denom = denom + r

    r_i = lax.broadcasted_iota(jnp.int32, (tm, tm), 0)
    c_i = lax.broadcasted_iota(jnp.int32, (tm, tm), 1)
    earlier = jnp.where(r_i < c_i, 1.0, 0.0).astype(BF16)
    rank = jnp.dot(sel.astype(BF16), earlier, preferred_element_type=F32) + run_ref[...]
    run_ref[...] = run_ref[...] + jnp.sum(sel, axis=1, keepdims=True)
    cnt_ref[...] = run_ref[...]

    slot8 = lax.broadcasted_iota(jnp.int32, (IDX_ROWS, tm), 0)
    slot_w = lax.broadcasted_iota(jnp.int32, (LANES, tm), 0)
    idx8 = jnp.zeros((IDX_ROWS, tm), F32)
    loc8 = jnp.zeros((IDX_ROWS, tm), F32)
    gate_w = jnp.zeros((LANES, tm), F32)
    for k in range(TOP_K):
        loc = jnp.sum(jnp.where(picks[k], rank, 0.0), axis=0, keepdims=True)
        idx8 = jnp.where(slot8 == k, firsts[k], idx8)
        loc8 = jnp.where(slot8 == k, loc, loc8)
        gate_w = jnp.where(slot_w == k, raws[k] / denom * ROUTED_SCALE, gate_w)
    idx_ref[...] = idx8.astype(jnp.int32)
    loc_ref[...] = loc8.astype(jnp.int32)
    gate_ref[...] = gate_w.T[:, :IDX_ROWS]


def _postmix(oa, ys, x2, wglu, woa, wob, ga, gs, gf, wr, rb, *, tm):
    t, d = x2.shape
    da = oa.shape[1]
    dsm = ys.shape[1]
    e = rb.shape[0]
    row = lambda i: (i, 0)
    col = lambda i: (0, i)
    fix = lambda i: (0, 0)
    return pl.pallas_call(
        functools.partial(_postmix_kernel, n_experts=e),
        grid=(t // tm,),
        in_specs=[
            pl.BlockSpec((tm, da), row), pl.BlockSpec((tm, dsm), row), pl.BlockSpec((tm, d), row),
            pl.BlockSpec((dsm, dsm), fix), pl.BlockSpec((da, d), fix), pl.BlockSpec((dsm, d), fix),
            pl.BlockSpec((1, da), fix), pl.BlockSpec((1, dsm), fix), pl.BlockSpec((1, d), fix),
            pl.BlockSpec((d, 2 * e), fix), pl.BlockSpec((e, 1), fix),
        ],
        out_specs=[
            pl.BlockSpec((tm, d), row), pl.BlockSpec((IDX_ROWS, tm), col), pl.BlockSpec((tm, IDX_ROWS), row),
            pl.BlockSpec((IDX_ROWS, tm), col), pl.BlockSpec((e, 1), fix),
        ],
        out_shape=[
            jax.ShapeDtypeStruct((t, d), F32), jax.ShapeDtypeStruct((IDX_ROWS, t), jnp.int32),
            jax.ShapeDtypeStruct((t, IDX_ROWS), F32), jax.ShapeDtypeStruct((IDX_ROWS, t), jnp.int32),
            jax.ShapeDtypeStruct((e, 1), F32),
        ],
        scratch_shapes=[pltpu.VMEM((e, 1), F32)],
        compiler_params=_params(("arbitrary",)),
        name="postmix_router",
    )(oa, ys, x2, wglu, woa, wob, ga, gs, gf, wr, rb)


def _rows_to_tiles(ref, packed, sub):
    rows = packed.shape[0]
    for s in range(sub):
        ref[pl.ds(s, rows, stride=sub), :] = packed[:, s * LANES:(s + 1) * LANES]


def _tiles_to_rows(ref, rows, sub):
    return jnp.concatenate([ref[pl.ds(s, rows, stride=sub), :] for s in range(sub)], axis=1)


def _tile_rows(ref, r, sub):
    start = r * sub if isinstance(r, int) else pl.multiple_of(r * sub, sub)
    return ref.at[pl.ds(start, sub)]


def _shared_kernel(dest_ref, tab_init_hbm, h1_ref, gf_ref, wg_ref, wu_ref, wd_ref, h1s_ref, mp_ref, tab_ref,
                   sem, *, sub, n_tokens):
    tm = h1_ref.shape[0]
    base = pl.program_id(0) * tm

    @pl.when(pl.program_id(0) == 0)
    def _():
        init = pltpu.make_async_copy(tab_init_hbm, tab_ref, sem)
        init.start()
        init.wait()

    for t in range(tm):
        for k in range(TOP_K):
            tab_ref[dest_ref[k * tm + t]] = k * n_tokens + base + t
    h1 = h1_ref[...]
    m = _rms(h1, gf_ref[...])
    _rows_to_tiles(mp_ref, _pack_halves(m), sub)
    mb = m.astype(BF16)
    a = jnp.dot(mb, wg_ref[...], preferred_element_type=F32)
    b = jnp.dot(mb, wu_ref[...], preferred_element_type=F32)
    hid = (a * _sigmoid(a) * b).astype(BF16)
    h1s_ref[...] = h1 + jnp.dot(hid, wd_ref[...], preferred_element_type=F32)


def _shared(dest, table0, h1, gf, wg, wu, wd, *, tm):
    t, d = h1.shape
    f = wg.shape[1]
    sub = d // 2 // LANES
    row = lambda i: (i, 0)
    fix = lambda i: (0, 0)
    return pl.pallas_call(
        functools.partial(_shared_kernel, sub=sub, n_tokens=t),
        grid=(t // tm,),
        in_specs=[
            pl.BlockSpec((IDX_ROWS * tm,), lambda i: (i,), memory_space=pltpu.SMEM),
            pl.BlockSpec(memory_space=pl.ANY),
            pl.BlockSpec((tm, d), row), pl.BlockSpec((1, d), fix),
            pl.BlockSpec((d, f), fix), pl.BlockSpec((d, f), fix), pl.BlockSpec((f, d), fix),
        ],
        out_specs=[pl.BlockSpec((tm, d), row), pl.BlockSpec((tm * sub, LANES), row),
                   pl.BlockSpec(memory_space=pltpu.SMEM)],
        out_shape=[jax.ShapeDtypeStruct((t, d), F32), jax.ShapeDtypeStruct((t * sub, LANES), jnp.int32),
                   jax.ShapeDtypeStruct(table0.shape, jnp.int32)],
        scratch_shapes=[pltpu.SemaphoreType.DMA(())],
        compiler_params=_params(("arbitrary",)),
        name="shared_table",
    )(dest, table0, h1, gf, wg, wu, wd)


def _experts_kernel(be_ref, first_ref, nxt_ref, nu_ref, tab_ref, mp_hbm, wg_hbm, wu_hbm, wd_hbm, yb_hbm,
                    mp_ref, wgf_ref, wuf_ref, wdf_ref, wgb_ref, wub_ref, wdb_ref, x0_ref, x1_ref, y0_ref, y1_ref,
                    wsem, msem, ssem, *, sub, tm, n_tokens):
    i = pl.program_id(0)
    nu = nu_ref[0]
    xs = (x0_ref, x1_ref)
    ys = (y0_ref, y1_ref)
    spare = TOP_K * n_tokens

    def token_of(e):
        return e & (n_tokens - 1) if n_tokens & (n_tokens - 1) == 0 else lax.rem(e, n_tokens)

    def gather(blk, r, s):
        tok = token_of(tab_ref[blk * tm + r])
        dst = r * sub if isinstance(r, int) else pl.multiple_of(r * sub, sub)
        xs[s][pl.ds(dst, sub), :] = mp_ref[pl.ds(pl.multiple_of(tok * sub, sub), sub), :]

    def scatter(entry, r, s):
        return pltpu.make_async_copy(_tile_rows(ys[s], r, sub), _tile_rows(yb_hbm, entry, sub), ssem.at[s])

    def scatter_wait(s):
        pltpu.make_async_copy(ys[s], yb_hbm.at[pl.ds(0, tm * sub)], ssem.at[s]).wait()

    def weight_copies(e):
        return (pltpu.make_async_copy(wg_hbm.at[e], wgf_ref, wsem.at[0]),
                pltpu.make_async_copy(wu_hbm.at[e], wuf_ref, wsem.at[1]),
                pltpu.make_async_copy(wd_hbm.at[e], wdf_ref, wsem.at[2]))

    @pl.when(i == 0)
    def _():
        for c in weight_copies(be_ref[0]):
            c.start()
        resident = pltpu.make_async_copy(mp_hbm, mp_ref, msem)
        resident.start()
        resident.wait()

        def first_rows(r, carry):
            gather(0, r, 0)
            return carry

        lax.fori_loop(0, tm, first_rows, 0)
        y1_ref[...] = jnp.zeros(y1_ref.shape, y1_ref.dtype)

    @pl.when((i < nu) & (first_ref[jnp.minimum(i, first_ref.shape[0] - 1)] != 0))
    def _():
        for c in weight_copies(be_ref[i]):
            c.wait()
        wgb_ref[...] = wgf_ref[...].astype(BF16)
        wub_ref[...] = wuf_ref[...].astype(BF16)
        wdb_ref[...] = wdf_ref[...].astype(BF16)

        @pl.when(nxt_ref[i] >= 0)
        def _():
            for c in weight_copies(nxt_ref[i]):
                c.start()

    def block(p):
        @pl.when(i >= 1)
        def _():
            scatter_wait(p)

        nb = jnp.minimum(i + 1, nu - 1)
        pb = jnp.maximum(i - 1, 0)
        for r in range(tm):
            gather(nb, r, 1 - p)
        for r in range(tm):
            entry = jnp.where(i >= 1, tab_ref[pb * tm + r], spare + r)
            scatter(entry, r, 1 - p).start()
        lo, hi = _unpack_halves(_tiles_to_rows(xs[p], tm, sub))
        lo = lo.astype(BF16)
        hi = hi.astype(BF16)
        n = lo.shape[1]
        a = (jnp.dot(lo, wgb_ref[:n, :], preferred_element_type=F32)
             + jnp.dot(hi, wgb_ref[n:, :], preferred_element_type=F32))
        b = (jnp.dot(lo, wub_ref[:n, :], preferred_element_type=F32)
             + jnp.dot(hi, wub_ref[n:, :], preferred_element_type=F32))
        hid = (a * _sigmoid(a) * b).astype(BF16)
        _rows_to_tiles(ys[p], _pack_halves(jnp.dot(hid, wdb_ref[...], preferred_element_type=F32)), sub)

    def flush(p):
        def last_rows(r, carry):
            scatter(tab_ref[(nu - 1) * tm + r], r, 1 - p).start()
            return carry

        scatter_wait(p)
        lax.fori_loop(0, tm, last_rows, 0)
        scatter_wait(1 - p)

    for p in range(2):
        @pl.when((i < nu) & (lax.rem(i, 2) == p))
        def _(p=p):
            block(p)

        @pl.when((i == nu) & (lax.rem(i, 2) == p))
        def _(p=p):
            flush(p)


def _experts(blk_e, first, nxt, n_used, table, mp, wg, wu, wd, *, tm):
    _, d, f = wg.shape
    sub = d // 2 // LANES
    n_tokens = mp.shape[0] // sub
    n_blocks = table.shape[0] // tm
    return pl.pallas_call(
        functools.partial(_experts_kernel, sub=sub, tm=tm, n_tokens=n_tokens),
        grid_spec=pltpu.PrefetchScalarGridSpec(
            num_scalar_prefetch=5,
            grid=(n_blocks + 1,),
            in_specs=[pl.BlockSpec(memory_space=pl.ANY)] * 4,
            out_specs=pl.BlockSpec(memory_space=pl.ANY),
            scratch_shapes=[pltpu.VMEM(mp.shape, jnp.int32),
                            pltpu.VMEM((d, f), F32), pltpu.VMEM((d, f), F32), pltpu.VMEM((f, d), F32),
                            pltpu.VMEM((d, f), BF16), pltpu.VMEM((d, f), BF16), pltpu.VMEM((f, d), BF16),
                            pltpu.VMEM((tm * sub, LANES), jnp.int32), pltpu.VMEM((tm * sub, LANES), jnp.int32),
                            pltpu.VMEM((tm * sub, LANES), jnp.int32), pltpu.VMEM((tm * sub, LANES), jnp.int32),
                            pltpu.SemaphoreType.DMA((3,)), pltpu.SemaphoreType.DMA(()),
                            pltpu.SemaphoreType.DMA((2,))],
        ),
        out_shape=jax.ShapeDtypeStruct(((TOP_K * n_tokens + tm) * sub, LANES), jnp.int32),
        compiler_params=pltpu.CompilerParams(dimension_semantics=("arbitrary",),
                                             vmem_limit_bytes=EXPERTS_VMEM_LIMIT),
        name="routed_experts",
    )(blk_e, first, nxt, n_used, table, mp, wg, wu, wd)


def _final_kernel(h_ref, gate_ref, p_ref, gp_ref, wgate_ref, wproj_ref, *rest, sub):
    y_refs, o_ref = rest[:TOP_K], rest[TOP_K]
    tm = h_ref.shape[0]
    n = h_ref.shape[1] // 2
    n_grp = COMBINE_ROW_GROUPS if tm % (8 * COMBINE_ROW_GROUPS) == 0 else 1
    gr = tm // n_grp
    rows = [pl.ds(g * gr, gr) for g in range(n_grp)]
    pps = [jnp.dot(p_ref[r, :].astype(BF16), wproj_ref[...], preferred_element_type=F32) for r in rows]
    gates = [gate_ref[r, :] for r in rows]
    acc_los = [h_ref[r, :n] for r in rows]
    acc_his = [h_ref[r, n:] for r in rows]
    for k in range(TOP_K):
        for g in range(n_grp):
            packed = jnp.concatenate([y_refs[k][pl.ds(g * gr * sub + s, gr, stride=sub), :] for s in range(sub)],
                                     axis=1)
            lo, hi = _unpack_halves(packed)
            acc_los[g] = acc_los[g] + gates[g][:, k:k + 1] * lo
            acc_his[g] = acc_his[g] + gates[g][:, k:k + 1] * hi
    h2s = [jnp.concatenate([lo, hi], axis=1) for lo, hi in zip(acc_los, acc_his)]
    nrms = [_rms(h2, gp_ref[...]).astype(BF16) for h2 in h2s]
    gts = [_sigmoid(jnp.dot(nrm, wgate_ref[...], preferred_element_type=F32)) for nrm in nrms]
    for r, h2, gt, pp in zip(rows, h2s, gts, pps):
        o_ref[r, :] = h2 + gt * pp


def _final(h1s, gate, p2, gp, wgate, wproj, yb, *, tm):
    t, d = h1s.shape
    dp = p2.shape[1]
    sub = d // 2 // LANES
    nt = t // tm
    row = lambda i: (i, 0)
    fix = lambda i: (0, 0)
    y_specs = [pl.BlockSpec((tm * sub, LANES), lambda i, k=k: (k * nt + i, 0)) for k in range(TOP_K)]
    return pl.pallas_call(
        functools.partial(_final_kernel, sub=sub),
        grid=(nt,),
        in_specs=[
            pl.BlockSpec((tm, d), row), pl.BlockSpec((tm, IDX_ROWS), row), pl.BlockSpec((tm, dp), row),
            pl.BlockSpec((1, d), fix), pl.BlockSpec((d, d), fix), pl.BlockSpec((dp, d), fix),
        ] + y_specs,
        out_specs=pl.BlockSpec((tm, d), row),
        out_shape=jax.ShapeDtypeStruct((t, d), F32),
        compiler_params=_params(("parallel",)),
        name="combine_ple",
    )(h1s, gate, p2, gp, wgate, wproj, *([yb] * TOP_K))


def _largest_tile(n, cap):
    t = min(n, cap)
    while n % t:
        t //= 2
    return t


def _layer(h, p_l, norm_mix, w_in, q_norm, k_norm, lam_re, lam_im, log_dt, b_re, b_im, c_re, c_im, d_skip,
           w_glu, attn_out_norm, ssm_out_norm, w_out, norm_ffn, w_router, router_bias, w_exp_gate,
           w_exp_up, w_exp_down, w_sh_gate, w_sh_up, w_sh_down, norm_ple, w_ple_gate, w_ple_proj):
    bsz, seq, d = h.shape
    t = bsz * seq
    head_dim = q_norm.shape[-1]
    d_attn = attn_out_norm.shape[-1]
    n_heads = d_attn // head_dim
    n_experts = w_router.shape[-1]
    x2 = h.reshape(t, d)

    tn = d_attn if d_attn % 128 == 0 else head_dim
    scale = 1.0 / math.sqrt(head_dim)
    hg = jnp.concatenate([jnp.tile(q_norm * scale, n_heads), jnp.tile(k_norm, n_heads),
                          jnp.ones((w_in.shape[1] - 2 * d_attn,), F32)])[None, :]
    z, u = _inproj(x2, norm_mix[None, :], w_in.astype(BF16), hg, d_qkv=3 * d_attn,
                   n_norm_tiles=2 * d_attn // tn, head_dim=head_dim, tm=_largest_tile(t, 1024), tn=tn)

    o_attn = _attention(z, batch=bsz, seq=seq, n_heads=n_heads, head_dim=head_dim, blk=_largest_tile(seq, 256))

    n_chunks = seq // S5_CHUNK
    mats = _s5_prepare(lam_re, lam_im, log_dt, b_re, b_im, c_re, c_im, d_skip, n_chunks)
    y_ssm = _s5(u, *mats, n_chunks=n_chunks)

    tm = _largest_tile(t, 256)
    w_out_b = w_out.astype(BF16)
    wr_hi = w_router.astype(BF16)
    wr_lo = (w_router - wr_hi.astype(F32)).astype(BF16)
    h1, idx, gate, loc, counts = _postmix(
        o_attn, y_ssm, x2, w_glu.astype(BF16), w_out_b[:d_attn], w_out_b[d_attn:],
        attn_out_norm[None, :], ssm_out_norm[None, :], norm_ffn[None, :],
        jnp.concatenate([wr_hi, wr_lo], axis=1), router_bias[:, None], tm=_largest_tile(t, 512))

    rb = _largest_tile(t, 256)
    n_pad = t * TOP_K + n_experts * rb
    n_blocks = n_pad // rb
    cnt = counts[:, 0].astype(jnp.int32)
    pcnt = (cnt + rb - 1) // rb * rb
    pend = jnp.cumsum(pcnt)
    pstart = pend - pcnt
    onehot = idx[:, :, None] == jnp.arange(n_experts, dtype=jnp.int32)[None, None, :]
    dest = jnp.sum(jnp.where(onehot, pstart[None, None, :], 0), axis=-1) + loc
    dest = dest.reshape(IDX_ROWS, t // tm, tm).transpose(1, 0, 2).reshape(-1)
    n_used = (pend[-1] // rb).astype(jnp.int32)
    blk_ids = jnp.arange(n_blocks, dtype=jnp.int32)
    starts = (blk_ids * rb)[:, None]
    owner = (starts >= pstart[None, :]) & (starts < pend[None, :])
    blk_e = jnp.sum(jnp.where(owner, jnp.arange(n_experts, dtype=jnp.int32)[None, :], 0), axis=1)
    is_first = jnp.any(owner & (starts == pstart[None, :]), axis=1).astype(jnp.int32)
    first_pos = jnp.where(is_first > 0, blk_ids, n_blocks)
    nxt_pos = jnp.concatenate([lax.cummin(first_pos, axis=0, reverse=True)[1:],
                               jnp.full((1,), n_blocks, jnp.int32)])
    nxt_hit = nxt_pos[:, None] == blk_ids[None, :]
    nxt = jnp.where(nxt_pos < n_blocks, jnp.sum(jnp.where(nxt_hit, blk_e[None, :], 0), axis=1), -1)
    last_e = jnp.sum(jnp.where(blk_ids == n_used - 1, blk_e, 0))
    blk_e = jnp.where(blk_ids < n_used, blk_e, last_e).astype(jnp.int32)

    table0 = TOP_K * t + jnp.arange(n_pad, dtype=jnp.int32) % rb
    h1s, mp, table = _shared(dest, table0, h1, norm_ffn[None, :], w_sh_gate.astype(BF16), w_sh_up.astype(BF16),
                             w_sh_down.astype(BF16), tm=tm)
    yb = _experts(blk_e, is_first, nxt.astype(jnp.int32), n_used[None], table, mp, w_exp_gate, w_exp_up,
                  w_exp_down, tm=rb)
    out = _final(h1s, gate, p_l.reshape(t, -1), norm_ple[None, :], w_ple_gate.astype(BF16),
                 w_ple_proj.astype(BF16), yb, tm=tm)
    return out.reshape(bsz, seq, d)


def kernel(x, p, norm_mix, w_in, q_norm, k_norm, ssm_lam_re, ssm_lam_im, ssm_log_dt, ssm_b_re, ssm_b_im,
           ssm_c_re, ssm_c_im, ssm_d, w_glu, attn_out_norm, ssm_out_norm, w_out, norm_ffn, w_router,
           router_bias, w_exp_gate, w_exp_up, w_exp_down, w_sh_gate, w_sh_up, w_sh_down, norm_ple,
           w_ple_gate, w_ple_proj):
    h = x
    for i in range(p.shape[0]):
        h = _layer(h, p[i], norm_mix[i], w_in[i], q_norm[i], k_norm[i], ssm_lam_re[i], ssm_lam_im[i],
                   ssm_log_dt[i], ssm_b_re[i], ssm_b_im[i], ssm_c_re[i], ssm_c_im[i], ssm_d[i], w_glu[i],
                   attn_out_norm[i], ssm_out_norm[i], w_out[i], norm_ffn[i], w_router[i], router_bias[i],
                   w_exp_gate[i], w_exp_up[i], w_exp_down[i], w_sh_gate[i], w_sh_up[i], w_sh_down[i],
                   norm_ple[i], w_ple_gate[i], w_ple_proj[i])
    return h
```

```python
import functools
import math

import jax
import jax.numpy as jnp
from jax import lax
from jax.experimental import pallas as pl
from jax.experimental.pallas import tpu as pltpu

NORM_EPS = 1e-6
TOP_K = 6
ROUTED_SCALE = 2.5
IDX_ROWS = 8
LANES = 128
S5_CHUNK = 8
POSTMIX_ROW_GROUPS = 2
S5_ROW_GROUPS = 2
COMBINE_ROW_GROUPS = 2
VMEM_LIMIT = 56 * 1024 * 1024
EXPERTS_VMEM_LIMIT = 62 * 1024 * 1024

F32 = jnp.float32
BF16 = jnp.bfloat16
HIGH_HALF = -65536


def _rms(x, g):
    return x * lax.rsqrt(jnp.mean(x * x, axis=-1, keepdims=True) + NORM_EPS) * g


def _sigmoid(x):
    return 1.0 / (1.0 + jnp.exp(-x))


def _params(sem):
    return pltpu.CompilerParams(dimension_semantics=sem, vmem_limit_bytes=VMEM_LIMIT)


def _pack_halves(x):
    n = x.shape[1] // 2
    lo = lax.bitcast_convert_type(x[:, :n].astype(BF16).astype(F32), jnp.int32)
    hi = lax.bitcast_convert_type(x[:, n:].astype(BF16).astype(F32), jnp.int32)
    return (hi & HIGH_HALF) | lax.shift_right_logical(lo, 16)


def _unpack_halves(p):
    lo = lax.bitcast_convert_type(lax.shift_left(p, 16), F32)
    hi = lax.bitcast_convert_type(p & HIGH_HALF, F32)
    return lo, hi


def _inproj_kernel(x_ref, g_ref, w_ref, hg_ref, o_ref, u_ref, xn_ref, *, n_norm_tiles, n_qkv_tiles, head_dim):
    j = pl.program_id(1)

    @pl.when(j == 0)
    def _():
        xn_ref[...] = _rms(x_ref[...], g_ref[...]).astype(BF16)

    acc = jnp.dot(xn_ref[...], w_ref[...], preferred_element_type=F32)
    tn = acc.shape[1]

    @pl.when(j < n_norm_tiles)
    def _():
        hg = hg_ref[...]
        for h in range(tn // head_dim):
            sl = slice(h * head_dim, (h + 1) * head_dim)
            o_ref[:, sl] = _rms(acc[:, sl], hg[:, sl]).astype(o_ref.dtype)

    @pl.when((j >= n_norm_tiles) & (j < n_qkv_tiles))
    def _():
        o_ref[...] = acc.astype(o_ref.dtype)

    @pl.when(j >= n_qkv_tiles)
    def _():
        u_ref[...] = acc


def _inproj(x2, g, w, hg, *, d_qkv, n_norm_tiles, head_dim, tm, tn):
    t, d = x2.shape
    n = w.shape[1]
    nq = d_qkv // tn
    return pl.pallas_call(
        functools.partial(_inproj_kernel, n_norm_tiles=n_norm_tiles, n_qkv_tiles=nq, head_dim=head_dim),
        grid=(t // tm, n // tn),
        in_specs=[
            pl.BlockSpec((tm, d), lambda i, j: (i, 0)),
            pl.BlockSpec((1, d), lambda i, j: (0, 0)),
            pl.BlockSpec((d, tn), lambda i, j: (0, j)),
            pl.BlockSpec((1, tn), lambda i, j: (0, j)),
        ],
        out_specs=[
            pl.BlockSpec((tm, tn), lambda i, j: (i, jnp.minimum(j, nq - 1))),
            pl.BlockSpec((tm, tn), lambda i, j: (i, jnp.maximum(j - nq, 0))),
        ],
        out_shape=[jax.ShapeDtypeStruct((t, d_qkv), BF16), jax.ShapeDtypeStruct((t, n - d_qkv), F32)],
        scratch_shapes=[pltpu.VMEM((tm, d), BF16)],
        compiler_params=_params(("parallel", "arbitrary")),
        name="inproj",
    )(x2, g, w, hg)


SKIP_AFTER = 110.0


def _attn_kernel(q_ref, k_ref, v_ref, o_ref, *, blk, hd, heads):
    i = pl.program_id(2)
    row = lax.broadcasted_iota(jnp.int32, (blk, blk), 0)
    col = lax.broadcasted_iota(jnp.int32, (blk, blk), 1)
    causal = col < row
    later = jnp.where(row > col, 1.0, 0.0).astype(BF16)
    lanes = [slice(h * hd, (h + 1) * hd) for h in range(heads)]
    qs = [q_ref[:, sl] for sl in lanes]

    def step(kb, accs, runs, diag):
        start = pl.multiple_of(kb * blk, blk)
        zs, sps, inners, new_accs, new_runs = [], [], [], [], []
        for h, sl in enumerate(lanes):
            k = k_ref[pl.ds(start, blk), sl]
            zs.append(lax.dot_general(qs[h], k, (((1,), (1,)), ((), ())), preferred_element_type=F32))
        for z in zs:
            sp = jnp.maximum(z, 0.0) + jnp.log(1.0 + jnp.exp(-jnp.abs(z)))
            sps.append(jnp.where(causal, sp, 0.0) if diag else sp)
        for sp in sps:
            hi = sp.astype(BF16)
            lo = (sp - hi.astype(F32)).astype(BF16)
            inners.append(jnp.dot(hi, later, preferred_element_type=F32)
                          + jnp.dot(lo, later, preferred_element_type=F32))
        for h, sl in enumerate(lanes):
            w = jnp.exp(zs[h] - sps[h] - (inners[h] + runs[h]))
            if diag:
                w = jnp.where(causal, w, 0.0)
            v = v_ref[pl.ds(start, blk), sl]
            new_accs.append(accs[h] + jnp.dot(w.astype(BF16), v, preferred_element_type=F32))
            new_runs.append(runs[h] + jnp.sum(sps[h], axis=-1, keepdims=True))
        return tuple(new_accs), tuple(new_runs)

    def keep_going(runs):
        low = runs[0]
        for r in runs[1:]:
            low = jnp.minimum(low, r)
        return (jnp.min(low) < SKIP_AFTER).astype(jnp.int32)

    accs = tuple(jnp.zeros((blk, hd), F32) for _ in lanes)
    runs = tuple(jnp.zeros((blk, 1), F32) for _ in lanes)
    accs, runs = step(i, accs, runs, True)

    def cond(c):
        n, go, _, _ = c
        return (n < i) & (go > 0)

    def body(c):
        n, _, accs, runs = c
        accs, runs = step(i - 1 - n, accs, runs, False)
        return n + 1, keep_going(runs), accs, runs

    _, _, accs, _ = lax.while_loop(cond, body, (jnp.int32(0), keep_going(runs), accs, runs))
    for h, sl in enumerate(lanes):
        o_ref[:, sl] = accs[h].astype(o_ref.dtype)


def _attention(z, *, batch, seq, n_heads, head_dim, blk):
    nq = seq // blk
    heads = next(c for c in (4, 2, 1) if n_heads % c == 0)
    hw = heads * head_dim
    ng = n_heads // heads
    return pl.pallas_call(
        functools.partial(_attn_kernel, blk=blk, hd=head_dim, heads=heads),
        grid=(batch, ng, nq),
        in_specs=[
            pl.BlockSpec((blk, hw), lambda b, h, i: (b * nq + i, h)),
            pl.BlockSpec((seq, hw), lambda b, h, i: (b, ng + h)),
            pl.BlockSpec((seq, hw), lambda b, h, i: (b, 2 * ng + h)),
        ],
        out_specs=pl.BlockSpec((blk, hw), lambda b, h, i: (b * nq + i, h)),
        out_shape=jax.ShapeDtypeStruct((batch * seq, n_heads * head_dim), BF16),
        compiler_params=_params(("parallel", "parallel", "arbitrary")),
        name="sb_attention",
    )(z, z, z)


def _s5_prepare(lam_re, lam_im, log_dt, b_re, b_im, c_re, c_im, d_skip, n_chunks):
    hp = lax.Precision.HIGHEST
    chunk = S5_CHUNK
    g, p = lam_re.shape
    h = b_re.shape[-1]
    gg = LANES // h
    o = g // gg
    dt = jnp.exp(log_dt)[:, None]
    ks = jnp.arange(chunk + 1, dtype=F32)[None, :, None]
    mag = jnp.exp(lam_re[:, None, :] * dt[:, None, :] * ks)
    ang = lam_im[:, None, :] * dt[:, None, :] * ks
    ak_re = mag * jnp.cos(ang)
    ak_im = mag * jnp.sin(ang)
    nr = ak_re[:, 1] - 1.0
    ni = ak_im[:, 1]
    den = lam_re * lam_re + lam_im * lam_im
    coef_re = ((nr * lam_re + ni * lam_im) / den)[..., None]
    coef_im = ((ni * lam_re - nr * lam_im) / den)[..., None]
    bbar_re = coef_re * b_re - coef_im * b_im
    bbar_im = coef_re * b_im + coef_im * b_re
    ca_re = c_re[:, None] * ak_re[:, :, None, :] - c_im[:, None] * ak_im[:, :, None, :]
    ca_im = c_re[:, None] * ak_im[:, :, None, :] + c_im[:, None] * ak_re[:, :, None, :]

    cam_re = ca_re[:, :chunk].transpose(0, 3, 1, 2).reshape(g, p, chunk * h)
    cam_im = ca_im[:, :chunk].transpose(0, 3, 1, 2).reshape(g, p, chunk * h)
    kt = (jnp.einsum("gph,gpm->ghm", bbar_re, cam_re, precision=hp)
          - jnp.einsum("gph,gpm->ghm", bbar_im, cam_im, precision=hp))
    def same_group(n_rows, row_block, n_cols, col_block):
        r = (jnp.arange(n_rows) // row_block) % gg
        c = (jnp.arange(n_cols) // col_block) % gg
        return (r[:, None] == c[None, :]).astype(F32)

    tile_h = jnp.tile(jnp.eye(h, dtype=F32), (1, gg))
    tile_p = jnp.tile(jnp.eye(p, dtype=F32), (1, gg))
    kt = kt.reshape(o, gg, h, chunk, h).transpose(0, 3, 1, 2, 4).reshape(o, chunk, LANES, h)
    bd = jnp.einsum("okrh,hc->okrc", kt, tile_h, precision=hp) * same_group(LANES, h, LANES, h)
    dvec = d_skip.reshape(o, LANES)
    lags = [bd[:, k] for k in range(chunk)]
    lags[0] = lags[0] + dvec[:, :, None] * jnp.eye(LANES, dtype=F32)[None]
    m_lags = jnp.stack(lags, axis=1)

    def expand_in(ak_part_a, ak_part_b, b_a, b_b, sign):
        rk_a = jnp.stack([ak_part_a[:, chunk - 1 - i] for i in range(chunk)], axis=1)
        rk_b = jnp.stack([ak_part_b[:, chunk - 1 - i] for i in range(chunk)], axis=1)
        val = (rk_a[:, :, None, :] * b_a.transpose(0, 2, 1)[:, None]
               + sign * rk_b[:, :, None, :] * b_b.transpose(0, 2, 1)[:, None])
        val = val.reshape(o, gg, chunk, h, p).transpose(0, 2, 1, 3, 4).reshape(o, chunk * LANES, p)
        wide = jnp.einsum("orp,pc->orc", val.astype(BF16), tile_p.astype(BF16), preferred_element_type=BF16)
        return wide * same_group(chunk * LANES, h, gg * p, p).astype(BF16)

    m_in = (expand_in(ak_re, ak_im, bbar_re, bbar_im, -1.0), expand_in(ak_re, ak_im, bbar_im, bbar_re, 1.0))

    def expand_out(ca):
        val = ca[:, 1:].reshape(o, gg, chunk, h, p).transpose(0, 4, 2, 1, 3).reshape(o, p, chunk * LANES)
        wide = jnp.einsum("pc,opr->ocr", tile_p.astype(BF16), val.astype(BF16), preferred_element_type=BF16)
        return wide * same_group(gg * p, p, chunk * LANES, h).astype(BF16)

    m_out = (expand_out(ca_re), -expand_out(ca_im))

    steps = max(1, int(math.ceil(math.log2(max(n_chunks, 2)))))
    cr, ci = ak_re[:, chunk].reshape(o, gg * p), ak_im[:, chunk].reshape(o, gg * p)
    sc_a, sc_b = [], []
    for _ in range(steps):
        sc_a.append(jnp.concatenate([cr, cr], axis=-1))
        sc_b.append(jnp.concatenate([-ci, ci], axis=-1))
        cr, ci = cr * cr - ci * ci, 2.0 * cr * ci
    return (m_lags.astype(BF16),) + m_in + m_out + (jnp.stack(sc_a, axis=1), jnp.stack(sc_b, axis=1))


def _s5_kernel(u_ref, lag_ref, min_re_ref, min_im_ref, mout_re_ref, mout_im_ref, sa_ref, sb_ref, y_ref, mi_ref,
               *, n_chunks, steps):
    chunk = S5_CHUNK
    n_all = u_ref.shape[0] // chunk
    for i in range(chunk):
        for j in range(chunk):
            blk = lag_ref[j - i] if j >= i else jnp.zeros((LANES, LANES), mi_ref.dtype)
            mi_ref[i * LANES:(i + 1) * LANES, j * LANES:(j + 1) * LANES] = blk
    n_grp = S5_ROW_GROUPS if (n_all // n_chunks) % S5_ROW_GROUPS == 0 else 1
    n = n_all // n_grp
    us = [jnp.concatenate([u_ref[pl.ds(g * n * chunk + i, n, stride=chunk), :] for i in range(chunk)],
                          axis=1).astype(BF16) for g in range(n_grp)]
    ys = [jnp.dot(u, mi_ref[...], preferred_element_type=F32) for u in us]
    xs = [jnp.concatenate([jnp.dot(u, min_re_ref[...], preferred_element_type=F32),
                           jnp.dot(u, min_im_ref[...], preferred_element_type=F32)], axis=1)
          for u in us]
    half = xs[0].shape[1] // 2
    c = lax.rem(lax.broadcasted_iota(jnp.int32, xs[0].shape, 0), n_chunks)
    sa = sa_ref[...]
    sb = sb_ref[...]
    for k in range(steps):
        sh = 1 << k
        sh_xs = [jnp.where(c >= sh, pltpu.roll(x, sh, axis=0), 0.0) for x in xs]
        xs = [x + s * sa[k:k + 1, :] + pltpu.roll(s, half, axis=1) * sb[k:k + 1, :] for x, s in zip(xs, sh_xs)]
    s_ins = [jnp.where(c >= 1, pltpu.roll(x, 1, axis=0), 0.0) for x in xs]
    ys = [y + jnp.dot(s_in[:, :half].astype(BF16), mout_re_ref[...], preferred_element_type=F32)
          + jnp.dot(s_in[:, half:].astype(BF16), mout_im_ref[...], preferred_element_type=F32)
          for y, s_in in zip(ys, s_ins)]
    for g, y in enumerate(ys):
        for i in range(chunk):
            y_ref[pl.ds(g * n * chunk + i, n, stride=chunk), :] = y[:, i * LANES:(i + 1) * LANES]


def _s5(u, m_lags, m_in_re, m_in_im, m_out_re, m_out_im, sc_a, sc_b, *, n_chunks):
    t, d_ssm = u.shape
    o, lh, st = m_in_re.shape
    steps = sc_a.shape[1]
    mat = lambda i: (i, 0, 0)
    return pl.pallas_call(
        functools.partial(_s5_kernel, n_chunks=n_chunks, steps=steps),
        grid=(o,),
        in_specs=[
            pl.BlockSpec((t, LANES), lambda i: (0, i)),
            pl.BlockSpec((None,) + m_lags.shape[1:], lambda i: (i, 0, 0, 0)),
            pl.BlockSpec((None, lh, st), mat), pl.BlockSpec((None, lh, st), mat),
            pl.BlockSpec((None, st, lh), mat), pl.BlockSpec((None, st, lh), mat),
            pl.BlockSpec((None, steps, 2 * st), mat), pl.BlockSpec((None, steps, 2 * st), mat),
        ],
        out_specs=pl.BlockSpec((t, LANES), lambda i: (0, i)),
        out_shape=jax.ShapeDtypeStruct((t, d_ssm), F32),
        scratch_shapes=[pltpu.VMEM((lh, lh), BF16)],
        compiler_params=_params(("parallel",)),
        name="s5_chunked",
    )(u, m_lags, m_in_re, m_in_im, m_out_re, m_out_im, sc_a, sc_b)


def _postmix_kernel(oa_ref, ys_ref, x_ref, wglu_ref, woa_ref, wob_ref, ga_ref, gs_ref, gf_ref,
                    wr_ref, rb_ref, h1_ref, idx_ref, gate_ref, loc_ref, cnt_ref, run_ref, *, n_experts):
    step = pl.program_id(0)

    @pl.when(step == 0)
    def _():
        run_ref[...] = jnp.zeros(run_ref.shape, F32)

    tm = x_ref.shape[0]
    n_groups = POSTMIX_ROW_GROUPS if tm % (8 * POSTMIX_ROW_GROUPS) == 0 else 1
    rows = [pl.ds(g * (tm // n_groups), tm // n_groups) for g in range(n_groups)]

    def dot(a, w_ref):
        return jnp.dot(a, w_ref[...], preferred_element_type=F32)

    ys = [ys_ref[r, :] for r in rows]
    ys = [0.5 * y * (1.0 + jnp.tanh(math.sqrt(2.0 / math.pi) * (y + 0.044715 * (y * y * y)))) for y in ys]
    glu = [dot(y.astype(BF16), wglu_ref) for y in ys]
    ys = [y * _sigmoid(g) for y, g in zip(ys, glu)]
    nas = [_rms(oa_ref[r, :].astype(F32), ga_ref[...]).astype(BF16) for r in rows]
    nss = [_rms(y, gs_ref[...]).astype(BF16) for y in ys]
    h1s = [x_ref[r, :] + dot(na, woa_ref) + dot(ns, wob_ref) for r, na, ns in zip(rows, nas, nss)]
    for r, h1 in zip(rows, h1s):
        h1_ref[r, :] = h1

    ms = [_rms(h1, gf_ref[...]) for h1 in h1s]
    m_his = [m.astype(BF16) for m in ms]
    m_los = [(m - m_hi.astype(F32)).astype(BF16) for m, m_hi in zip(ms, m_his)]
    r_his = [dot(m_hi, wr_ref) for m_hi in m_his]
    r_los = [dot(m_lo, wr_ref) for m_lo in m_los]
    wide_t = jnp.concatenate([(r_hi + r_lo).T for r_hi, r_lo in zip(r_his, r_los)], axis=1)
    logits = wide_t[:n_experts] + wide_t[n_experts:]
    scores = _sigmoid(logits)
    vals = scores + rb_ref[...]
    eidx = lax.broadcasted_iota(jnp.int32, vals.shape, 0).astype(F32)
    picks, firsts, raws = [], [], []
    sel = jnp.zeros(vals.shape, F32)
    for _ in range(TOP_K):
        mx = jnp.max(vals, axis=0, keepdims=True)
        first = jnp.min(jnp.where(vals == mx, eidx, float(n_experts)), axis=0, keepdims=True)
        pick = eidx == first
        picks.append(pick)
        firsts.append(first)
        raws.append(jnp.sum(jnp.where(pick, scores, 0.0), axis=0, keepdims=True))
        vals = jnp.where(pick, -jnp.inf, vals)
        sel = jnp.where(pick, 1.0, sel)
    denom = raws[0]
    for r in raws[1:]:
        denom = denom + r

    r_i = lax.broadcasted_iota(jnp.int32, (tm, tm), 0)
    c_i = lax.broadcasted_iota(jnp.int32, (tm, tm), 1)
    earlier = jnp.where(r_i < c_i, 1.0, 0.0).astype(BF16)
    rank = jnp.dot(sel.astype(BF16), earlier, preferred_element_type=F32) + run_ref[...]
    run_ref[...] = run_ref[...] + jnp.sum(sel, axis=1, keepdims=True)
    cnt_ref[...] = run_ref[...]

    slot8 = lax.broadcasted_iota(jnp.int32, (IDX_ROWS, tm), 0)
    slot_w = lax.broadcasted_iota(jnp.int32, (LANES, tm), 0)
    idx8 = jnp.zeros((IDX_ROWS, tm), F32)
    loc8 = jnp.zeros((IDX_ROWS, tm), F32)
    gate_w = jnp.zeros((LANES, tm), F32)
    for k in range(TOP_K):
        loc = jnp.sum(jnp.where(picks[k], rank, 0.0), axis=0, keepdims=True)
        idx8 = jnp.where(slot8 == k, firsts[k], idx8)
        loc8 = jnp.where(slot8 == k, loc, loc8)
        gate_w = jnp.where(slot_w == k, raws[k] / denom * ROUTED_SCALE, gate_w)
    idx_ref[...] = idx8.astype(jnp.int32)
    loc_ref[...] = loc8.astype(jnp.int32)
    gate_ref[...] = gate_w.T[:, :IDX_ROWS]


def _postmix(oa, ys, x2, wglu, woa, wob, ga, gs, gf, wr, rb, *, tm):
    t, d = x2.shape
    da = oa.shape[1]
    dsm = ys.shape[1]
    e = rb.shape[0]
    row = lambda i: (i, 0)
    col = lambda i: (0, i)
    fix = lambda i: (0, 0)
    return pl.pallas_call(
        functools.partial(_postmix_kernel, n_experts=e),
        grid=(t // tm,),
        in_specs=[
            pl.BlockSpec((tm, da), row), pl.BlockSpec((tm, dsm), row), pl.BlockSpec((tm, d), row),
            pl.BlockSpec((dsm, dsm), fix), pl.BlockSpec((da, d), fix), pl.BlockSpec((dsm, d), fix),
            pl.BlockSpec((1, da), fix), pl.BlockSpec((1, dsm), fix), pl.BlockSpec((1, d), fix),
            pl.BlockSpec((d, 2 * e), fix), pl.BlockSpec((e, 1), fix),
        ],
        out_specs=[
            pl.BlockSpec((tm, d), row), pl.BlockSpec((IDX_ROWS, tm), col), pl.BlockSpec((tm, IDX_ROWS), row),
            pl.BlockSpec((IDX_ROWS, tm), col), pl.BlockSpec((e, 1), fix),
        ],
        out_shape=[
            jax.ShapeDtypeStruct((t, d), F32), jax.ShapeDtypeStruct((IDX_ROWS, t), jnp.int32),
            jax.ShapeDtypeStruct((t, IDX_ROWS), F32), jax.ShapeDtypeStruct((IDX_ROWS, t), jnp.int32),
            jax.ShapeDtypeStruct((e, 1), F32),
        ],
        scratch_shapes=[pltpu.VMEM((e, 1), F32)],
        compiler_params=_params(("arbitrary",)),
        name="postmix_router",
    )(oa, ys, x2, wglu, woa, wob, ga, gs, gf, wr, rb)


def _rows_to_tiles(ref, packed, sub):
    rows = packed.shape[0]
    for s in range(sub):
        ref[pl.ds(s, rows, stride=sub), :] = packed[:, s * LANES:(s + 1) * LANES]


def _tiles_to_rows(ref, rows, sub):
    return jnp.concatenate([ref[pl.ds(s, rows, stride=sub), :] for s in range(sub)], axis=1)


def _tile_rows(ref, r, sub):
    start = r * sub if isinstance(r, int) else pl.multiple_of(r * sub, sub)
    return ref.at[pl.ds(start, sub)]


def _shared_kernel(dest_ref, tab_init_hbm, h1_ref, gf_ref, wg_ref, wu_ref, wd_ref, h1s_ref, mp_ref, tab_ref,
                   sem, *, sub, n_tokens):
    tm = h1_ref.shape[0]
    base = pl.program_id(0) * tm

    @pl.when(pl.program_id(0) == 0)
    def _():
        init = pltpu.make_async_copy(tab_init_hbm, tab_ref, sem)
        init.start()
        init.wait()

    for t in range(tm):
        for k in range(TOP_K):
            tab_ref[dest_ref[k * tm + t]] = k * n_tokens + base + t
    h1 = h1_ref[...]
    m = _rms(h1, gf_ref[...])
    _rows_to_tiles(mp_ref, _pack_halves(m), sub)
    mb = m.astype(BF16)
    a = jnp.dot(mb, wg_ref[...], preferred_element_type=F32)
    b = jnp.dot(mb, wu_ref[...], preferred_element_type=F32)
    hid = (a * _sigmoid(a) * b).astype(BF16)
    h1s_ref[...] = h1 + jnp.dot(hid, wd_ref[...], preferred_element_type=F32)


def _shared(dest, table0, h1, gf, wg, wu, wd, *, tm):
    t, d = h1.shape
    f = wg.shape[1]
    sub = d // 2 // LANES
    row = lambda i: (i, 0)
    fix = lambda i: (0, 0)
    return pl.pallas_call(
        functools.partial(_shared_kernel, sub=sub, n_tokens=t),
        grid=(t // tm,),
        in_specs=[
            pl.BlockSpec((IDX_ROWS * tm,), lambda i: (i,), memory_space=pltpu.SMEM),
            pl.BlockSpec(memory_space=pl.ANY),
            pl.BlockSpec((tm, d), row), pl.BlockSpec((1, d), fix),
            pl.BlockSpec((d, f), fix), pl.BlockSpec((d, f), fix), pl.BlockSpec((f, d), fix),
        ],
        out_specs=[pl.BlockSpec((tm, d), row), pl.BlockSpec((tm * sub, LANES), row),
                   pl.BlockSpec(memory_space=pltpu.SMEM)],
        out_shape=[jax.ShapeDtypeStruct((t, d), F32), jax.ShapeDtypeStruct((t * sub, LANES), jnp.int32),
                   jax.ShapeDtypeStruct(table0.shape, jnp.int32)],
        scratch_shapes=[pltpu.SemaphoreType.DMA(())],
        compiler_params=_params(("arbitrary",)),
        name="shared_table",
    )(dest, table0, h1, gf, wg, wu, wd)


def _experts_kernel(be_ref, first_ref, nxt_ref, nu_ref, tab_ref, mp_hbm, wg_hbm, wu_hbm, wd_hbm, yb_hbm,
                    mp_ref, wgf_ref, wuf_ref, wdf_ref, wgb_ref, wub_ref, wdb_ref, x0_ref, x1_ref, y0_ref, y1_ref,
                    wsem, msem, ssem, *, sub, tm, n_tokens):
    i = pl.program_id(0)
    nu = nu_ref[0]
    xs = (x0_ref, x1_ref)
    ys = (y0_ref, y1_ref)
    spare = TOP_K * n_tokens

    def token_of(e):
        return e & (n_tokens - 1) if n_tokens & (n_tokens - 1) == 0 else lax.rem(e, n_tokens)

    def gather(blk, r, s):
        tok = token_of(tab_ref[blk * tm + r])
        dst = r * sub if isinstance(r, int) else pl.multiple_of(r * sub, sub)
        xs[s][pl.ds(dst, sub), :] = mp_ref[pl.ds(pl.multiple_of(tok * sub, sub), sub), :]

    def scatter(entry, r, s):
        return pltpu.make_async_copy(_tile_rows(ys[s], r, sub), _tile_rows(yb_hbm, entry, sub), ssem.at[s])

    def scatter_wait(s):
        pltpu.make_async_copy(ys[s], yb_hbm.at[pl.ds(0, tm * sub)], ssem.at[s]).wait()

    def weight_copies(e):
        return (pltpu.make_async_copy(wg_hbm.at[e], wgf_ref, wsem.at[0]),
                pltpu.make_async_copy(wu_hbm.at[e], wuf_ref, wsem.at[1]),
                pltpu.make_async_copy(wd_hbm.at[e], wdf_ref, wsem.at[2]))

    @pl.when(i == 0)
    def _():
        for c in weight_copies(be_ref[0]):
            c.start()
        resident = pltpu.make_async_copy(mp_hbm, mp_ref, msem)
        resident.start()
        resident.wait()

        def first_rows(r, carry):
            gather(0, r, 0)
            return carry

        lax.fori_loop(0, tm, first_rows, 0)
        y1_ref[...] = jnp.zeros(y1_ref.shape, y1_ref.dtype)

    @pl.when((i < nu) & (first_ref[jnp.minimum(i, first_ref.shape[0] - 1)] != 0))
    def _():
        for c in weight_copies(be_ref[i]):
            c.wait()
        wgb_ref[...] = wgf_ref[...].astype(BF16)
        wub_ref[...] = wuf_ref[...].astype(BF16)
        wdb_ref[...] = wdf_ref[...].astype(BF16)

        @pl.when(nxt_ref[i] >= 0)
        def _():
            for c in weight_copies(nxt_ref[i]):
                c.start()

    def block(p):
        @pl.when(i >= 1)
        def _():
            scatter_wait(p)

        nb = jnp.minimum(i + 1, nu - 1)
        pb = jnp.maximum(i - 1, 0)
        for r in range(tm):
            gather(nb, r, 1 - p)
        for r in range(tm):
            entry = jnp.where(i >= 1, tab_ref[pb * tm + r], spare + r)
            scatter(entry, r, 1 - p).start()
        lo, hi = _unpack_halves(_tiles_to_rows(xs[p], tm, sub))
        lo = lo.astype(BF16)
        hi = hi.astype(BF16)
        n = lo.shape[1]
        a = (jnp.dot(lo, wgb_ref[:n, :], preferred_element_type=F32)
             + jnp.dot(hi, wgb_ref[n:, :], preferred_element_type=F32))
        b = (jnp.dot(lo, wub_ref[:n, :], preferred_element_type=F32)
             + jnp.dot(hi, wub_ref[n:, :], preferred_element_type=F32))
        hid = (a * _sigmoid(a) * b).astype(BF16)
        _rows_to_tiles(ys[p], _pack_halves(jnp.dot(hid, wdb_ref[...], preferred_element_type=F32)), sub)

    def flush(p):
        def last_rows(r, carry):
            scatter(tab_ref[(nu - 1) * tm + r], r, 1 - p).start()
            return carry

        scatter_wait(p)
        lax.fori_loop(0, tm, last_rows, 0)
        scatter_wait(1 - p)

    for p in range(2):
        @pl.when((i < nu) & (lax.rem(i, 2) == p))
        def _(p=p):
            block(p)

        @pl.when((i == nu) & (lax.rem(i, 2) == p))
        def _(p=p):
            flush(p)


def _experts(blk_e, first, nxt, n_used, table, mp, wg, wu, wd, *, tm):
    _, d, f = wg.shape
    sub = d // 2 // LANES
    n_tokens = mp.shape[0] // sub
    n_blocks = table.shape[0] // tm
    return pl.pallas_call(
        functools.partial(_experts_kernel, sub=sub, tm=tm, n_tokens=n_tokens),
        grid_spec=pltpu.PrefetchScalarGridSpec(
            num_scalar_prefetch=5,
            grid=(n_blocks + 1,),
            in_specs=[pl.BlockSpec(memory_space=pl.ANY)] * 4,
            out_specs=pl.BlockSpec(memory_space=pl.ANY),
            scratch_shapes=[pltpu.VMEM(mp.shape, jnp.int32),
                            pltpu.VMEM((d, f), F32), pltpu.VMEM((d, f), F32), pltpu.VMEM((f, d), F32),
                            pltpu.VMEM((d, f), BF16), pltpu.VMEM((d, f), BF16), pltpu.VMEM((f, d), BF16),
                            pltpu.VMEM((tm * sub, LANES), jnp.int32), pltpu.VMEM((tm * sub, LANES), jnp.int32),
                            pltpu.VMEM((tm * sub, LANES), jnp.int32), pltpu.VMEM((tm * sub, LANES), jnp.int32),
                            pltpu.SemaphoreType.DMA((3,)), pltpu.SemaphoreType.DMA(()),
                            pltpu.SemaphoreType.DMA((2,))],
        ),
        out_shape=jax.ShapeDtypeStruct(((TOP_K * n_tokens + tm) * sub, LANES), jnp.int32),
        compiler_params=pltpu.CompilerParams(dimension_semantics=("arbitrary",),
                                             vmem_limit_bytes=EXPERTS_VMEM_LIMIT),
        name="routed_experts",
    )(blk_e, first, nxt, n_used, table, mp, wg, wu, wd)


def _final_kernel(h_ref, gate_ref, p_ref, gp_ref, wgate_ref, wproj_ref, *rest, sub):
    y_refs, o_ref = rest[:TOP_K], rest[TOP_K]
    tm = h_ref.shape[0]
    n = h_ref.shape[1] // 2
    n_grp = COMBINE_ROW_GROUPS if tm % (8 * COMBINE_ROW_GROUPS) == 0 else 1
    gr = tm // n_grp
    rows = [pl.ds(g * gr, gr) for g in range(n_grp)]
    pps = [jnp.dot(p_ref[r, :].astype(BF16), wproj_ref[...], preferred_element_type=F32) for r in rows]
    gates = [gate_ref[r, :] for r in rows]
    acc_los = [h_ref[r, :n] for r in rows]
    acc_his = [h_ref[r, n:] for r in rows]
    for k in range(TOP_K):
        for g in range(n_grp):
            packed = jnp.concatenate([y_refs[k][pl.ds(g * gr * sub + s, gr, stride=sub), :] for s in range(sub)],
                                     axis=1)
            lo, hi = _unpack_halves(packed)
            acc_los[g] = acc_los[g] + gates[g][:, k:k + 1] * lo
            acc_his[g] = acc_his[g] + gates[g][:, k:k + 1] * hi
    h2s = [jnp.concatenate([lo, hi], axis=1) for lo, hi in zip(acc_los, acc_his)]
    nrms = [_rms(h2, gp_ref[...]).astype(BF16) for h2 in h2s]
    gts = [_sigmoid(jnp.dot(nrm, wgate_ref[...], preferred_element_type=F32)) for nrm in nrms]
    for r, h2, gt, pp in zip(rows, h2s, gts, pps):
        o_ref[r, :] = h2 + gt * pp


def _final(h1s, gate, p2, gp, wgate, wproj, yb, *, tm):
    t, d = h1s.shape
    dp = p2.shape[1]
    sub = d // 2 // LANES
    nt = t // tm
    row = lambda i: (i, 0)
    fix = lambda i: (0, 0)
    y_specs = [pl.BlockSpec((tm * sub, LANES), lambda i, k=k: (k * nt + i, 0)) for k in range(TOP_K)]
    return pl.pallas_call(
        functools.partial(_final_kernel, sub=sub),
        grid=(nt,),
        in_specs=[
            pl.BlockSpec((tm, d), row), pl.BlockSpec((tm, IDX_ROWS), row), pl.BlockSpec((tm, dp), row),
            pl.BlockSpec((1, d), fix), pl.BlockSpec((d, d), fix), pl.BlockSpec((dp, d), fix),
        ] + y_specs,
        out_specs=pl.BlockSpec((tm, d), row),
        out_shape=jax.ShapeDtypeStruct((t, d), F32),
        compiler_params=_params(("parallel",)),
        name="combine_ple",
    )(h1s, gate, p2, gp, wgate, wproj, *([yb] * TOP_K))


def _largest_tile(n, cap):
    t = min(n, cap)
    while n % t:
        t //= 2
    return t


def _layer(h, p_l, norm_mix, w_in, q_norm, k_norm, lam_re, lam_im, log_dt, b_re, b_im, c_re, c_im, d_skip,
           w_glu, attn_out_norm, ssm_out_norm, w_out, norm_ffn, w_router, router_bias, w_exp_gate,
           w_exp_up, w_exp_down, w_sh_gate, w_sh_up, w_sh_down, norm_ple, w_ple_gate, w_ple_proj):
    bsz, seq, d = h.shape
    t = bsz * seq
    head_dim = q_norm.shape[-1]
    d_attn = attn_out_norm.shape[-1]
    n_heads = d_attn // head_dim
    n_experts = w_router.shape[-1]
    x2 = h.reshape(t, d)

    tn = d_attn if d_attn % 128 == 0 else head_dim
    scale = 1.0 / math.sqrt(head_dim)
    hg = jnp.concatenate([jnp.tile(q_norm * scale, n_heads), jnp.tile(k_norm, n_heads),
                          jnp.ones((w_in.shape[1] - 2 * d_attn,), F32)])[None, :]
    z, u = _inproj(x2, norm_mix[None, :], w_in.astype(BF16), hg, d_qkv=3 * d_attn,
                   n_norm_tiles=2 * d_attn // tn, head_dim=head_dim, tm=_largest_tile(t, 1024), tn=tn)

    o_attn = _attention(z, batch=bsz, seq=seq, n_heads=n_heads, head_dim=head_dim, blk=_largest_tile(seq, 256))

    n_chunks = seq // S5_CHUNK
    mats = _s5_prepare(lam_re, lam_im, log_dt, b_re, b_im, c_re, c_im, d_skip, n_chunks)
    y_ssm = _s5(u, *mats, n_chunks=n_chunks)

    tm = _largest_tile(t, 256)
    w_out_b = w_out.astype(BF16)
    wr_hi = w_router.astype(BF16)
    wr_lo = (w_router - wr_hi.astype(F32)).astype(BF16)
    h1, idx, gate, loc, counts = _postmix(
        o_attn, y_ssm, x2, w_glu.astype(BF16), w_out_b[:d_attn], w_out_b[d_attn:],
        attn_out_norm[None, :], ssm_out_norm[None, :], norm_ffn[None, :],
        jnp.concatenate([wr_hi, wr_lo], axis=1), router_bias[:, None], tm=_largest_tile(t, 512))

    rb = _largest_tile(t, 256)
    n_pad = t * TOP_K + n_experts * rb
    n_blocks = n_pad // rb
    cnt = counts[:, 0].astype(jnp.int32)
    pcnt = (cnt + rb - 1) // rb * rb
    pend = jnp.cumsum(pcnt)
    pstart = pend - pcnt
    onehot = idx[:, :, None] == jnp.arange(n_experts, dtype=jnp.int32)[None, None, :]
    dest = jnp.sum(jnp.where(onehot, pstart[None, None, :], 0), axis=-1) + loc
    dest = dest.reshape(IDX_ROWS, t // tm, tm).transpose(1, 0, 2).reshape(-1)
    n_used = (pend[-1] // rb).astype(jnp.int32)
    blk_ids = jnp.arange(n_blocks, dtype=jnp.int32)
    starts = (blk_ids * rb)[:, None]
    owner = (starts >= pstart[None, :]) & (starts < pend[None, :])
    blk_e = jnp.sum(jnp.where(owner, jnp.arange(n_experts, dtype=jnp.int32)[None, :], 0), axis=1)
    is_first = jnp.any(owner & (starts == pstart[None, :]), axis=1).astype(jnp.int32)
    first_pos = jnp.where(is_first > 0, blk_ids, n_blocks)
    nxt_pos = jnp.concatenate([lax.cummin(first_pos, axis=0, reverse=True)[1:],
                               jnp.full((1,), n_blocks, jnp.int32)])
    nxt_hit = nxt_pos[:, None] == blk_ids[None, :]
    nxt = jnp.where(nxt_pos < n_blocks, jnp.sum(jnp.where(nxt_hit, blk_e[None, :], 0), axis=1), -1)
    last_e = jnp.sum(jnp.where(blk_ids == n_used - 1, blk_e, 0))
    blk_e = jnp.where(blk_ids < n_used, blk_e, last_e).astype(jnp.int32)

    table0 = TOP_K * t + jnp.arange(n_pad, dtype=jnp.int32) % rb
    h1s, mp, table = _shared(dest, table0, h1, norm_ffn[None, :], w_sh_gate.astype(BF16), w_sh_up.astype(BF16),
                             w_sh_down.astype(BF16), tm=tm)
    yb = _experts(blk_e, is_first, nxt.astype(jnp.int32), n_used[None], table, mp, w_exp_gate, w_exp_up,
                  w_exp_down, tm=rb)
    out = _final(h1s, gate, p_l.reshape(t, -1), norm_ple[None, :], w_ple_gate.astype(BF16),
                 w_ple_proj.astype(BF16), yb, tm=tm)
    return out.reshape(bsz, seq, d)


def kernel(x, p, norm_mix, w_in, q_norm, k_norm, ssm_lam_re, ssm_lam_im, ssm_log_dt, ssm_b_re, ssm_b_im,
           ssm_c_re, ssm_c_im, ssm_d, w_glu, attn_out_norm, ssm_out_norm, w_out, norm_ffn, w_router,
           router_bias, w_exp_gate, w_exp_up, w_exp_down, w_sh_gate, w_sh_up, w_sh_down, norm_ple,
           w_ple_gate, w_ple_proj):
    h = x
    for i in range(p.shape[0]):
        h = _layer(h, p[i], norm_mix[i], w_in[i], q_norm[i], k_norm[i], ssm_lam_re[i], ssm_lam_im[i],
                   ssm_log_dt[i], ssm_b_re[i], ssm_b_im[i], ssm_c_re[i], ssm_c_im[i], ssm_d[i], w_glu[i],
                   attn_out_norm[i], ssm_out_norm[i], w_out[i], norm_ffn[i], w_router[i], router_bias[i],
                   w_exp_gate[i], w_exp_up[i], w_exp_down[i], w_sh_gate[i], w_sh_up[i], w_sh_down[i],
                   norm_ple[i], w_ple_gate[i], w_ple_proj[i])
    return h
```

```python
import functools
import math

import jax
import jax.numpy as jnp
from jax import lax
from jax.experimental import pallas as pl
from jax.experimental.pallas import tpu as pltpu

NORM_EPS = 1e-6
TOP_K = 6
ROUTED_SCALE = 2.5
IDX_ROWS = 8
LANES = 128
S5_CHUNK = 8
POSTMIX_ROW_GROUPS = 2
S5_ROW_GROUPS = 2
COMBINE_ROW_GROUPS = 2
VMEM_LIMIT = 56 * 1024 * 1024
EXPERTS_VMEM_LIMIT = 62 * 1024 * 1024

F32 = jnp.float32
BF16 = jnp.bfloat16
HIGH_HALF = -65536


def _rms(x, g):
    return x * lax.rsqrt(jnp.mean(x * x, axis=-1, keepdims=True) + NORM_EPS) * g


def _sigmoid(x):
    return 1.0 / (1.0 + jnp.exp(-x))


def _params(sem):
    return pltpu.CompilerParams(dimension_semantics=sem, vmem_limit_bytes=VMEM_LIMIT)


def _pack_halves(x):
    n = x.shape[1] // 2
    lo = lax.bitcast_convert_type(x[:, :n].astype(BF16).astype(F32), jnp.int32)
    hi = lax.bitcast_convert_type(x[:, n:].astype(BF16).astype(F32), jnp.int32)
    return (hi & HIGH_HALF) | lax.shift_right_logical(lo, 16)


def _unpack_halves(p):
    lo = lax.bitcast_convert_type(lax.shift_left(p, 16), F32)
    hi = lax.bitcast_convert_type(p & HIGH_HALF, F32)
    return lo, hi


def _inproj_kernel(x_ref, g_ref, w_ref, hg_ref, o_ref, u_ref, xn_ref, *, n_norm_tiles, n_qkv_tiles, head_dim):
    j = pl.program_id(1)

    @pl.when(j == 0)
    def _():
        xn_ref[...] = _rms(x_ref[...], g_ref[...]).astype(BF16)

    acc = jnp.dot(xn_ref[...], w_ref[...], preferred_element_type=F32)
    tn = acc.shape[1]

    @pl.when(j < n_norm_tiles)
    def _():
        hg = hg_ref[...]
        for h in range(tn // head_dim):
            sl = slice(h * head_dim, (h + 1) * head_dim)
            o_ref[:, sl] = _rms(acc[:, sl], hg[:, sl]).astype(o_ref.dtype)

    @pl.when((j >= n_norm_tiles) & (j < n_qkv_tiles))
    def _():
        o_ref[...] = acc.astype(o_ref.dtype)

    @pl.when(j >= n_qkv_tiles)
    def _():
        u_ref[...] = acc


def _inproj(x2, g, w, hg, *, d_qkv, n_norm_tiles, head_dim, tm, tn):
    t, d = x2.shape
    n = w.shape[1]
    nq = d_qkv // tn
    return pl.pallas_call(
        functools.partial(_inproj_kernel, n_norm_tiles=n_norm_tiles, n_qkv_tiles=nq, head_dim=head_dim),
        grid=(t // tm, n // tn),
        in_specs=[
            pl.BlockSpec((tm, d), lambda i, j: (i, 0)),
            pl.BlockSpec((1, d), lambda i, j: (0, 0)),
            pl.BlockSpec((d, tn), lambda i, j: (0, j)),
            pl.BlockSpec((1, tn), lambda i, j: (0, j)),
        ],
        out_specs=[
            pl.BlockSpec((tm, tn), lambda i, j: (i, jnp.minimum(j, nq - 1))),
            pl.BlockSpec((tm, tn), lambda i, j: (i, jnp.maximum(j - nq, 0))),
        ],
        out_shape=[jax.ShapeDtypeStruct((t, d_qkv), BF16), jax.ShapeDtypeStruct((t, n - d_qkv), F32)],
        scratch_shapes=[pltpu.VMEM((tm, d), BF16)],
        compiler_params=_params(("parallel", "arbitrary")),
        name="inproj",
    )(x2, g, w, hg)


SKIP_AFTER = 110.0


def _attn_kernel(q_ref, k_ref, v_ref, o_ref, *, blk, hd, heads):
    i = pl.program_id(2)
    row = lax.broadcasted_iota(jnp.int32, (blk, blk), 0)
    col = lax.broadcasted_iota(jnp.int32, (blk, blk), 1)
    causal = col < row
    later = jnp.where(row > col, 1.0, 0.0).astype(BF16)
    lanes = [slice(h * hd, (h + 1) * hd) for h in range(heads)]
    qs = [q_ref[:, sl] for sl in lanes]

    def step(kb, accs, runs, diag):
        start = pl.multiple_of(kb * blk, blk)
        zs, sps, inners, new_accs, new_runs = [], [], [], [], []
        for h, sl in enumerate(lanes):
            k = k_ref[pl.ds(start, blk), sl]
            zs.append(lax.dot_general(qs[h], k, (((1,), (1,)), ((), ())), preferred_element_type=F32))
        for z in zs:
            sp = jnp.maximum(z, 0.0) + jnp.log(1.0 + jnp.exp(-jnp.abs(z)))
            sps.append(jnp.where(causal, sp, 0.0) if diag else sp)
        for sp in sps:
            hi = sp.astype(BF16)
            lo = (sp - hi.astype(F32)).astype(BF16)
            inners.append(jnp.dot(hi, later, preferred_element_type=F32)
                          + jnp.dot(lo, later, preferred_element_type=F32))
        for h, sl in enumerate(lanes):
            w = jnp.exp(zs[h] - sps[h] - (inners[h] + runs[h]))
            if diag:
                w = jnp.where(causal, w, 0.0)
            v = v_ref[pl.ds(start, blk), sl]
            new_accs.append(accs[h] + jnp.dot(w.astype(BF16), v, preferred_element_type=F32))
            new_runs.append(runs[h] + jnp.sum(sps[h], axis=-1, keepdims=True))
        return tuple(new_accs), tuple(new_runs)

    def keep_going(runs):
        low = runs[0]
        for r in runs[1:]:
            low = jnp.minimum(low, r)
        return (jnp.min(low) < SKIP_AFTER).astype(jnp.int32)

    accs = tuple(jnp.zeros((blk, hd), F32) for _ in lanes)
    runs = tuple(jnp.zeros((blk, 1), F32) for _ in lanes)
    accs, runs = step(i, accs, runs, True)

    def cond(c):
        n, go, _, _ = c
        return (n < i) & (go > 0)

    def body(c):
        n, _, accs, runs = c
        accs, runs = step(i - 1 - n, accs, runs, False)
        return n + 1, keep_going(runs), accs, runs

    _, _, accs, _ = lax.while_loop(cond, body, (jnp.int32(0), keep_going(runs), accs, runs))
    for h, sl in enumerate(lanes):
        o_ref[:, sl] = accs[h].astype(o_ref.dtype)


def _attention(z, *, batch, seq, n_heads, head_dim, blk):
    nq = seq // blk
    heads = next(c for c in (8, 4, 2, 1) if n_heads % c == 0)
    hw = heads * head_dim
    ng = n_heads // heads
    return pl.pallas_call(
        functools.partial(_attn_kernel, blk=blk, hd=head_dim, heads=heads),
        grid=(batch, ng, nq),
        in_specs=[
            pl.BlockSpec((blk, hw), lambda b, h, i: (b * nq + i, h)),
            pl.BlockSpec((seq, hw), lambda b, h, i: (b, ng + h)),
            pl.BlockSpec((seq, hw), lambda b, h, i: (b, 2 * ng + h)),
        ],
        out_specs=pl.BlockSpec((blk, hw), lambda b, h, i: (b * nq + i, h)),
        out_shape=jax.ShapeDtypeStruct((batch * seq, n_heads * head_dim), BF16),
        compiler_params=_params(("parallel", "parallel", "arbitrary")),
        name="sb_attention",
    )(z, z, z)


def _s5_prepare(lam_re, lam_im, log_dt, b_re, b_im, c_re, c_im, d_skip, n_chunks):
    hp = lax.Precision.HIGHEST
    chunk = S5_CHUNK
    g, p = lam_re.shape
    h = b_re.shape[-1]
    gg = LANES // h
    o = g // gg
    dt = jnp.exp(log_dt)[:, None]
    ks = jnp.arange(chunk + 1, dtype=F32)[None, :, None]
    mag = jnp.exp(lam_re[:, None, :] * dt[:, None, :] * ks)
    ang = lam_im[:, None, :] * dt[:, None, :] * ks
    ak_re = mag * jnp.cos(ang)
    ak_im = mag * jnp.sin(ang)
    nr = ak_re[:, 1] - 1.0
    ni = ak_im[:, 1]
    den = lam_re * lam_re + lam_im * lam_im
    coef_re = ((nr * lam_re + ni * lam_im) / den)[..., None]
    coef_im = ((ni * lam_re - nr * lam_im) / den)[..., None]
    bbar_re = coef_re * b_re - coef_im * b_im
    bbar_im = coef_re * b_im + coef_im * b_re
    ca_re = c_re[:, None] * ak_re[:, :, None, :] - c_im[:, None] * ak_im[:, :, None, :]
    ca_im = c_re[:, None] * ak_im[:, :, None, :] + c_im[:, None] * ak_re[:, :, None, :]

    cam_re = ca_re[:, :chunk].transpose(0, 3, 1, 2).reshape(g, p, chunk * h)
    cam_im = ca_im[:, :chunk].transpose(0, 3, 1, 2).reshape(g, p, chunk * h)
    kt = (jnp.einsum("gph,gpm->ghm", bbar_re, cam_re, precision=hp)
          - jnp.einsum("gph,gpm->ghm", bbar_im, cam_im, precision=hp))
    def same_group(n_rows, row_block, n_cols, col_block):
        r = (jnp.arange(n_rows) // row_block) % gg
        c = (jnp.arange(n_cols) // col_block) % gg
        return (r[:, None] == c[None, :]).astype(F32)

    tile_h = jnp.tile(jnp.eye(h, dtype=F32), (1, gg))
    tile_p = jnp.tile(jnp.eye(p, dtype=F32), (1, gg))
    kt = kt.reshape(o, gg, h, chunk, h).transpose(0, 3, 1, 2, 4).reshape(o, chunk, LANES, h)
    bd = jnp.einsum("okrh,hc->okrc", kt, tile_h, precision=hp) * same_group(LANES, h, LANES, h)
    dvec = d_skip.reshape(o, LANES)
    lags = [bd[:, k] for k in range(chunk)]
    lags[0] = lags[0] + dvec[:, :, None] * jnp.eye(LANES, dtype=F32)[None]
    m_lags = jnp.stack(lags, axis=1)

    def expand_in(ak_part_a, ak_part_b, b_a, b_b, sign):
        rk_a = jnp.stack([ak_part_a[:, chunk - 1 - i] for i in range(chunk)], axis=1)
        rk_b = jnp.stack([ak_part_b[:, chunk - 1 - i] for i in range(chunk)], axis=1)
        val = (rk_a[:, :, None, :] * b_a.transpose(0, 2, 1)[:, None]
               + sign * rk_b[:, :, None, :] * b_b.transpose(0, 2, 1)[:, None])
        val = val.reshape(o, gg, chunk, h, p).transpose(0, 2, 1, 3, 4).reshape(o, chunk * LANES, p)
        wide = jnp.einsum("orp,pc->orc", val.astype(BF16), tile_p.astype(BF16), preferred_element_type=BF16)
        return wide * same_group(chunk * LANES, h, gg * p, p).astype(BF16)

    m_in = (expand_in(ak_re, ak_im, bbar_re, bbar_im, -1.0), expand_in(ak_re, ak_im, bbar_im, bbar_re, 1.0))

    def expand_out(ca):
        val = ca[:, 1:].reshape(o, gg, chunk, h, p).transpose(0, 4, 2, 1, 3).reshape(o, p, chunk * LANES)
        wide = jnp.einsum("pc,opr->ocr", tile_p.astype(BF16), val.astype(BF16), preferred_element_type=BF16)
        return wide * same_group(gg * p, p, chunk * LANES, h).astype(BF16)

    m_out = (expand_out(ca_re), -expand_out(ca_im))

    steps = max(1, int(math.ceil(math.log2(max(n_chunks, 2)))))
    cr, ci = ak_re[:, chunk].reshape(o, gg * p), ak_im[:, chunk].reshape(o, gg * p)
    sc_a, sc_b = [], []
    for _ in range(steps):
        sc_a.append(jnp.concatenate([cr, cr], axis=-1))
        sc_b.append(jnp.concatenate([-ci, ci], axis=-1))
        cr, ci = cr * cr - ci * ci, 2.0 * cr * ci
    return (m_lags.astype(BF16),) + m_in + m_out + (jnp.stack(sc_a, axis=1), jnp.stack(sc_b, axis=1))


def _s5_kernel(u_ref, lag_ref, min_re_ref, min_im_ref, mout_re_ref, mout_im_ref, sa_ref, sb_ref, y_ref, mi_ref,
               *, n_chunks, steps):
    chunk = S5_CHUNK
    n_all = u_ref.shape[0] // chunk
    for i in range(chunk):
        for j in range(chunk):
            blk = lag_ref[j - i] if j >= i else jnp.zeros((LANES, LANES), mi_ref.dtype)
            mi_ref[i * LANES:(i + 1) * LANES, j * LANES:(j + 1) * LANES] = blk
    n_grp = S5_ROW_GROUPS if (n_all // n_chunks) % S5_ROW_GROUPS == 0 else 1
    n = n_all // n_grp
    us = [jnp.concatenate([u_ref[pl.ds(g * n * chunk + i, n, stride=chunk), :] for i in range(chunk)],
                          axis=1).astype(BF16) for g in range(n_grp)]
    ys = [jnp.dot(u, mi_ref[...], preferred_element_type=F32) for u in us]
    xs = [jnp.concatenate([jnp.dot(u, min_re_ref[...], preferred_element_type=F32),
                           jnp.dot(u, min_im_ref[...], preferred_element_type=F32)], axis=1)
          for u in us]
    half = xs[0].shape[1] // 2
    c = lax.rem(lax.broadcasted_iota(jnp.int32, xs[0].shape, 0), n_chunks)
    sa = sa_ref[...]
    sb = sb_ref[...]
    for k in range(steps):
        sh = 1 << k
        sh_xs = [jnp.where(c >= sh, pltpu.roll(x, sh, axis=0), 0.0) for x in xs]
        xs = [x + s * sa[k:k + 1, :] + pltpu.roll(s, half, axis=1) * sb[k:k + 1, :] for x, s in zip(xs, sh_xs)]
    s_ins = [jnp.where(c >= 1, pltpu.roll(x, 1, axis=0), 0.0) for x in xs]
    ys = [y + jnp.dot(s_in[:, :half].astype(BF16), mout_re_ref[...], preferred_element_type=F32)
          + jnp.dot(s_in[:, half:].astype(BF16), mout_im_ref[...], preferred_element_type=F32)
          for y, s_in in zip(ys, s_ins)]
    for g, y in enumerate(ys):
        for i in range(chunk):
            y_ref[pl.ds(g * n * chunk + i, n, stride=chunk), :] = y[:, i * LANES:(i + 1) * LANES]


def _s5(u, m_lags, m_in_re, m_in_im, m_out_re, m_out_im, sc_a, sc_b, *, n_chunks):
    t, d_ssm = u.shape
    o, lh, st = m_in_re.shape
    steps = sc_a.shape[1]
    mat = lambda i: (i, 0, 0)
    return pl.pallas_call(
        functools.partial(_s5_kernel, n_chunks=n_chunks, steps=steps),
        grid=(o,),
        in_specs=[
            pl.BlockSpec((t, LANES), lambda i: (0, i)),
            pl.BlockSpec((None,) + m_lags.shape[1:], lambda i: (i, 0, 0, 0)),
            pl.BlockSpec((None, lh, st), mat), pl.BlockSpec((None, lh, st), mat),
            pl.BlockSpec((None, st, lh), mat), pl.BlockSpec((None, st, lh), mat),
            pl.BlockSpec((None, steps, 2 * st), mat), pl.BlockSpec((None, steps, 2 * st), mat),
        ],
        out_specs=pl.BlockSpec((t, LANES), lambda i: (0, i)),
        out_shape=jax.ShapeDtypeStruct((t, d_ssm), F32),
        scratch_shapes=[pltpu.VMEM((lh, lh), BF16)],
        compiler_params=_params(("parallel",)),
        name="s5_chunked",
    )(u, m_lags, m_in_re, m_in_im, m_out_re, m_out_im, sc_a, sc_b)


def _postmix_kernel(oa_ref, ys_ref, x_ref, wglu_ref, woa_ref, wob_ref, ga_ref, gs_ref, gf_ref,
                    wr_ref, rb_ref, h1_ref, idx_ref, gate_ref, loc_ref, cnt_ref, run_ref, *, n_experts):
    step = pl.program_id(0)

    @pl.when(step == 0)
    def _():
        run_ref[...] = jnp.zeros(run_ref.shape, F32)

    tm = x_ref.shape[0]
    n_groups = POSTMIX_ROW_GROUPS if tm % (8 * POSTMIX_ROW_GROUPS) == 0 else 1
    rows = [pl.ds(g * (tm // n_groups), tm // n_groups) for g in range(n_groups)]

    def dot(a, w_ref):
        return jnp.dot(a, w_ref[...], preferred_element_type=F32)

    ys = [ys_ref[r, :] for r in rows]
    ys = [0.5 * y * (1.0 + jnp.tanh(math.sqrt(2.0 / math.pi) * (y + 0.044715 * (y * y * y)))) for y in ys]
    glu = [dot(y.astype(BF16), wglu_ref) for y in ys]
    ys = [y * _sigmoid(g) for y, g in zip(ys, glu)]
    nas = [_rms(oa_ref[r, :].astype(F32), ga_ref[...]).astype(BF16) for r in rows]
    nss = [_rms(y, gs_ref[...]).astype(BF16) for y in ys]
    h1s = [x_ref[r, :] + dot(na, woa_ref) + dot(ns, wob_ref) for r, na, ns in zip(rows, nas, nss)]
    for r, h1 in zip(rows, h1s):
        h1_ref[r, :] = h1

    ms = [_rms(h1, gf_ref[...]) for h1 in h1s]
    m_his = [m.astype(BF16) for m in ms]
    m_los = [(m - m_hi.astype(F32)).astype(BF16) for m, m_hi in zip(ms, m_his)]
    r_his = [dot(m_hi, wr_ref) for m_hi in m_his]
    r_los = [dot(m_lo, wr_ref) for m_lo in m_los]
    wide_t = jnp.concatenate([(r_hi + r_lo).T for r_hi, r_lo in zip(r_his, r_los)], axis=1)
    logits = wide_t[:n_experts] + wide_t[n_experts:]
    scores = _sigmoid(logits)
    vals = scores + rb_ref[...]
    eidx = lax.broadcasted_iota(jnp.int32, vals.shape, 0).astype(F32)
    picks, firsts, raws = [], [], []
    sel = jnp.zeros(vals.shape, F32)
    for _ in range(TOP_K):
        mx = jnp.max(vals, axis=0, keepdims=True)
        first = jnp.min(jnp.where(vals == mx, eidx, float(n_experts)), axis=0, keepdims=True)
        pick = eidx == first
        picks.append(pick)
        firsts.append(first)
        raws.append(jnp.sum(jnp.where(pick, scores, 0.0), axis=0, keepdims=True))
        vals = jnp.where(pick, -jnp.inf, vals)
        sel = jnp.where(pick, 1.0, sel)
    denom = raws[0]
    for r in raws[1:]:
        denom = denom + r

    r_i = lax.broadcasted_iota(jnp.int32, (tm, tm), 0)
    c_i = lax.broadcasted_iota(jnp.int32, (tm, tm), 1)
    earlier = jnp.where(r_i < c_i, 1.0, 0.0).astype(BF16)
    rank = jnp.dot(sel.astype(BF16), earlier, preferred_element_type=F32) + run_ref[...]
    run_ref[...] = run_ref[...] + jnp.sum(sel, axis=1, keepdims=True)
    cnt_ref[...] = run_ref[...]

    slot8 = lax.broadcasted_iota(jnp.int32, (IDX_ROWS, tm), 0)
    slot_w = lax.broadcasted_iota(jnp.int32, (LANES, tm), 0)
    idx8 = jnp.zeros((IDX_ROWS, tm), F32)
    loc8 = jnp.zeros((IDX_ROWS, tm), F32)
    gate_w = jnp.zeros((LANES, tm), F32)
    for k in range(TOP_K):
        loc = jnp.sum(jnp.where(picks[k], rank, 0.0), axis=0, keepdims=True)
        idx8 = jnp.where(slot8 == k, firsts[k], idx8)
        loc8 = jnp.where(slot8 == k, loc, loc8)
        gate_w = jnp.where(slot_w == k, raws[k] / denom * ROUTED_SCALE, gate_w)
    idx_ref[...] = idx8.astype(jnp.int32)
    loc_ref[...] = loc8.astype(jnp.int32)
    gate_ref[...] = gate_w.T[:, :IDX_ROWS]


def _postmix(oa, ys, x2, wglu, woa, wob, ga, gs, gf, wr, rb, *, tm):
    t, d = x2.shape
    da = oa.shape[1]
    dsm = ys.shape[1]
    e = rb.shape[0]
    row = lambda i: (i, 0)
    col = lambda i: (0, i)
    fix = lambda i: (0, 0)
    return pl.pallas_call(
        functools.partial(_postmix_kernel, n_experts=e),
        grid=(t // tm,),
        in_specs=[
            pl.BlockSpec((tm, da), row), pl.BlockSpec((tm, dsm), row), pl.BlockSpec((tm, d), row),
            pl.BlockSpec((dsm, dsm), fix), pl.BlockSpec((da, d), fix), pl.BlockSpec((dsm, d), fix),
            pl.BlockSpec((1, da), fix), pl.BlockSpec((1, dsm), fix), pl.BlockSpec((1, d), fix),
            pl.BlockSpec((d, 2 * e), fix), pl.BlockSpec((e, 1), fix),
        ],
        out_specs=[
            pl.BlockSpec((tm, d), row), pl.BlockSpec((IDX_ROWS, tm), col), pl.BlockSpec((tm, IDX_ROWS), row),
            pl.BlockSpec((IDX_ROWS, tm), col), pl.BlockSpec((e, 1), fix),
        ],
        out_shape=[
            jax.ShapeDtypeStruct((t, d), F32), jax.ShapeDtypeStruct((IDX_ROWS, t), jnp.int32),
            jax.ShapeDtypeStruct((t, IDX_ROWS), F32), jax.ShapeDtypeStruct((IDX_ROWS, t), jnp.int32),
            jax.ShapeDtypeStruct((e, 1), F32),
        ],
        scratch_shapes=[pltpu.VMEM((e, 1), F32)],
        compiler_params=_params(("arbitrary",)),
        name="postmix_router",
    )(oa, ys, x2, wglu, woa, wob, ga, gs, gf, wr, rb)


def _rows_to_tiles(ref, packed, sub):
    rows = packed.shape[0]
    for s in range(sub):
        ref[pl.ds(s, rows, stride=sub), :] = packed[:, s * LANES:(s + 1) * LANES]


def _tiles_to_rows(ref, rows, sub):
    return jnp.concatenate([ref[pl.ds(s, rows, stride=sub), :] for s in range(sub)], axis=1)


def _tile_rows(ref, r, sub):
    start = r * sub if isinstance(r, int) else pl.multiple_of(r * sub, sub)
    return ref.at[pl.ds(start, sub)]


def _shared_kernel(dest_ref, tab_init_hbm, h1_ref, gf_ref, wg_ref, wu_ref, wd_ref, h1s_ref, mp_ref, tab_ref,
                   sem, *, sub, n_tokens):
    tm = h1_ref.shape[0]
    base = pl.program_id(0) * tm

    @pl.when(pl.program_id(0) == 0)
    def _():
        init = pltpu.make_async_copy(tab_init_hbm, tab_ref, sem)
        init.start()
        init.wait()

    for t in range(tm):
        for k in range(TOP_K):
            tab_ref[dest_ref[k * tm + t]] = k * n_tokens + base + t
    h1 = h1_ref[...]
    m = _rms(h1, gf_ref[...])
    _rows_to_tiles(mp_ref, _pack_halves(m), sub)
    mb = m.astype(BF16)
    a = jnp.dot(mb, wg_ref[...], preferred_element_type=F32)
    b = jnp.dot(mb, wu_ref[...], preferred_element_type=F32)
    hid = (a * _sigmoid(a) * b).astype(BF16)
    h1s_ref[...] = h1 + jnp.dot(hid, wd_ref[...], preferred_element_type=F32)


def _shared(dest, table0, h1, gf, wg, wu, wd, *, tm):
    t, d = h1.shape
    f = wg.shape[1]
    sub = d // 2 // LANES
    row = lambda i: (i, 0)
    fix = lambda i: (0, 0)
    return pl.pallas_call(
        functools.partial(_shared_kernel, sub=sub, n_tokens=t),
        grid=(t // tm,),
        in_specs=[
            pl.BlockSpec((IDX_ROWS * tm,), lambda i: (i,), memory_space=pltpu.SMEM),
            pl.BlockSpec(memory_space=pl.ANY),
            pl.BlockSpec((tm, d), row), pl.BlockSpec((1, d), fix),
            pl.BlockSpec((d, f), fix), pl.BlockSpec((d, f), fix), pl.BlockSpec((f, d), fix),
        ],
        out_specs=[pl.BlockSpec((tm, d), row), pl.BlockSpec((tm * sub, LANES), row),
                   pl.BlockSpec(memory_space=pltpu.SMEM)],
        out_shape=[jax.ShapeDtypeStruct((t, d), F32), jax.ShapeDtypeStruct((t * sub, LANES), jnp.int32),
                   jax.ShapeDtypeStruct(table0.shape, jnp.int32)],
        scratch_shapes=[pltpu.SemaphoreType.DMA(())],
        compiler_params=_params(("arbitrary",)),
        name="shared_table",
    )(dest, table0, h1, gf, wg, wu, wd)


def _experts_kernel(be_ref, first_ref, nxt_ref, nu_ref, tab_ref, mp_hbm, wg_hbm, wu_hbm, wd_hbm, yb_hbm,
                    mp_ref, wgf_ref, wuf_ref, wdf_ref, wgb_ref, wub_ref, wdb_ref, x0_ref, x1_ref, y0_ref, y1_ref,
                    wsem, msem, ssem, *, sub, tm, n_tokens):
    i = pl.program_id(0)
    nu = nu_ref[0]
    xs = (x0_ref, x1_ref)
    ys = (y0_ref, y1_ref)
    spare = TOP_K * n_tokens

    def token_of(e):
        return e & (n_tokens - 1) if n_tokens & (n_tokens - 1) == 0 else lax.rem(e, n_tokens)

    def gather(blk, r, s):
        tok = token_of(tab_ref[blk * tm + r])
        dst = r * sub if isinstance(r, int) else pl.multiple_of(r * sub, sub)
        xs[s][pl.ds(dst, sub), :] = mp_ref[pl.ds(pl.multiple_of(tok * sub, sub), sub), :]

    def scatter(entry, r, s):
        return pltpu.make_async_copy(_tile_rows(ys[s], r, sub), _tile_rows(yb_hbm, entry, sub), ssem.at[s])

    def scatter_wait(s):
        pltpu.make_async_copy(ys[s], yb_hbm.at[pl.ds(0, tm * sub)], ssem.at[s]).wait()

    def weight_copies(e):
        return (pltpu.make_async_copy(wg_hbm.at[e], wgf_ref, wsem.at[0]),
                pltpu.make_async_copy(wu_hbm.at[e], wuf_ref, wsem.at[1]),
                pltpu.make_async_copy(wd_hbm.at[e], wdf_ref, wsem.at[2]))

    @pl.when(i == 0)
    def _():
        for c in weight_copies(be_ref[0]):
            c.start()
        resident = pltpu.make_async_copy(mp_hbm, mp_ref, msem)
        resident.start()
        resident.wait()

        def first_rows(r, carry):
            gather(0, r, 0)
            return carry

        lax.fori_loop(0, tm, first_rows, 0)
        y1_ref[...] = jnp.zeros(y1_ref.shape, y1_ref.dtype)

    @pl.when((i < nu) & (first_ref[jnp.minimum(i, first_ref.shape[0] - 1)] != 0))
    def _():
        for c in weight_copies(be_ref[i]):
            c.wait()
        wgb_ref[...] = wgf_ref[...].astype(BF16)
        wub_ref[...] = wuf_ref[...].astype(BF16)
        wdb_ref[...] = wdf_ref[...].astype(BF16)

        @pl.when(nxt_ref[i] >= 0)
        def _():
            for c in weight_copies(nxt_ref[i]):
                c.start()

    def block(p):
        @pl.when(i >= 1)
        def _():
            scatter_wait(p)

        nb = jnp.minimum(i + 1, nu - 1)
        pb = jnp.maximum(i - 1, 0)
        for r in range(tm):
            gather(nb, r, 1 - p)
        for r in range(tm):
            entry = jnp.where(i >= 1, tab_ref[pb * tm + r], spare + r)
            scatter(entry, r, 1 - p).start()
        lo, hi = _unpack_halves(_tiles_to_rows(xs[p], tm, sub))
        lo = lo.astype(BF16)
        hi = hi.astype(BF16)
        n = lo.shape[1]
        a = (jnp.dot(lo, wgb_ref[:n, :], preferred_element_type=F32)
             + jnp.dot(hi, wgb_ref[n:, :], preferred_element_type=F32))
        b = (jnp.dot(lo, wub_ref[:n, :], preferred_element_type=F32)
             + jnp.dot(hi, wub_ref[n:, :], preferred_element_type=F32))
        hid = (a * _sigmoid(a) * b).astype(BF16)
        _rows_to_tiles(ys[p], _pack_halves(jnp.dot(hid, wdb_ref[...], preferred_element_type=F32)), sub)

    def flush(p):
        def last_rows(r, carry):
            scatter(tab_ref[(nu - 1) * tm + r], r, 1 - p).start()
            return carry

        scatter_wait(p)
        lax.fori_loop(0, tm, last_rows, 0)
        scatter_wait(1 - p)

    for p in range(2):
        @pl.when((i < nu) & (lax.rem(i, 2) == p))
        def _(p=p):
            block(p)

        @pl.when((i == nu) & (lax.rem(i, 2) == p))
        def _(p=p):
            flush(p)


def _experts(blk_e, first, nxt, n_used, table, mp, wg, wu, wd, *, tm):
    _, d, f = wg.shape
    sub = d // 2 // LANES
    n_tokens = mp.shape[0] // sub
    n_blocks = table.shape[0] // tm
    return pl.pallas_call(
        functools.partial(_experts_kernel, sub=sub, tm=tm, n_tokens=n_tokens),
        grid_spec=pltpu.PrefetchScalarGridSpec(
            num_scalar_prefetch=5,
            grid=(n_blocks + 1,),
            in_specs=[pl.BlockSpec(memory_space=pl.ANY)] * 4,
            out_specs=pl.BlockSpec(memory_space=pl.ANY),
            scratch_shapes=[pltpu.VMEM(mp.shape, jnp.int32),
                            pltpu.VMEM((d, f), F32), pltpu.VMEM((d, f), F32), pltpu.VMEM((f, d), F32),
                            pltpu.VMEM((d, f), BF16), pltpu.VMEM((d, f), BF16), pltpu.VMEM((f, d), BF16),
                            pltpu.VMEM((tm * sub, LANES), jnp.int32), pltpu.VMEM((tm * sub, LANES), jnp.int32),
                            pltpu.VMEM((tm * sub, LANES), jnp.int32), pltpu.VMEM((tm * sub, LANES), jnp.int32),
                            pltpu.SemaphoreType.DMA((3,)), pltpu.SemaphoreType.DMA(()),
                            pltpu.SemaphoreType.DMA((2,))],
        ),
        out_shape=jax.ShapeDtypeStruct(((TOP_K * n_tokens + tm) * sub, LANES), jnp.int32),
        compiler_params=pltpu.CompilerParams(dimension_semantics=("arbitrary",),
                                             vmem_limit_bytes=EXPERTS_VMEM_LIMIT),
        name="routed_experts",
    )(blk_e, first, nxt, n_used, table, mp, wg, wu, wd)


def _final_kernel(h_ref, gate_ref, p_ref, gp_ref, wgate_ref, wproj_ref, *rest, sub):
    y_refs, o_ref = rest[:TOP_K], rest[TOP_K]
    tm = h_ref.shape[0]
    n = h_ref.shape[1] // 2
    n_grp = COMBINE_ROW_GROUPS if tm % (8 * COMBINE_ROW_GROUPS) == 0 else 1
    gr = tm // n_grp
    rows = [pl.ds(g * gr, gr) for g in range(n_grp)]
    pps = [jnp.dot(p_ref[r, :].astype(BF16), wproj_ref[...], preferred_element_type=F32) for r in rows]
    gates = [gate_ref[r, :] for r in rows]
    acc_los = [h_ref[r, :n] for r in rows]
    acc_his = [h_ref[r, n:] for r in rows]
    for k in range(TOP_K):
        for g in range(n_grp):
            packed = jnp.concatenate([y_refs[k][pl.ds(g * gr * sub + s, gr, stride=sub), :] for s in range(sub)],
                                     axis=1)
            lo, hi = _unpack_halves(packed)
            acc_los[g] = acc_los[g] + gates[g][:, k:k + 1] * lo
            acc_his[g] = acc_his[g] + gates[g][:, k:k + 1] * hi
    h2s = [jnp.concatenate([lo, hi], axis=1) for lo, hi in zip(acc_los, acc_his)]
    nrms = [_rms(h2, gp_ref[...]).astype(BF16) for h2 in h2s]
    gts = [_sigmoid(jnp.dot(nrm, wgate_ref[...], preferred_element_type=F32)) for nrm in nrms]
    for r, h2, gt, pp in zip(rows, h2s, gts, pps):
        o_ref[r, :] = h2 + gt * pp


def _final(h1s, gate, p2, gp, wgate, wproj, yb, *, tm):
    t, d = h1s.shape
    dp = p2.shape[1]
    sub = d // 2 // LANES
    nt = t // tm
    row = lambda i: (i, 0)
    fix = lambda i: (0, 0)
    y_specs = [pl.BlockSpec((tm * sub, LANES), lambda i, k=k: (k * nt + i, 0)) for k in range(TOP_K)]
    return pl.pallas_call(
        functools.partial(_final_kernel, sub=sub),
        grid=(nt,),
        in_specs=[
            pl.BlockSpec((tm, d), row), pl.BlockSpec((tm, IDX_ROWS), row), pl.BlockSpec((tm, dp), row),
            pl.BlockSpec((1, d), fix), pl.BlockSpec((d, d), fix), pl.BlockSpec((dp, d), fix),
        ] + y_specs,
        out_specs=pl.BlockSpec((tm, d), row),
        out_shape=jax.ShapeDtypeStruct((t, d), F32),
        compiler_params=_params(("parallel",)),
        name="combine_ple",
    )(h1s, gate, p2, gp, wgate, wproj, *([yb] * TOP_K))


def _largest_tile(n, cap):
    t = min(n, cap)
    while n % t:
        t //= 2
    return t


def _layer(h, p_l, norm_mix, w_in, q_norm, k_norm, lam_re, lam_im, log_dt, b_re, b_im, c_re, c_im, d_skip,
           w_glu, attn_out_norm, ssm_out_norm, w_out, norm_ffn, w_router, router_bias, w_exp_gate,
           w_exp_up, w_exp_down, w_sh_gate, w_sh_up, w_sh_down, norm_ple, w_ple_gate, w_ple_proj):
    bsz, seq, d = h.shape
    t = bsz * seq
    head_dim = q_norm.shape[-1]
    d_attn = attn_out_norm.shape[-1]
    n_heads = d_attn // head_dim
    n_experts = w_router.shape[-1]
    x2 = h.reshape(t, d)

    tn = d_attn if d_attn % 128 == 0 else head_dim
    scale = 1.0 / math.sqrt(head_dim)
    hg = jnp.concatenate([jnp.tile(q_norm * scale, n_heads), jnp.tile(k_norm, n_heads),
                          jnp.ones((w_in.shape[1] - 2 * d_attn,), F32)])[None, :]
    z, u = _inproj(x2, norm_mix[None, :], w_in.astype(BF16), hg, d_qkv=3 * d_attn,
                   n_norm_tiles=2 * d_attn // tn, head_dim=head_dim, tm=_largest_tile(t, 1024), tn=tn)

    o_attn = _attention(z, batch=bsz, seq=seq, n_heads=n_heads, head_dim=head_dim, blk=_largest_tile(seq, 256))

    n_chunks = seq // S5_CHUNK
    mats = _s5_prepare(lam_re, lam_im, log_dt, b_re, b_im, c_re, c_im, d_skip, n_chunks)
    y_ssm = _s5(u, *mats, n_chunks=n_chunks)

    tm = _largest_tile(t, 256)
    tm_sh = _largest_tile(t, 512)
    w_out_b = w_out.astype(BF16)
    wr_hi = w_router.astype(BF16)
    wr_lo = (w_router - wr_hi.astype(F32)).astype(BF16)
    h1, idx, gate, loc, counts = _postmix(
        o_attn, y_ssm, x2, w_glu.astype(BF16), w_out_b[:d_attn], w_out_b[d_attn:],
        attn_out_norm[None, :], ssm_out_norm[None, :], norm_ffn[None, :],
        jnp.concatenate([wr_hi, wr_lo], axis=1), router_bias[:, None], tm=_largest_tile(t, 512))

    rb = _largest_tile(t, 256)
    n_pad = t * TOP_K + n_experts * rb
    n_blocks = n_pad // rb
    cnt = counts[:, 0].astype(jnp.int32)
    pcnt = (cnt + rb - 1) // rb * rb
    pend = jnp.cumsum(pcnt)
    pstart = pend - pcnt
    onehot = idx[:, :, None] == jnp.arange(n_experts, dtype=jnp.int32)[None, None, :]
    dest = jnp.sum(jnp.where(onehot, pstart[None, None, :], 0), axis=-1) + loc
    dest = dest.reshape(IDX_ROWS, t // tm_sh, tm_sh).transpose(1, 0, 2).reshape(-1)
    n_used = (pend[-1] // rb).astype(jnp.int32)
    blk_ids = jnp.arange(n_blocks, dtype=jnp.int32)
    starts = (blk_ids * rb)[:, None]
    owner = (starts >= pstart[None, :]) & (starts < pend[None, :])
    blk_e = jnp.sum(jnp.where(owner, jnp.arange(n_experts, dtype=jnp.int32)[None, :], 0), axis=1)
    is_first = jnp.any(owner & (starts == pstart[None, :]), axis=1).astype(jnp.int32)
    first_pos = jnp.where(is_first > 0, blk_ids, n_blocks)
    nxt_pos = jnp.concatenate([lax.cummin(first_pos, axis=0, reverse=True)[1:],
                               jnp.full((1,), n_blocks, jnp.int32)])
    nxt_hit = nxt_pos[:, None] == blk_ids[None, :]
    nxt = jnp.where(nxt_pos < n_blocks, jnp.sum(jnp.where(nxt_hit, blk_e[None, :], 0), axis=1), -1)
    last_e = jnp.sum(jnp.where(blk_ids == n_used - 1, blk_e, 0))
    blk_e = jnp.where(blk_ids < n_used, blk_e, last_e).astype(jnp.int32)

    table0 = TOP_K * t + jnp.arange(n_pad, dtype=jnp.int32) % rb
    h1s, mp, table = _shared(dest, table0, h1, norm_ffn[None, :], w_sh_gate.astype(BF16), w_sh_up.astype(BF16),
                             w_sh_down.astype(BF16), tm=tm_sh)
    yb = _experts(blk_e, is_first, nxt.astype(jnp.int32), n_used[None], table, mp, w_exp_gate, w_exp_up,
                  w_exp_down, tm=rb)
    out = _final(h1s, gate, p_l.reshape(t, -1), norm_ple[None, :], w_ple_gate.astype(BF16),
                 w_ple_proj.astype(BF16), yb, tm=tm)
    return out.reshape(bsz, seq, d)


def kernel(x, p, norm_mix, w_in, q_norm, k_norm, ssm_lam_re, ssm_lam_im, ssm_log_dt, ssm_b_re, ssm_b_im,
           ssm_c_re, ssm_c_im, ssm_d, w_glu, attn_out_norm, ssm_out_norm, w_out, norm_ffn, w_router,
           router_bias, w_exp_gate, w_exp_up, w_exp_down, w_sh_gate, w_sh_up, w_sh_down, norm_ple,
           w_ple_gate, w_ple_proj):
    h = x
    for i in range(p.shape[0]):
        h = _layer(h, p[i], norm_mix[i], w_in[i], q_norm[i], k_norm[i], ssm_lam_re[i], ssm_lam_im[i],
                   ssm_log_dt[i], ssm_b_re[i], ssm_b_im[i], ssm_c_re[i], ssm_c_im[i], ssm_d[i], w_glu[i],
                   attn_out_norm[i], ssm_out_norm[i], w_out[i], norm_ffn[i], w_router[i], router_bias[i],
                   w_exp_gate[i], w_exp_up[i], w_exp_down[i], w_sh_gate[i], w_sh_up[i], w_sh_down[i],
                   norm_ple[i], w_ple_gate[i], w_ple_proj[i])
    return h
```

```python
import functools
import math

import jax
import jax.numpy as jnp
from jax import lax
from jax.experimental import pallas as pl
from jax.experimental.pallas import tpu as pltpu

NORM_EPS = 1e-6
TOP_K = 6
ROUTED_SCALE = 2.5
IDX_ROWS = 8
LANES = 128
S5_CHUNK = 8
POSTMIX_ROW_GROUPS = 2
S5_ROW_GROUPS = 2
COMBINE_ROW_GROUPS = 2
VMEM_LIMIT = 56 * 1024 * 1024
EXPERTS_VMEM_LIMIT = 62 * 1024 * 1024

F32 = jnp.float32
BF16 = jnp.bfloat16
HIGH_HALF = -65536


def _rms(x, g):
    return x * lax.rsqrt(jnp.mean(x * x, axis=-1, keepdims=True) + NORM_EPS) * g


def _sigmoid(x):
    return 1.0 / (1.0 + jnp.exp(-x))


def _params(sem):
    return pltpu.CompilerParams(dimension_semantics=sem, vmem_limit_bytes=VMEM_LIMIT)


def _pack_halves(x):
    n = x.shape[1] // 2
    lo = lax.bitcast_convert_type(x[:, :n].astype(BF16).astype(F32), jnp.int32)
    hi = lax.bitcast_convert_type(x[:, n:].astype(BF16).astype(F32), jnp.int32)
    return (hi & HIGH_HALF) | lax.shift_right_logical(lo, 16)


def _unpack_halves(p):
    lo = lax.bitcast_convert_type(lax.shift_left(p, 16), F32)
    hi = lax.bitcast_convert_type(p & HIGH_HALF, F32)
    return lo, hi


def _inproj_kernel(x_ref, g_ref, w_ref, hg_ref, o_ref, u_ref, xn_ref, *, n_norm_tiles, n_qkv_tiles, head_dim):
    j = pl.program_id(1)

    @pl.when(j == 0)
    def _():
        xn_ref[...] = _rms(x_ref[...], g_ref[...]).astype(BF16)

    acc = jnp.dot(xn_ref[...], w_ref[...], preferred_element_type=F32)
    tn = acc.shape[1]

    @pl.when(j < n_norm_tiles)
    def _():
        hg = hg_ref[...]
        for h in range(tn // head_dim):
            sl = slice(h * head_dim, (h + 1) * head_dim)
            o_ref[:, sl] = _rms(acc[:, sl], hg[:, sl]).astype(o_ref.dtype)

    @pl.when((j >= n_norm_tiles) & (j < n_qkv_tiles))
    def _():
        o_ref[...] = acc.astype(o_ref.dtype)

    @pl.when(j >= n_qkv_tiles)
    def _():
        u_ref[...] = acc


def _inproj(x2, g, w, hg, *, d_qkv, n_norm_tiles, head_dim, tm, tn):
    t, d = x2.shape
    n = w.shape[1]
    nq = d_qkv // tn
    return pl.pallas_call(
        functools.partial(_inproj_kernel, n_norm_tiles=n_norm_tiles, n_qkv_tiles=nq, head_dim=head_dim),
        grid=(t // tm, n // tn),
        in_specs=[
            pl.BlockSpec((tm, d), lambda i, j: (i, 0)),
            pl.BlockSpec((1, d), lambda i, j: (0, 0)),
            pl.BlockSpec((d, tn), lambda i, j: (0, j)),
            pl.BlockSpec((1, tn), lambda i, j: (0, j)),
        ],
        out_specs=[
            pl.BlockSpec((tm, tn), lambda i, j: (i, jnp.minimum(j, nq - 1))),
            pl.BlockSpec((tm, tn), lambda i, j: (i, jnp.maximum(j - nq, 0))),
        ],
        out_shape=[jax.ShapeDtypeStruct((t, d_qkv), BF16), jax.ShapeDtypeStruct((t, n - d_qkv), F32)],
        scratch_shapes=[pltpu.VMEM((tm, d), BF16)],
        compiler_params=_params(("parallel", "arbitrary")),
        name="inproj",
    )(x2, g, w, hg)


SKIP_AFTER = 110.0


def _attn_kernel(q_ref, k_ref, v_ref, o_ref, *, blk, hd, heads):
    i = pl.program_id(2)
    row = lax.broadcasted_iota(jnp.int32, (blk, blk), 0)
    col = lax.broadcasted_iota(jnp.int32, (blk, blk), 1)
    causal = col < row
    later = jnp.where(row > col, 1.0, 0.0).astype(BF16)
    lanes = [slice(h * hd, (h + 1) * hd) for h in range(heads)]
    qs = [q_ref[:, sl] for sl in lanes]

    def step(kb, accs, runs, diag):
        start = pl.multiple_of(kb * blk, blk)
        zs, sps, inners, new_accs, new_runs = [], [], [], [], []
        for h, sl in enumerate(lanes):
            k = k_ref[pl.ds(start, blk), sl]
            zs.append(lax.dot_general(qs[h], k, (((1,), (1,)), ((), ())), preferred_element_type=F32))
        for z in zs:
            sp = jnp.maximum(z, 0.0) + jnp.log(1.0 + jnp.exp(-jnp.abs(z)))
            sps.append(jnp.where(causal, sp, 0.0) if diag else sp)
        for sp in sps:
            hi = sp.astype(BF16)
            lo = (sp - hi.astype(F32)).astype(BF16)
            inners.append(jnp.dot(hi, later, preferred_element_type=F32)
                          + jnp.dot(lo, later, preferred_element_type=F32))
        for h, sl in enumerate(lanes):
            w = jnp.exp(zs[h] - sps[h] - (inners[h] + runs[h]))
            if diag:
                w = jnp.where(causal, w, 0.0)
            v = v_ref[pl.ds(start, blk), sl]
            new_accs.append(accs[h] + jnp.dot(w.astype(BF16), v, preferred_element_type=F32))
            new_runs.append(runs[h] + jnp.sum(sps[h], axis=-1, keepdims=True))
        return tuple(new_accs), tuple(new_runs)

    def keep_going(runs):
        low = runs[0]
        for r in runs[1:]:
            low = jnp.minimum(low, r)
        return (jnp.min(low) < SKIP_AFTER).astype(jnp.int32)

    accs = tuple(jnp.zeros((blk, hd), F32) for _ in lanes)
    runs = tuple(jnp.zeros((blk, 1), F32) for _ in lanes)
    accs, runs = step(i, accs, runs, True)

    def cond(c):
        n, go, _, _ = c
        return (n < i) & (go > 0)

    def body(c):
        n, _, accs, runs = c
        accs, runs = step(i - 1 - n, accs, runs, False)
        return n + 1, keep_going(runs), accs, runs

    _, _, accs, _ = lax.while_loop(cond, body, (jnp.int32(0), keep_going(runs), accs, runs))
    for h, sl in enumerate(lanes):
        o_ref[:, sl] = accs[h].astype(o_ref.dtype)


def _attention(z, *, batch, seq, n_heads, head_dim, blk):
    nq = seq // blk
    heads = next(c for c in (8, 4, 2, 1) if n_heads % c == 0)
    hw = heads * head_dim
    ng = n_heads // heads
    return pl.pallas_call(
        functools.partial(_attn_kernel, blk=blk, hd=head_dim, heads=heads),
        grid=(batch, ng, nq),
        in_specs=[
            pl.BlockSpec((blk, hw), lambda b, h, i: (b * nq + i, h)),
            pl.BlockSpec((seq, hw), lambda b, h, i: (b, ng + h)),
            pl.BlockSpec((seq, hw), lambda b, h, i: (b, 2 * ng + h)),
        ],
        out_specs=pl.BlockSpec((blk, hw), lambda b, h, i: (b * nq + i, h)),
        out_shape=jax.ShapeDtypeStruct((batch * seq, n_heads * head_dim), BF16),
        compiler_params=_params(("parallel", "parallel", "arbitrary")),
        name="sb_attention",
    )(z, z, z)


def _s5_prepare(lam_re, lam_im, log_dt, b_re, b_im, c_re, c_im, d_skip, n_chunks):
    hp = lax.Precision.HIGHEST
    chunk = S5_CHUNK
    g, p = lam_re.shape
    h = b_re.shape[-1]
    gg = LANES // h
    o = g // gg
    dt = jnp.exp(log_dt)[:, None]
    ks = jnp.arange(chunk + 1, dtype=F32)[None, :, None]
    mag = jnp.exp(lam_re[:, None, :] * dt[:, None, :] * ks)
    ang = lam_im[:, None, :] * dt[:, None, :] * ks
    ak_re = mag * jnp.cos(ang)
    ak_im = mag * jnp.sin(ang)
    nr = ak_re[:, 1] - 1.0
    ni = ak_im[:, 1]
    den = lam_re * lam_re + lam_im * lam_im
    coef_re = ((nr * lam_re + ni * lam_im) / den)[..., None]
    coef_im = ((ni * lam_re - nr * lam_im) / den)[..., None]
    bbar_re = coef_re * b_re - coef_im * b_im
    bbar_im = coef_re * b_im + coef_im * b_re
    ca_re = c_re[:, None] * ak_re[:, :, None, :] - c_im[:, None] * ak_im[:, :, None, :]
    ca_im = c_re[:, None] * ak_im[:, :, None, :] + c_im[:, None] * ak_re[:, :, None, :]

    cam_re = ca_re[:, :chunk].transpose(0, 3, 1, 2).reshape(g, p, chunk * h)
    cam_im = ca_im[:, :chunk].transpose(0, 3, 1, 2).reshape(g, p, chunk * h)
    kt = (jnp.einsum("gph,gpm->ghm", bbar_re, cam_re, precision=hp)
          - jnp.einsum("gph,gpm->ghm", bbar_im, cam_im, precision=hp))
    def same_group(n_rows, row_block, n_cols, col_block):
        r = (jnp.arange(n_rows) // row_block) % gg
        c = (jnp.arange(n_cols) // col_block) % gg
        return (r[:, None] == c[None, :]).astype(F32)

    tile_h = jnp.tile(jnp.eye(h, dtype=F32), (1, gg))
    tile_p = jnp.tile(jnp.eye(p, dtype=F32), (1, gg))
    kt = kt.reshape(o, gg, h, chunk, h).transpose(0, 3, 1, 2, 4).reshape(o, chunk, LANES, h)
    bd = jnp.einsum("okrh,hc->okrc", kt, tile_h, precision=hp) * same_group(LANES, h, LANES, h)
    dvec = d_skip.reshape(o, LANES)
    lags = [bd[:, k] for k in range(chunk)]
    lags[0] = lags[0] + dvec[:, :, None] * jnp.eye(LANES, dtype=F32)[None]
    m_lags = jnp.stack(lags, axis=1)

    def expand_in(ak_part_a, ak_part_b, b_a, b_b, sign):
        rk_a = jnp.stack([ak_part_a[:, chunk - 1 - i] for i in range(chunk)], axis=1)
        rk_b = jnp.stack([ak_part_b[:, chunk - 1 - i] for i in range(chunk)], axis=1)
        val = (rk_a[:, :, None, :] * b_a.transpose(0, 2, 1)[:, None]
               + sign * rk_b[:, :, None, :] * b_b.transpose(0, 2, 1)[:, None])
        val = val.reshape(o, gg, chunk, h, p).transpose(0, 2, 1, 3, 4).reshape(o, chunk * LANES, p)
        wide = jnp.einsum("orp,pc->orc", val.astype(BF16), tile_p.astype(BF16), preferred_element_type=BF16)
        return wide * same_group(chunk * LANES, h, gg * p, p).astype(BF16)

    m_in = (expand_in(ak_re, ak_im, bbar_re, bbar_im, -1.0), expand_in(ak_re, ak_im, bbar_im, bbar_re, 1.0))

    def expand_out(ca):
        val = ca[:, 1:].reshape(o, gg, chunk, h, p).transpose(0, 4, 2, 1, 3).reshape(o, p, chunk * LANES)
        wide = jnp.einsum("pc,opr->ocr", tile_p.astype(BF16), val.astype(BF16), preferred_element_type=BF16)
        return wide * same_group(gg * p, p, chunk * LANES, h).astype(BF16)

    m_out = (expand_out(ca_re), -expand_out(ca_im))

    steps = max(1, int(math.ceil(math.log2(max(n_chunks, 2)))))
    cr, ci = ak_re[:, chunk].reshape(o, gg * p), ak_im[:, chunk].reshape(o, gg * p)
    sc_a, sc_b = [], []
    for _ in range(steps):
        sc_a.append(jnp.concatenate([cr, cr], axis=-1))
        sc_b.append(jnp.concatenate([-ci, ci], axis=-1))
        cr, ci = cr * cr - ci * ci, 2.0 * cr * ci
    return (m_lags.astype(BF16),) + m_in + m_out + (jnp.stack(sc_a, axis=1), jnp.stack(sc_b, axis=1))


def _s5_kernel(u_ref, lag_ref, min_re_ref, min_im_ref, mout_re_ref, mout_im_ref, sa_ref, sb_ref, y_ref, mi_ref,
               *, n_chunks, steps):
    chunk = S5_CHUNK
    n_all = u_ref.shape[0] // chunk
    for i in range(chunk):
        for j in range(chunk):
            blk = lag_ref[j - i] if j >= i else jnp.zeros((LANES, LANES), mi_ref.dtype)
            mi_ref[i * LANES:(i + 1) * LANES, j * LANES:(j + 1) * LANES] = blk
    n_grp = S5_ROW_GROUPS if (n_all // n_chunks) % S5_ROW_GROUPS == 0 else 1
    n = n_all // n_grp
    us = [jnp.concatenate([u_ref[pl.ds(g * n * chunk + i, n, stride=chunk), :] for i in range(chunk)],
                          axis=1).astype(BF16) for g in range(n_grp)]
    ys = [jnp.dot(u, mi_ref[...], preferred_element_type=F32) for u in us]
    xs = [jnp.concatenate([jnp.dot(u, min_re_ref[...], preferred_element_type=F32),
                           jnp.dot(u, min_im_ref[...], preferred_element_type=F32)], axis=1)
          for u in us]
    half = xs[0].shape[1] // 2
    c = lax.rem(lax.broadcasted_iota(jnp.int32, xs[0].shape, 0), n_chunks)
    sa = sa_ref[...]
    sb = sb_ref[...]
    for k in range(steps):
        sh = 1 << k
        sh_xs = [jnp.where(c >= sh, pltpu.roll(x, sh, axis=0), 0.0) for x in xs]
        xs = [x + s * sa[k:k + 1, :] + pltpu.roll(s, half, axis=1) * sb[k:k + 1, :] for x, s in zip(xs, sh_xs)]
    s_ins = [jnp.where(c >= 1, pltpu.roll(x, 1, axis=0), 0.0) for x in xs]
    ys = [y + jnp.dot(s_in[:, :half].astype(BF16), mout_re_ref[...], preferred_element_type=F32)
          + jnp.dot(s_in[:, half:].astype(BF16), mout_im_ref[...], preferred_element_type=F32)
          for y, s_in in zip(ys, s_ins)]
    for g, y in enumerate(ys):
        for i in range(chunk):
            y_ref[pl.ds(g * n * chunk + i, n, stride=chunk), :] = y[:, i * LANES:(i + 1) * LANES]


def _s5(u, m_lags, m_in_re, m_in_im, m_out_re, m_out_im, sc_a, sc_b, *, n_chunks):
    t, d_ssm = u.shape
    o, lh, st = m_in_re.shape
    steps = sc_a.shape[1]
    mat = lambda i: (i, 0, 0)
    return pl.pallas_call(
        functools.partial(_s5_kernel, n_chunks=n_chunks, steps=steps),
        grid=(o,),
        in_specs=[
            pl.BlockSpec((t, LANES), lambda i: (0, i)),
            pl.BlockSpec((None,) + m_lags.shape[1:], lambda i: (i, 0, 0, 0)),
            pl.BlockSpec((None, lh, st), mat), pl.BlockSpec((None, lh, st), mat),
            pl.BlockSpec((None, st, lh), mat), pl.BlockSpec((None, st, lh), mat),
            pl.BlockSpec((None, steps, 2 * st), mat), pl.BlockSpec((None, steps, 2 * st), mat),
        ],
        out_specs=pl.BlockSpec((t, LANES), lambda i: (0, i)),
        out_shape=jax.ShapeDtypeStruct((t, d_ssm), F32),
        scratch_shapes=[pltpu.VMEM((lh, lh), BF16)],
        compiler_params=_params(("parallel",)),
        name="s5_chunked",
    )(u, m_lags, m_in_re, m_in_im, m_out_re, m_out_im, sc_a, sc_b)


def _postmix_kernel(oa_ref, ys_ref, x_ref, wglu_ref, woa_ref, wob_ref, ga_ref, gs_ref, gf_ref,
                    wr_ref, rb_ref, h1_ref, idx_ref, gate_ref, loc_ref, cnt_ref, run_ref, *, n_experts):
    step = pl.program_id(0)

    @pl.when(step == 0)
    def _():
        run_ref[...] = jnp.zeros(run_ref.shape, F32)

    tm = x_ref.shape[0]
    n_groups = POSTMIX_ROW_GROUPS if tm % (8 * POSTMIX_ROW_GROUPS) == 0 else 1
    rows = [pl.ds(g * (tm // n_groups), tm // n_groups) for g in range(n_groups)]

    def dot(a, w_ref):
        return jnp.dot(a, w_ref[...], preferred_element_type=F32)

    ys = [ys_ref[r, :] for r in rows]
    ys = [0.5 * y * (1.0 + jnp.tanh(math.sqrt(2.0 / math.pi) * (y + 0.044715 * (y * y * y)))) for y in ys]
    glu = [dot(y.astype(BF16), wglu_ref) for y in ys]
    ys = [y * _sigmoid(g) for y, g in zip(ys, glu)]
    nas = [_rms(oa_ref[r, :].astype(F32), ga_ref[...]).astype(BF16) for r in rows]
    nss = [_rms(y, gs_ref[...]).astype(BF16) for y in ys]
    h1s = [x_ref[r, :] + dot(na, woa_ref) + dot(ns, wob_ref) for r, na, ns in zip(rows, nas, nss)]
    for r, h1 in zip(rows, h1s):
        h1_ref[r, :] = h1

    ms = [_rms(h1, gf_ref[...]) for h1 in h1s]
    m_his = [m.astype(BF16) for m in ms]
    m_los = [(m - m_hi.astype(F32)).astype(BF16) for m, m_hi in zip(ms, m_his)]
    r_his = [dot(m_hi, wr_ref) for m_hi in m_his]
    r_los = [dot(m_lo, wr_ref) for m_lo in m_los]
    wide_t = jnp.concatenate([(r_hi + r_lo).T for r_hi, r_lo in zip(r_his, r_los)], axis=1)
    logits = wide_t[:n_experts] + wide_t[n_experts:]
    scores = _sigmoid(logits)
    vals = scores + rb_ref[...]
    eidx = lax.broadcasted_iota(jnp.int32, vals.shape, 0).astype(F32)
    picks, firsts, raws = [], [], []
    sel = jnp.zeros(vals.shape, F32)
    for _ in range(TOP_K):
        mx = jnp.max(vals, axis=0, keepdims=True)
        first = jnp.min(jnp.where(vals == mx, eidx, float(n_experts)), axis=0, keepdims=True)
        pick = eidx == first
        picks.append(pick)
        firsts.append(first)
        raws.append(jnp.sum(jnp.where(pick, scores, 0.0), axis=0, keepdims=True))
        vals = jnp.where(pick, -jnp.inf, vals)
        sel = jnp.where(pick, 1.0, sel)
    denom = raws[0]
    for r in raws[1:]:
        denom = denom + r

    r_i = lax.broadcasted_iota(jnp.int32, (tm, tm), 0)
    c_i = lax.broadcasted_iota(jnp.int32, (tm, tm), 1)
    earlier = jnp.where(r_i < c_i, 1.0, 0.0).astype(BF16)
    rank = jnp.dot(sel.astype(BF16), earlier, preferred_element_type=F32) + run_ref[...]
    run_ref[...] = run_ref[...] + jnp.sum(sel, axis=1, keepdims=True)
    cnt_ref[...] = run_ref[...]

    slot8 = lax.broadcasted_iota(jnp.int32, (IDX_ROWS, tm), 0)
    slot_w = lax.broadcasted_iota(jnp.int32, (LANES, tm), 0)
    idx8 = jnp.zeros((IDX_ROWS, tm), F32)
    loc8 = jnp.zeros((IDX_ROWS, tm), F32)
    gate_w = jnp.zeros((LANES, tm), F32)
    for k in range(TOP_K):
        loc = jnp.sum(jnp.where(picks[k], rank, 0.0), axis=0, keepdims=True)
        idx8 = jnp.where(slot8 == k, firsts[k], idx8)
        loc8 = jnp.where(slot8 == k, loc, loc8)
        gate_w = jnp.where(slot_w == k, raws[k] / denom * ROUTED_SCALE, gate_w)
    idx_ref[...] = idx8.astype(jnp.int32)
    loc_ref[...] = loc8.astype(jnp.int32)
    gate_ref[...] = gate_w.T[:, :IDX_ROWS]


def _postmix(oa, ys, x2, wglu, woa, wob, ga, gs, gf, wr, rb, *, tm):
    t, d = x2.shape
    da = oa.shape[1]
    dsm = ys.shape[1]
    e = rb.shape[0]
    row = lambda i: (i, 0)
    col = lambda i: (0, i)
    fix = lambda i: (0, 0)
    return pl.pallas_call(
        functools.partial(_postmix_kernel, n_experts=e),
        grid=(t // tm,),
        in_specs=[
            pl.BlockSpec((tm, da), row), pl.BlockSpec((tm, dsm), row), pl.BlockSpec((tm, d), row),
            pl.BlockSpec((dsm, dsm), fix), pl.BlockSpec((da, d), fix), pl.BlockSpec((dsm, d), fix),
            pl.BlockSpec((1, da), fix), pl.BlockSpec((1, dsm), fix), pl.BlockSpec((1, d), fix),
            pl.BlockSpec((d, 2 * e), fix), pl.BlockSpec((e, 1), fix),
        ],
        out_specs=[
            pl.BlockSpec((tm, d), row), pl.BlockSpec((IDX_ROWS, tm), col), pl.BlockSpec((tm, IDX_ROWS), row),
            pl.BlockSpec((IDX_ROWS, tm), col), pl.BlockSpec((e, 1), fix),
        ],
        out_shape=[
            jax.ShapeDtypeStruct((t, d), F32), jax.ShapeDtypeStruct((IDX_ROWS, t), jnp.int32),
            jax.ShapeDtypeStruct((t, IDX_ROWS), F32), jax.ShapeDtypeStruct((IDX_ROWS, t), jnp.int32),
            jax.ShapeDtypeStruct((e, 1), F32),
        ],
        scratch_shapes=[pltpu.VMEM((e, 1), F32)],
        compiler_params=_params(("arbitrary",)),
        name="postmix_router",
    )(oa, ys, x2, wglu, woa, wob, ga, gs, gf, wr, rb)


def _rows_to_tiles(ref, packed, sub):
    rows = packed.shape[0]
    for s in range(sub):
        ref[pl.ds(s, rows, stride=sub), :] = packed[:, s * LANES:(s + 1) * LANES]


def _tiles_to_rows(ref, rows, sub):
    return jnp.concatenate([ref[pl.ds(s, rows, stride=sub), :] for s in range(sub)], axis=1)


def _tile_rows(ref, r, sub):
    start = r * sub if isinstance(r, int) else pl.multiple_of(r * sub, sub)
    return ref.at[pl.ds(start, sub)]


def _shared_kernel(dest_ref, tab_init_hbm, h1_ref, gf_ref, wg_ref, wu_ref, wd_ref, h1s_ref, mp_ref, tab_ref,
                   sem, *, sub, n_tokens):
    tm = h1_ref.shape[0]
    base = pl.program_id(0) * tm

    @pl.when(pl.program_id(0) == 0)
    def _():
        init = pltpu.make_async_copy(tab_init_hbm, tab_ref, sem)
        init.start()
        init.wait()

    for t in range(tm):
        for k in range(TOP_K):
            tab_ref[dest_ref[k * tm + t]] = k * n_tokens + base + t
    h1 = h1_ref[...]
    m = _rms(h1, gf_ref[...])
    _rows_to_tiles(mp_ref, _pack_halves(m), sub)
    mb = m.astype(BF16)
    a = jnp.dot(mb, wg_ref[...], preferred_element_type=F32)
    b = jnp.dot(mb, wu_ref[...], preferred_element_type=F32)
    hid = (a * _sigmoid(a) * b).astype(BF16)
    h1s_ref[...] = h1 + jnp.dot(hid, wd_ref[...], preferred_element_type=F32)


def _shared(dest, table0, h1, gf, wg, wu, wd, *, tm):
    t, d = h1.shape
    f = wg.shape[1]
    sub = d // 2 // LANES
    row = lambda i: (i, 0)
    fix = lambda i: (0, 0)
    return pl.pallas_call(
        functools.partial(_shared_kernel, sub=sub, n_tokens=t),
        grid=(t // tm,),
        in_specs=[
            pl.BlockSpec((IDX_ROWS * tm,), lambda i: (i,), memory_space=pltpu.SMEM),
            pl.BlockSpec(memory_space=pl.ANY),
            pl.BlockSpec((tm, d), row), pl.BlockSpec((1, d), fix),
            pl.BlockSpec((d, f), fix), pl.BlockSpec((d, f), fix), pl.BlockSpec((f, d), fix),
        ],
        out_specs=[pl.BlockSpec((tm, d), row), pl.BlockSpec((tm * sub, LANES), row),
                   pl.BlockSpec(memory_space=pltpu.SMEM)],
        out_shape=[jax.ShapeDtypeStruct((t, d), F32), jax.ShapeDtypeStruct((t * sub, LANES), jnp.int32),
                   jax.ShapeDtypeStruct(table0.shape, jnp.int32)],
        scratch_shapes=[pltpu.SemaphoreType.DMA(())],
        compiler_params=_params(("arbitrary",)),
        name="shared_table",
    )(dest, table0, h1, gf, wg, wu, wd)


def _experts_kernel(be_ref, first_ref, nxt_ref, nu_ref, tab_ref, mp_hbm, wg_hbm, wu_hbm, wd_hbm, yb_hbm,
                    mp_ref, wgf_ref, wuf_ref, wdf_ref, wgb_ref, wub_ref, wdb_ref, x0_ref, x1_ref, y0_ref, y1_ref,
                    wsem, msem, ssem, *, sub, tm, n_tokens):
    i = pl.program_id(0)
    nu = nu_ref[0]
    xs = (x0_ref, x1_ref)
    ys = (y0_ref, y1_ref)
    spare = TOP_K * n_tokens

    def token_of(e):
        return e & (n_tokens - 1) if n_tokens & (n_tokens - 1) == 0 else lax.rem(e, n_tokens)

    def gather(blk, r, s):
        tok = token_of(tab_ref[blk * tm + r])
        dst = r * sub if isinstance(r, int) else pl.multiple_of(r * sub, sub)
        xs[s][pl.ds(dst, sub), :] = mp_ref[pl.ds(pl.multiple_of(tok * sub, sub), sub), :]

    def scatter(entry, r, s):
        return pltpu.make_async_copy(_tile_rows(ys[s], r, sub), _tile_rows(yb_hbm, entry, sub), ssem.at[s])

    def scatter_wait(s):
        pltpu.make_async_copy(ys[s], yb_hbm.at[pl.ds(0, tm * sub)], ssem.at[s]).wait()

    def weight_copies(e):
        return (pltpu.make_async_copy(wg_hbm.at[e], wgf_ref, wsem.at[0]),
                pltpu.make_async_copy(wu_hbm.at[e], wuf_ref, wsem.at[1]),
                pltpu.make_async_copy(wd_hbm.at[e], wdf_ref, wsem.at[2]))

    @pl.when(i == 0)
    def _():
        for c in weight_copies(be_ref[0]):
            c.start()
        resident = pltpu.make_async_copy(mp_hbm, mp_ref, msem)
        resident.start()
        resident.wait()

        def first_rows(r, carry):
            gather(0, r, 0)
            return carry

        lax.fori_loop(0, tm, first_rows, 0)
        y1_ref[...] = jnp.zeros(y1_ref.shape, y1_ref.dtype)

    @pl.when((i < nu) & (first_ref[jnp.minimum(i, first_ref.shape[0] - 1)] != 0))
    def _():
        for c in weight_copies(be_ref[i]):
            c.wait()
        wgb_ref[...] = wgf_ref[...].astype(BF16)
        wub_ref[...] = wuf_ref[...].astype(BF16)
        wdb_ref[...] = wdf_ref[...].astype(BF16)

        @pl.when(nxt_ref[i] >= 0)
        def _():
            for c in weight_copies(nxt_ref[i]):
                c.start()

    def block(p):
        @pl.when(i >= 1)
        def _():
            scatter_wait(p)

        nb = jnp.minimum(i + 1, nu - 1)
        pb = jnp.maximum(i - 1, 0)
        for r in range(tm):
            gather(nb, r, 1 - p)
        for r in range(tm):
            entry = jnp.where(i >= 1, tab_ref[pb * tm + r], spare + r)
            scatter(entry, r, 1 - p).start(priority=r % 2)
        lo, hi = _unpack_halves(_tiles_to_rows(xs[p], tm, sub))
        lo = lo.astype(BF16)
        hi = hi.astype(BF16)
        n = lo.shape[1]
        a = (jnp.dot(lo, wgb_ref[:n, :], preferred_element_type=F32)
             + jnp.dot(hi, wgb_ref[n:, :], preferred_element_type=F32))
        b = (jnp.dot(lo, wub_ref[:n, :], preferred_element_type=F32)
             + jnp.dot(hi, wub_ref[n:, :], preferred_element_type=F32))
        hid = (a * _sigmoid(a) * b).astype(BF16)
        _rows_to_tiles(ys[p], _pack_halves(jnp.dot(hid, wdb_ref[...], preferred_element_type=F32)), sub)

    def flush(p):
        def last_rows(r, carry):
            scatter(tab_ref[(nu - 1) * tm + r], r, 1 - p).start()
            return carry

        scatter_wait(p)
        lax.fori_loop(0, tm, last_rows, 0)
        scatter_wait(1 - p)

    for p in range(2):
        @pl.when((i < nu) & (lax.rem(i, 2) == p))
        def _(p=p):
            block(p)

        @pl.when((i == nu) & (lax.rem(i, 2) == p))
        def _(p=p):
            flush(p)


def _experts(blk_e, first, nxt, n_used, table, mp, wg, wu, wd, *, tm):
    _, d, f = wg.shape
    sub = d // 2 // LANES
    n_tokens = mp.shape[0] // sub
    n_blocks = table.shape[0] // tm
    return pl.pallas_call(
        functools.partial(_experts_kernel, sub=sub, tm=tm, n_tokens=n_tokens),
        grid_spec=pltpu.PrefetchScalarGridSpec(
            num_scalar_prefetch=5,
            grid=(n_blocks + 1,),
            in_specs=[pl.BlockSpec(memory_space=pl.ANY)] * 4,
            out_specs=pl.BlockSpec(memory_space=pl.ANY),
            scratch_shapes=[pltpu.VMEM(mp.shape, jnp.int32),
                            pltpu.VMEM((d, f), F32), pltpu.VMEM((d, f), F32), pltpu.VMEM((f, d), F32),
                            pltpu.VMEM((d, f), BF16), pltpu.VMEM((d, f), BF16), pltpu.VMEM((f, d), BF16),
                            pltpu.VMEM((tm * sub, LANES), jnp.int32), pltpu.VMEM((tm * sub, LANES), jnp.int32),
                            pltpu.VMEM((tm * sub, LANES), jnp.int32), pltpu.VMEM((tm * sub, LANES), jnp.int32),
                            pltpu.SemaphoreType.DMA((3,)), pltpu.SemaphoreType.DMA(()),
                            pltpu.SemaphoreType.DMA((2,))],
        ),
        out_shape=jax.ShapeDtypeStruct(((TOP_K * n_tokens + tm) * sub, LANES), jnp.int32),
        compiler_params=pltpu.CompilerParams(dimension_semantics=("arbitrary",),
                                             vmem_limit_bytes=EXPERTS_VMEM_LIMIT),
        name="routed_experts",
    )(blk_e, first, nxt, n_used, table, mp, wg, wu, wd)


def _final_kernel(h_ref, gate_ref, p_ref, gp_ref, wgate_ref, wproj_ref, *rest, sub):
    y_refs, o_ref = rest[:TOP_K], rest[TOP_K]
    tm = h_ref.shape[0]
    n = h_ref.shape[1] // 2
    n_grp = COMBINE_ROW_GROUPS if tm % (8 * COMBINE_ROW_GROUPS) == 0 else 1
    gr = tm // n_grp
    rows = [pl.ds(g * gr, gr) for g in range(n_grp)]
    pps = [jnp.dot(p_ref[r, :].astype(BF16), wproj_ref[...], preferred_element_type=F32) for r in rows]
    gates = [gate_ref[r, :] for r in rows]
    acc_los = [h_ref[r, :n] for r in rows]
    acc_his = [h_ref[r, n:] for r in rows]
    for k in range(TOP_K):
        for g in range(n_grp):
            packed = jnp.concatenate([y_refs[k][pl.ds(g * gr * sub + s, gr, stride=sub), :] for s in range(sub)],
                                     axis=1)
            lo, hi = _unpack_halves(packed)
            acc_los[g] = acc_los[g] + gates[g][:, k:k + 1] * lo
            acc_his[g] = acc_his[g] + gates[g][:, k:k + 1] * hi
    h2s = [jnp.concatenate([lo, hi], axis=1) for lo, hi in zip(acc_los, acc_his)]
    nrms = [_rms(h2, gp_ref[...]).astype(BF16) for h2 in h2s]
    gts = [_sigmoid(jnp.dot(nrm, wgate_ref[...], preferred_element_type=F32)) for nrm in nrms]
    for r, h2, gt, pp in zip(rows, h2s, gts, pps):
        o_ref[r, :] = h2 + gt * pp


def _final(h1s, gate, p2, gp, wgate, wproj, yb, *, tm):
    t, d = h1s.shape
    dp = p2.shape[1]
    sub = d // 2 // LANES
    nt = t // tm
    row = lambda i: (i, 0)
    fix = lambda i: (0, 0)
    y_specs = [pl.BlockSpec((tm * sub, LANES), lambda i, k=k: (k * nt + i, 0)) for k in range(TOP_K)]
    return pl.pallas_call(
        functools.partial(_final_kernel, sub=sub),
        grid=(nt,),
        in_specs=[
            pl.BlockSpec((tm, d), row), pl.BlockSpec((tm, IDX_ROWS), row), pl.BlockSpec((tm, dp), row),
            pl.BlockSpec((1, d), fix), pl.BlockSpec((d, d), fix), pl.BlockSpec((dp, d), fix),
        ] + y_specs,
        out_specs=pl.BlockSpec((tm, d), row),
        out_shape=jax.ShapeDtypeStruct((t, d), F32),
        compiler_params=_params(("parallel",)),
        name="combine_ple",
    )(h1s, gate, p2, gp, wgate, wproj, *([yb] * TOP_K))


def _largest_tile(n, cap):
    t = min(n, cap)
    while n % t:
        t //= 2
    return t


def _layer(h, p_l, norm_mix, w_in, q_norm, k_norm, lam_re, lam_im, log_dt, b_re, b_im, c_re, c_im, d_skip,
           w_glu, attn_out_norm, ssm_out_norm, w_out, norm_ffn, w_router, router_bias, w_exp_gate,
           w_exp_up, w_exp_down, w_sh_gate, w_sh_up, w_sh_down, norm_ple, w_ple_gate, w_ple_proj):
    bsz, seq, d = h.shape
    t = bsz * seq
    head_dim = q_norm.shape[-1]
    d_attn = attn_out_norm.shape[-1]
    n_heads = d_attn // head_dim
    n_experts = w_router.shape[-1]
    x2 = h.reshape(t, d)

    tn = d_attn if d_attn % 128 == 0 else head_dim
    scale = 1.0 / math.sqrt(head_dim)
    hg = jnp.concatenate([jnp.tile(q_norm * scale, n_heads), jnp.tile(k_norm, n_heads),
                          jnp.ones((w_in.shape[1] - 2 * d_attn,), F32)])[None, :]
    z, u = _inproj(x2, norm_mix[None, :], w_in.astype(BF16), hg, d_qkv=3 * d_attn,
                   n_norm_tiles=2 * d_attn // tn, head_dim=head_dim, tm=_largest_tile(t, 1024), tn=tn)

    o_attn = _attention(z, batch=bsz, seq=seq, n_heads=n_heads, head_dim=head_dim, blk=_largest_tile(seq, 256))

    n_chunks = seq // S5_CHUNK
    mats = _s5_prepare(lam_re, lam_im, log_dt, b_re, b_im, c_re, c_im, d_skip, n_chunks)
    y_ssm = _s5(u, *mats, n_chunks=n_chunks)

    tm = _largest_tile(t, 256)
    tm_sh = _largest_tile(t, 512)
    w_out_b = w_out.astype(BF16)
    wr_hi = w_router.astype(BF16)
    wr_lo = (w_router - wr_hi.astype(F32)).astype(BF16)
    h1, idx, gate, loc, counts = _postmix(
        o_attn, y_ssm, x2, w_glu.astype(BF16), w_out_b[:d_attn], w_out_b[d_attn:],
        attn_out_norm[None, :], ssm_out_norm[None, :], norm_ffn[None, :],
        jnp.concatenate([wr_hi, wr_lo], axis=1), router_bias[:, None], tm=_largest_tile(t, 512))

    rb = _largest_tile(t, 256)
    n_pad = t * TOP_K + n_experts * rb
    n_blocks = n_pad // rb
    cnt = counts[:, 0].astype(jnp.int32)
    pcnt = (cnt + rb - 1) // rb * rb
    pend = jnp.cumsum(pcnt)
    pstart = pend - pcnt
    onehot = idx[:, :, None] == jnp.arange(n_experts, dtype=jnp.int32)[None, None, :]
    dest = jnp.sum(jnp.where(onehot, pstart[None, None, :], 0), axis=-1) + loc
    dest = dest.reshape(IDX_ROWS, t // tm_sh, tm_sh).transpose(1, 0, 2).reshape(-1)
    n_used = (pend[-1] // rb).astype(jnp.int32)
    blk_ids = jnp.arange(n_blocks, dtype=jnp.int32)
    starts = (blk_ids * rb)[:, None]
    owner = (starts >= pstart[None, :]) & (starts < pend[None, :])
    blk_e = jnp.sum(jnp.where(owner, jnp.arange(n_experts, dtype=jnp.int32)[None, :], 0), axis=1)
    is_first = jnp.any(owner & (starts == pstart[None, :]), axis=1).astype(jnp.int32)
    first_pos = jnp.where(is_first > 0, blk_ids, n_blocks)
    nxt_pos = jnp.concatenate([lax.cummin(first_pos, axis=0, reverse=True)[1:],
                               jnp.full((1,), n_blocks, jnp.int32)])
    nxt_hit = nxt_pos[:, None] == blk_ids[None, :]
    nxt = jnp.where(nxt_pos < n_blocks, jnp.sum(jnp.where(nxt_hit, blk_e[None, :], 0), axis=1), -1)
    last_e = jnp.sum(jnp.where(blk_ids == n_used - 1, blk_e, 0))
    blk_e = jnp.where(blk_ids < n_used, blk_e, last_e).astype(jnp.int32)

    table0 = TOP_K * t + jnp.arange(n_pad, dtype=jnp.int32) % rb
    h1s, mp, table = _shared(dest, table0, h1, norm_ffn[None, :], w_sh_gate.astype(BF16), w_sh_up.astype(BF16),
                             w_sh_down.astype(BF16), tm=tm_sh)
    yb = _experts(blk_e, is_first, nxt.astype(jnp.int32), n_used[None], table, mp, w_exp_gate, w_exp_up,
                  w_exp_down, tm=rb)
    out = _final(h1s, gate, p_l.reshape(t, -1), norm_ple[None, :], w_ple_gate.astype(BF16),
                 w_ple_proj.astype(BF16), yb, tm=tm)
    return out.reshape(bsz, seq, d)


def kernel(x, p, norm_mix, w_in, q_norm, k_norm, ssm_lam_re, ssm_lam_im, ssm_log_dt, ssm_b_re, ssm_b_im,
           ssm_c_re, ssm_c_im, ssm_d, w_glu, attn_out_norm, ssm_out_norm, w_out, norm_ffn, w_router,
           router_bias, w_exp_gate, w_exp_up, w_exp_down, w_sh_gate, w_sh_up, w_sh_down, norm_ple,
           w_ple_gate, w_ple_proj):
    h = x
    for i in range(p.shape[0]):
        h = _layer(h, p[i], norm_mix[i], w_in[i], q_norm[i], k_norm[i], ssm_lam_re[i], ssm_lam_im[i],
                   ssm_log_dt[i], ssm_b_re[i], ssm_b_im[i], ssm_c_re[i], ssm_c_im[i], ssm_d[i], w_glu[i],
                   attn_out_norm[i], ssm_out_norm[i], w_out[i], norm_ffn[i], w_router[i], router_bias[i],
                   w_exp_gate[i], w_exp_up[i], w_exp_down[i], w_sh_gate[i], w_sh_up[i], w_sh_down[i],
                   norm_ple[i], w_ple_gate[i], w_ple_proj[i])
    return h
```
